```python
import math
import jax
import jax.numpy as jnp
from jax import lax
import numpy as np

D_MODEL = 4096
BATCH = 8
SEQ = 4096
DEPTH = 1

N_META = 16
CHUNK = 128
RET_HEADS = D_MODEL // 512
RET_DK = 256
RET_DV = 256
SB_HEADS = D_MODEL // 256
SB_DH = 128
RET_WIDTH = RET_HEADS * RET_DV
RET_QK_WIDTH = RET_HEADS * RET_DK
SB_WIDTH = SB_HEADS * SB_DH
MIX_WIDTH = RET_WIDTH + SB_WIDTH
IN_COLS = 2 * RET_QK_WIDTH + 2 * RET_WIDTH + 3 * SB_WIDTH
D_FF = ((8 * D_MODEL // 3 + 255) // 256) * 256
CONV_W = 3
EPS = 1e-6
ROPE_BASE = 10000.0

kernel_name = "hymba_retention_stickbreaking_convffn"


def rms_norm(x, w):
    xf = x.astype(jnp.float32)
    y = xf * lax.rsqrt(jnp.mean(xf * xf, axis=-1, keepdims=True) + EPS)
    return (y * w.astype(jnp.float32)).astype(x.dtype)


def head_rms_norm(y, w):
    b, t, h, d = y.shape
    yf = y.astype(jnp.float32)
    yf = yf * lax.rsqrt(jnp.mean(yf * yf, axis=-1, keepdims=True) + EPS)
    yf = yf * w.astype(jnp.float32).reshape(h, d)
    return yf.reshape(b, t, h * d)


def apply_rotary(x, pos):
    half = x.shape[-1] // 2
    inv = ROPE_BASE ** (-jnp.arange(half, dtype=jnp.float32) / half)
    ang = pos[:, None] * inv[None, :]
    cos = jnp.cos(ang)[None, :, None, :].astype(x.dtype)
    sin = jnp.sin(ang)[None, :, None, :].astype(x.dtype)
    x1, x2 = x[..., :half], x[..., half:]
    return jnp.concatenate([x1 * cos - x2 * sin, x1 * sin + x2 * cos], axis=-1)


def retention_chunkwise(q, k, v):
    b, tp, h, dk = q.shape
    dv = v.shape[-1]
    nc = tp // CHUNK
    log_gamma = jnp.log(1.0 - 2.0 ** (-5.0 - jnp.arange(h, dtype=jnp.float32)))
    idx = jnp.arange(CHUNK, dtype=jnp.float32)
    diff = idx[:, None] - idx[None, :]
    intra_decay = jnp.where(diff[None] >= 0, jnp.exp(jnp.maximum(diff, 0.0)[None] * log_gamma[:, None, None]), 0.0)
    q_decay = jnp.exp((idx[None, :] + 1.0) * log_gamma[:, None])
    k_decay = jnp.exp((CHUNK - 1.0 - idx[None, :]) * log_gamma[:, None])
    chunk_decay = jnp.exp(CHUNK * log_gamma)

    def to_chunks(a):
        return a.reshape(b, nc, CHUNK, h, a.shape[-1]).transpose(1, 0, 3, 2, 4)

    qc, kc, vc = to_chunks(q), to_chunks(k), to_chunks(v)

    def body(state, inp):
        qi, ki, vi = inp
        s = jnp.einsum('bhnd,bhmd->bhnm', qi, ki) * intra_decay[None]
        intra = jnp.einsum('bhnm,bhme->bhne', s, vi)
        cross = jnp.einsum('bhnd,bhde->bhne', qi, state) * q_decay[None, :, :, None]
        new_state = state * chunk_decay[None, :, None, None] + jnp.einsum(
            'bhmd,bhme->bhde', ki * k_decay[None, :, :, None], vi)
        return new_state, intra + cross

    state0 = jnp.zeros((b, h, dk, dv), jnp.float32)
    _, ys = lax.scan(body, state0, (qc, kc, vc))
    return ys.transpose(1, 0, 3, 2, 4).reshape(b, tp, h, dv)


def stick_breaking_attention(q, k, v, key_valid):
    b, tp, h, d = q.shape
    nb = tp // CHUNK
    scale = 1.0 / math.sqrt(d)
    qh = q.transpose(0, 2, 1, 3)
    kh = k.transpose(0, 2, 1, 3)
    vh = v.transpose(0, 2, 1, 3)
    qb = qh.reshape(b, h, nb, CHUNK, d).transpose(2, 0, 1, 3, 4)
    starts = jnp.arange(nb, dtype=jnp.int32) * CHUNK
    key_pos = jnp.arange(tp, dtype=jnp.int32)

    def block(args):
        qblk, start = args
        z = jnp.einsum('bhqd,bhkd->bhqk', qblk, kh).astype(jnp.float32) * scale
        qpos = start + jnp.arange(CHUNK, dtype=jnp.int32)
        mask = (key_pos[None, :] < qpos[:, None]) & key_valid[None, :]
        log_beta = jax.nn.log_sigmoid(z)
        log_keep = jnp.where(mask, jax.nn.log_sigmoid(-z), 0.0)
        rev = lax.cumsum(log_keep, axis=3, reverse=True)
        after = jnp.concatenate([rev[..., 1:], jnp.zeros_like(rev[..., :1])], axis=-1)
        w = jnp.where(mask, jnp.exp(log_beta + after), 0.0)
        return jnp.einsum('bhqk,bhkd->bhqd', w.astype(vh.dtype), vh)

    out = lax.map(block, (qb, starts))
    return out.transpose(1, 0, 3, 2, 4).reshape(b, tp, h, d)


def hybrid_mixer(h, w_in, ret_gn_w, sb_norm_w, w_out):
    b, t, _ = h.shape
    n_pad = CHUNK - N_META
    tp = t + n_pad
    proj = h @ w_in
    proj = jnp.pad(proj, ((0, 0), (n_pad, 0), (0, 0)))
    valid = jnp.arange(tp) >= n_pad
    vmask = valid.astype(proj.dtype)[None, :, None, None]
    pos = (jnp.arange(tp) - n_pad).astype(jnp.float32)

    o = 0
    rq = proj[..., o:o + RET_QK_WIDTH]; o += RET_QK_WIDTH
    rk = proj[..., o:o + RET_QK_WIDTH]; o += RET_QK_WIDTH
    rv = proj[..., o:o + RET_WIDTH]; o += RET_WIDTH
    rg = proj[..., o:o + RET_WIDTH]; o += RET_WIDTH
    sq = proj[..., o:o + SB_WIDTH]; o += SB_WIDTH
    sk = proj[..., o:o + SB_WIDTH]; o += SB_WIDTH
    sv = proj[..., o:o + SB_WIDTH]

    rq = apply_rotary(rq.reshape(b, tp, RET_HEADS, RET_DK), pos) * (RET_DK ** -0.5)
    rk = apply_rotary(rk.reshape(b, tp, RET_HEADS, RET_DK), pos) * vmask
    rv = rv.reshape(b, tp, RET_HEADS, RET_DV) * vmask
    ry = retention_chunkwise(rq, rk, rv)
    ret_out = jax.nn.silu(rg.astype(jnp.float32)) * head_rms_norm(ry, ret_gn_w)

    sq = sq.reshape(b, tp, SB_HEADS, SB_DH)
    sk = sk.reshape(b, tp, SB_HEADS, SB_DH) * vmask
    sv = sv.reshape(b, tp, SB_HEADS, SB_DH) * vmask
    sy = stick_breaking_attention(sq, sk, sv, valid)
    sb_out = head_rms_norm(sy, sb_norm_w)

    mixed = jnp.concatenate([ret_out, sb_out], axis=-1)[:, n_pad:].astype(h.dtype)
    return mixed @ w_out


def conv_ffn(h, w_up, conv_w, conv_b, w_down):
    t = h.shape[1]
    gu = h @ w_up
    gate, up = gu[..., :D_FF], gu[..., D_FF:]
    gp = jnp.pad(gate, ((0, 0), (CONV_W - 1, 0), (0, 0)))
    conv = conv_b[None, None, :]
    for i in range(CONV_W):
        conv = conv + gp[:, i:i + t] * conv_w[i][None, None, :]
    return (jax.nn.silu(conv) * up) @ w_down


def _fwd_setup_inputs(seed: int = 0) -> dict:
    key = jax.random.key(seed)
    ks = jax.random.split(key, 16)
    f32 = jnp.float32

    def gain(k, n):
        return 1.0 + 0.01 * jax.random.normal(k, (DEPTH, n), f32)

    return {
        "x": jax.random.normal(ks[0], (BATCH, SEQ, D_MODEL), f32),
        "meta_tokens": jax.random.normal(ks[1], (N_META, D_MODEL), f32),
        "attn_pre_norm_w": gain(ks[2], D_MODEL),
        "w_in": jax.random.normal(ks[3], (DEPTH, D_MODEL, IN_COLS), f32) * D_MODEL ** -0.5,
        "ret_gn_w": gain(ks[4], RET_WIDTH),
        "sb_norm_w": gain(ks[5], SB_WIDTH),
        "w_out": jax.random.normal(ks[6], (DEPTH, MIX_WIDTH, D_MODEL), f32) * MIX_WIDTH ** -0.5,
        "attn_post_norm_w": gain(ks[7], D_MODEL),
        "ffn_pre_norm_w": gain(ks[8], D_MODEL),
        "w_up": jax.random.normal(ks[9], (DEPTH, D_MODEL, 2 * D_FF), f32) * D_MODEL ** -0.5,
        "conv_w": jax.random.normal(ks[10], (DEPTH, CONV_W, D_FF), f32) * CONV_W ** -0.5,
        "conv_b": 0.01 * jax.random.normal(ks[11], (DEPTH, D_FF), f32),
        "w_down": jax.random.normal(ks[12], (DEPTH, D_FF, D_MODEL), f32) * D_FF ** -0.5,
        "ffn_post_norm_w": gain(ks[13], D_MODEL),
    }


def _fwd_reference(x, meta_tokens, attn_pre_norm_w, w_in, ret_gn_w, sb_norm_w, w_out,
              attn_post_norm_w, ffn_pre_norm_w, w_up, conv_w, conv_b, w_down, ffn_post_norm_w):
    b = x.shape[0]
    meta = jnp.broadcast_to(meta_tokens[None].astype(x.dtype), (b, N_META, x.shape[-1]))
    h = jnp.concatenate([meta, x], axis=1)
    for l in range(DEPTH):
        a = hybrid_mixer(rms_norm(h, attn_pre_norm_w[l]), w_in[l], ret_gn_w[l], sb_norm_w[l], w_out[l])
        h = h + rms_norm(a, attn_post_norm_w[l])
        f = conv_ffn(rms_norm(h, ffn_pre_norm_w[l]), w_up[l], conv_w[l], conv_b[l], w_down[l])
        h = h + rms_norm(f, ffn_post_norm_w[l])
    return h[:, N_META:]


import jax as _jax
import jax.numpy as _jnp

TWIN_FORMAT = 'train_step'
FWD_PARAMS = ['x', 'meta_tokens', 'attn_pre_norm_w', 'w_in', 'ret_gn_w', 'sb_norm_w', 'w_out', 'attn_post_norm_w', 'ffn_pre_norm_w', 'w_up', 'conv_w', 'conv_b', 'w_down', 'ffn_post_norm_w']
TWIN_WEIGHTS = ['meta_tokens', 'attn_pre_norm_w', 'w_in', 'ret_gn_w', 'sb_norm_w', 'w_out', 'attn_post_norm_w', 'ffn_pre_norm_w', 'w_up', 'conv_w', 'conv_b', 'w_down', 'ffn_post_norm_w']
TWIN_DIFF_INPUT = 'x'
TWIN_INPUTS = ['x', 'meta_tokens', 'attn_pre_norm_w', 'w_in', 'ret_gn_w', 'sb_norm_w', 'w_out', 'attn_post_norm_w', 'ffn_pre_norm_w', 'w_up', 'conv_w', 'conv_b', 'w_down', 'ffn_post_norm_w', 'loss_target', 'm_meta_tokens', 'm_attn_pre_norm_w', 'm_w_in', 'm_ret_gn_w', 'm_sb_norm_w', 'm_w_out', 'm_attn_post_norm_w', 'm_ffn_pre_norm_w', 'm_w_up', 'm_conv_w', 'm_conv_b', 'm_w_down', 'm_ffn_post_norm_w', 'v_meta_tokens', 'v_attn_pre_norm_w', 'v_w_in', 'v_ret_gn_w', 'v_sb_norm_w', 'v_w_out', 'v_attn_post_norm_w', 'v_ffn_pre_norm_w', 'v_w_up', 'v_conv_w', 'v_conv_b', 'v_w_down', 'v_ffn_post_norm_w']
TWIN_OUTPUTS = ['loss', 'grad_x', 'grad_meta_tokens', 'grad_attn_pre_norm_w', 'grad_w_in', 'grad_ret_gn_w', 'grad_sb_norm_w', 'grad_w_out', 'grad_attn_post_norm_w', 'grad_ffn_pre_norm_w', 'grad_w_up', 'grad_conv_w', 'grad_conv_b', 'grad_w_down', 'grad_ffn_post_norm_w', 'delta_meta_tokens', 'delta_attn_pre_norm_w', 'delta_w_in', 'delta_ret_gn_w', 'delta_sb_norm_w', 'delta_w_out', 'delta_attn_post_norm_w', 'delta_ffn_pre_norm_w', 'delta_w_up', 'delta_conv_w', 'delta_conv_b', 'delta_w_down', 'delta_ffn_post_norm_w', 'new_m_meta_tokens', 'new_m_attn_pre_norm_w', 'new_m_w_in', 'new_m_ret_gn_w', 'new_m_sb_norm_w', 'new_m_w_out', 'new_m_attn_post_norm_w', 'new_m_ffn_pre_norm_w', 'new_m_w_up', 'new_m_conv_w', 'new_m_conv_b', 'new_m_w_down', 'new_m_ffn_post_norm_w', 'new_v_meta_tokens', 'new_v_attn_pre_norm_w', 'new_v_w_in', 'new_v_ret_gn_w', 'new_v_sb_norm_w', 'new_v_w_out', 'new_v_attn_post_norm_w', 'new_v_ffn_pre_norm_w', 'new_v_w_up', 'new_v_conv_w', 'new_v_conv_b', 'new_v_w_down', 'new_v_ffn_post_norm_w']
TWIN_LEAF_KINDS = {'loss': 'loss', 'grad_x': 'grad_x', 'grad_meta_tokens': 'grad_w', 'grad_attn_pre_norm_w': 'grad_w', 'grad_w_in': 'grad_w', 'grad_ret_gn_w': 'grad_w', 'grad_sb_norm_w': 'grad_w', 'grad_w_out': 'grad_w', 'grad_attn_post_norm_w': 'grad_w', 'grad_ffn_pre_norm_w': 'grad_w', 'grad_w_up': 'grad_w', 'grad_conv_w': 'grad_w', 'grad_conv_b': 'grad_w', 'grad_w_down': 'grad_w', 'grad_ffn_post_norm_w': 'grad_w', 'delta_meta_tokens': 'delta_w', 'delta_attn_pre_norm_w': 'delta_w', 'delta_w_in': 'delta_w', 'delta_ret_gn_w': 'delta_w', 'delta_sb_norm_w': 'delta_w', 'delta_w_out': 'delta_w', 'delta_attn_post_norm_w': 'delta_w', 'delta_ffn_pre_norm_w': 'delta_w', 'delta_w_up': 'delta_w', 'delta_conv_w': 'delta_w', 'delta_conv_b': 'delta_w', 'delta_w_down': 'delta_w', 'delta_ffn_post_norm_w': 'delta_w', 'new_m_meta_tokens': 'new_m', 'new_m_attn_pre_norm_w': 'new_m', 'new_m_w_in': 'new_m', 'new_m_ret_gn_w': 'new_m', 'new_m_sb_norm_w': 'new_m', 'new_m_w_out': 'new_m', 'new_m_attn_post_norm_w': 'new_m', 'new_m_ffn_pre_norm_w': 'new_m', 'new_m_w_up': 'new_m', 'new_m_conv_w': 'new_m', 'new_m_conv_b': 'new_m', 'new_m_w_down': 'new_m', 'new_m_ffn_post_norm_w': 'new_m', 'new_v_meta_tokens': 'new_v', 'new_v_attn_pre_norm_w': 'new_v', 'new_v_w_in': 'new_v', 'new_v_ret_gn_w': 'new_v', 'new_v_sb_norm_w': 'new_v', 'new_v_w_out': 'new_v', 'new_v_attn_post_norm_w': 'new_v', 'new_v_ffn_pre_norm_w': 'new_v', 'new_v_w_up': 'new_v', 'new_v_conv_w': 'new_v', 'new_v_conv_b': 'new_v', 'new_v_w_down': 'new_v', 'new_v_ffn_post_norm_w': 'new_v'}


def _forward(args):
    return _fwd_reference(*[args[k] for k in FWD_PARAMS])


def _output_shape():
    out = _jax.eval_shape(lambda: _forward(_fwd_setup_inputs(0)))
    return out.shape, out.dtype

N_MICROBATCH = 1
ADAM_LR = 0.001
ADAM_B1 = 0.9
ADAM_B2 = 0.999
ADAM_EPS = 1e-08
ADAM_WD = 0.01
ADAM_STEP = 10
PER_EXAMPLE_BATCH_AXIS = {'x': 0, 'loss_target': 0}
SHARED_INPUTS = []
_WEIGHT_DTYPES = {'meta_tokens': _jnp.float32, 'attn_pre_norm_w': _jnp.float32, 'w_in': _jnp.float32, 'ret_gn_w': _jnp.float32, 'sb_norm_w': _jnp.float32, 'w_out': _jnp.float32, 'attn_post_norm_w': _jnp.float32, 'ffn_pre_norm_w': _jnp.float32, 'w_up': _jnp.float32, 'conv_w': _jnp.float32, 'conv_b': _jnp.float32, 'w_down': _jnp.float32, 'ffn_post_norm_w': _jnp.float32}
MOMENT_SCALE = {'meta_tokens': 7.614637e-03, 'attn_pre_norm_w': 1.623046e-01, 'w_in': 8.618875e-02, 'ret_gn_w': 8.359654e-02, 'sb_norm_w': 1.561054e-01, 'w_out': 1.146609e-01, 'attn_post_norm_w': 7.977524e+00, 'ffn_pre_norm_w': 1.184209e-01, 'w_up': 5.103753e-02, 'conv_w': 5.236421e-02, 'conv_b': 7.411957e-02, 'w_down': 8.797519e-02, 'ffn_post_norm_w': 7.994882e+00}


def _to_microbatches(a, axis):
    t = _jnp.moveaxis(a, axis, 0)
    t = t.reshape((N_MICROBATCH, t.shape[0] // N_MICROBATCH) + t.shape[1:])
    return _jnp.moveaxis(t, 1, axis + 1)


def setup_inputs(seed: int = 0) -> dict:
    inp = _fwd_setup_inputs(seed)
    key = _jax.random.fold_in(_jax.random.key(seed), 7919)
    shape, _ = _output_shape()
    out = dict(inp)
    out["loss_target"] = _jax.random.normal(_jax.random.fold_in(key, 0), shape, _jnp.float32)
    for i, name in enumerate(TWIN_WEIGHTS):
        w = inp[name].astype(_jnp.float32)
        if MOMENT_SCALE is None:
            s = _jnp.sqrt(_jnp.mean(_jnp.square(w)) + 1e-30)
        else:
            s = MOMENT_SCALE[name]
        km, kv = _jax.random.split(_jax.random.fold_in(key, i + 1))
        out[name] = w
        out["m_" + name] = s * _jax.random.normal(km, w.shape, _jnp.float32)
        out["v_" + name] = (s * s) * _jax.random.uniform(kv, w.shape, _jnp.float32, 0.5, 1.5)
    if N_MICROBATCH > 1:
        for name, axis in PER_EXAMPLE_BATCH_AXIS.items():
            out[name] = _to_microbatches(out[name], axis)
    return {'x': out['x'], 'meta_tokens': out['meta_tokens'], 'attn_pre_norm_w': out['attn_pre_norm_w'], 'w_in': out['w_in'], 'ret_gn_w': out['ret_gn_w'], 'sb_norm_w': out['sb_norm_w'], 'w_out': out['w_out'], 'attn_post_norm_w': out['attn_post_norm_w'], 'ffn_pre_norm_w': out['ffn_pre_norm_w'], 'w_up': out['w_up'], 'conv_w': out['conv_w'], 'conv_b': out['conv_b'], 'w_down': out['w_down'], 'ffn_post_norm_w': out['ffn_post_norm_w'], 'loss_target': out['loss_target'], 'm_meta_tokens': out['m_meta_tokens'], 'm_attn_pre_norm_w': out['m_attn_pre_norm_w'], 'm_w_in': out['m_w_in'], 'm_ret_gn_w': out['m_ret_gn_w'], 'm_sb_norm_w': out['m_sb_norm_w'], 'm_w_out': out['m_w_out'], 'm_attn_post_norm_w': out['m_attn_post_norm_w'], 'm_ffn_pre_norm_w': out['m_ffn_pre_norm_w'], 'm_w_up': out['m_w_up'], 'm_conv_w': out['m_conv_w'], 'm_conv_b': out['m_conv_b'], 'm_w_down': out['m_w_down'], 'm_ffn_post_norm_w': out['m_ffn_post_norm_w'], 'v_meta_tokens': out['v_meta_tokens'], 'v_attn_pre_norm_w': out['v_attn_pre_norm_w'], 'v_w_in': out['v_w_in'], 'v_ret_gn_w': out['v_ret_gn_w'], 'v_sb_norm_w': out['v_sb_norm_w'], 'v_w_out': out['v_w_out'], 'v_attn_post_norm_w': out['v_attn_post_norm_w'], 'v_ffn_pre_norm_w': out['v_ffn_pre_norm_w'], 'v_w_up': out['v_w_up'], 'v_conv_w': out['v_conv_w'], 'v_conv_b': out['v_conv_b'], 'v_w_down': out['v_w_down'], 'v_ffn_post_norm_w': out['v_ffn_post_norm_w']}


def _loss(weights, diff, rest, loss_target):
    with _jax.named_scope("forward"):
        args = {**rest, TWIN_DIFF_INPUT: diff, **{k: w.astype(_WEIGHT_DTYPES[k]) for k, w in weights.items()}}
        y = _forward(args)
    with _jax.named_scope("loss_head"):
        err = _jnp.square(y.astype(_jnp.float32) - loss_target)
        return 0.5 * _jnp.sum(_jnp.mean(err, axis=-1)) if err.ndim else 0.5 * err


def _adamw(w, g, m, v):
    m = ADAM_B1 * m + (1.0 - ADAM_B1) * g
    v = ADAM_B2 * v + (1.0 - ADAM_B2) * _jnp.square(g)
    m_hat = m / (1.0 - ADAM_B1 ** ADAM_STEP)
    v_hat = v / (1.0 - ADAM_B2 ** ADAM_STEP)
    delta = -ADAM_LR * (m_hat / (_jnp.sqrt(v_hat) + ADAM_EPS) + ADAM_WD * w)
    return delta, m, v


def reference(x, meta_tokens, attn_pre_norm_w, w_in, ret_gn_w, sb_norm_w, w_out, attn_post_norm_w, ffn_pre_norm_w, w_up, conv_w, conv_b, w_down, ffn_post_norm_w, loss_target, m_meta_tokens, m_attn_pre_norm_w, m_w_in, m_ret_gn_w, m_sb_norm_w, m_w_out, m_attn_post_norm_w, m_ffn_pre_norm_w, m_w_up, m_conv_w, m_conv_b, m_w_down, m_ffn_post_norm_w, v_meta_tokens, v_attn_pre_norm_w, v_w_in, v_ret_gn_w, v_sb_norm_w, v_w_out, v_attn_post_norm_w, v_ffn_pre_norm_w, v_w_up, v_conv_w, v_conv_b, v_w_down, v_ffn_post_norm_w):
    given = dict(x=x, meta_tokens=meta_tokens, attn_pre_norm_w=attn_pre_norm_w, w_in=w_in, ret_gn_w=ret_gn_w, sb_norm_w=sb_norm_w, w_out=w_out, attn_post_norm_w=attn_post_norm_w, ffn_pre_norm_w=ffn_pre_norm_w, w_up=w_up, conv_w=conv_w, conv_b=conv_b, w_down=w_down, ffn_post_norm_w=ffn_post_norm_w, loss_target=loss_target, m_meta_tokens=m_meta_tokens, m_attn_pre_norm_w=m_attn_pre_norm_w, m_w_in=m_w_in, m_ret_gn_w=m_ret_gn_w, m_sb_norm_w=m_sb_norm_w, m_w_out=m_w_out, m_attn_post_norm_w=m_attn_post_norm_w, m_ffn_pre_norm_w=m_ffn_pre_norm_w, m_w_up=m_w_up, m_conv_w=m_conv_w, m_conv_b=m_conv_b, m_w_down=m_w_down, m_ffn_post_norm_w=m_ffn_post_norm_w, v_meta_tokens=v_meta_tokens, v_attn_pre_norm_w=v_attn_pre_norm_w, v_w_in=v_w_in, v_ret_gn_w=v_ret_gn_w, v_sb_norm_w=v_sb_norm_w, v_w_out=v_w_out, v_attn_post_norm_w=v_attn_post_norm_w, v_ffn_pre_norm_w=v_ffn_pre_norm_w, v_w_up=v_w_up, v_conv_w=v_conv_w, v_conv_b=v_conv_b, v_w_down=v_w_down, v_ffn_post_norm_w=v_ffn_post_norm_w)
    weights = {n: given[n] for n in TWIN_WEIGHTS}
    shared = {n: given[n] for n in SHARED_INPUTS}
    per_example = {n: given[n] for n in ['x']}
    grad_fn = _jax.value_and_grad(_loss, argnums=(0, 1))

    def one_microbatch(ex, loss_target):
        ex = dict(ex)
        diff = ex.pop(TWIN_DIFF_INPUT)
        return grad_fn(weights, diff, {**shared, **ex}, loss_target)

    if N_MICROBATCH == 1:
        loss, (grad_w, grad_x) = one_microbatch(per_example, given["loss_target"])
    else:
        def body(carry, xs):
            loss_sum, grad_sum = carry
            l_k, (gw_k, gx_k) = one_microbatch(xs[0], xs[1])
            with _jax.named_scope("update"):
                return (loss_sum + l_k, _jax.tree.map(_jnp.add, grad_sum, gw_k)), gx_k

        init = (_jnp.zeros((), _jnp.float32), _jax.tree.map(_jnp.zeros_like, weights))
        (loss, grad_w), grad_x = _jax.lax.scan(body, init, (per_example, given["loss_target"]))
    with _jax.named_scope("update"):
        delta_w, new_m, new_v = {}, {}, {}
        for n in TWIN_WEIGHTS:
            delta_w[n], new_m[n], new_v[n] = _adamw(weights[n], grad_w[n], given["m_" + n], given["v_" + n])
    return (loss, grad_x, *[grad_w[n] for n in TWIN_WEIGHTS], *[delta_w[n] for n in TWIN_WEIGHTS],
            *[new_m[n] for n in TWIN_WEIGHTS], *[new_v[n] for n in TWIN_WEIGHTS])
```

```python
import functools
import math

import jax
import jax.numpy as jnp
from jax import lax
from jax.experimental import pallas as pl
from jax.experimental.pallas import tpu as pltpu

F32 = jnp.float32
BF16 = jnp.bfloat16
MESH = pl.DeviceIdType.MESH

EPS = 1e-6
ROPE_BASE = 10000.0
N_META = 16
CHUNK = 128
N_PAD = CHUNK - N_META
RET_DK = 256
SB_DH = 128
CONV_W = 3

ADAM_LR = 0.001
ADAM_B1 = 0.9
ADAM_B2 = 0.999
ADAM_EPS = 1e-08
ADAM_WD = 0.01
ADAM_STEP = 10

V7X_VMEM_BYTES = 64 * 2 ** 20
VMEM_LIMIT = V7X_VMEM_BYTES - 8 * 2 ** 20
MM_BUDGET = 36 * 2 ** 20
V7X_BF16_FLOPS = 0.9e15
V7X_HBM_BPS = 3.2e12
GRID_STEP_S = 0.35e-6

NN = (((1,), (0,)), ((), ()))
NT = (((1,), (1,)), ((), ()))
TN = (((0,), (0,)), ((), ()))


def _cp(sem, vmem=VMEM_LIMIT):
    return pltpu.CompilerParams(dimension_semantics=sem, vmem_limit_bytes=vmem)


def _dot(a, b, dims=NN):
    return lax.dot_general(a, b, dims, preferred_element_type=F32)


def _sigmoid(x):
    return 1.0 / (1.0 + jnp.exp(-x))


def _divisors(n, align):
    return [d for d in range(align, n + 1, align) if n % d == 0]


def _mm_tiles(mode, m, n, k, out_bytes):
    best = None
    tms = [d for d in _divisors(m, 128 if mode == "tn" else 16) if d >= 128]
    tns = [d for d in _divisors(n, 128) if d >= 128]
    tks = [d for d in _divisors(k, 128) if d >= 128]
    flops = 2.0 * m * n * k
    for tm in tms:
        for tn in tns:
            if tm * tn > 2112 * 1024:
                continue
            for tk in tks:
                nk = k // tk
                foot = 4 * (tm * tk + tk * tn) + 2 * tm * tn * out_bytes
                if nk > 1:
                    foot += 4 * tm * tn
                if foot > MM_BUDGET:
                    continue
                steps = (m // tm) * (n // tn) * nk
                for swap in (False, True):
                    if nk > 1:
                        traffic = 2.0 * (n // tn) * m * k + 2.0 * (m // tm) * n * k
                    elif swap:
                        traffic = 2.0 * n * k + 2.0 * (n // tn) * m * k
                    else:
                        traffic = 2.0 * m * k + 2.0 * (m // tm) * n * k
                    traffic += out_bytes * m * n
                    t = max(flops / V7X_BF16_FLOPS, traffic / V7X_HBM_BPS) + steps * GRID_STEP_S
                    if best is None or t < best[0]:
                        best = (t, tm, tn, tk, swap)
    assert best is not None, (mode, m, n, k)
    return best[1:]


def _matmul(a, b, mode, out_dtype, name):
    if mode == "nn":
        (m, k), (k2, n) = a.shape, b.shape
    elif mode == "nt":
        (m, k), (n, k2) = a.shape, b.shape
    else:
        (k, m), (k2, n) = a.shape, b.shape
    assert k == k2 and a.dtype == BF16 and b.dtype == BF16
    tm, tn, tk, swap = _mm_tiles(mode, m, n, k, jnp.dtype(out_dtype).itemsize)
    nk = k // tk
    dims = {"nn": NN, "nt": NT, "tn": TN}[mode]

    def ij(g0, g1):
        return (g1, g0) if swap else (g0, g1)

    if mode == "tn":
        a_spec = pl.BlockSpec((tk, tm), lambda g0, g1, kk: (kk, ij(g0, g1)[0]))
    else:
        a_spec = pl.BlockSpec((tm, tk), lambda g0, g1, kk: (ij(g0, g1)[0], kk))
    if mode == "nt":
        b_spec = pl.BlockSpec((tn, tk), lambda g0, g1, kk: (ij(g0, g1)[1], kk))
    else:
        b_spec = pl.BlockSpec((tk, tn), lambda g0, g1, kk: (kk, ij(g0, g1)[1]))
    o_spec = pl.BlockSpec((tm, tn), lambda g0, g1, kk: ij(g0, g1))

    def kern(a_ref, b_ref, o_ref, *acc):
        prod = _dot(a_ref[...], b_ref[...], dims)
        if nk == 1:
            o_ref[...] = prod.astype(out_dtype)
        else:
            kk = pl.program_id(2)

            @pl.when(kk == 0)
            def _():
                acc[0][...] = prod

            @pl.when(kk > 0)
            def _():
                acc[0][...] += prod

            @pl.when(kk == nk - 1)
            def _():
                o_ref[...] = acc[0][...].astype(out_dtype)

    grid = (n // tn, m // tm, nk) if swap else (m // tm, n // tn, nk)
    return pl.pallas_call(
        kern, name=name, grid=grid, in_specs=[a_spec, b_spec], out_specs=o_spec,
        out_shape=jax.ShapeDtypeStruct((m, n), out_dtype),
        scratch_shapes=[pltpu.VMEM((tm, tn), F32)] if nk > 1 else [],
        compiler_params=_cp(("parallel", "parallel", "arbitrary")),
    )(a, b)


def _rms_fwd(x, w):
    r = lax.rsqrt(jnp.mean(x * x, axis=-1, keepdims=True) + EPS)
    return x * r * w


def _rms_bwd(x, w, g):
    r = lax.rsqrt(jnp.mean(x * x, axis=-1, keepdims=True) + EPS)
    gw = g * w
    dx = r * gw - x * (r * r * r * jnp.mean(gw * x, axis=-1, keepdims=True))
    return dx, g * (x * r)


def _row_spec(d):
    return pl.BlockSpec((CHUNK, d), lambda i: (i, 0))


def _vec_spec(d):
    return pl.BlockSpec((1, d), lambda i: (0, 0))


def _prenorm(h0, w):
    tp, d = h0.shape

    def kern(h_ref, w_ref, o_ref):
        o_ref[...] = _rms_fwd(h_ref[...], w_ref[...]).astype(BF16)

    return pl.pallas_call(
        kern, name="prenorm1", grid=(tp // CHUNK,),
        in_specs=[_row_spec(d), _vec_spec(d)], out_specs=_row_spec(d),
        out_shape=jax.ShapeDtypeStruct((tp, d), BF16),
        compiler_params=_cp(("parallel",)),
    )(h0, w)


def _mid_fwd(h0, a, w_post, w_pre):
    tp, d = h0.shape

    def kern(h_ref, a_ref, wp_ref, wq_ref, h1_ref, n2_ref):
        h1 = h_ref[...] + _rms_fwd(a_ref[...], wp_ref[...])
        h1_ref[...] = h1
        n2_ref[...] = _rms_fwd(h1, wq_ref[...]).astype(BF16)

    return pl.pallas_call(
        kern, name="mid_fwd", grid=(tp // CHUNK,),
        in_specs=[_row_spec(d), _row_spec(d), _vec_spec(d), _vec_spec(d)],
        out_specs=[_row_spec(d), _row_spec(d)],
        out_shape=[jax.ShapeDtypeStruct((tp, d), F32), jax.ShapeDtypeStruct((tp, d), BF16)],
        compiler_params=_cp(("parallel",)),
    )(h0, a, w_post, w_pre)


def _loss_bwd(h1, f, tgt, w_post):
    tp, d = h1.shape

    def kern(h_ref, f_ref, t_ref, w_ref, dy_ref, df_ref, loss_ref, dw_ref):
        i = pl.program_id(0)

        @pl.when(i == 0)
        def _():
            dy_ref[...] = jnp.zeros_like(dy_ref)
            df_ref[...] = jnp.zeros_like(df_ref)
            loss_ref[...] = jnp.zeros_like(loss_ref)
            dw_ref[...] = jnp.zeros_like(dw_ref)

        @pl.when(i > 0)
        def _():
            fv = f_ref[...]
            w = w_ref[...]
            err = h_ref[...] + _rms_fwd(fv, w) - t_ref[...]
            loss_ref[...] += 0.5 * jnp.sum(jnp.mean(err * err, axis=-1, keepdims=True))
            dy = err * (1.0 / d)
            dy_ref[...] = dy
            dfv, dwr = _rms_bwd(fv, w, dy)
            df_ref[...] = dfv.astype(BF16)
            dw_ref[...] += jnp.sum(dwr, axis=0, keepdims=True)

    return pl.pallas_call(
        kern, name="loss_bwd", grid=(tp // CHUNK,),
        in_specs=[_row_spec(d), _row_spec(d),
                  pl.BlockSpec((CHUNK, d), lambda i: (jnp.maximum(i - 1, 0), 0)), _vec_spec(d)],
        out_specs=[_row_spec(d), _row_spec(d), pl.BlockSpec((1, 128), lambda i: (0, 0)), _vec_spec(d)],
        out_shape=[jax.ShapeDtypeStruct((tp, d), F32), jax.ShapeDtypeStruct((tp, d), BF16),
                   jax.ShapeDtypeStruct((1, 128), F32), jax.ShapeDtypeStruct((1, d), F32)],
        compiler_params=_cp(("arbitrary",)),
    )(h1, f, tgt, w_post)


def _mid_bwd(dy, dn2, h1, a, w_pre, w_post):
    tp, d = h1.shape

    def kern(dy_ref, dn_ref, h_ref, a_ref, wq_ref, wp_ref, dh_ref, da_ref, dwq_ref, dwp_ref):
        @pl.when(pl.program_id(0) == 0)
        def _():
            dwq_ref[...] = jnp.zeros_like(dwq_ref)
            dwp_ref[...] = jnp.zeros_like(dwp_ref)

        dx, dwq = _rms_bwd(h_ref[...], wq_ref[...], dn_ref[...])
        dh = dy_ref[...] + dx
        dh_ref[...] = dh
        da, dwp = _rms_bwd(a_ref[...], wp_ref[...], dh)
        da_ref[...] = da.astype(BF16)
        dwq_ref[...] += jnp.sum(dwq, axis=0, keepdims=True)
        dwp_ref[...] += jnp.sum(dwp, axis=0, keepdims=True)

    return pl.pallas_call(
        kern, name="mid_bwd", grid=(tp // CHUNK,),
        in_specs=[_row_spec(d)] * 4 + [_vec_spec(d)] * 2,
        out_specs=[_row_spec(d), _row_spec(d), _vec_spec(d), _vec_spec(d)],
        out_shape=[jax.ShapeDtypeStruct((tp, d), F32), jax.ShapeDtypeStruct((tp, d), BF16),
                   jax.ShapeDtypeStruct((1, d), F32), jax.ShapeDtypeStruct((1, d), F32)],
        compiler_params=_cp(("arbitrary",)),
    )(dy, dn2, h1, a, w_pre, w_post)


def _pre_bwd(dh1, dn1, h0, w_pre):
    tp, d = h0.shape
    s = tp - CHUNK

    def kern(dh_ref, dn_ref, h_ref, w_ref, gx_ref, gm_ref, dw_ref):
        i = pl.program_id(0)
        dx, dwr = _rms_bwd(h_ref[...], w_ref[...], dn_ref[...])
        dh0 = dh_ref[...] + dx
        gx_ref[...] = dh0

        @pl.when(i == 0)
        def _():
            gm_ref[...] = dh0[N_PAD:, :]
            dw_ref[...] = jnp.zeros_like(dw_ref)

        dw_ref[...] += jnp.sum(dwr, axis=0, keepdims=True)

    return pl.pallas_call(
        kern, name="pre_bwd", grid=(tp // CHUNK,),
        in_specs=[_row_spec(d)] * 3 + [_vec_spec(d)],
        out_specs=[pl.BlockSpec((CHUNK, d), lambda i: (jnp.maximum(i - 1, 0), 0)),
                   pl.BlockSpec((N_META, d), lambda i: (0, 0)), _vec_spec(d)],
        out_shape=[jax.ShapeDtypeStruct((s, d), F32), jax.ShapeDtypeStruct((N_META, d), F32),
                   jax.ShapeDtypeStruct((1, d), F32)],
        compiler_params=_cp(("arbitrary",)),
    )(dh1, dn1, h0, w_pre)


def _ffn_cols(dff):
    return dff // 2 if dff % 256 == 0 else dff


def _ffn_fwd(gu, conv_w, conv_b):
    tp, two_dff = gu.shape
    dff = two_dff // 2
    tc = _ffn_cols(dff)
    nj = dff // tc
    r8 = CHUNK // 8

    def kern(g_ref, gp_ref, u_ref, w_ref, b_ref, o_ref):
        i = pl.program_id(1)
        prev = gp_ref[...] * (i > 0).astype(F32)
        ext = jnp.concatenate([prev, g_ref[...]], axis=0)
        w = w_ref[...]
        conv = (b_ref[...] + w[0:1] * pltpu.roll(ext, 2, 0)[8:] + w[1:2] * pltpu.roll(ext, 1, 0)[8:]
                + w[2:3] * ext[8:])
        o_ref[...] = (conv * _sigmoid(conv) * u_ref[...]).astype(BF16)

    return pl.pallas_call(
        kern, name="ffn_fwd", grid=(nj, tp // CHUNK),
        in_specs=[pl.BlockSpec((CHUNK, tc), lambda j, i: (i, j)),
                  pl.BlockSpec((8, tc), lambda j, i: (jnp.maximum(i * r8 - 1, 0), j)),
                  pl.BlockSpec((CHUNK, tc), lambda j, i: (i, j + nj)),
                  pl.BlockSpec((CONV_W, tc), lambda j, i: (0, j)),
                  pl.BlockSpec((1, tc), lambda j, i: (0, j))],
        out_specs=pl.BlockSpec((CHUNK, tc), lambda j, i: (i, j)),
        out_shape=jax.ShapeDtypeStruct((tp, dff), BF16),
        compiler_params=_cp(("parallel", "parallel")),
    )(gu, gu, gu, conv_w, conv_b)


def _ffn_bwd(gu, dact, conv_w, conv_b):
    tp, two_dff = gu.shape
    dff = two_dff // 2
    tc = _ffn_cols(dff)
    nj = dff // tc
    ni = tp // CHUNK
    r8 = CHUNK // 8

    def kern(g_ref, gp_ref, gn_ref, u_ref, un_ref, d_ref, dn_ref, w_ref, b_ref, dg_ref, du_ref, st_ref):
        i = pl.program_id(1)

        @pl.when(i == 0)
        def _():
            st_ref[...] = jnp.zeros_like(st_ref)

        first = (i > 0).astype(F32)
        last = (i < ni - 1).astype(F32)
        gate = g_ref[...]
        ext = jnp.concatenate([gp_ref[...] * first, gate, gn_ref[...]], axis=0)
        w = w_ref[...]
        r1 = pltpu.roll(ext, 1, 0)
        r2 = pltpu.roll(ext, 2, 0)
        conv = (b_ref[...] + w[0:1] * r2 + w[1:2] * r1 + w[2:3] * ext)[8:]
        up = jnp.concatenate([u_ref[...], un_ref[...]], axis=0)
        da = jnp.concatenate([d_ref[...], dn_ref[...] * last], axis=0)
        sg = _sigmoid(conv)
        dc = da * up * (sg * (1.0 + conv * (1.0 - sg)))
        du_ref[...] = (d_ref[...] * (conv * sg)[:CHUNK]).astype(BF16)
        n = CHUNK + 8
        dgate = w[2:3] * dc + w[1:2] * pltpu.roll(dc, n - 1, 0) + w[0:1] * pltpu.roll(dc, n - 2, 0)
        dg_ref[...] = dgate[:CHUNK].astype(BF16)
        dcm = dc[:CHUNK]
        s0 = jnp.sum(dcm * r2[8:8 + CHUNK], axis=0, keepdims=True)
        s1 = jnp.sum(dcm * r1[8:8 + CHUNK], axis=0, keepdims=True)
        s2 = jnp.sum(dcm * gate, axis=0, keepdims=True)
        s3 = jnp.sum(dcm, axis=0, keepdims=True)
        row = lax.broadcasted_iota(jnp.int32, (8, tc), 0)
        st_ref[...] += jnp.where(row == 0, s0, jnp.where(row == 1, s1, jnp.where(row == 2, s2,
                                 jnp.where(row == 3, s3, 0.0))))

    main = lambda off: pl.BlockSpec((CHUNK, tc), lambda j, i: (i, j + off))
    nxt = lambda off: pl.BlockSpec((8, tc), lambda j, i: (jnp.minimum((i + 1) * r8, ni * r8 - 1), j + off))
    return pl.pallas_call(
        kern, name="ffn_bwd", grid=(nj, ni),
        in_specs=[main(0), pl.BlockSpec((8, tc), lambda j, i: (jnp.maximum(i * r8 - 1, 0), j)), nxt(0),
                  main(nj), nxt(nj), main(0), nxt(0),
                  pl.BlockSpec((CONV_W, tc), lambda j, i: (0, j)), pl.BlockSpec((1, tc), lambda j, i: (0, j))],
        out_specs=[main(0), main(0), pl.BlockSpec((8, tc), lambda j, i: (0, j))],
        out_shape=[jax.ShapeDtypeStruct((tp, dff), BF16), jax.ShapeDtypeStruct((tp, dff), BF16),
                   jax.ShapeDtypeStruct((8, dff), F32)],
        compiler_params=_cp(("parallel", "arbitrary")),
    )(gu, gu, gu, gu, gu, dact, dact, conv_w, conv_b)


def _rot(x, cs, sn):
    x1, x2 = x[:, :128], x[:, 128:]
    return jnp.concatenate([x1 * cs - x2 * sn, x1 * sn + x2 * cs], axis=1)


def _rot_t(x, cs, sn):
    x1, x2 = x[:, :128], x[:, 128:]
    return jnp.concatenate([x1 * cs + x2 * sn, x2 * cs - x1 * sn], axis=1)


def _ret_tables(rh):
    lg = jnp.log(1.0 - 2.0 ** (-5.0 - jnp.arange(rh, dtype=F32)))
    idx = jnp.arange(CHUNK, dtype=F32)
    diff = idx[:, None] - idx[None, :]
    intra = jnp.where(diff[None] >= 0, jnp.exp(jnp.maximum(diff, 0.0)[None] * lg[:, None, None]), 0.0)
    qdec = jnp.exp((idx[None, :] + 1.0) * lg[:, None])[..., None]
    kdec = jnp.exp((CHUNK - 1.0 - idx[None, :]) * lg[:, None])[..., None]
    cdec = jnp.exp(CHUNK * lg)[:, None, None]
    return intra, qdec, kdec, cdec


def _ret_specs(rh, nc, rev):
    cc = (lambda c: nc - 1 - c) if rev else (lambda c: c)
    col = lambda sec: pl.BlockSpec((CHUNK, RET_DK), lambda h, c: (cc(c), sec * rh + h))
    tab = [pl.BlockSpec((CHUNK, 128), lambda h, c: (cc(c), 0))] * 2
    dec = [pl.BlockSpec((1, CHUNK, CHUNK), lambda h, c: (h, 0, 0)),
           pl.BlockSpec((1, CHUNK, 1), lambda h, c: (h, 0, 0)),
           pl.BlockSpec((1, CHUNK, 1), lambda h, c: (h, 0, 0)),
           pl.BlockSpec((1, 1, 1), lambda h, c: (h, 0, 0))]
    hw = pl.BlockSpec((1, RET_DK), lambda h, c: (0, h))
    hcol = pl.BlockSpec((CHUNK, RET_DK), lambda h, c: (cc(c), h))
    st = pl.BlockSpec((1, 1, RET_DK, RET_DK), lambda h, c: (h, cc(c), 0, 0))
    return col, tab, dec, hw, hcol, st


def _ret_fwd(proj, cos, sin, tables, gnw, rh):
    tp = proj.shape[0]
    nc = tp // CHUNK
    col, tab, dec, hw, hcol, st = _ret_specs(rh, nc, False)

    def kern(q_ref, k_ref, v_ref, g_ref, cos_ref, sin_ref, in_ref, qd_ref, kd_ref, cd_ref, w_ref,
             out_ref, ry_ref, st_ref, state):
        @pl.when(pl.program_id(1) == 0)
        def _():
            state[...] = jnp.zeros_like(state)

        cs, sn = cos_ref[...], sin_ref[...]
        q = (_rot(q_ref[...], cs, sn) * (RET_DK ** -0.5)).astype(BF16)
        kf = _rot(k_ref[...], cs, sn)
        k = kf.astype(BF16)
        v = v_ref[...].astype(BF16)
        s_old = state[...]
        s_b = s_old.astype(BF16)
        st_ref[0, 0] = s_b
        sc = _dot(q, k, NT) * in_ref[0]
        ry = _dot(sc.astype(BF16), v) + _dot(q, s_b) * qd_ref[0]
        state[...] = s_old * cd_ref[0] + _dot((kf * kd_ref[0]).astype(BF16), v, TN)
        ry_ref[...] = ry
        g = g_ref[...]
        out_ref[...] = (g * _sigmoid(g) * _rms_fwd(ry, w_ref[...])).astype(BF16)

    return pl.pallas_call(
        kern, name="ret_fwd", grid=(rh, nc),
        in_specs=[col(0), col(1), col(2), col(3)] + tab + dec + [hw],
        out_specs=[hcol, hcol, st],
        out_shape=[jax.ShapeDtypeStruct((tp, rh * RET_DK), BF16), jax.ShapeDtypeStruct((tp, rh * RET_DK), F32),
                   jax.ShapeDtypeStruct((rh, nc, RET_DK, RET_DK), BF16)],
        scratch_shapes=[pltpu.VMEM((RET_DK, RET_DK), F32)],
        compiler_params=_cp(("parallel", "arbitrary")),
    )(proj, proj, proj, proj, cos, sin, *tables, gnw)


def _ret_bwd(proj, dmix, ry_all, states, cos, sin, tables, gnw, rh):
    tp = proj.shape[0]
    nc = tp // CHUNK
    col, tab, dec, hw, hcol, st = _ret_specs(rh, nc, True)

    def kern(q_ref, k_ref, v_ref, g_ref, cos_ref, sin_ref, in_ref, qd_ref, kd_ref, cd_ref, w_ref,
             do_ref, ry_ref, st_ref, dq_ref, dk_ref, dv_ref, dg_ref, dw_ref, ds):
        @pl.when(pl.program_id(1) == 0)
        def _():
            ds[...] = jnp.zeros_like(ds)
            dw_ref[...] = jnp.zeros_like(dw_ref)

        cs, sn = cos_ref[...], sin_ref[...]
        qf = _rot(q_ref[...], cs, sn) * (RET_DK ** -0.5)
        q = qf.astype(BF16)
        kf = _rot(k_ref[...], cs, sn)
        k = kf.astype(BF16)
        vf = v_ref[...]
        v = vf.astype(BF16)
        g = g_ref[...]
        ry = ry_ref[...]
        w = w_ref[...]
        dout = do_ref[...]
        sg = _sigmoid(g)
        dhn = dout * (g * sg)
        dry, dwr = _rms_bwd(ry, w, dhn)
        dg_ref[...] = (dout * _rms_fwd(ry, w) * (sg * (1.0 + g * (1.0 - sg)))).astype(BF16)
        dw_ref[...] += jnp.sum(dwr, axis=0, keepdims=True)

        dmat = in_ref[0]
        qd, kd = qd_ref[0], kd_ref[0]
        dyb = dry.astype(BF16)
        p = (_dot(q, k, NT) * dmat).astype(BF16)
        dp = (_dot(dyb, v, NT) * dmat).astype(BF16)
        ady = (dry * qd).astype(BF16)
        ds_old = ds[...]
        ds_b = ds_old.astype(BF16)
        dq = _dot(dp, k) + _dot(ady, st_ref[0, 0], NT)
        dk = _dot(dp, q, TN) + _dot(v, ds_b, NT) * kd
        dv = _dot(p, dyb, TN) + _dot((kf * kd).astype(BF16), ds_b)
        ds[...] = ds_old * cd_ref[0] + _dot(q, ady, TN)
        dq_ref[...] = _rot_t(dq * (RET_DK ** -0.5), cs, sn).astype(BF16)
        dk_ref[...] = _rot_t(dk, cs, sn).astype(BF16)
        dv_ref[...] = dv.astype(BF16)

    rw = rh * RET_DK
    outs = pl.pallas_call(
        kern, name="ret_bwd", grid=(rh, nc),
        in_specs=[col(0), col(1), col(2), col(3)] + tab + dec + [hw, hcol, hcol, st],
        out_specs=[hcol, hcol, hcol, hcol, hw],
        out_shape=[jax.ShapeDtypeStruct((tp, rw), BF16)] * 4 + [jax.ShapeDtypeStruct((1, rw), F32)],
        scratch_shapes=[pltpu.VMEM((RET_DK, RET_DK), F32)],
        compiler_params=_cp(("parallel", "arbitrary")),
    )(proj, proj, proj, proj, cos, sin, *tables, gnw, dmix, ry_all, states)
    return outs


def _sb_block(q, k, i, kb, scale):
    z = _dot(q, k, NT) * scale
    qpos = i * CHUNK + lax.broadcasted_iota(jnp.int32, (CHUNK, CHUNK), 0)
    kpos = kb * CHUNK + lax.broadcasted_iota(jnp.int32, (CHUNK, CHUNK), 1)
    mask = (kpos < qpos) & (kpos >= N_PAD)
    t = jnp.log1p(jnp.exp(-jnp.abs(z)))
    lb = jnp.minimum(z, 0.0) - t
    lk = jnp.where(mask, -jnp.maximum(z, 0.0) - t, 0.0)
    return mask, lb, lk


def _tri_sum(x, tri):
    hi = x.astype(BF16)
    lo = (x - hi.astype(F32)).astype(BF16)
    return _dot(hi, tri) + _dot(lo, tri)


def _tri(strict_upper):
    r = lax.broadcasted_iota(jnp.int32, (CHUNK, CHUNK), 0)
    c = lax.broadcasted_iota(jnp.int32, (CHUNK, CHUNK), 1)
    return ((r > c) if strict_upper else (r < c)).astype(BF16)


def _sb_fwd(proj, sbw, sh, col0):
    tp = proj.shape[0]
    nq = tp // CHUNK
    assert nq <= 128
    scale = 1.0 / math.sqrt(SB_DH)
    hq = pl.BlockSpec((CHUNK, SB_DH), lambda h, i: (i, h))

    def kern(q_ref, k_ref, v_ref, w_ref, out_ref, sy_ref, ao_ref, a_run):
        i = pl.program_id(1)
        q = q_ref[...].astype(BF16)
        upper = _tri(True)
        lane = lax.broadcasted_iota(jnp.int32, (CHUNK, CHUNK), 1)
        a_run[...] = jnp.zeros_like(a_run)
        sy_ref[...] = jnp.zeros_like(sy_ref)
        ao_ref[...] = jnp.zeros_like(ao_ref)

        def body(jj, carry):
            kb = i - jj
            rows = pl.ds(pl.multiple_of(kb * CHUNK, CHUNK), CHUNK)
            k = k_ref[rows, :].astype(BF16)
            v = v_ref[rows, :].astype(BF16)
            mask, lb, lk = _sb_block(q, k, i, kb, scale)
            a = a_run[...]
            wgt = jnp.where(mask, jnp.exp(lb + a + _tri_sum(lk, upper)), 0.0)
            sy_ref[...] += _dot(wgt.astype(BF16), v)
            ao_ref[0] = jnp.where(lane == kb, a, ao_ref[0])
            a_run[...] = a + jnp.sum(lk, axis=1, keepdims=True)
            return carry

        lax.fori_loop(0, i + 1, body, 0)
        out_ref[...] = _rms_fwd(sy_ref[...], w_ref[...]).astype(BF16)

    kv = lambda sec: pl.BlockSpec((tp, SB_DH), lambda h, i: (0, col0 + sec * sh + h))
    return pl.pallas_call(
        kern, name="sb_fwd", grid=(sh, nq),
        in_specs=[pl.BlockSpec((CHUNK, SB_DH), lambda h, i: (i, col0 + h)), kv(1), kv(2),
                  pl.BlockSpec((1, SB_DH), lambda h, i: (0, h))],
        out_specs=[hq, hq, pl.BlockSpec((1, CHUNK, 128), lambda h, i: (h, i, 0))],
        out_shape=[jax.ShapeDtypeStruct((tp, sh * SB_DH), BF16), jax.ShapeDtypeStruct((tp, sh * SB_DH), F32),
                   jax.ShapeDtypeStruct((sh, tp, 128), F32)],
        scratch_shapes=[pltpu.VMEM((CHUNK, CHUNK), F32)],
        compiler_params=_cp(("parallel", "arbitrary")),
    )(proj, proj, proj, sbw)


def _sb_bwd(proj, dmix, sy_all, aoff, sbw, sh, col0, dcol0):
    tp = proj.shape[0]
    nq = tp // CHUNK
    scale = 1.0 / math.sqrt(SB_DH)
    hq = pl.BlockSpec((CHUNK, SB_DH), lambda h, i: (i, h))

    def kern(q_ref, k_ref, v_ref, w_ref, do_ref, sy_ref, ao_ref, dq_ref, dk_ref, dv_ref, dw_ref,
             dk_acc, dv_acc, dq_acc, e_run):
        i = pl.program_id(1)

        @pl.when(i == 0)
        def _():
            dk_acc[...] = jnp.zeros_like(dk_acc)
            dv_acc[...] = jnp.zeros_like(dv_acc)
            dw_ref[...] = jnp.zeros_like(dw_ref)

        q = q_ref[...].astype(BF16)
        dsy, dwr = _rms_bwd(sy_ref[...], w_ref[...], do_ref[...])
        dw_ref[...] += jnp.sum(dwr, axis=0, keepdims=True)
        dsy_b = dsy.astype(BF16)
        atile = ao_ref[0]
        upper = _tri(True)
        lower = _tri(False)
        lane = lax.broadcasted_iota(jnp.int32, (CHUNK, CHUNK), 1)

        e_run[...] = jnp.zeros_like(e_run)
        dq_acc[...] = jnp.zeros_like(dq_acc)

        def body(kb, carry):
            rows = pl.ds(pl.multiple_of(kb * CHUNK, CHUNK), CHUNK)
            k = k_ref[rows, :].astype(BF16)
            v = v_ref[rows, :].astype(BF16)
            mask, lb, lk = _sb_block(q, k, i, kb, scale)
            a = jnp.sum(jnp.where(lane == kb, atile, 0.0), axis=1, keepdims=True)
            wgt = jnp.where(mask, jnp.exp(lb + a + _tri_sum(lk, upper)), 0.0)
            wb = wgt.astype(BF16)
            e = wgt * _dot(dsy_b, v, NT)
            dv_acc[rows, :] += _dot(wb, dsy_b, TN)
            sig = jnp.exp(lb)
            e_prev = e_run[...]
            e_all = e_prev + _tri_sum(e, lower)
            dz = (jnp.where(mask, e * (1.0 - sig) - e_all * sig, 0.0) * scale).astype(BF16)
            dk_acc[rows, :] += _dot(dz, q, TN)
            dq_acc[...] += _dot(dz, k)
            e_run[...] = e_prev + jnp.sum(e, axis=1, keepdims=True)
            return carry

        lax.fori_loop(0, i + 1, body, 0)
        dq_ref[...] = dq_acc[...].astype(BF16)

        @pl.when(i == nq - 1)
        def _():
            dk_ref[...] = dk_acc[...].astype(BF16)
            dv_ref[...] = dv_acc[...].astype(BF16)

    kv = lambda sec: pl.BlockSpec((tp, SB_DH), lambda h, i: (0, col0 + sec * sh + h))
    hfull = pl.BlockSpec((tp, SB_DH), lambda h, i: (0, h))
    sw = sh * SB_DH
    return pl.pallas_call(
        kern, name="sb_bwd", grid=(sh, nq),
        in_specs=[pl.BlockSpec((CHUNK, SB_DH), lambda h, i: (i, col0 + h)), kv(1), kv(2),
                  pl.BlockSpec((1, SB_DH), lambda h, i: (0, h)),
                  pl.BlockSpec((CHUNK, SB_DH), lambda h, i: (i, dcol0 + h)), hq,
                  pl.BlockSpec((1, CHUNK, 128), lambda h, i: (h, i, 0))],
        out_specs=[hq, hfull, hfull, pl.BlockSpec((1, SB_DH), lambda h, i: (0, h))],
        out_shape=[jax.ShapeDtypeStruct((tp, sw), BF16)] * 3 + [jax.ShapeDtypeStruct((1, sw), F32)],
        scratch_shapes=[pltpu.VMEM((tp, SB_DH), F32), pltpu.VMEM((tp, SB_DH), F32),
                        pltpu.VMEM((CHUNK, SB_DH), F32), pltpu.VMEM((CHUNK, CHUNK), F32)],
        compiler_params=_cp(("parallel", "arbitrary")),
    )(proj, proj, proj, sbw, dmix, sy_all, aoff)


def _local_step(x, tgt, meta, w_in, w_out, w_up, w_down, conv_w, conv_b,
                pre1_w, gn_w, sb_w, post1_w, pre2_w, post2_w):
    s, d = x.shape
    tp = s + CHUNK
    rh, sh = d // 512, d // 256
    rw = rh * RET_DK
    h0 = jnp.concatenate([jnp.zeros((N_PAD, d), F32), meta, x], axis=0)
    pos = jnp.arange(tp, dtype=F32) - N_PAD
    inv = ROPE_BASE ** (-jnp.arange(128, dtype=F32) / 128)
    ang = pos[:, None] * inv[None, :]
    cos, sin = jnp.cos(ang), jnp.sin(ang)
    tables = _ret_tables(rh)
    sb_col0 = 4 * rw // SB_DH

    n1 = _prenorm(h0, pre1_w)
    proj = _matmul(n1, w_in, "nn", F32, "mm_proj")
    ret_out, ry, states = _ret_fwd(proj, cos, sin, tables, gn_w, rh)
    sb_out, sy, aoff = _sb_fwd(proj, sb_w, sh, sb_col0)
    mixed = jnp.concatenate([ret_out, sb_out], axis=1)
    a = _matmul(mixed, w_out, "nn", F32, "mm_out")
    h1, n2 = _mid_fwd(h0, a, post1_w, pre2_w)
    gu = _matmul(n2, w_up, "nn", F32, "mm_up")
    act = _ffn_fwd(gu, conv_w, conv_b)
    f = _matmul(act, w_down, "nn", F32, "mm_down")

    dy, d_f, loss, dw_post2 = _loss_bwd(h1, f, tgt, post2_w)
    d_act = _matmul(d_f, w_down, "nt", F32, "mm_dact")
    g_down = _matmul(act, d_f, "tn", BF16, "mm_gdown")
    d_gate, d_up, ffn_stats = _ffn_bwd(gu, d_act, conv_w, conv_b)
    d_gu = jnp.concatenate([d_gate, d_up], axis=1)
    d_n2 = _matmul(d_gu, w_up, "nt", F32, "mm_dn2")
    g_up = _matmul(n2, d_gu, "tn", BF16, "mm_gup")
    dh1, d_a, dw_pre2, dw_post1 = _mid_bwd(dy, d_n2, h1, a, pre2_w, post1_w)
    d_mix = _matmul(d_a, w_out, "nt", F32, "mm_dmix")
    g_out = _matmul(mixed, d_a, "tn", BF16, "mm_gout")
    d_rq, d_rk, d_rv, d_rg, dw_gn = _ret_bwd(proj, d_mix, ry, states, cos, sin, tables, gn_w, rh)
    d_sq, d_sk, d_sv, dw_sb = _sb_bwd(proj, d_mix, sy, aoff, sb_w, sh, sb_col0, rw // SB_DH)
    d_proj = jnp.concatenate([d_rq, d_rk, d_rv, d_rg, d_sq, d_sk, d_sv], axis=1)
    d_n1 = _matmul(d_proj, w_in, "nt", F32, "mm_dn1")
    g_in = _matmul(n1, d_proj, "tn", BF16, "mm_gin")
    grad_x, g_meta, dw_pre1 = _pre_bwd(dh1, d_n1, h0, pre1_w)

    small = dict(loss=loss, meta=g_meta, pre1=dw_pre1, gn=dw_gn, sb=dw_sb, post1=dw_post1, pre2=dw_pre2,
                 conv_w=ffn_stats[0:3], conv_b=ffn_stats[3:4], post2=dw_post2)
    return grad_x, dict(w_in=g_in, w_out=g_out, w_up=g_up, w_down=g_down), small


def _coords():
    return lax.axis_index("x"), lax.axis_index("y"), lax.axis_index("c")


def _other_chips(x, y):
    return [(1 - x, y), (x, 1 - y), (1 - x, 1 - y)]


ANY = pl.BlockSpec(memory_space=pl.ANY)
VM = pl.BlockSpec(memory_space=pltpu.VMEM)


def _gather4_small(v):
    r = v.shape[0]

    def kern(v_ref, o_ref, send, recv):
        x, y, c = _coords()
        o_ref[2 * x + y] = v_ref[...]
        cps = [pltpu.make_async_remote_copy(v_ref, o_ref.at[2 * x + y], send.at[j], recv.at[j],
                                            device_id=(px, py, c), device_id_type=MESH)
               for j, (px, py) in enumerate(_other_chips(x, y))]
        for cp in cps:
            cp.start()
        for cp in cps:
            cp.wait()

    return pl.pallas_call(
        kern, name="gather_small", in_specs=[VM], out_specs=VM,
        out_shape=jax.ShapeDtypeStruct((4, r, 128), F32),
        scratch_shapes=[pltpu.SemaphoreType.DMA((3,)), pltpu.SemaphoreType.DMA((3,))],
    )(v)


def _allreduce8_small(v):
    r = v.shape[0]
    flips = [(fx, fy, fc) for fx in (0, 1) for fy in (0, 1) for fc in (0, 1)][1:]

    def kern(v_ref, o_ref, buf, send, recv):
        x, y, c = _coords()
        me = 4 * x + 2 * y + c
        buf[me] = v_ref[...]
        cps = [pltpu.make_async_remote_copy(v_ref, buf.at[me], send.at[j], recv.at[j],
                                            device_id=(x ^ fx, y ^ fy, c ^ fc), device_id_type=MESH)
               for j, (fx, fy, fc) in enumerate(flips)]
        for cp in cps:
            cp.start()
        for cp in cps:
            cp.wait()
        tot = buf[0]
        for j in range(1, 8):
            tot = tot + buf[j]
        o_ref[...] = tot

    return pl.pallas_call(
        kern, name="allreduce_small", in_specs=[VM], out_specs=VM,
        out_shape=jax.ShapeDtypeStruct((r, 128), F32),
        scratch_shapes=[pltpu.VMEM((8, r, 128), F32), pltpu.SemaphoreType.DMA((7,)),
                        pltpu.SemaphoreType.DMA((7,))],
    )(v)


def _piece(ref, col_sharded, rows, cols, s, hc):
    half = rows // 2
    if col_sharded:
        return ref.at[pl.ds(pl.multiple_of(hc * half, 16), half), pl.ds(pl.multiple_of(s * cols, 128), cols)]
    return ref.at[pl.ds(pl.multiple_of(s * rows + hc * half, 16), half), :]


def _gather_weight(shard, col_sharded, name):
    rows, cols = shard.shape
    full_shape = (rows, 4 * cols) if col_sharded else (4 * rows, cols)
    half = rows // 2

    def kern(w_ref, o_ref, send, recv, lsem):
        x, y, c = _coords()
        s = 2 * x + y
        sib = (x, y, 1 - c)
        chips = _other_chips(x, y)
        pc = functools.partial(_piece, o_ref, col_sharded, rows, cols)
        lo = pltpu.make_async_copy(w_ref.at[pl.ds(0, half), :], pc(s, 0), lsem.at[0])
        hi = pltpu.make_async_copy(w_ref.at[pl.ds(half, half), :], pc(s, 1), lsem.at[1])
        lo.start()
        hi.start()
        mine = w_ref.at[pl.ds(pl.multiple_of(c * half, 16), half), :]
        first = [pltpu.make_async_remote_copy(mine, pc(s, c), send.at[j], recv.at[j],
                                              device_id=(px, py, c), device_id_type=MESH)
                 for j, (px, py) in enumerate(chips)]
        for cp in first:
            cp.start()
        passed = [pltpu.make_async_remote_copy(pc(2 * px + py, c), pc(2 * px + py, c), send.at[3 + j], recv.at[3 + j],
                                               device_id=sib, device_id_type=MESH)
                  for j, (px, py) in enumerate(chips)]
        for j in range(3):
            first[j].wait_recv()
            passed[j].start()
        for j, (px, py) in enumerate(chips):
            pltpu.make_async_remote_copy(pc(2 * px + py, 1 - c), pc(2 * px + py, 1 - c), send.at[3 + j],
                                         recv.at[3 + j], device_id=sib, device_id_type=MESH).wait_recv()
        for cp in first + passed:
            cp.wait_send()
        lo.wait()
        hi.wait()

    return pl.pallas_call(
        kern, name=name, in_specs=[ANY], out_specs=ANY,
        out_shape=jax.ShapeDtypeStruct(full_shape, BF16),
        scratch_shapes=[pltpu.SemaphoreType.DMA((6,)), pltpu.SemaphoreType.DMA((6,)),
                        pltpu.SemaphoreType.DMA((2,))],
    )(shard)


def _rs_pair(g, col_sharded, rows, cols, name):
    half = rows // 2

    def kern(g_ref, o_ref, send, recv):
        x, y, c = _coords()
        cps = [pltpu.make_async_remote_copy(_piece(g_ref, col_sharded, rows, cols, s, 1 - c), o_ref.at[s],
                                            send.at[s], recv.at[s], device_id=(x, y, 1 - c), device_id_type=MESH)
               for s in range(4)]
        for cp in cps:
            cp.start()
        for cp in cps:
            cp.wait()

    return pl.pallas_call(
        kern, name=name, in_specs=[ANY], out_specs=ANY,
        out_shape=jax.ShapeDtypeStruct((4, half, cols), BF16),
        scratch_shapes=[pltpu.SemaphoreType.DMA((4,)), pltpu.SemaphoreType.DMA((4,))],
    )(g)


def _half_tiles(half, cols):
    tr = max(d for d in _divisors(half, 16) if d * cols * 4 <= 4 * 2 ** 20)
    return tr, half // tr


def _half_spec(col_sharded, tr, nt, cols, which):
    if col_sharded:
        return pl.BlockSpec((tr, cols), lambda s, t, ids: (ids[0] * nt + t, which(s, ids)))
    return pl.BlockSpec((tr, cols), lambda s, t, ids: ((2 * which(s, ids) + ids[0]) * nt + t, 0))


def _rs_add(g, r1, ids, col_sharded, rows, cols, name):
    half = rows // 2
    tr, nt = _half_tiles(half, cols)

    def kern(ids_ref, g_ref, r_ref, o_ref):
        o_ref[0] = (g_ref[...].astype(F32) + r_ref[0].astype(F32)).astype(BF16)

    slab = pl.BlockSpec((1, tr, cols), lambda s, t, ids: (s, t, 0))
    return pl.pallas_call(
        kern, name=name,
        grid_spec=pltpu.PrefetchScalarGridSpec(
            num_scalar_prefetch=1, grid=(4, nt),
            in_specs=[_half_spec(col_sharded, tr, nt, cols, lambda s, ids: s), slab], out_specs=slab),
        out_shape=jax.ShapeDtypeStruct((4, half, cols), BF16),
        compiler_params=_cp(("parallel", "parallel")),
    )(ids, g, r1)


def _rs_scatter(p, name):
    _, half, cols = p.shape

    def kern(p_ref, o_ref, send, recv):
        x, y, c = _coords()
        cps = [pltpu.make_async_remote_copy(p_ref.at[2 * px + py], o_ref.at[j], send.at[j], recv.at[j],
                                            device_id=(px, py, c), device_id_type=MESH)
               for j, (px, py) in enumerate(_other_chips(x, y))]
        for cp in cps:
            cp.start()
        for cp in cps:
            cp.wait()

    return pl.pallas_call(
        kern, name=name, in_specs=[ANY], out_specs=ANY,
        out_shape=jax.ShapeDtypeStruct((3, half, cols), BF16),
        scratch_shapes=[pltpu.SemaphoreType.DMA((3,)), pltpu.SemaphoreType.DMA((3,))],
    )(p)


def _rs_total(g, r1, r2, ids, col_sharded, rows, cols, name):
    half = rows // 2
    tr, nt = _half_tiles(half, cols)

    def kern(ids_ref, g_ref, r1_ref, r2_ref, o_ref):
        tot = g_ref[...].astype(F32) + r1_ref[0].astype(F32)
        for j in range(3):
            tot = tot + r2_ref[j].astype(F32)
        o_ref[...] = tot

    return pl.pallas_call(
        kern, name=name,
        grid_spec=pltpu.PrefetchScalarGridSpec(
            num_scalar_prefetch=1, grid=(1, nt),
            in_specs=[_half_spec(col_sharded, tr, nt, cols, lambda s, ids: ids[1]),
                      pl.BlockSpec((1, tr, cols), lambda s, t, ids: (ids[1], t, 0)),
                      pl.BlockSpec((3, tr, cols), lambda s, t, ids: (0, t, 0))],
            out_specs=pl.BlockSpec((tr, cols), lambda s, t, ids: (t, 0))),
        out_shape=jax.ShapeDtypeStruct((half, cols), F32),
        compiler_params=_cp(("parallel", "parallel")),
    )(ids, g, r1, r2)


def _rs_exchange(t, name):
    half, cols = t.shape

    def kern(t_ref, o_ref, send, recv, lsem):
        x, y, c = _coords()
        mine = o_ref.at[pl.ds(pl.multiple_of(c * half, 8), half), :]
        loc = pltpu.make_async_copy(t_ref, mine, lsem)
        loc.start()
        cp = pltpu.make_async_remote_copy(t_ref, mine, send, recv, device_id=(x, y, 1 - c), device_id_type=MESH)
        cp.start()
        cp.wait()
        loc.wait()

    return pl.pallas_call(
        kern, name=name, in_specs=[ANY], out_specs=ANY,
        out_shape=jax.ShapeDtypeStruct((2 * half, cols), F32),
        scratch_shapes=[pltpu.SemaphoreType.DMA, pltpu.SemaphoreType.DMA, pltpu.SemaphoreType.DMA],
    )(t)


def _reduce_scatter(g, ids, col_sharded, rows, cols, tag):
    r1 = _rs_pair(g, col_sharded, rows, cols, "rs_pair_" + tag)
    p = _rs_add(g, r1, ids, col_sharded, rows, cols, "rs_add_" + tag)
    r2 = _rs_scatter(p, "rs_scatter_" + tag)
    t = _rs_total(g, r1, r2, ids, col_sharded, rows, cols, "rs_total_" + tag)
    return _rs_exchange(t, "rs_exchange_" + tag)


def _adamw_vals(w, g, m, v):
    m = ADAM_B1 * m + (1.0 - ADAM_B1) * g
    v = ADAM_B2 * v + (1.0 - ADAM_B2) * (g * g)
    m_hat = m / (1.0 - ADAM_B1 ** ADAM_STEP)
    v_hat = v / (1.0 - ADAM_B2 ** ADAM_STEP)
    delta = -ADAM_LR * (m_hat / (jnp.sqrt(v_hat) + ADAM_EPS) + ADAM_WD * w)
    return delta, m, v


def _adamw(w, g, m, v, name):
    rows, cols = w.shape
    tr = max(d for d in _divisors(rows, 8) if d * cols * 4 <= 2 * 2 ** 20)

    def kern(w_ref, g_ref, m_ref, v_ref, d_ref, mo_ref, vo_ref):
        d_ref[...], mo_ref[...], vo_ref[...] = _adamw_vals(w_ref[...], g_ref[...], m_ref[...], v_ref[...])

    spec = pl.BlockSpec((tr, cols), lambda i: (i, 0))
    return pl.pallas_call(
        kern, name=name, grid=(rows // tr,), in_specs=[spec] * 4, out_specs=[spec] * 3,
        out_shape=[jax.ShapeDtypeStruct((rows, cols), F32)] * 3,
        compiler_params=_cp(("parallel",)),
    )(w, g, m, v)


def _pack(arrs):
    flat = []
    for a in arrs:
        a = a.reshape(-1)
        flat.append(jnp.pad(a, (0, (-a.shape[0]) % 1024)))
    return jnp.concatenate(flat).reshape(-1, 128)


def _unpack(slab, shapes):
    out, off = [], 0
    flat = slab.reshape(slab.shape[:-2] + (-1,))
    for shp in shapes:
        n = math.prod(shp)
        out.append(flat[..., off:off + n].reshape(slab.shape[:-2] + tuple(shp)))
        off += n + (-n) % 1024
    return out


BIG = ("w_in", "w_out", "w_up", "w_down")
COL_SHARDED = dict(w_in=True, w_out=False, w_up=True, w_down=False)
SMALL = ("meta_tokens", "attn_pre_norm_w", "ret_gn_w", "sb_norm_w", "attn_post_norm_w", "ffn_pre_norm_w",
         "conv_w", "conv_b", "ffn_post_norm_w")
ORDER = ("meta_tokens", "attn_pre_norm_w", "w_in", "ret_gn_w", "sb_norm_w", "w_out", "attn_post_norm_w",
         "ffn_pre_norm_w", "w_up", "conv_w", "conv_b", "w_down", "ffn_post_norm_w")


def kernel(x, meta_tokens, attn_pre_norm_w, w_in, ret_gn_w, sb_norm_w, w_out, attn_post_norm_w, ffn_pre_norm_w, w_up, conv_w, conv_b, w_down, ffn_post_norm_w, loss_target, m_meta_tokens, m_attn_pre_norm_w, m_w_in, m_ret_gn_w, m_sb_norm_w, m_w_out, m_attn_post_norm_w, m_ffn_pre_norm_w, m_w_up, m_conv_w, m_conv_b, m_w_down, m_ffn_post_norm_w, v_meta_tokens, v_attn_pre_norm_w, v_w_in, v_ret_gn_w, v_sb_norm_w, v_w_out, v_attn_post_norm_w, v_ffn_pre_norm_w, v_w_up, v_conv_w, v_conv_b, v_w_down, v_ffn_post_norm_w):
    args = dict(locals())
    w = {n: args[n] for n in ORDER}
    m = {n: args["m_" + n] for n in ORDER}
    v = {n: args["v_" + n] for n in ORDER}
    xi, yi, ci = _coords()
    shard_id = 2 * xi + yi
    ids = jnp.stack([ci, shard_id]).astype(jnp.int32)
    d = x.shape[-1]

    mshape, cshape = w["meta_tokens"].shape, w["conv_w"][0].shape
    got = _unpack(_gather4_small(_pack([w["meta_tokens"], w["conv_w"][0]])), [mshape, cshape])
    meta_full = jnp.moveaxis(got[0], 0, 1).reshape(N_META, d)
    conv_w_full = jnp.moveaxis(got[1], 0, 1).reshape(CONV_W, -1)

    shards = {n: w[n][0].astype(BF16) for n in BIG}
    full = {n: _gather_weight(shards[n], COL_SHARDED[n], "gather_" + n) for n in BIG}

    grad_x, g_big, g_small = _local_step(
        x[0], loss_target[0], meta_full, full["w_in"], full["w_out"], full["w_up"], full["w_down"],
        conv_w_full, w["conv_b"], w["attn_pre_norm_w"], w["ret_gn_w"], w["sb_norm_w"],
        w["attn_post_norm_w"], w["ffn_pre_norm_w"], w["ffn_post_norm_w"])

    names = ("loss", "meta", "pre1", "gn", "sb", "post1", "pre2", "conv_w", "conv_b", "post2")
    tot = _unpack(_allreduce8_small(_pack([g_small[n] for n in names])), [g_small[n].shape for n in names])
    tot = dict(zip(names, tot))
    loss = tot["loss"][0, 0]
    mcols, ccols = mshape[1], cshape[1]
    grads = {
        "meta_tokens": lax.dynamic_slice_in_dim(tot["meta"], shard_id * mcols, mcols, axis=1),
        "attn_pre_norm_w": tot["pre1"], "ret_gn_w": tot["gn"], "sb_norm_w": tot["sb"],
        "attn_post_norm_w": tot["post1"], "ffn_pre_norm_w": tot["pre2"],
        "conv_w": lax.dynamic_slice_in_dim(tot["conv_w"], shard_id * ccols, ccols, axis=1)[None],
        "conv_b": tot["conv_b"], "ffn_post_norm_w": tot["post2"],
    }

    for n in BIG:
        rows, cols = w[n][0].shape
        grads[n] = _reduce_scatter(g_big[n], ids, COL_SHARDED[n], rows, cols, n)[None]

    delta, new_m, new_v = {}, {}, {}
    for n in BIG:
        dl, mo, vo = _adamw(w[n][0], grads[n][0], m[n][0], v[n][0], "adamw_" + n)
        delta[n], new_m[n], new_v[n] = dl[None], mo[None], vo[None]
    shapes = [w[n].shape for n in SMALL]
    packed = [_pack([src[n] for n in SMALL]) for src in (w, grads, m, v)]
    outs = _adamw(*packed, "adamw_small")
    for dst, slab in zip((delta, new_m, new_v), outs):
        for n, a in zip(SMALL, _unpack(slab, shapes)):
            dst[n] = a

    return (loss, grad_x[None], *[grads[n] for n in ORDER], *[delta[n] for n in ORDER],
            *[new_m[n] for n in ORDER], *[new_v[n] for n in ORDER])
```

```python
import functools
import math

import jax
import jax.numpy as jnp
from jax import lax
from jax.experimental import pallas as pl
from jax.experimental.pallas import tpu as pltpu

F32 = jnp.float32
BF16 = jnp.bfloat16
MESH = pl.DeviceIdType.MESH

EPS = 1e-6
ROPE_BASE = 10000.0
N_META = 16
CHUNK = 128
N_PAD = CHUNK - N_META
RET_DK = 256
SB_DH = 128
CONV_W = 3

ADAM_LR = 0.001
ADAM_B1 = 0.9
ADAM_B2 = 0.999
ADAM_EPS = 1e-08
ADAM_WD = 0.01
ADAM_STEP = 10

V7X_VMEM_BYTES = 64 * 2 ** 20
VMEM_LIMIT = V7X_VMEM_BYTES - 8 * 2 ** 20
MM_BUDGET = 36 * 2 ** 20
V7X_BF16_FLOPS = 0.9e15
V7X_HBM_BPS = 3.2e12
GRID_STEP_S = 0.35e-6

NN = (((1,), (0,)), ((), ()))
NT = (((1,), (1,)), ((), ()))
TN = (((0,), (0,)), ((), ()))


def _cp(sem, vmem=VMEM_LIMIT):
    return pltpu.CompilerParams(dimension_semantics=sem, vmem_limit_bytes=vmem)


def _dot(a, b, dims=NN):
    return lax.dot_general(a, b, dims, preferred_element_type=F32)


def _sigmoid(x):
    return 1.0 / (1.0 + jnp.exp(-x))


def _divisors(n, align):
    return [d for d in range(align, n + 1, align) if n % d == 0]


def _mm_tiles(mode, m, n, k, out_bytes):
    best = None
    tms = [d for d in _divisors(m, 128 if mode == "tn" else 16) if d >= 128]
    tns = [d for d in _divisors(n, 128) if d >= 128]
    tks = [d for d in _divisors(k, 128) if d >= 128]
    flops = 2.0 * m * n * k
    for tm in tms:
        for tn in tns:
            if tm * tn > 2112 * 1024:
                continue
            for tk in tks:
                nk = k // tk
                foot = 4 * (tm * tk + tk * tn) + 2 * tm * tn * out_bytes
                if nk > 1:
                    foot += 4 * tm * tn
                if foot > MM_BUDGET:
                    continue
                steps = (m // tm) * (n // tn) * nk
                for swap in (False, True):
                    if nk > 1:
                        traffic = 2.0 * (n // tn) * m * k + 2.0 * (m // tm) * n * k
                    elif swap:
                        traffic = 2.0 * n * k + 2.0 * (n // tn) * m * k
                    else:
                        traffic = 2.0 * m * k + 2.0 * (m // tm) * n * k
                    traffic += out_bytes * m * n
                    t = max(flops / V7X_BF16_FLOPS, traffic / V7X_HBM_BPS) + steps * GRID_STEP_S
                    if best is None or t < best[0]:
                        best = (t, tm, tn, tk, swap)
    assert best is not None, (mode, m, n, k)
    return best[1:]


def _matmul(a, b, mode, out_dtype, name):
    if mode == "nn":
        (m, k), (k2, n) = a.shape, b.shape
    elif mode == "nt":
        (m, k), (n, k2) = a.shape, b.shape
    else:
        (k, m), (k2, n) = a.shape, b.shape
    assert k == k2 and a.dtype == BF16 and b.dtype == BF16
    tm, tn, tk, swap = _mm_tiles(mode, m, n, k, jnp.dtype(out_dtype).itemsize)
    nk = k // tk
    dims = {"nn": NN, "nt": NT, "tn": TN}[mode]

    def ij(g0, g1):
        return (g1, g0) if swap else (g0, g1)

    if mode == "tn":
        a_spec = pl.BlockSpec((tk, tm), lambda g0, g1, kk: (kk, ij(g0, g1)[0]))
    else:
        a_spec = pl.BlockSpec((tm, tk), lambda g0, g1, kk: (ij(g0, g1)[0], kk))
    if mode == "nt":
        b_spec = pl.BlockSpec((tn, tk), lambda g0, g1, kk: (ij(g0, g1)[1], kk))
    else:
        b_spec = pl.BlockSpec((tk, tn), lambda g0, g1, kk: (kk, ij(g0, g1)[1]))
    o_spec = pl.BlockSpec((tm, tn), lambda g0, g1, kk: ij(g0, g1))

    def kern(a_ref, b_ref, o_ref, *acc):
        prod = _dot(a_ref[...], b_ref[...], dims)
        if nk == 1:
            o_ref[...] = prod.astype(out_dtype)
        else:
            kk = pl.program_id(2)

            @pl.when(kk == 0)
            def _():
                acc[0][...] = prod

            @pl.when(kk > 0)
            def _():
                acc[0][...] += prod

            @pl.when(kk == nk - 1)
            def _():
                o_ref[...] = acc[0][...].astype(out_dtype)

    grid = (n // tn, m // tm, nk) if swap else (m // tm, n // tn, nk)
    return pl.pallas_call(
        kern, name=name, grid=grid, in_specs=[a_spec, b_spec], out_specs=o_spec,
        out_shape=jax.ShapeDtypeStruct((m, n), out_dtype),
        scratch_shapes=[pltpu.VMEM((tm, tn), F32)] if nk > 1 else [],
        compiler_params=_cp(("parallel", "parallel", "arbitrary")),
    )(a, b)


def _rms_fwd(x, w):
    r = lax.rsqrt(jnp.mean(x * x, axis=-1, keepdims=True) + EPS)
    return x * r * w


def _rms_bwd(x, w, g):
    r = lax.rsqrt(jnp.mean(x * x, axis=-1, keepdims=True) + EPS)
    gw = g * w
    dx = r * gw - x * (r * r * r * jnp.mean(gw * x, axis=-1, keepdims=True))
    return dx, g * (x * r)


def _row_spec(d):
    return pl.BlockSpec((CHUNK, d), lambda i: (i, 0))


def _vec_spec(d):
    return pl.BlockSpec((1, d), lambda i: (0, 0))


def _prenorm(h0, w):
    tp, d = h0.shape

    def kern(h_ref, w_ref, o_ref):
        o_ref[...] = _rms_fwd(h_ref[...], w_ref[...]).astype(BF16)

    return pl.pallas_call(
        kern, name="prenorm1", grid=(tp // CHUNK,),
        in_specs=[_row_spec(d), _vec_spec(d)], out_specs=_row_spec(d),
        out_shape=jax.ShapeDtypeStruct((tp, d), BF16),
        compiler_params=_cp(("parallel",)),
    )(h0, w)


def _mid_fwd(h0, a, w_post, w_pre):
    tp, d = h0.shape

    def kern(h_ref, a_ref, wp_ref, wq_ref, h1_ref, n2_ref):
        h1 = h_ref[...] + _rms_fwd(a_ref[...], wp_ref[...])
        h1_ref[...] = h1
        n2_ref[...] = _rms_fwd(h1, wq_ref[...]).astype(BF16)

    return pl.pallas_call(
        kern, name="mid_fwd", grid=(tp // CHUNK,),
        in_specs=[_row_spec(d), _row_spec(d), _vec_spec(d), _vec_spec(d)],
        out_specs=[_row_spec(d), _row_spec(d)],
        out_shape=[jax.ShapeDtypeStruct((tp, d), F32), jax.ShapeDtypeStruct((tp, d), BF16)],
        compiler_params=_cp(("parallel",)),
    )(h0, a, w_post, w_pre)


def _loss_bwd(h1, f, tgt, w_post):
    tp, d = h1.shape

    def kern(h_ref, f_ref, t_ref, w_ref, dy_ref, df_ref, loss_ref, dw_ref):
        i = pl.program_id(0)

        @pl.when(i == 0)
        def _():
            dy_ref[...] = jnp.zeros_like(dy_ref)
            df_ref[...] = jnp.zeros_like(df_ref)
            loss_ref[...] = jnp.zeros_like(loss_ref)
            dw_ref[...] = jnp.zeros_like(dw_ref)

        @pl.when(i > 0)
        def _():
            fv = f_ref[...]
            w = w_ref[...]
            err = h_ref[...] + _rms_fwd(fv, w) - t_ref[...]
            loss_ref[...] += 0.5 * jnp.sum(jnp.mean(err * err, axis=-1, keepdims=True))
            dy = err * (1.0 / d)
            dy_ref[...] = dy
            dfv, dwr = _rms_bwd(fv, w, dy)
            df_ref[...] = dfv.astype(BF16)
            dw_ref[...] += jnp.sum(dwr, axis=0, keepdims=True)

    return pl.pallas_call(
        kern, name="loss_bwd", grid=(tp // CHUNK,),
        in_specs=[_row_spec(d), _row_spec(d),
                  pl.BlockSpec((CHUNK, d), lambda i: (jnp.maximum(i - 1, 0), 0)), _vec_spec(d)],
        out_specs=[_row_spec(d), _row_spec(d), pl.BlockSpec((1, 128), lambda i: (0, 0)), _vec_spec(d)],
        out_shape=[jax.ShapeDtypeStruct((tp, d), F32), jax.ShapeDtypeStruct((tp, d), BF16),
                   jax.ShapeDtypeStruct((1, 128), F32), jax.ShapeDtypeStruct((1, d), F32)],
        compiler_params=_cp(("arbitrary",)),
    )(h1, f, tgt, w_post)


def _mid_bwd(dy, dn2, h1, a, w_pre, w_post):
    tp, d = h1.shape

    def kern(dy_ref, dn_ref, h_ref, a_ref, wq_ref, wp_ref, dh_ref, da_ref, dwq_ref, dwp_ref):
        @pl.when(pl.program_id(0) == 0)
        def _():
            dwq_ref[...] = jnp.zeros_like(dwq_ref)
            dwp_ref[...] = jnp.zeros_like(dwp_ref)

        dx, dwq = _rms_bwd(h_ref[...], wq_ref[...], dn_ref[...])
        dh = dy_ref[...] + dx
        dh_ref[...] = dh
        da, dwp = _rms_bwd(a_ref[...], wp_ref[...], dh)
        da_ref[...] = da.astype(BF16)
        dwq_ref[...] += jnp.sum(dwq, axis=0, keepdims=True)
        dwp_ref[...] += jnp.sum(dwp, axis=0, keepdims=True)

    return pl.pallas_call(
        kern, name="mid_bwd", grid=(tp // CHUNK,),
        in_specs=[_row_spec(d)] * 4 + [_vec_spec(d)] * 2,
        out_specs=[_row_spec(d), _row_spec(d), _vec_spec(d), _vec_spec(d)],
        out_shape=[jax.ShapeDtypeStruct((tp, d), F32), jax.ShapeDtypeStruct((tp, d), BF16),
                   jax.ShapeDtypeStruct((1, d), F32), jax.ShapeDtypeStruct((1, d), F32)],
        compiler_params=_cp(("arbitrary",)),
    )(dy, dn2, h1, a, w_pre, w_post)


def _pre_bwd(dh1, dn1, h0, w_pre):
    tp, d = h0.shape
    s = tp - CHUNK

    def kern(dh_ref, dn_ref, h_ref, w_ref, gx_ref, gm_ref, dw_ref):
        i = pl.program_id(0)
        dx, dwr = _rms_bwd(h_ref[...], w_ref[...], dn_ref[...])
        dh0 = dh_ref[...] + dx
        gx_ref[...] = dh0

        @pl.when(i == 0)
        def _():
            gm_ref[...] = dh0[N_PAD:, :]
            dw_ref[...] = jnp.zeros_like(dw_ref)

        dw_ref[...] += jnp.sum(dwr, axis=0, keepdims=True)

    return pl.pallas_call(
        kern, name="pre_bwd", grid=(tp // CHUNK,),
        in_specs=[_row_spec(d)] * 3 + [_vec_spec(d)],
        out_specs=[pl.BlockSpec((CHUNK, d), lambda i: (jnp.maximum(i - 1, 0), 0)),
                   pl.BlockSpec((N_META, d), lambda i: (0, 0)), _vec_spec(d)],
        out_shape=[jax.ShapeDtypeStruct((s, d), F32), jax.ShapeDtypeStruct((N_META, d), F32),
                   jax.ShapeDtypeStruct((1, d), F32)],
        compiler_params=_cp(("arbitrary",)),
    )(dh1, dn1, h0, w_pre)


def _ffn_cols(dff):
    return dff // 2 if dff % 256 == 0 else dff


def _ffn_fwd(gu, conv_w, conv_b):
    tp, two_dff = gu.shape
    dff = two_dff // 2
    tc = _ffn_cols(dff)
    nj = dff // tc
    r8 = CHUNK // 8

    def kern(g_ref, gp_ref, u_ref, w_ref, b_ref, o_ref):
        i = pl.program_id(1)
        prev = gp_ref[...] * (i > 0).astype(F32)
        ext = jnp.concatenate([prev, g_ref[...]], axis=0)
        w = w_ref[...]
        conv = (b_ref[...] + w[0:1] * pltpu.roll(ext, 2, 0)[8:] + w[1:2] * pltpu.roll(ext, 1, 0)[8:]
                + w[2:3] * ext[8:])
        o_ref[...] = (conv * _sigmoid(conv) * u_ref[...]).astype(BF16)

    return pl.pallas_call(
        kern, name="ffn_fwd", grid=(nj, tp // CHUNK),
        in_specs=[pl.BlockSpec((CHUNK, tc), lambda j, i: (i, j)),
                  pl.BlockSpec((8, tc), lambda j, i: (jnp.maximum(i * r8 - 1, 0), j)),
                  pl.BlockSpec((CHUNK, tc), lambda j, i: (i, j + nj)),
                  pl.BlockSpec((CONV_W, tc), lambda j, i: (0, j)),
                  pl.BlockSpec((1, tc), lambda j, i: (0, j))],
        out_specs=pl.BlockSpec((CHUNK, tc), lambda j, i: (i, j)),
        out_shape=jax.ShapeDtypeStruct((tp, dff), BF16),
        compiler_params=_cp(("parallel", "parallel")),
    )(gu, gu, gu, conv_w, conv_b)


def _ffn_bwd(gu, dact, conv_w, conv_b):
    tp, two_dff = gu.shape
    dff = two_dff // 2
    tc = _ffn_cols(dff)
    nj = dff // tc
    ni = tp // CHUNK
    r8 = CHUNK // 8

    def kern(g_ref, gp_ref, gn_ref, u_ref, un_ref, d_ref, dn_ref, w_ref, b_ref, dg_ref, du_ref, st_ref):
        i = pl.program_id(1)

        @pl.when(i == 0)
        def _():
            st_ref[...] = jnp.zeros_like(st_ref)

        first = (i > 0).astype(F32)
        last = (i < ni - 1).astype(F32)
        gate = g_ref[...]
        ext = jnp.concatenate([gp_ref[...] * first, gate, gn_ref[...]], axis=0)
        w = w_ref[...]
        r1 = pltpu.roll(ext, 1, 0)
        r2 = pltpu.roll(ext, 2, 0)
        conv = (b_ref[...] + w[0:1] * r2 + w[1:2] * r1 + w[2:3] * ext)[8:]
        up = jnp.concatenate([u_ref[...], un_ref[...]], axis=0)
        da = jnp.concatenate([d_ref[...], dn_ref[...] * last], axis=0)
        sg = _sigmoid(conv)
        dc = da * up * (sg * (1.0 + conv * (1.0 - sg)))
        du_ref[...] = (d_ref[...] * (conv * sg)[:CHUNK]).astype(BF16)
        n = CHUNK + 8
        dgate = w[2:3] * dc + w[1:2] * pltpu.roll(dc, n - 1, 0) + w[0:1] * pltpu.roll(dc, n - 2, 0)
        dg_ref[...] = dgate[:CHUNK].astype(BF16)
        dcm = dc[:CHUNK]
        s0 = jnp.sum(dcm * r2[8:8 + CHUNK], axis=0, keepdims=True)
        s1 = jnp.sum(dcm * r1[8:8 + CHUNK], axis=0, keepdims=True)
        s2 = jnp.sum(dcm * gate, axis=0, keepdims=True)
        s3 = jnp.sum(dcm, axis=0, keepdims=True)
        row = lax.broadcasted_iota(jnp.int32, (8, tc), 0)
        st_ref[...] += jnp.where(row == 0, s0, jnp.where(row == 1, s1, jnp.where(row == 2, s2,
                                 jnp.where(row == 3, s3, 0.0))))

    main = lambda off: pl.BlockSpec((CHUNK, tc), lambda j, i: (i, j + off))
    nxt = lambda off: pl.BlockSpec((8, tc), lambda j, i: (jnp.minimum((i + 1) * r8, ni * r8 - 1), j + off))
    return pl.pallas_call(
        kern, name="ffn_bwd", grid=(nj, ni),
        in_specs=[main(0), pl.BlockSpec((8, tc), lambda j, i: (jnp.maximum(i * r8 - 1, 0), j)), nxt(0),
                  main(nj), nxt(nj), main(0), nxt(0),
                  pl.BlockSpec((CONV_W, tc), lambda j, i: (0, j)), pl.BlockSpec((1, tc), lambda j, i: (0, j))],
        out_specs=[main(0), main(0), pl.BlockSpec((8, tc), lambda j, i: (0, j))],
        out_shape=[jax.ShapeDtypeStruct((tp, dff), BF16), jax.ShapeDtypeStruct((tp, dff), BF16),
                   jax.ShapeDtypeStruct((8, dff), F32)],
        compiler_params=_cp(("parallel", "arbitrary")),
    )(gu, gu, gu, gu, gu, dact, dact, conv_w, conv_b)


def _rot(x, cs, sn):
    x1, x2 = x[:, :128], x[:, 128:]
    return jnp.concatenate([x1 * cs - x2 * sn, x1 * sn + x2 * cs], axis=1)


def _rot_t(x, cs, sn):
    x1, x2 = x[:, :128], x[:, 128:]
    return jnp.concatenate([x1 * cs + x2 * sn, x2 * cs - x1 * sn], axis=1)


def _ret_tables(rh):
    lg = jnp.log(1.0 - 2.0 ** (-5.0 - jnp.arange(rh, dtype=F32)))
    idx = jnp.arange(CHUNK, dtype=F32)
    diff = idx[:, None] - idx[None, :]
    intra = jnp.where(diff[None] >= 0, jnp.exp(jnp.maximum(diff, 0.0)[None] * lg[:, None, None]), 0.0)
    qdec = jnp.exp((idx[None, :] + 1.0) * lg[:, None])[..., None]
    kdec = jnp.exp((CHUNK - 1.0 - idx[None, :]) * lg[:, None])[..., None]
    cdec = jnp.exp(CHUNK * lg)[:, None, None]
    return intra, qdec, kdec, cdec


def _ret_specs(rh, nc, rev):
    cc = (lambda c: nc - 1 - c) if rev else (lambda c: c)
    col = lambda sec: pl.BlockSpec((CHUNK, RET_DK), lambda h, c: (cc(c), sec * rh + h))
    tab = [pl.BlockSpec((CHUNK, 128), lambda h, c: (cc(c), 0))] * 2
    dec = [pl.BlockSpec((1, CHUNK, CHUNK), lambda h, c: (h, 0, 0)),
           pl.BlockSpec((1, CHUNK, 1), lambda h, c: (h, 0, 0)),
           pl.BlockSpec((1, CHUNK, 1), lambda h, c: (h, 0, 0)),
           pl.BlockSpec((1, 1, 1), lambda h, c: (h, 0, 0))]
    hw = pl.BlockSpec((1, RET_DK), lambda h, c: (0, h))
    hcol = pl.BlockSpec((CHUNK, RET_DK), lambda h, c: (cc(c), h))
    st = pl.BlockSpec((1, 1, RET_DK, RET_DK), lambda h, c: (h, cc(c), 0, 0))
    return col, tab, dec, hw, hcol, st


def _ret_fwd(proj, cos, sin, tables, gnw, rh):
    tp = proj.shape[0]
    nc = tp // CHUNK
    col, tab, dec, hw, hcol, st = _ret_specs(rh, nc, False)

    def kern(q_ref, k_ref, v_ref, g_ref, cos_ref, sin_ref, in_ref, qd_ref, kd_ref, cd_ref, w_ref,
             out_ref, ry_ref, st_ref, state):
        @pl.when(pl.program_id(1) == 0)
        def _():
            state[...] = jnp.zeros_like(state)

        cs, sn = cos_ref[...], sin_ref[...]
        q = (_rot(q_ref[...], cs, sn) * (RET_DK ** -0.5)).astype(BF16)
        kf = _rot(k_ref[...], cs, sn)
        k = kf.astype(BF16)
        v = v_ref[...].astype(BF16)
        s_old = state[...]
        s_b = s_old.astype(BF16)
        st_ref[0, 0] = s_b
        sc = _dot(q, k, NT) * in_ref[0]
        ry = _dot(sc.astype(BF16), v) + _dot(q, s_b) * qd_ref[0]
        state[...] = s_old * cd_ref[0] + _dot((kf * kd_ref[0]).astype(BF16), v, TN)
        ry_ref[...] = ry
        g = g_ref[...]
        out_ref[...] = (g * _sigmoid(g) * _rms_fwd(ry, w_ref[...])).astype(BF16)

    return pl.pallas_call(
        kern, name="ret_fwd", grid=(rh, nc),
        in_specs=[col(0), col(1), col(2), col(3)] + tab + dec + [hw],
        out_specs=[hcol, hcol, st],
        out_shape=[jax.ShapeDtypeStruct((tp, rh * RET_DK), BF16), jax.ShapeDtypeStruct((tp, rh * RET_DK), F32),
                   jax.ShapeDtypeStruct((rh, nc, RET_DK, RET_DK), BF16)],
        scratch_shapes=[pltpu.VMEM((RET_DK, RET_DK), F32)],
        compiler_params=_cp(("parallel", "arbitrary")),
    )(proj, proj, proj, proj, cos, sin, *tables, gnw)


def _ret_bwd(proj, dmix, ry_all, states, cos, sin, tables, gnw, rh):
    tp = proj.shape[0]
    nc = tp // CHUNK
    col, tab, dec, hw, hcol, st = _ret_specs(rh, nc, True)

    def kern(q_ref, k_ref, v_ref, g_ref, cos_ref, sin_ref, in_ref, qd_ref, kd_ref, cd_ref, w_ref,
             do_ref, ry_ref, st_ref, dq_ref, dk_ref, dv_ref, dg_ref, dw_ref, ds):
        @pl.when(pl.program_id(1) == 0)
        def _():
            ds[...] = jnp.zeros_like(ds)
            dw_ref[...] = jnp.zeros_like(dw_ref)

        cs, sn = cos_ref[...], sin_ref[...]
        qf = _rot(q_ref[...], cs, sn) * (RET_DK ** -0.5)
        q = qf.astype(BF16)
        kf = _rot(k_ref[...], cs, sn)
        k = kf.astype(BF16)
        vf = v_ref[...]
        v = vf.astype(BF16)
        g = g_ref[...]
        ry = ry_ref[...]
        w = w_ref[...]
        dout = do_ref[...]
        sg = _sigmoid(g)
        dhn = dout * (g * sg)
        dry, dwr = _rms_bwd(ry, w, dhn)
        dg_ref[...] = (dout * _rms_fwd(ry, w) * (sg * (1.0 + g * (1.0 - sg)))).astype(BF16)
        dw_ref[...] += jnp.sum(dwr, axis=0, keepdims=True)

        dmat = in_ref[0]
        qd, kd = qd_ref[0], kd_ref[0]
        dyb = dry.astype(BF16)
        p = (_dot(q, k, NT) * dmat).astype(BF16)
        dp = (_dot(dyb, v, NT) * dmat).astype(BF16)
        ady = (dry * qd).astype(BF16)
        ds_old = ds[...]
        ds_b = ds_old.astype(BF16)
        dq = _dot(dp, k) + _dot(ady, st_ref[0, 0], NT)
        dk = _dot(dp, q, TN) + _dot(v, ds_b, NT) * kd
        dv = _dot(p, dyb, TN) + _dot((kf * kd).astype(BF16), ds_b)
        ds[...] = ds_old * cd_ref[0] + _dot(q, ady, TN)
        dq_ref[...] = _rot_t(dq * (RET_DK ** -0.5), cs, sn).astype(BF16)
        dk_ref[...] = _rot_t(dk, cs, sn).astype(BF16)
        dv_ref[...] = dv.astype(BF16)

    rw = rh * RET_DK
    outs = pl.pallas_call(
        kern, name="ret_bwd", grid=(rh, nc),
        in_specs=[col(0), col(1), col(2), col(3)] + tab + dec + [hw, hcol, hcol, st],
        out_specs=[hcol, hcol, hcol, hcol, hw],
        out_shape=[jax.ShapeDtypeStruct((tp, rw), BF16)] * 4 + [jax.ShapeDtypeStruct((1, rw), F32)],
        scratch_shapes=[pltpu.VMEM((RET_DK, RET_DK), F32)],
        compiler_params=_cp(("parallel", "arbitrary")),
    )(proj, proj, proj, proj, cos, sin, *tables, gnw, dmix, ry_all, states)
    return outs


def _sb_tile(tp):
    return 3 * CHUNK if tp % (3 * CHUNK) == 0 else CHUNK


def _sb_block(q, k, qpos, kb, scale):
    z = _dot(q, k, NT) * scale
    kpos = kb * CHUNK + lax.broadcasted_iota(jnp.int32, qpos.shape, 1)
    mask = (kpos < qpos) & (kpos >= N_PAD)
    t = jnp.log1p(jnp.exp(-jnp.abs(z)))
    lb = jnp.minimum(z, 0.0) - t
    lk = jnp.where(mask, -jnp.maximum(z, 0.0) - t, 0.0)
    return mask, lb, lk


def _tri_sum(x, tri):
    hi = x.astype(BF16)
    lo = (x - hi.astype(F32)).astype(BF16)
    return _dot(hi, tri) + _dot(lo, tri)


def _tri(strict_upper):
    r = lax.broadcasted_iota(jnp.int32, (CHUNK, CHUNK), 0)
    c = lax.broadcasted_iota(jnp.int32, (CHUNK, CHUNK), 1)
    return ((r > c) if strict_upper else (r < c)).astype(BF16)


def _sb_fwd(proj, sbw, sh, col0):
    tp = proj.shape[0]
    tq = _sb_tile(tp)
    nsub, nq = tq // CHUNK, tp // tq
    assert tp // CHUNK <= 128
    scale = 1.0 / math.sqrt(SB_DH)
    hq = pl.BlockSpec((tq, SB_DH), lambda h, i: (i, h))

    def kern(q_ref, k_ref, v_ref, w_ref, out_ref, sy_ref, ao_ref, a_run, k16, v16):
        i = pl.program_id(1)

        @pl.when(i == 0)
        def _():
            k16[...] = k_ref[...].astype(BF16)
            v16[...] = v_ref[...].astype(BF16)

        q = q_ref[...].astype(BF16)
        upper = _tri(True)
        lane = lax.broadcasted_iota(jnp.int32, (tq, CHUNK), 1)
        qpos = i * tq + lax.broadcasted_iota(jnp.int32, (tq, CHUNK), 0)
        a_run[...] = jnp.zeros_like(a_run)
        sy_ref[...] = jnp.zeros_like(sy_ref)
        ao_ref[...] = jnp.zeros_like(ao_ref)

        def body(jj, carry):
            a, acc, at = a_run[...], sy_ref[...], ao_ref[0]
            for sub in reversed(range(nsub)):
                kb = (i - jj) * nsub + sub
                rows = pl.ds(pl.multiple_of(kb * CHUNK, CHUNK), CHUNK)
                mask, lb, lk = _sb_block(q, k16[rows, :], qpos, kb, scale)
                wgt = jnp.where(mask, jnp.exp(lb + a + _tri_sum(lk, upper)), 0.0)
                acc = acc + _dot(wgt.astype(BF16), v16[rows, :])
                at = jnp.where(lane == kb, a, at)
                a = a + jnp.sum(lk, axis=1, keepdims=True)
            a_run[...], sy_ref[...], ao_ref[0] = a, acc, at
            return carry

        lax.fori_loop(0, i + 1, body, 0)
        out_ref[...] = _rms_fwd(sy_ref[...], w_ref[...]).astype(BF16)

    kv = lambda sec: pl.BlockSpec((tp, SB_DH), lambda h, i: (0, col0 + sec * sh + h))
    return pl.pallas_call(
        kern, name="sb_fwd", grid=(sh, nq),
        in_specs=[pl.BlockSpec((tq, SB_DH), lambda h, i: (i, col0 + h)), kv(1), kv(2),
                  pl.BlockSpec((1, SB_DH), lambda h, i: (0, h))],
        out_specs=[hq, hq, pl.BlockSpec((1, tq, 128), lambda h, i: (h, i, 0))],
        out_shape=[jax.ShapeDtypeStruct((tp, sh * SB_DH), BF16), jax.ShapeDtypeStruct((tp, sh * SB_DH), F32),
                   jax.ShapeDtypeStruct((sh, tp, 128), F32)],
        scratch_shapes=[pltpu.VMEM((tq, CHUNK), F32), pltpu.VMEM((tp, SB_DH), BF16), pltpu.VMEM((tp, SB_DH), BF16)],
        compiler_params=_cp(("parallel", "arbitrary")),
    )(proj, proj, proj, sbw)


def _sb_bwd(proj, dmix, sy_all, aoff, sbw, sh, col0, dcol0):
    tp = proj.shape[0]
    tq = _sb_tile(tp)
    nsub, nq = tq // CHUNK, tp // tq
    scale = 1.0 / math.sqrt(SB_DH)
    hq = pl.BlockSpec((tq, SB_DH), lambda h, i: (i, h))

    def kern(q_ref, k_ref, v_ref, w_ref, do_ref, sy_ref, ao_ref, dq_ref, dk_ref, dv_ref, dw_ref,
             dk_acc, dv_acc, dq_acc, e_run, k16, v16):
        i = pl.program_id(1)

        @pl.when(i == 0)
        def _():
            dk_acc[...] = jnp.zeros_like(dk_acc)
            dv_acc[...] = jnp.zeros_like(dv_acc)
            dw_ref[...] = jnp.zeros_like(dw_ref)
            k16[...] = k_ref[...].astype(BF16)
            v16[...] = v_ref[...].astype(BF16)

        q = q_ref[...].astype(BF16)
        dsy, dwr = _rms_bwd(sy_ref[...], w_ref[...], do_ref[...])
        dw_ref[...] += jnp.sum(dwr, axis=0, keepdims=True)
        dsy_b = dsy.astype(BF16)
        atile = ao_ref[0]
        upper = _tri(True)
        lower = _tri(False)
        lane = lax.broadcasted_iota(jnp.int32, (tq, CHUNK), 1)
        qpos = i * tq + lax.broadcasted_iota(jnp.int32, (tq, CHUNK), 0)

        e_run[...] = jnp.zeros_like(e_run)
        dq_acc[...] = jnp.zeros_like(dq_acc)

        def body(jj, carry):
            e_prev, dq = e_run[...], dq_acc[...]
            for sub in range(nsub):
                kb = jj * nsub + sub
                rows = pl.ds(pl.multiple_of(kb * CHUNK, CHUNK), CHUNK)
                k, v = k16[rows, :], v16[rows, :]
                mask, lb, lk = _sb_block(q, k, qpos, kb, scale)
                a = jnp.sum(jnp.where(lane == kb, atile, 0.0), axis=1, keepdims=True)
                wgt = jnp.where(mask, jnp.exp(lb + a + _tri_sum(lk, upper)), 0.0)
                e = wgt * _dot(dsy_b, v, NT)
                dv_acc[rows, :] += _dot(wgt.astype(BF16), dsy_b, TN)
                sig = jnp.exp(lb)
                e_all = e_prev + _tri_sum(e, lower)
                dz = (jnp.where(mask, e * (1.0 - sig) - e_all * sig, 0.0) * scale).astype(BF16)
                dk_acc[rows, :] += _dot(dz, q, TN)
                dq = dq + _dot(dz, k)
                e_prev = e_prev + jnp.sum(e, axis=1, keepdims=True)
            e_run[...], dq_acc[...] = e_prev, dq
            return carry

        lax.fori_loop(0, i + 1, body, 0)
        dq_ref[...] = dq_acc[...].astype(BF16)

        @pl.when(i == nq - 1)
        def _():
            dk_ref[...] = dk_acc[...].astype(BF16)
            dv_ref[...] = dv_acc[...].astype(BF16)

    kv = lambda sec: pl.BlockSpec((tp, SB_DH), lambda h, i: (0, col0 + sec * sh + h))
    hfull = pl.BlockSpec((tp, SB_DH), lambda h, i: (0, h))
    sw = sh * SB_DH
    return pl.pallas_call(
        kern, name="sb_bwd", grid=(sh, nq),
        in_specs=[pl.BlockSpec((tq, SB_DH), lambda h, i: (i, col0 + h)), kv(1), kv(2),
                  pl.BlockSpec((1, SB_DH), lambda h, i: (0, h)),
                  pl.BlockSpec((tq, SB_DH), lambda h, i: (i, dcol0 + h)), hq,
                  pl.BlockSpec((1, tq, 128), lambda h, i: (h, i, 0))],
        out_specs=[hq, hfull, hfull, pl.BlockSpec((1, SB_DH), lambda h, i: (0, h))],
        out_shape=[jax.ShapeDtypeStruct((tp, sw), BF16)] * 3 + [jax.ShapeDtypeStruct((1, sw), F32)],
        scratch_shapes=[pltpu.VMEM((tp, SB_DH), F32), pltpu.VMEM((tp, SB_DH), F32),
                        pltpu.VMEM((tq, SB_DH), F32), pltpu.VMEM((tq, CHUNK), F32),
                        pltpu.VMEM((tp, SB_DH), BF16), pltpu.VMEM((tp, SB_DH), BF16)],
        compiler_params=_cp(("parallel", "arbitrary")),
    )(proj, proj, proj, sbw, dmix, sy_all, aoff)


def _local_step(x, tgt, meta, w_in, w_out, w_up, w_down, conv_w, conv_b,
                pre1_w, gn_w, sb_w, post1_w, pre2_w, post2_w):
    s, d = x.shape
    tp = s + CHUNK
    rh, sh = d // 512, d // 256
    rw = rh * RET_DK
    h0 = jnp.concatenate([jnp.zeros((N_PAD, d), F32), meta, x], axis=0)
    pos = jnp.arange(tp, dtype=F32) - N_PAD
    inv = ROPE_BASE ** (-jnp.arange(128, dtype=F32) / 128)
    ang = pos[:, None] * inv[None, :]
    cos, sin = jnp.cos(ang), jnp.sin(ang)
    tables = _ret_tables(rh)
    sb_col0 = 4 * rw // SB_DH

    n1 = _prenorm(h0, pre1_w)
    proj = _matmul(n1, w_in, "nn", F32, "mm_proj")
    ret_out, ry, states = _ret_fwd(proj, cos, sin, tables, gn_w, rh)
    sb_out, sy, aoff = _sb_fwd(proj, sb_w, sh, sb_col0)
    mixed = jnp.concatenate([ret_out, sb_out], axis=1)
    a = _matmul(mixed, w_out, "nn", F32, "mm_out")
    h1, n2 = _mid_fwd(h0, a, post1_w, pre2_w)
    gu = _matmul(n2, w_up, "nn", F32, "mm_up")
    act = _ffn_fwd(gu, conv_w, conv_b)
    f = _matmul(act, w_down, "nn", F32, "mm_down")

    dy, d_f, loss, dw_post2 = _loss_bwd(h1, f, tgt, post2_w)
    d_act = _matmul(d_f, w_down, "nt", F32, "mm_dact")
    g_down = _matmul(act, d_f, "tn", BF16, "mm_gdown")
    d_gate, d_up, ffn_stats = _ffn_bwd(gu, d_act, conv_w, conv_b)
    d_gu = jnp.concatenate([d_gate, d_up], axis=1)
    d_n2 = _matmul(d_gu, w_up, "nt", F32, "mm_dn2")
    g_up = _matmul(n2, d_gu, "tn", BF16, "mm_gup")
    dh1, d_a, dw_pre2, dw_post1 = _mid_bwd(dy, d_n2, h1, a, pre2_w, post1_w)
    d_mix = _matmul(d_a, w_out, "nt", F32, "mm_dmix")
    g_out = _matmul(mixed, d_a, "tn", BF16, "mm_gout")
    d_rq, d_rk, d_rv, d_rg, dw_gn = _ret_bwd(proj, d_mix, ry, states, cos, sin, tables, gn_w, rh)
    d_sq, d_sk, d_sv, dw_sb = _sb_bwd(proj, d_mix, sy, aoff, sb_w, sh, sb_col0, rw // SB_DH)
    d_proj = jnp.concatenate([d_rq, d_rk, d_rv, d_rg, d_sq, d_sk, d_sv], axis=1)
    d_n1 = _matmul(d_proj, w_in, "nt", F32, "mm_dn1")
    g_in = _matmul(n1, d_proj, "tn", BF16, "mm_gin")
    grad_x, g_meta, dw_pre1 = _pre_bwd(dh1, d_n1, h0, pre1_w)

    small = dict(loss=loss, meta=g_meta, pre1=dw_pre1, gn=dw_gn, sb=dw_sb, post1=dw_post1, pre2=dw_pre2,
                 conv_w=ffn_stats[0:3], conv_b=ffn_stats[3:4], post2=dw_post2)
    return grad_x, dict(w_in=g_in, w_out=g_out, w_up=g_up, w_down=g_down), small


def _coords():
    return lax.axis_index("x"), lax.axis_index("y"), lax.axis_index("c")


def _other_chips(x, y):
    return [(1 - x, y), (x, 1 - y), (1 - x, 1 - y)]


ANY = pl.BlockSpec(memory_space=pl.ANY)
VM = pl.BlockSpec(memory_space=pltpu.VMEM)


def _gather4_small(v):
    r = v.shape[0]

    def kern(v_ref, o_ref, send, recv):
        x, y, c = _coords()
        o_ref[2 * x + y] = v_ref[...]
        cps = [pltpu.make_async_remote_copy(v_ref, o_ref.at[2 * x + y], send.at[j], recv.at[j],
                                            device_id=(px, py, c), device_id_type=MESH)
               for j, (px, py) in enumerate(_other_chips(x, y))]
        for cp in cps:
            cp.start()
        for cp in cps:
            cp.wait()

    return pl.pallas_call(
        kern, name="gather_small", in_specs=[VM], out_specs=VM,
        out_shape=jax.ShapeDtypeStruct((4, r, 128), F32),
        scratch_shapes=[pltpu.SemaphoreType.DMA((3,)), pltpu.SemaphoreType.DMA((3,))],
    )(v)


def _allreduce8_small(v):
    r = v.shape[0]
    flips = [(fx, fy, fc) for fx in (0, 1) for fy in (0, 1) for fc in (0, 1)][1:]

    def kern(v_ref, o_ref, buf, send, recv):
        x, y, c = _coords()
        me = 4 * x + 2 * y + c
        buf[me] = v_ref[...]
        cps = [pltpu.make_async_remote_copy(v_ref, buf.at[me], send.at[j], recv.at[j],
                                            device_id=(x ^ fx, y ^ fy, c ^ fc), device_id_type=MESH)
               for j, (fx, fy, fc) in enumerate(flips)]
        for cp in cps:
            cp.start()
        for cp in cps:
            cp.wait()
        tot = buf[0]
        for j in range(1, 8):
            tot = tot + buf[j]
        o_ref[...] = tot

    return pl.pallas_call(
        kern, name="allreduce_small", in_specs=[VM], out_specs=VM,
        out_shape=jax.ShapeDtypeStruct((r, 128), F32),
        scratch_shapes=[pltpu.VMEM((8, r, 128), F32), pltpu.SemaphoreType.DMA((7,)),
                        pltpu.SemaphoreType.DMA((7,))],
    )(v)


def _piece(ref, col_sharded, rows, cols, s, hc):
    half = rows // 2
    if col_sharded:
        return ref.at[pl.ds(pl.multiple_of(hc * half, 16), half), pl.ds(pl.multiple_of(s * cols, 128), cols)]
    return ref.at[pl.ds(pl.multiple_of(s * rows + hc * half, 16), half), :]


def _cast_place(w32, c_id, s_id, col_sharded, name):
    rows, cols = w32.shape
    full_shape = (rows, 4 * cols) if col_sharded else (4 * rows, cols)
    tr = max(d for d in _divisors(rows, 16) if d * cols * 4 <= 4 * 2 ** 20)
    nt = rows // tr

    def kern(c_ref, s_ref, w_ref, o_ref):
        o_ref[...] = w_ref[...].astype(BF16)

    if col_sharded:
        o_spec = pl.BlockSpec((tr, cols), lambda t, c_ref, s_ref: (t, s_ref[0]))
    else:
        o_spec = pl.BlockSpec((tr, cols), lambda t, c_ref, s_ref: (s_ref[0] * nt + t, 0))
    return pl.pallas_call(
        kern, name=name,
        grid_spec=pltpu.PrefetchScalarGridSpec(
            num_scalar_prefetch=2, grid=(nt,),
            in_specs=[pl.BlockSpec((tr, cols), lambda t, c_ref, s_ref: (t, 0))], out_specs=o_spec),
        out_shape=jax.ShapeDtypeStruct(full_shape, BF16),
        compiler_params=_cp(("parallel",)),
    )(c_id, s_id, w32)


def _gather_weight(full, rows, cols, col_sharded, name):
    def kern(w_ref, o_ref, send, recv):
        x, y, c = _coords()
        s = 2 * x + y
        sib = (x, y, 1 - c)
        chips = _other_chips(x, y)
        pc = functools.partial(_piece, o_ref, col_sharded, rows, cols)
        mine = _piece(w_ref, col_sharded, rows, cols, s, c)
        first = [pltpu.make_async_remote_copy(mine, pc(s, c), send.at[j], recv.at[j],
                                              device_id=(px, py, c), device_id_type=MESH)
                 for j, (px, py) in enumerate(chips)]
        for cp in first:
            cp.start()
        passed = [pltpu.make_async_remote_copy(pc(2 * px + py, c), pc(2 * px + py, c), send.at[3 + j], recv.at[3 + j],
                                               device_id=sib, device_id_type=MESH)
                  for j, (px, py) in enumerate(chips)]
        for j in range(3):
            first[j].wait_recv()
            passed[j].start()
        for j, (px, py) in enumerate(chips):
            pltpu.make_async_remote_copy(pc(2 * px + py, 1 - c), pc(2 * px + py, 1 - c), send.at[3 + j],
                                         recv.at[3 + j], device_id=sib, device_id_type=MESH).wait_recv()
        for cp in first + passed:
            cp.wait_send()

    return pl.pallas_call(
        kern, name=name, in_specs=[ANY], out_specs=ANY, input_output_aliases={0: 0},
        out_shape=jax.ShapeDtypeStruct(full.shape, BF16),
        scratch_shapes=[pltpu.SemaphoreType.DMA((6,)), pltpu.SemaphoreType.DMA((6,))],
    )(full)


def _rs_pair(g, col_sharded, rows, cols, name):
    half = rows // 2

    def kern(g_ref, o_ref, send, recv):
        x, y, c = _coords()
        cps = [pltpu.make_async_remote_copy(_piece(g_ref, col_sharded, rows, cols, s, 1 - c), o_ref.at[s],
                                            send.at[s], recv.at[s], device_id=(x, y, 1 - c), device_id_type=MESH)
               for s in range(4)]
        for cp in cps:
            cp.start()
        for cp in cps:
            cp.wait()

    return pl.pallas_call(
        kern, name=name, in_specs=[ANY], out_specs=ANY,
        out_shape=jax.ShapeDtypeStruct((4, half, cols), BF16),
        scratch_shapes=[pltpu.SemaphoreType.DMA((4,)), pltpu.SemaphoreType.DMA((4,))],
    )(g)


def _half_tiles(half, cols):
    tr = max(d for d in _divisors(half, 16) if d * cols * 4 <= 4 * 2 ** 20)
    return tr, half // tr


def _half_spec(col_sharded, tr, nt, cols, own):
    which = (lambda s, s_ref: s_ref[0]) if own else (lambda s, s_ref: s)
    if col_sharded:
        return pl.BlockSpec((tr, cols), lambda s, t, c_ref, s_ref: (c_ref[0] * nt + t, which(s, s_ref)))
    return pl.BlockSpec((tr, cols), lambda s, t, c_ref, s_ref: ((2 * which(s, s_ref) + c_ref[0]) * nt + t, 0))


def _rs_add(g, r1, c_id, s_id, col_sharded, rows, cols, name):
    half = rows // 2
    tr, nt = _half_tiles(half, cols)

    def kern(c_ref, s_ref, g_ref, r_ref, o_ref):
        o_ref[0] = (g_ref[...].astype(F32) + r_ref[0].astype(F32)).astype(BF16)

    slab = pl.BlockSpec((1, tr, cols), lambda s, t, c_ref, s_ref: (s, t, 0))
    return pl.pallas_call(
        kern, name=name,
        grid_spec=pltpu.PrefetchScalarGridSpec(
            num_scalar_prefetch=2, grid=(4, nt),
            in_specs=[_half_spec(col_sharded, tr, nt, cols, False), slab], out_specs=slab),
        out_shape=jax.ShapeDtypeStruct((4, half, cols), BF16),
        compiler_params=_cp(("parallel", "parallel")),
    )(c_id, s_id, g, r1)


def _rs_scatter(p, name):
    _, half, cols = p.shape

    def kern(p_ref, o_ref, send, recv):
        x, y, c = _coords()
        cps = [pltpu.make_async_remote_copy(p_ref.at[2 * px + py], o_ref.at[j], send.at[j], recv.at[j],
                                            device_id=(px, py, c), device_id_type=MESH)
               for j, (px, py) in enumerate(_other_chips(x, y))]
        for cp in cps:
            cp.start()
        for cp in cps:
            cp.wait()

    return pl.pallas_call(
        kern, name=name, in_specs=[ANY], out_specs=ANY,
        out_shape=jax.ShapeDtypeStruct((3, half, cols), BF16),
        scratch_shapes=[pltpu.SemaphoreType.DMA((3,)), pltpu.SemaphoreType.DMA((3,))],
    )(p)


def _rs_total(g, r1, r2, c_id, s_id, col_sharded, rows, cols, name):
    half = rows // 2
    tr, nt = _half_tiles(half, cols)

    def kern(c_ref, s_ref, g_ref, r1_ref, r2_ref, o_ref):
        tot = g_ref[...].astype(F32) + r1_ref[0].astype(F32)
        for j in range(3):
            tot = tot + r2_ref[j].astype(F32)
        o_ref[...] = tot

    return pl.pallas_call(
        kern, name=name,
        grid_spec=pltpu.PrefetchScalarGridSpec(
            num_scalar_prefetch=2, grid=(1, nt),
            in_specs=[_half_spec(col_sharded, tr, nt, cols, True),
                      pl.BlockSpec((1, tr, cols), lambda s, t, c_ref, s_ref: (s_ref[0], t, 0)),
                      pl.BlockSpec((3, tr, cols), lambda s, t, c_ref, s_ref: (0, t, 0))],
            out_specs=pl.BlockSpec((tr, cols), lambda s, t, c_ref, s_ref: (c_ref[0] * nt + t, 0))),
        out_shape=jax.ShapeDtypeStruct((rows, cols), F32),
        compiler_params=_cp(("parallel", "parallel")),
    )(c_id, s_id, g, r1, r2)


def _rs_exchange(t, name):
    rows, cols = t.shape
    half = rows // 2

    def kern(t_ref, o_ref, send, recv):
        x, y, c = _coords()
        mine = pl.ds(pl.multiple_of(c * half, 8), half)
        cp = pltpu.make_async_remote_copy(t_ref.at[mine, :], o_ref.at[mine, :], send, recv,
                                          device_id=(x, y, 1 - c), device_id_type=MESH)
        cp.start()
        cp.wait()

    return pl.pallas_call(
        kern, name=name, in_specs=[ANY], out_specs=ANY, input_output_aliases={0: 0},
        out_shape=jax.ShapeDtypeStruct((rows, cols), F32),
        scratch_shapes=[pltpu.SemaphoreType.DMA, pltpu.SemaphoreType.DMA],
    )(t)


def _reduce_scatter(g, c_id, s_id, col_sharded, rows, cols, tag):
    r1 = _rs_pair(g, col_sharded, rows, cols, "rs_pair_" + tag)
    p = _rs_add(g, r1, c_id, s_id, col_sharded, rows, cols, "rs_add_" + tag)
    r2 = _rs_scatter(p, "rs_scatter_" + tag)
    t = _rs_total(g, r1, r2, c_id, s_id, col_sharded, rows, cols, "rs_total_" + tag)
    return _rs_exchange(t, "rs_exchange_" + tag)


def _adamw_vals(w, g, m, v):
    m = ADAM_B1 * m + (1.0 - ADAM_B1) * g
    v = ADAM_B2 * v + (1.0 - ADAM_B2) * (g * g)
    m_hat = m / (1.0 - ADAM_B1 ** ADAM_STEP)
    v_hat = v / (1.0 - ADAM_B2 ** ADAM_STEP)
    delta = -ADAM_LR * (m_hat / (jnp.sqrt(v_hat) + ADAM_EPS) + ADAM_WD * w)
    return delta, m, v


def _adamw(w, g, m, v, name):
    rows, cols = w.shape
    tr = max(d for d in _divisors(rows, 8) if d * cols * 4 <= 2 * 2 ** 20)

    def kern(w_ref, g_ref, m_ref, v_ref, d_ref, mo_ref, vo_ref):
        d_ref[...], mo_ref[...], vo_ref[...] = _adamw_vals(w_ref[...], g_ref[...], m_ref[...], v_ref[...])

    spec = pl.BlockSpec((tr, cols), lambda i: (i, 0))
    return pl.pallas_call(
        kern, name=name, grid=(rows // tr,), in_specs=[spec] * 4, out_specs=[spec] * 3,
        out_shape=[jax.ShapeDtypeStruct((rows, cols), F32)] * 3,
        compiler_params=_cp(("parallel",)),
    )(w, g, m, v)


def _pack(arrs):
    flat = []
    for a in arrs:
        a = a.reshape(-1)
        flat.append(jnp.pad(a, (0, (-a.shape[0]) % 1024)))
    return jnp.concatenate(flat).reshape(-1, 128)


def _unpack(slab, shapes):
    out, off = [], 0
    flat = slab.reshape(slab.shape[:-2] + (-1,))
    for shp in shapes:
        n = math.prod(shp)
        out.append(flat[..., off:off + n].reshape(slab.shape[:-2] + tuple(shp)))
        off += n + (-n) % 1024
    return out


BIG = ("w_in", "w_out", "w_up", "w_down")
COL_SHARDED = dict(w_in=True, w_out=False, w_up=True, w_down=False)
SMALL = ("meta_tokens", "attn_pre_norm_w", "ret_gn_w", "sb_norm_w", "attn_post_norm_w", "ffn_pre_norm_w",
         "conv_w", "conv_b", "ffn_post_norm_w")
ORDER = ("meta_tokens", "attn_pre_norm_w", "w_in", "ret_gn_w", "sb_norm_w", "w_out", "attn_post_norm_w",
         "ffn_pre_norm_w", "w_up", "conv_w", "conv_b", "w_down", "ffn_post_norm_w")


def kernel(x, meta_tokens, attn_pre_norm_w, w_in, ret_gn_w, sb_norm_w, w_out, attn_post_norm_w, ffn_pre_norm_w, w_up, conv_w, conv_b, w_down, ffn_post_norm_w, loss_target, m_meta_tokens, m_attn_pre_norm_w, m_w_in, m_ret_gn_w, m_sb_norm_w, m_w_out, m_attn_post_norm_w, m_ffn_pre_norm_w, m_w_up, m_conv_w, m_conv_b, m_w_down, m_ffn_post_norm_w, v_meta_tokens, v_attn_pre_norm_w, v_w_in, v_ret_gn_w, v_sb_norm_w, v_w_out, v_attn_post_norm_w, v_ffn_pre_norm_w, v_w_up, v_conv_w, v_conv_b, v_w_down, v_ffn_post_norm_w):
    args = dict(locals())
    w = {n: args[n] for n in ORDER}
    m = {n: args["m_" + n] for n in ORDER}
    v = {n: args["v_" + n] for n in ORDER}
    xi, yi, ci = _coords()
    shard_id = 2 * xi + yi
    c_id = ci.astype(jnp.int32).reshape(1)
    s_id = shard_id.astype(jnp.int32).reshape(1)
    d = x.shape[-1]

    mshape, cshape = w["meta_tokens"].shape, w["conv_w"][0].shape
    got = _unpack(_gather4_small(_pack([w["meta_tokens"], w["conv_w"][0]])), [mshape, cshape])
    meta_full = jnp.moveaxis(got[0], 0, 1).reshape(N_META, d)
    conv_w_full = jnp.moveaxis(got[1], 0, 1).reshape(CONV_W, -1)

    full = {}
    for n in BIG:
        rows, cols = w[n][0].shape
        placed = _cast_place(w[n][0], c_id, s_id, COL_SHARDED[n], "place_" + n)
        full[n] = _gather_weight(placed, rows, cols, COL_SHARDED[n], "gather_" + n)

    grad_x, g_big, g_small = _local_step(
        x[0], loss_target[0], meta_full, full["w_in"], full["w_out"], full["w_up"], full["w_down"],
        conv_w_full, w["conv_b"], w["attn_pre_norm_w"], w["ret_gn_w"], w["sb_norm_w"],
        w["attn_post_norm_w"], w["ffn_pre_norm_w"], w["ffn_post_norm_w"])

    names = ("loss", "meta", "pre1", "gn", "sb", "post1", "pre2", "conv_w", "conv_b", "post2")
    tot = _unpack(_allreduce8_small(_pack([g_small[n] for n in names])), [g_small[n].shape for n in names])
    tot = dict(zip(names, tot))
    loss = tot["loss"][0, 0]
    mcols, ccols = mshape[1], cshape[1]
    grads = {
        "meta_tokens": lax.dynamic_slice_in_dim(tot["meta"], shard_id * mcols, mcols, axis=1),
        "attn_pre_norm_w": tot["pre1"], "ret_gn_w": tot["gn"], "sb_norm_w": tot["sb"],
        "attn_post_norm_w": tot["post1"], "ffn_pre_norm_w": tot["pre2"],
        "conv_w": lax.dynamic_slice_in_dim(tot["conv_w"], shard_id * ccols, ccols, axis=1)[None],
        "conv_b": tot["conv_b"], "ffn_post_norm_w": tot["post2"],
    }

    for n in BIG:
        rows, cols = w[n][0].shape
        grads[n] = _reduce_scatter(g_big[n], c_id, s_id, COL_SHARDED[n], rows, cols, n)[None]

    delta, new_m, new_v = {}, {}, {}
    for n in BIG:
        dl, mo, vo = _adamw(w[n][0], grads[n][0], m[n][0], v[n][0], "adamw_" + n)
        delta[n], new_m[n], new_v[n] = dl[None], mo[None], vo[None]
    shapes = [w[n].shape for n in SMALL]
    packed = [_pack([src[n] for n in SMALL]) for src in (w, grads, m, v)]
    outs = _adamw(*packed, "adamw_small")
    for dst, slab in zip((delta, new_m, new_v), outs):
        for n, a in zip(SMALL, _unpack(slab, shapes)):
            dst[n] = a

    return (loss, grad_x[None], *[grads[n] for n in ORDER], *[delta[n] for n in ORDER],
            *[new_m[n] for n in ORDER], *[new_v[n] for n in ORDER])
```

```python
import functools
import math

import jax
import jax.numpy as jnp
from jax import lax
from jax.experimental import pallas as pl
from jax.experimental.pallas import tpu as pltpu

F32 = jnp.float32
BF16 = jnp.bfloat16
MESH = pl.DeviceIdType.MESH

EPS = 1e-6
ROPE_BASE = 10000.0
N_META = 16
CHUNK = 128
N_PAD = CHUNK - N_META
RET_DK = 256
SB_DH = 128
CONV_W = 3

ADAM_LR = 0.001
ADAM_B1 = 0.9
ADAM_B2 = 0.999
ADAM_EPS = 1e-08
ADAM_WD = 0.01
ADAM_STEP = 10

V7X_VMEM_BYTES = 64 * 2 ** 20
VMEM_LIMIT = V7X_VMEM_BYTES - 8 * 2 ** 20
MM_BUDGET = 36 * 2 ** 20
V7X_BF16_FLOPS = 0.9e15
V7X_HBM_BPS = 3.2e12
GRID_STEP_S = 0.35e-6

NN = (((1,), (0,)), ((), ()))
NT = (((1,), (1,)), ((), ()))
TN = (((0,), (0,)), ((), ()))


def _cp(sem, vmem=VMEM_LIMIT):
    return pltpu.CompilerParams(dimension_semantics=sem, vmem_limit_bytes=vmem)


def _dot(a, b, dims=NN):
    return lax.dot_general(a, b, dims, preferred_element_type=F32)


def _sigmoid(x):
    return 1.0 / (1.0 + jnp.exp(-x))


def _divisors(n, align):
    return [d for d in range(align, n + 1, align) if n % d == 0]


def _mm_tiles(mode, m, n, k, out_bytes):
    best = None
    tms = [d for d in _divisors(m, 128 if mode == "tn" else 16) if d >= 128]
    tns = [d for d in _divisors(n, 128) if d >= 128]
    tks = [d for d in _divisors(k, 128) if d >= 128]
    flops = 2.0 * m * n * k
    for tm in tms:
        for tn in tns:
            if tm * tn > 2112 * 1024:
                continue
            for tk in tks:
                nk = k // tk
                foot = 4 * (tm * tk + tk * tn) + 2 * tm * tn * out_bytes
                if nk > 1:
                    foot += 4 * tm * tn
                if foot > MM_BUDGET:
                    continue
                steps = (m // tm) * (n // tn) * nk
                for swap in (False, True):
                    if nk > 1:
                        traffic = 2.0 * (n // tn) * m * k + 2.0 * (m // tm) * n * k
                    elif swap:
                        traffic = 2.0 * n * k + 2.0 * (n // tn) * m * k
                    else:
                        traffic = 2.0 * m * k + 2.0 * (m // tm) * n * k
                    traffic += out_bytes * m * n
                    t = max(flops / V7X_BF16_FLOPS, traffic / V7X_HBM_BPS) + steps * GRID_STEP_S
                    if best is None or t < best[0]:
                        best = (t, tm, tn, tk, swap)
    assert best is not None, (mode, m, n, k)
    return best[1:]


class _Exchange:
    def __init__(self, operands, out_shapes, aliases, nsem, start, finish):
        self.operands, self.out_shapes, self.aliases, self.nsem = operands, out_shapes, aliases, nsem
        self.start, self.finish = start, finish


def _carried(exchanges, in_base, out_base):
    ins = [a for e in exchanges for a in e.operands]
    outs = [s for e in exchanges for s in e.out_shapes]
    sems = [pltpu.SemaphoreType.DMA((e.nsem,)) for e in exchanges for _ in (0, 1)]
    alias, i, o = {}, in_base, out_base
    for e in exchanges:
        alias.update({i + k: o + v for k, v in e.aliases.items()})
        i, o = i + len(e.operands), o + len(e.out_shapes)
    return ins, outs, sems, alias


def _run_carried(exchanges, phase, in_refs, out_refs, sem_refs):
    i = o = 0
    for n, e in enumerate(exchanges):
        ni, no = len(e.operands), len(e.out_shapes)
        getattr(e, phase)(in_refs[i:i + ni], out_refs[o:o + no], sem_refs[2 * n], sem_refs[2 * n + 1])
        i, o = i + ni, o + no


def _run_exchange(e, name):
    ins, outs, sems, alias = _carried([e], 0, 0)

    def kern(*refs):
        in_refs, out_refs, sem_refs = refs[:len(ins)], refs[len(ins):len(ins) + len(outs)], refs[len(ins) + len(outs):]
        _run_carried([e], "start", in_refs, out_refs, sem_refs)
        _run_carried([e], "finish", in_refs, out_refs, sem_refs)

    return pl.pallas_call(
        kern, name=name, in_specs=[ANY] * len(ins), out_specs=[ANY] * len(outs), out_shape=outs,
        input_output_aliases=alias, scratch_shapes=sems,
    )(*ins)


def _matmul(a, b, mode, out_dtype, name, carry=()):
    if mode == "nn":
        (m, k), (k2, n) = a.shape, b.shape
    elif mode == "nt":
        (m, k), (n, k2) = a.shape, b.shape
    else:
        (k, m), (k2, n) = a.shape, b.shape
    assert k == k2 and a.dtype == BF16 and b.dtype == BF16
    tm, tn, tk, swap = _mm_tiles(mode, m, n, k, jnp.dtype(out_dtype).itemsize)
    nk = k // tk
    dims = {"nn": NN, "nt": NT, "tn": TN}[mode]

    def ij(g0, g1):
        return (g1, g0) if swap else (g0, g1)

    if mode == "tn":
        a_spec = pl.BlockSpec((tk, tm), lambda g0, g1, kk: (kk, ij(g0, g1)[0]))
    else:
        a_spec = pl.BlockSpec((tm, tk), lambda g0, g1, kk: (ij(g0, g1)[0], kk))
    if mode == "nt":
        b_spec = pl.BlockSpec((tn, tk), lambda g0, g1, kk: (ij(g0, g1)[1], kk))
    else:
        b_spec = pl.BlockSpec((tk, tn), lambda g0, g1, kk: (kk, ij(g0, g1)[1]))
    o_spec = pl.BlockSpec((tm, tn), lambda g0, g1, kk: ij(g0, g1))

    grid = (n // tn, m // tm, nk) if swap else (m // tm, n // tn, nk)
    carry = list(carry)
    x_in, x_out, x_sems, alias = _carried(carry, 2, 1)
    acc_shapes = [pltpu.VMEM((tm, tn), F32)] if nk > 1 else []

    def kern(a_ref, b_ref, *rest):
        x_in_refs, o_ref = rest[:len(x_in)], rest[len(x_in)]
        x_out_refs = rest[len(x_in) + 1:len(x_in) + 1 + len(x_out)]
        tail = rest[len(x_in) + 1 + len(x_out):]
        acc, sem_refs = tail[:len(acc_shapes)], tail[len(acc_shapes):]
        pid = [pl.program_id(ax) for ax in range(3)]
        if carry:
            @pl.when((pid[0] == 0) & (pid[1] == 0) & (pid[2] == 0))
            def _():
                _run_carried(carry, "start", x_in_refs, x_out_refs, sem_refs)

        prod = _dot(a_ref[...], b_ref[...], dims)
        if nk == 1:
            o_ref[...] = prod.astype(out_dtype)
        else:
            kk = pid[2]

            @pl.when(kk == 0)
            def _():
                acc[0][...] = prod

            @pl.when(kk > 0)
            def _():
                acc[0][...] += prod

            @pl.when(kk == nk - 1)
            def _():
                o_ref[...] = acc[0][...].astype(out_dtype)

        if carry:
            @pl.when((pid[0] == grid[0] - 1) & (pid[1] == grid[1] - 1) & (pid[2] == grid[2] - 1))
            def _():
                _run_carried(carry, "finish", x_in_refs, x_out_refs, sem_refs)

    sem = ("arbitrary",) * 3 if carry else ("parallel", "parallel", "arbitrary")
    res = pl.pallas_call(
        kern, name=name, grid=grid, in_specs=[a_spec, b_spec] + [ANY] * len(x_in),
        out_specs=[o_spec] + [ANY] * len(x_out),
        out_shape=[jax.ShapeDtypeStruct((m, n), out_dtype)] + x_out,
        input_output_aliases=alias, scratch_shapes=acc_shapes + x_sems,
        compiler_params=_cp(sem),
    )(a, b, *x_in)
    return (res[0], res[1:]) if carry else res[0]


def _rms_fwd(x, w):
    r = lax.rsqrt(jnp.mean(x * x, axis=-1, keepdims=True) + EPS)
    return x * r * w


def _rms_bwd(x, w, g):
    r = lax.rsqrt(jnp.mean(x * x, axis=-1, keepdims=True) + EPS)
    gw = g * w
    dx = r * gw - x * (r * r * r * jnp.mean(gw * x, axis=-1, keepdims=True))
    return dx, g * (x * r)


def _row_spec(d):
    return pl.BlockSpec((CHUNK, d), lambda i: (i, 0))


def _vec_spec(d):
    return pl.BlockSpec((1, d), lambda i: (0, 0))


def _prenorm(h0, w):
    tp, d = h0.shape

    def kern(h_ref, w_ref, o_ref):
        o_ref[...] = _rms_fwd(h_ref[...], w_ref[...]).astype(BF16)

    return pl.pallas_call(
        kern, name="prenorm1", grid=(tp // CHUNK,),
        in_specs=[_row_spec(d), _vec_spec(d)], out_specs=_row_spec(d),
        out_shape=jax.ShapeDtypeStruct((tp, d), BF16),
        compiler_params=_cp(("parallel",)),
    )(h0, w)


def _mid_fwd(h0, a, w_post, w_pre):
    tp, d = h0.shape

    def kern(h_ref, a_ref, wp_ref, wq_ref, h1_ref, n2_ref):
        h1 = h_ref[...] + _rms_fwd(a_ref[...], wp_ref[...])
        h1_ref[...] = h1
        n2_ref[...] = _rms_fwd(h1, wq_ref[...]).astype(BF16)

    return pl.pallas_call(
        kern, name="mid_fwd", grid=(tp // CHUNK,),
        in_specs=[_row_spec(d), _row_spec(d), _vec_spec(d), _vec_spec(d)],
        out_specs=[_row_spec(d), _row_spec(d)],
        out_shape=[jax.ShapeDtypeStruct((tp, d), F32), jax.ShapeDtypeStruct((tp, d), BF16)],
        compiler_params=_cp(("parallel",)),
    )(h0, a, w_post, w_pre)


def _loss_bwd(h1, f, tgt, w_post):
    tp, d = h1.shape

    def kern(h_ref, f_ref, t_ref, w_ref, dy_ref, df_ref, loss_ref, dw_ref):
        i = pl.program_id(0)

        @pl.when(i == 0)
        def _():
            dy_ref[...] = jnp.zeros_like(dy_ref)
            df_ref[...] = jnp.zeros_like(df_ref)
            loss_ref[...] = jnp.zeros_like(loss_ref)
            dw_ref[...] = jnp.zeros_like(dw_ref)

        @pl.when(i > 0)
        def _():
            fv = f_ref[...]
            w = w_ref[...]
            err = h_ref[...] + _rms_fwd(fv, w) - t_ref[...]
            loss_ref[...] += 0.5 * jnp.sum(jnp.mean(err * err, axis=-1, keepdims=True))
            dy = err * (1.0 / d)
            dy_ref[...] = dy
            dfv, dwr = _rms_bwd(fv, w, dy)
            df_ref[...] = dfv.astype(BF16)
            dw_ref[...] += jnp.sum(dwr, axis=0, keepdims=True)

    return pl.pallas_call(
        kern, name="loss_bwd", grid=(tp // CHUNK,),
        in_specs=[_row_spec(d), _row_spec(d),
                  pl.BlockSpec((CHUNK, d), lambda i: (jnp.maximum(i - 1, 0), 0)), _vec_spec(d)],
        out_specs=[_row_spec(d), _row_spec(d), pl.BlockSpec((1, 128), lambda i: (0, 0)), _vec_spec(d)],
        out_shape=[jax.ShapeDtypeStruct((tp, d), F32), jax.ShapeDtypeStruct((tp, d), BF16),
                   jax.ShapeDtypeStruct((1, 128), F32), jax.ShapeDtypeStruct((1, d), F32)],
        compiler_params=_cp(("arbitrary",)),
    )(h1, f, tgt, w_post)


def _mid_bwd(dy, dn2, h1, a, w_pre, w_post):
    tp, d = h1.shape

    def kern(dy_ref, dn_ref, h_ref, a_ref, wq_ref, wp_ref, dh_ref, da_ref, dwq_ref, dwp_ref):
        @pl.when(pl.program_id(0) == 0)
        def _():
            dwq_ref[...] = jnp.zeros_like(dwq_ref)
            dwp_ref[...] = jnp.zeros_like(dwp_ref)

        dx, dwq = _rms_bwd(h_ref[...], wq_ref[...], dn_ref[...])
        dh = dy_ref[...] + dx
        dh_ref[...] = dh
        da, dwp = _rms_bwd(a_ref[...], wp_ref[...], dh)
        da_ref[...] = da.astype(BF16)
        dwq_ref[...] += jnp.sum(dwq, axis=0, keepdims=True)
        dwp_ref[...] += jnp.sum(dwp, axis=0, keepdims=True)

    return pl.pallas_call(
        kern, name="mid_bwd", grid=(tp // CHUNK,),
        in_specs=[_row_spec(d)] * 4 + [_vec_spec(d)] * 2,
        out_specs=[_row_spec(d), _row_spec(d), _vec_spec(d), _vec_spec(d)],
        out_shape=[jax.ShapeDtypeStruct((tp, d), F32), jax.ShapeDtypeStruct((tp, d), BF16),
                   jax.ShapeDtypeStruct((1, d), F32), jax.ShapeDtypeStruct((1, d), F32)],
        compiler_params=_cp(("arbitrary",)),
    )(dy, dn2, h1, a, w_pre, w_post)


def _pre_bwd(dh1, dn1, h0, w_pre):
    tp, d = h0.shape
    s = tp - CHUNK

    def kern(dh_ref, dn_ref, h_ref, w_ref, gx_ref, gm_ref, dw_ref):
        i = pl.program_id(0)
        dx, dwr = _rms_bwd(h_ref[...], w_ref[...], dn_ref[...])
        dh0 = dh_ref[...] + dx
        gx_ref[...] = dh0

        @pl.when(i == 0)
        def _():
            gm_ref[...] = dh0[N_PAD:, :]
            dw_ref[...] = jnp.zeros_like(dw_ref)

        dw_ref[...] += jnp.sum(dwr, axis=0, keepdims=True)

    return pl.pallas_call(
        kern, name="pre_bwd", grid=(tp // CHUNK,),
        in_specs=[_row_spec(d)] * 3 + [_vec_spec(d)],
        out_specs=[pl.BlockSpec((CHUNK, d), lambda i: (jnp.maximum(i - 1, 0), 0)),
                   pl.BlockSpec((N_META, d), lambda i: (0, 0)), _vec_spec(d)],
        out_shape=[jax.ShapeDtypeStruct((s, d), F32), jax.ShapeDtypeStruct((N_META, d), F32),
                   jax.ShapeDtypeStruct((1, d), F32)],
        compiler_params=_cp(("arbitrary",)),
    )(dh1, dn1, h0, w_pre)


def _ffn_cols(dff):
    return dff // 2 if dff % 256 == 0 else dff


def _ffn_fwd(gu, conv_w, conv_b):
    tp, two_dff = gu.shape
    dff = two_dff // 2
    tc = _ffn_cols(dff)
    nj = dff // tc
    r8 = CHUNK // 8

    def kern(g_ref, gp_ref, u_ref, w_ref, b_ref, o_ref):
        i = pl.program_id(1)
        prev = gp_ref[...] * (i > 0).astype(F32)
        ext = jnp.concatenate([prev, g_ref[...]], axis=0)
        w = w_ref[...]
        conv = (b_ref[...] + w[0:1] * pltpu.roll(ext, 2, 0)[8:] + w[1:2] * pltpu.roll(ext, 1, 0)[8:]
                + w[2:3] * ext[8:])
        o_ref[...] = (conv * _sigmoid(conv) * u_ref[...]).astype(BF16)

    return pl.pallas_call(
        kern, name="ffn_fwd", grid=(nj, tp // CHUNK),
        in_specs=[pl.BlockSpec((CHUNK, tc), lambda j, i: (i, j)),
                  pl.BlockSpec((8, tc), lambda j, i: (jnp.maximum(i * r8 - 1, 0), j)),
                  pl.BlockSpec((CHUNK, tc), lambda j, i: (i, j + nj)),
                  pl.BlockSpec((CONV_W, tc), lambda j, i: (0, j)),
                  pl.BlockSpec((1, tc), lambda j, i: (0, j))],
        out_specs=pl.BlockSpec((CHUNK, tc), lambda j, i: (i, j)),
        out_shape=jax.ShapeDtypeStruct((tp, dff), BF16),
        compiler_params=_cp(("parallel", "parallel")),
    )(gu, gu, gu, conv_w, conv_b)


def _ffn_bwd(gu, dact, conv_w, conv_b):
    tp, two_dff = gu.shape
    dff = two_dff // 2
    tc = _ffn_cols(dff)
    nj = dff // tc
    ni = tp // CHUNK
    r8 = CHUNK // 8

    def kern(g_ref, gp_ref, gn_ref, u_ref, un_ref, d_ref, dn_ref, w_ref, b_ref, dg_ref, du_ref, st_ref):
        i = pl.program_id(1)

        @pl.when(i == 0)
        def _():
            st_ref[...] = jnp.zeros_like(st_ref)

        first = (i > 0).astype(F32)
        last = (i < ni - 1).astype(F32)
        gate = g_ref[...]
        ext = jnp.concatenate([gp_ref[...] * first, gate, gn_ref[...]], axis=0)
        w = w_ref[...]
        r1 = pltpu.roll(ext, 1, 0)
        r2 = pltpu.roll(ext, 2, 0)
        conv = (b_ref[...] + w[0:1] * r2 + w[1:2] * r1 + w[2:3] * ext)[8:]
        up = jnp.concatenate([u_ref[...], un_ref[...]], axis=0)
        da = jnp.concatenate([d_ref[...], dn_ref[...] * last], axis=0)
        sg = _sigmoid(conv)
        dc = da * up * (sg * (1.0 + conv * (1.0 - sg)))
        du_ref[...] = (d_ref[...] * (conv * sg)[:CHUNK]).astype(BF16)
        n = CHUNK + 8
        dgate = w[2:3] * dc + w[1:2] * pltpu.roll(dc, n - 1, 0) + w[0:1] * pltpu.roll(dc, n - 2, 0)
        dg_ref[...] = dgate[:CHUNK].astype(BF16)
        dcm = dc[:CHUNK]
        s0 = jnp.sum(dcm * r2[8:8 + CHUNK], axis=0, keepdims=True)
        s1 = jnp.sum(dcm * r1[8:8 + CHUNK], axis=0, keepdims=True)
        s2 = jnp.sum(dcm * gate, axis=0, keepdims=True)
        s3 = jnp.sum(dcm, axis=0, keepdims=True)
        row = lax.broadcasted_iota(jnp.int32, (8, tc), 0)
        st_ref[...] += jnp.where(row == 0, s0, jnp.where(row == 1, s1, jnp.where(row == 2, s2,
                                 jnp.where(row == 3, s3, 0.0))))

    main = lambda off: pl.BlockSpec((CHUNK, tc), lambda j, i: (i, j + off))
    nxt = lambda off: pl.BlockSpec((8, tc), lambda j, i: (jnp.minimum((i + 1) * r8, ni * r8 - 1), j + off))
    return pl.pallas_call(
        kern, name="ffn_bwd", grid=(nj, ni),
        in_specs=[main(0), pl.BlockSpec((8, tc), lambda j, i: (jnp.maximum(i * r8 - 1, 0), j)), nxt(0),
                  main(nj), nxt(nj), main(0), nxt(0),
                  pl.BlockSpec((CONV_W, tc), lambda j, i: (0, j)), pl.BlockSpec((1, tc), lambda j, i: (0, j))],
        out_specs=[main(0), main(0), pl.BlockSpec((8, tc), lambda j, i: (0, j))],
        out_shape=[jax.ShapeDtypeStruct((tp, dff), BF16), jax.ShapeDtypeStruct((tp, dff), BF16),
                   jax.ShapeDtypeStruct((8, dff), F32)],
        compiler_params=_cp(("parallel", "arbitrary")),
    )(gu, gu, gu, gu, gu, dact, dact, conv_w, conv_b)


def _rot(x, cs, sn):
    x1, x2 = x[:, :128], x[:, 128:]
    return jnp.concatenate([x1 * cs - x2 * sn, x1 * sn + x2 * cs], axis=1)


def _rot_t(x, cs, sn):
    x1, x2 = x[:, :128], x[:, 128:]
    return jnp.concatenate([x1 * cs + x2 * sn, x2 * cs - x1 * sn], axis=1)


def _ret_tables(rh):
    lg = jnp.log(1.0 - 2.0 ** (-5.0 - jnp.arange(rh, dtype=F32)))
    idx = jnp.arange(CHUNK, dtype=F32)
    diff = idx[:, None] - idx[None, :]
    intra = jnp.where(diff[None] >= 0, jnp.exp(jnp.maximum(diff, 0.0)[None] * lg[:, None, None]), 0.0)
    qdec = jnp.exp((idx[None, :] + 1.0) * lg[:, None])[..., None]
    kdec = jnp.exp((CHUNK - 1.0 - idx[None, :]) * lg[:, None])[..., None]
    cdec = jnp.exp(CHUNK * lg)[:, None, None]
    return intra, qdec, kdec, cdec


def _ret_specs(rh, nc, rev):
    cc = (lambda c: nc - 1 - c) if rev else (lambda c: c)
    col = lambda sec: pl.BlockSpec((CHUNK, RET_DK), lambda h, c: (cc(c), sec * rh + h))
    tab = [pl.BlockSpec((CHUNK, 128), lambda h, c: (cc(c), 0))] * 2
    dec = [pl.BlockSpec((1, CHUNK, CHUNK), lambda h, c: (h, 0, 0)),
           pl.BlockSpec((1, CHUNK, 1), lambda h, c: (h, 0, 0)),
           pl.BlockSpec((1, CHUNK, 1), lambda h, c: (h, 0, 0)),
           pl.BlockSpec((1, 1, 1), lambda h, c: (h, 0, 0))]
    hw = pl.BlockSpec((1, RET_DK), lambda h, c: (0, h))
    hcol = pl.BlockSpec((CHUNK, RET_DK), lambda h, c: (cc(c), h))
    st = pl.BlockSpec((1, 1, RET_DK, RET_DK), lambda h, c: (h, cc(c), 0, 0))
    return col, tab, dec, hw, hcol, st


def _ret_fwd(proj, cos, sin, tables, gnw, rh):
    tp = proj.shape[0]
    nc = tp // CHUNK
    col, tab, dec, hw, hcol, st = _ret_specs(rh, nc, False)

    def kern(q_ref, k_ref, v_ref, g_ref, cos_ref, sin_ref, in_ref, qd_ref, kd_ref, cd_ref, w_ref,
             out_ref, ry_ref, st_ref, state):
        @pl.when(pl.program_id(1) == 0)
        def _():
            state[...] = jnp.zeros_like(state)

        cs, sn = cos_ref[...], sin_ref[...]
        q = (_rot(q_ref[...], cs, sn) * (RET_DK ** -0.5)).astype(BF16)
        kf = _rot(k_ref[...], cs, sn)
        k = kf.astype(BF16)
        v = v_ref[...].astype(BF16)
        s_old = state[...]
        s_b = s_old.astype(BF16)
        st_ref[0, 0] = s_b
        sc = _dot(q, k, NT) * in_ref[0]
        ry = _dot(sc.astype(BF16), v) + _dot(q, s_b) * qd_ref[0]
        state[...] = s_old * cd_ref[0] + _dot((kf * kd_ref[0]).astype(BF16), v, TN)
        ry_ref[...] = ry
        g = g_ref[...]
        out_ref[...] = (g * _sigmoid(g) * _rms_fwd(ry, w_ref[...])).astype(BF16)

    return pl.pallas_call(
        kern, name="ret_fwd", grid=(rh, nc),
        in_specs=[col(0), col(1), col(2), col(3)] + tab + dec + [hw],
        out_specs=[hcol, hcol, st],
        out_shape=[jax.ShapeDtypeStruct((tp, rh * RET_DK), BF16), jax.ShapeDtypeStruct((tp, rh * RET_DK), F32),
                   jax.ShapeDtypeStruct((rh, nc, RET_DK, RET_DK), BF16)],
        scratch_shapes=[pltpu.VMEM((RET_DK, RET_DK), F32)],
        compiler_params=_cp(("parallel", "arbitrary")),
    )(proj, proj, proj, proj, cos, sin, *tables, gnw)


def _ret_bwd(proj, dmix, ry_all, states, cos, sin, tables, gnw, rh):
    tp = proj.shape[0]
    nc = tp // CHUNK
    col, tab, dec, hw, hcol, st = _ret_specs(rh, nc, True)

    def kern(q_ref, k_ref, v_ref, g_ref, cos_ref, sin_ref, in_ref, qd_ref, kd_ref, cd_ref, w_ref,
             do_ref, ry_ref, st_ref, dq_ref, dk_ref, dv_ref, dg_ref, dw_ref, ds):
        @pl.when(pl.program_id(1) == 0)
        def _():
            ds[...] = jnp.zeros_like(ds)
            dw_ref[...] = jnp.zeros_like(dw_ref)

        cs, sn = cos_ref[...], sin_ref[...]
        qf = _rot(q_ref[...], cs, sn) * (RET_DK ** -0.5)
        q = qf.astype(BF16)
        kf = _rot(k_ref[...], cs, sn)
        k = kf.astype(BF16)
        vf = v_ref[...]
        v = vf.astype(BF16)
        g = g_ref[...]
        ry = ry_ref[...]
        w = w_ref[...]
        dout = do_ref[...]
        sg = _sigmoid(g)
        dhn = dout * (g * sg)
        dry, dwr = _rms_bwd(ry, w, dhn)
        dg_ref[...] = (dout * _rms_fwd(ry, w) * (sg * (1.0 + g * (1.0 - sg)))).astype(BF16)
        dw_ref[...] += jnp.sum(dwr, axis=0, keepdims=True)

        dmat = in_ref[0]
        qd, kd = qd_ref[0], kd_ref[0]
        dyb = dry.astype(BF16)
        p = (_dot(q, k, NT) * dmat).astype(BF16)
        dp = (_dot(dyb, v, NT) * dmat).astype(BF16)
        ady = (dry * qd).astype(BF16)
        ds_old = ds[...]
        ds_b = ds_old.astype(BF16)
        dq = _dot(dp, k) + _dot(ady, st_ref[0, 0], NT)
        dk = _dot(dp, q, TN) + _dot(v, ds_b, NT) * kd
        dv = _dot(p, dyb, TN) + _dot((kf * kd).astype(BF16), ds_b)
        ds[...] = ds_old * cd_ref[0] + _dot(q, ady, TN)
        dq_ref[...] = _rot_t(dq * (RET_DK ** -0.5), cs, sn).astype(BF16)
        dk_ref[...] = _rot_t(dk, cs, sn).astype(BF16)
        dv_ref[...] = dv.astype(BF16)

    rw = rh * RET_DK
    outs = pl.pallas_call(
        kern, name="ret_bwd", grid=(rh, nc),
        in_specs=[col(0), col(1), col(2), col(3)] + tab + dec + [hw, hcol, hcol, st],
        out_specs=[hcol, hcol, hcol, hcol, hw],
        out_shape=[jax.ShapeDtypeStruct((tp, rw), BF16)] * 4 + [jax.ShapeDtypeStruct((1, rw), F32)],
        scratch_shapes=[pltpu.VMEM((RET_DK, RET_DK), F32)],
        compiler_params=_cp(("parallel", "arbitrary")),
    )(proj, proj, proj, proj, cos, sin, *tables, gnw, dmix, ry_all, states)
    return outs


def _sb_tile(tp):
    return 3 * CHUNK if tp % (3 * CHUNK) == 0 else CHUNK


def _sb_block(q, k, qpos, kb, scale):
    z = _dot(q, k, NT) * scale
    kpos = kb * CHUNK + lax.broadcasted_iota(jnp.int32, qpos.shape, 1)
    mask = (kpos < qpos) & (kpos >= N_PAD)
    t = jnp.log1p(jnp.exp(-jnp.abs(z)))
    lb = jnp.minimum(z, 0.0) - t
    lk = jnp.where(mask, -jnp.maximum(z, 0.0) - t, 0.0)
    return mask, lb, lk


def _tri_sum(x, tri):
    hi = x.astype(BF16)
    lo = (x - hi.astype(F32)).astype(BF16)
    return _dot(hi, tri) + _dot(lo, tri)


def _tri(strict_upper):
    r = lax.broadcasted_iota(jnp.int32, (CHUNK, CHUNK), 0)
    c = lax.broadcasted_iota(jnp.int32, (CHUNK, CHUNK), 1)
    return ((r > c) if strict_upper else (r < c)).astype(BF16)


def _sb_fwd(proj, sbw, sh, col0, carry=()):
    tp = proj.shape[0]
    tq = _sb_tile(tp)
    nsub, nq = tq // CHUNK, tp // tq
    assert tp // CHUNK <= 128
    scale = 1.0 / math.sqrt(SB_DH)
    hq = pl.BlockSpec((tq, SB_DH), lambda h, i: (i, h))

    carry = list(carry)
    x_in, x_out, x_sems, alias = _carried(carry, 4, 3)

    def kern(q_ref, k_ref, v_ref, w_ref, *rest):
        x_in_refs, rest = rest[:len(x_in)], rest[len(x_in):]
        out_ref, sy_ref, ao_ref = rest[:3]
        x_out_refs, (a_run, k16, v16), sem_refs = rest[3:3 + len(x_out)], rest[3 + len(x_out):6 + len(x_out)], rest[6 + len(x_out):]
        h, i = pl.program_id(0), pl.program_id(1)
        if carry:
            @pl.when((h == 0) & (i == 0))
            def _():
                _run_carried(carry, "start", x_in_refs, x_out_refs, sem_refs)

        @pl.when(i == 0)
        def _():
            k16[...] = k_ref[...].astype(BF16)
            v16[...] = v_ref[...].astype(BF16)

        q = q_ref[...].astype(BF16)
        upper = _tri(True)
        lane = lax.broadcasted_iota(jnp.int32, (tq, CHUNK), 1)
        qpos = i * tq + lax.broadcasted_iota(jnp.int32, (tq, CHUNK), 0)
        a_run[...] = jnp.zeros_like(a_run)
        sy_ref[...] = jnp.zeros_like(sy_ref)
        ao_ref[...] = jnp.zeros_like(ao_ref)

        def body(jj, carry):
            a, acc, at = a_run[...], sy_ref[...], ao_ref[0]
            for sub in reversed(range(nsub)):
                kb = (i - jj) * nsub + sub
                rows = pl.ds(pl.multiple_of(kb * CHUNK, CHUNK), CHUNK)
                mask, lb, lk = _sb_block(q, k16[rows, :], qpos, kb, scale)
                wgt = jnp.where(mask, jnp.exp(lb + a + _tri_sum(lk, upper)), 0.0)
                acc = acc + _dot(wgt.astype(BF16), v16[rows, :])
                at = jnp.where(lane == kb, a, at)
                a = a + jnp.sum(lk, axis=1, keepdims=True)
            a_run[...], sy_ref[...], ao_ref[0] = a, acc, at
            return carry

        lax.fori_loop(0, i + 1, body, 0)
        out_ref[...] = _rms_fwd(sy_ref[...], w_ref[...]).astype(BF16)
        if carry:
            @pl.when((h == sh - 1) & (i == nq - 1))
            def _():
                _run_carried(carry, "finish", x_in_refs, x_out_refs, sem_refs)

    kv = lambda sec: pl.BlockSpec((tp, SB_DH), lambda h, i: (0, col0 + sec * sh + h))
    res = pl.pallas_call(
        kern, name="sb_fwd", grid=(sh, nq),
        in_specs=[pl.BlockSpec((tq, SB_DH), lambda h, i: (i, col0 + h)), kv(1), kv(2),
                  pl.BlockSpec((1, SB_DH), lambda h, i: (0, h))] + [ANY] * len(x_in),
        out_specs=[hq, hq, pl.BlockSpec((1, tq, 128), lambda h, i: (h, i, 0))] + [ANY] * len(x_out),
        out_shape=[jax.ShapeDtypeStruct((tp, sh * SB_DH), BF16), jax.ShapeDtypeStruct((tp, sh * SB_DH), F32),
                   jax.ShapeDtypeStruct((sh, tp, 128), F32)] + x_out,
        input_output_aliases=alias,
        scratch_shapes=[pltpu.VMEM((tq, CHUNK), F32), pltpu.VMEM((tp, SB_DH), BF16),
                        pltpu.VMEM((tp, SB_DH), BF16)] + x_sems,
        compiler_params=_cp(("arbitrary", "arbitrary") if carry else ("parallel", "arbitrary")),
    )(proj, proj, proj, sbw, *x_in)
    return res[:3], res[3:]


def _sb_bwd(proj, dmix, sy_all, aoff, sbw, sh, col0, dcol0, carry=()):
    tp = proj.shape[0]
    tq = _sb_tile(tp)
    nsub, nq = tq // CHUNK, tp // tq
    scale = 1.0 / math.sqrt(SB_DH)
    hq = pl.BlockSpec((tq, SB_DH), lambda h, i: (i, h))
    carry = list(carry)
    x_in, x_out, x_sems, alias = _carried(carry, 7, 4)

    def kern(q_ref, k_ref, v_ref, w_ref, do_ref, sy_ref, ao_ref, *rest):
        x_in_refs, rest = rest[:len(x_in)], rest[len(x_in):]
        dq_ref, dk_ref, dv_ref, dw_ref = rest[:4]
        x_out_refs, rest = rest[4:4 + len(x_out)], rest[4 + len(x_out):]
        (dk_acc, dv_acc, dq_acc, e_run, k16, v16), sem_refs = rest[:6], rest[6:]
        h, i = pl.program_id(0), pl.program_id(1)
        if carry:
            @pl.when((h == 0) & (i == 0))
            def _():
                _run_carried(carry, "start", x_in_refs, x_out_refs, sem_refs)

        @pl.when(i == 0)
        def _():
            dk_acc[...] = jnp.zeros_like(dk_acc)
            dv_acc[...] = jnp.zeros_like(dv_acc)
            dw_ref[...] = jnp.zeros_like(dw_ref)
            k16[...] = k_ref[...].astype(BF16)
            v16[...] = v_ref[...].astype(BF16)

        q = q_ref[...].astype(BF16)
        dsy, dwr = _rms_bwd(sy_ref[...], w_ref[...], do_ref[...])
        dw_ref[...] += jnp.sum(dwr, axis=0, keepdims=True)
        dsy_b = dsy.astype(BF16)
        atile = ao_ref[0]
        upper = _tri(True)
        lower = _tri(False)
        lane = lax.broadcasted_iota(jnp.int32, (tq, CHUNK), 1)
        qpos = i * tq + lax.broadcasted_iota(jnp.int32, (tq, CHUNK), 0)

        e_run[...] = jnp.zeros_like(e_run)
        dq_acc[...] = jnp.zeros_like(dq_acc)

        def body(jj, carry):
            e_prev, dq = e_run[...], dq_acc[...]
            for sub in range(nsub):
                kb = jj * nsub + sub
                rows = pl.ds(pl.multiple_of(kb * CHUNK, CHUNK), CHUNK)
                k, v = k16[rows, :], v16[rows, :]
                mask, lb, lk = _sb_block(q, k, qpos, kb, scale)
                a = jnp.sum(jnp.where(lane == kb, atile, 0.0), axis=1, keepdims=True)
                wgt = jnp.where(mask, jnp.exp(lb + a + _tri_sum(lk, upper)), 0.0)
                e = wgt * _dot(dsy_b, v, NT)
                dv_acc[rows, :] += _dot(wgt.astype(BF16), dsy_b, TN)
                sig = jnp.exp(lb)
                e_all = e_prev + _tri_sum(e, lower)
                dz = (jnp.where(mask, e * (1.0 - sig) - e_all * sig, 0.0) * scale).astype(BF16)
                dk_acc[rows, :] += _dot(dz, q, TN)
                dq = dq + _dot(dz, k)
                e_prev = e_prev + jnp.sum(e, axis=1, keepdims=True)
            e_run[...], dq_acc[...] = e_prev, dq
            return carry

        lax.fori_loop(0, i + 1, body, 0)
        dq_ref[...] = dq_acc[...].astype(BF16)

        @pl.when(i == nq - 1)
        def _():
            dk_ref[...] = dk_acc[...].astype(BF16)
            dv_ref[...] = dv_acc[...].astype(BF16)

        if carry:
            @pl.when((h == sh - 1) & (i == nq - 1))
            def _():
                _run_carried(carry, "finish", x_in_refs, x_out_refs, sem_refs)

    kv = lambda sec: pl.BlockSpec((tp, SB_DH), lambda h, i: (0, col0 + sec * sh + h))
    hfull = pl.BlockSpec((tp, SB_DH), lambda h, i: (0, h))
    sw = sh * SB_DH
    res = pl.pallas_call(
        kern, name="sb_bwd", grid=(sh, nq),
        in_specs=[pl.BlockSpec((tq, SB_DH), lambda h, i: (i, col0 + h)), kv(1), kv(2),
                  pl.BlockSpec((1, SB_DH), lambda h, i: (0, h)),
                  pl.BlockSpec((tq, SB_DH), lambda h, i: (i, dcol0 + h)), hq,
                  pl.BlockSpec((1, tq, 128), lambda h, i: (h, i, 0))] + [ANY] * len(x_in),
        out_specs=[hq, hfull, hfull, pl.BlockSpec((1, SB_DH), lambda h, i: (0, h))] + [ANY] * len(x_out),
        out_shape=[jax.ShapeDtypeStruct((tp, sw), BF16)] * 3 + [jax.ShapeDtypeStruct((1, sw), F32)] + x_out,
        input_output_aliases=alias,
        scratch_shapes=[pltpu.VMEM((tp, SB_DH), F32), pltpu.VMEM((tp, SB_DH), F32),
                        pltpu.VMEM((tq, SB_DH), F32), pltpu.VMEM((tq, CHUNK), F32),
                        pltpu.VMEM((tp, SB_DH), BF16), pltpu.VMEM((tp, SB_DH), BF16)] + x_sems,
        compiler_params=_cp(("arbitrary", "arbitrary") if carry else ("parallel", "arbitrary")),
    )(proj, proj, proj, sbw, dmix, sy_all, aoff, *x_in)
    return res[:4], res[4:]


def _local_step(x, tgt, meta, net, conv_w, conv_b, pre1_w, gn_w, sb_w, post1_w, pre2_w, post2_w):
    s, d = x.shape

    def mm(host, a, b, mode, dtype):
        carry = net.carry(host)
        out = _matmul(a, b, mode, dtype, host, carry)
        if carry:
            out, extra = out
            net.took(host, extra)
        return out

    tp = s + CHUNK
    rh, sh = d // 512, d // 256
    rw = rh * RET_DK
    h0 = jnp.concatenate([jnp.zeros((N_PAD, d), F32), meta, x], axis=0)
    pos = jnp.arange(tp, dtype=F32) - N_PAD
    inv = ROPE_BASE ** (-jnp.arange(128, dtype=F32) / 128)
    ang = pos[:, None] * inv[None, :]
    cos, sin = jnp.cos(ang), jnp.sin(ang)
    tables = _ret_tables(rh)
    sb_col0 = 4 * rw // SB_DH

    n1 = _prenorm(h0, pre1_w)
    proj = mm("mm_proj", n1, net.weight("w_in"), "nn", F32)
    ret_out, ry, states = _ret_fwd(proj, cos, sin, tables, gn_w, rh)
    (sb_out, sy, aoff), extra = _sb_fwd(proj, sb_w, sh, sb_col0, net.carry("sb_fwd"))
    net.took("sb_fwd", extra)
    mixed = jnp.concatenate([ret_out, sb_out], axis=1)
    a = mm("mm_out", mixed, net.weight("w_out"), "nn", F32)
    h1, n2 = _mid_fwd(h0, a, post1_w, pre2_w)
    gu = mm("mm_up", n2, net.weight("w_up"), "nn", F32)
    act = _ffn_fwd(gu, conv_w, conv_b)
    f = mm("mm_down", act, net.weight("w_down"), "nn", F32)

    dy, d_f, loss, dw_post2 = _loss_bwd(h1, f, tgt, post2_w)
    d_act = mm("mm_dact", d_f, net.weight("w_down"), "nt", F32)
    net.grad("w_down", mm("mm_gdown", act, d_f, "tn", BF16))
    d_gate, d_up, ffn_stats = _ffn_bwd(gu, d_act, conv_w, conv_b)
    d_gu = jnp.concatenate([d_gate, d_up], axis=1)
    d_n2 = mm("mm_dn2", d_gu, net.weight("w_up"), "nt", F32)
    net.grad("w_up", mm("mm_gup", n2, d_gu, "tn", BF16))
    dh1, d_a, dw_pre2, dw_post1 = _mid_bwd(dy, d_n2, h1, a, pre2_w, post1_w)
    d_mix = mm("mm_dmix", d_a, net.weight("w_out"), "nt", F32)
    net.grad("w_out", mm("mm_gout", mixed, d_a, "tn", BF16))
    d_rq, d_rk, d_rv, d_rg, dw_gn = _ret_bwd(proj, d_mix, ry, states, cos, sin, tables, gn_w, rh)
    (d_sq, d_sk, d_sv, dw_sb), extra = _sb_bwd(proj, d_mix, sy, aoff, sb_w, sh, sb_col0, rw // SB_DH,
                                               net.carry("sb_bwd"))
    net.took("sb_bwd", extra)
    d_proj = jnp.concatenate([d_rq, d_rk, d_rv, d_rg, d_sq, d_sk, d_sv], axis=1)
    d_n1 = mm("mm_dn1", d_proj, net.weight("w_in"), "nt", F32)
    net.grad("w_in", mm("mm_gin", n1, d_proj, "tn", BF16))
    grad_x, g_meta, dw_pre1 = _pre_bwd(dh1, d_n1, h0, pre1_w)

    small = dict(loss=loss, meta=g_meta, pre1=dw_pre1, gn=dw_gn, sb=dw_sb, post1=dw_post1, pre2=dw_pre2,
                 conv_w=ffn_stats[0:3], conv_b=ffn_stats[3:4], post2=dw_post2)
    return grad_x, small


def _coords():
    return lax.axis_index("x"), lax.axis_index("y"), lax.axis_index("c")


def _other_chips(x, y):
    return [(1 - x, y), (x, 1 - y), (1 - x, 1 - y)]


ANY = pl.BlockSpec(memory_space=pl.ANY)
VM = pl.BlockSpec(memory_space=pltpu.VMEM)


def _gather4_small(v):
    r = v.shape[0]

    def kern(v_ref, o_ref, send, recv):
        x, y, c = _coords()
        o_ref[2 * x + y] = v_ref[...]
        cps = [pltpu.make_async_remote_copy(v_ref, o_ref.at[2 * x + y], send.at[j], recv.at[j],
                                            device_id=(px, py, c), device_id_type=MESH)
               for j, (px, py) in enumerate(_other_chips(x, y))]
        for cp in cps:
            cp.start()
        for cp in cps:
            cp.wait()

    return pl.pallas_call(
        kern, name="gather_small", in_specs=[VM], out_specs=VM,
        out_shape=jax.ShapeDtypeStruct((4, r, 128), F32),
        scratch_shapes=[pltpu.SemaphoreType.DMA((3,)), pltpu.SemaphoreType.DMA((3,))],
    )(v)


def _allreduce8_small(v):
    r = v.shape[0]
    flips = [(fx, fy, fc) for fx in (0, 1) for fy in (0, 1) for fc in (0, 1)][1:]

    def kern(v_ref, o_ref, buf, send, recv):
        x, y, c = _coords()
        me = 4 * x + 2 * y + c
        buf[me] = v_ref[...]
        cps = [pltpu.make_async_remote_copy(v_ref, buf.at[me], send.at[j], recv.at[j],
                                            device_id=(x ^ fx, y ^ fy, c ^ fc), device_id_type=MESH)
               for j, (fx, fy, fc) in enumerate(flips)]
        for cp in cps:
            cp.start()
        for cp in cps:
            cp.wait()
        tot = buf[0]
        for j in range(1, 8):
            tot = tot + buf[j]
        o_ref[...] = tot

    return pl.pallas_call(
        kern, name="allreduce_small", in_specs=[VM], out_specs=VM,
        out_shape=jax.ShapeDtypeStruct((r, 128), F32),
        scratch_shapes=[pltpu.VMEM((8, r, 128), F32), pltpu.SemaphoreType.DMA((7,)),
                        pltpu.SemaphoreType.DMA((7,))],
    )(v)


def _piece(ref, col_sharded, rows, cols, s, hc):
    half = rows // 2
    if col_sharded:
        return ref.at[pl.ds(pl.multiple_of(hc * half, 16), half), pl.ds(pl.multiple_of(s * cols, 128), cols)]
    return ref.at[pl.ds(pl.multiple_of(s * rows + hc * half, 16), half), :]


def _cast_place(w32, c_id, s_id, col_sharded, name):
    rows, cols = w32.shape
    full_shape = (rows, 4 * cols) if col_sharded else (4 * rows, cols)
    tr = max(d for d in _divisors(rows, 16) if d * cols * 4 <= 4 * 2 ** 20)
    nt = rows // tr

    def kern(c_ref, s_ref, w_ref, o_ref):
        o_ref[...] = w_ref[...].astype(BF16)

    if col_sharded:
        o_spec = pl.BlockSpec((tr, cols), lambda t, c_ref, s_ref: (t, s_ref[0]))
    else:
        o_spec = pl.BlockSpec((tr, cols), lambda t, c_ref, s_ref: (s_ref[0] * nt + t, 0))
    return pl.pallas_call(
        kern, name=name,
        grid_spec=pltpu.PrefetchScalarGridSpec(
            num_scalar_prefetch=2, grid=(nt,),
            in_specs=[pl.BlockSpec((tr, cols), lambda t, c_ref, s_ref: (t, 0))], out_specs=o_spec),
        out_shape=jax.ShapeDtypeStruct(full_shape, BF16),
        compiler_params=_cp(("parallel",)),
    )(c_id, s_id, w32)


def _gather_exchange(full, rows, cols, col_sharded):
    def copies(w_ref, o_ref, send, recv):
        x, y, c = _coords()
        s = 2 * x + y
        sib = (x, y, 1 - c)
        chips = _other_chips(x, y)
        pc = functools.partial(_piece, o_ref, col_sharded, rows, cols)
        mine = _piece(w_ref, col_sharded, rows, cols, s, c)
        first = [pltpu.make_async_remote_copy(mine, pc(s, c), send.at[j], recv.at[j],
                                              device_id=(px, py, c), device_id_type=MESH)
                 for j, (px, py) in enumerate(chips)]
        passed = [pltpu.make_async_remote_copy(pc(2 * px + py, c), pc(2 * px + py, c), send.at[3 + j], recv.at[3 + j],
                                               device_id=sib, device_id_type=MESH)
                  for j, (px, py) in enumerate(chips)]
        from_sib = [pltpu.make_async_remote_copy(pc(2 * px + py, 1 - c), pc(2 * px + py, 1 - c), send.at[3 + j],
                                                 recv.at[3 + j], device_id=sib, device_id_type=MESH)
                    for j, (px, py) in enumerate(chips)]
        return first, passed, from_sib

    def start(ins, outs, send, recv):
        for cp in copies(ins[0], outs[0], send, recv)[0]:
            cp.start()

    def finish(ins, outs, send, recv):
        first, passed, from_sib = copies(ins[0], outs[0], send, recv)
        for j in range(3):
            first[j].wait_recv()
            passed[j].start()
        for cp in from_sib:
            cp.wait_recv()
        for cp in first + passed:
            cp.wait_send()

    return _Exchange([full], [jax.ShapeDtypeStruct(full.shape, BF16)], {0: 0}, 6, start, finish)


def _rs_pair(g, col_sharded, rows, cols, name):
    half = rows // 2

    def kern(g_ref, o_ref, send, recv):
        x, y, c = _coords()
        cps = [pltpu.make_async_remote_copy(_piece(g_ref, col_sharded, rows, cols, s, 1 - c), o_ref.at[s],
                                            send.at[s], recv.at[s], device_id=(x, y, 1 - c), device_id_type=MESH)
               for s in range(4)]
        for cp in cps:
            cp.start()
        for cp in cps:
            cp.wait()

    return pl.pallas_call(
        kern, name=name, in_specs=[ANY], out_specs=ANY,
        out_shape=jax.ShapeDtypeStruct((4, half, cols), BF16),
        scratch_shapes=[pltpu.SemaphoreType.DMA((4,)), pltpu.SemaphoreType.DMA((4,))],
    )(g)


def _half_tiles(half, cols):
    tr = max(d for d in _divisors(half, 16) if d * cols * 4 <= 4 * 2 ** 20)
    return tr, half // tr


def _half_spec(col_sharded, tr, nt, cols, own):
    which = (lambda s, s_ref: s_ref[0]) if own else (lambda s, s_ref: s)
    if col_sharded:
        return pl.BlockSpec((tr, cols), lambda s, t, c_ref, s_ref: (c_ref[0] * nt + t, which(s, s_ref)))
    return pl.BlockSpec((tr, cols), lambda s, t, c_ref, s_ref: ((2 * which(s, s_ref) + c_ref[0]) * nt + t, 0))


def _rs_add(g, r1, c_id, s_id, col_sharded, rows, cols, name):
    half = rows // 2
    tr, nt = _half_tiles(half, cols)

    def kern(c_ref, s_ref, g_ref, r_ref, o_ref):
        o_ref[0] = (g_ref[...].astype(F32) + r_ref[0].astype(F32)).astype(BF16)

    slab = pl.BlockSpec((1, tr, cols), lambda s, t, c_ref, s_ref: (s, t, 0))
    return pl.pallas_call(
        kern, name=name,
        grid_spec=pltpu.PrefetchScalarGridSpec(
            num_scalar_prefetch=2, grid=(4, nt),
            in_specs=[_half_spec(col_sharded, tr, nt, cols, False), slab], out_specs=slab),
        out_shape=jax.ShapeDtypeStruct((4, half, cols), BF16),
        compiler_params=_cp(("parallel", "parallel")),
    )(c_id, s_id, g, r1)


def _scatter_exchange(p):
    _, half, cols = p.shape

    def copies(p_ref, o_ref, send, recv):
        x, y, c = _coords()
        return [pltpu.make_async_remote_copy(p_ref.at[2 * px + py], o_ref.at[j], send.at[j], recv.at[j],
                                             device_id=(px, py, c), device_id_type=MESH)
                for j, (px, py) in enumerate(_other_chips(x, y))]

    def start(ins, outs, send, recv):
        for cp in copies(ins[0], outs[0], send, recv):
            cp.start()

    def finish(ins, outs, send, recv):
        for cp in copies(ins[0], outs[0], send, recv):
            cp.wait()

    return _Exchange([p], [jax.ShapeDtypeStruct((3, half, cols), BF16)], {}, 3, start, finish)


def _rs_total(g, r1, r2, c_id, s_id, col_sharded, rows, cols, name):
    half = rows // 2
    tr, nt = _half_tiles(half, cols)

    def kern(c_ref, s_ref, g_ref, r1_ref, r2_ref, o_ref):
        tot = g_ref[...].astype(F32) + r1_ref[0].astype(F32)
        for j in range(3):
            tot = tot + r2_ref[j].astype(F32)
        o_ref[...] = tot

    return pl.pallas_call(
        kern, name=name,
        grid_spec=pltpu.PrefetchScalarGridSpec(
            num_scalar_prefetch=2, grid=(1, nt),
            in_specs=[_half_spec(col_sharded, tr, nt, cols, True),
                      pl.BlockSpec((1, tr, cols), lambda s, t, c_ref, s_ref: (s_ref[0], t, 0)),
                      pl.BlockSpec((3, tr, cols), lambda s, t, c_ref, s_ref: (0, t, 0))],
            out_specs=pl.BlockSpec((tr, cols), lambda s, t, c_ref, s_ref: (c_ref[0] * nt + t, 0))),
        out_shape=jax.ShapeDtypeStruct((rows, cols), F32),
        compiler_params=_cp(("parallel", "parallel")),
    )(c_id, s_id, g, r1, r2)


def _rs_exchange(t, name):
    rows, cols = t.shape
    half = rows // 2

    def kern(t_ref, o_ref, send, recv):
        x, y, c = _coords()
        mine = pl.ds(pl.multiple_of(c * half, 8), half)
        cp = pltpu.make_async_remote_copy(t_ref.at[mine, :], o_ref.at[mine, :], send, recv,
                                          device_id=(x, y, 1 - c), device_id_type=MESH)
        cp.start()
        cp.wait()

    return pl.pallas_call(
        kern, name=name, in_specs=[ANY], out_specs=ANY, input_output_aliases={0: 0},
        out_shape=jax.ShapeDtypeStruct((rows, cols), F32),
        scratch_shapes=[pltpu.SemaphoreType.DMA, pltpu.SemaphoreType.DMA],
    )(t)


class _Whole:
    def __init__(self, full):
        self.full, self.grads = dict(full), {}

    def weight(self, n):
        return self.full[n]

    def carry(self, host):
        return []

    def took(self, host, outs):
        pass

    def grad(self, n, g):
        self.grads[n] = g


BIG = ("w_in", "w_out", "w_up", "w_down")
COL_SHARDED = dict(w_in=True, w_out=False, w_up=True, w_down=False)
GATHER_ON = dict(mm_proj="w_out", sb_fwd="w_up", mm_up="w_down")
SCATTER_ON = dict(mm_dn2=("w_down",), sb_bwd=("w_up", "w_out"))


class _Sharded(_Whole):
    def __init__(self, shards, c_id, s_id):
        self.shards, self.c_id, self.s_id = shards, c_id, s_id
        self.full, self.grads, self.r1, self.p, self.r2 = {}, {}, {}, {}, {}
        self.placed = {n: _cast_place(shards[n], c_id, s_id, COL_SHARDED[n], "place_" + n) for n in BIG}
        self.full["w_in"], = _run_exchange(self._gather("w_in"), "gather_w_in")

    def _gather(self, n):
        return _gather_exchange(self.placed[n], *self.shards[n].shape, COL_SHARDED[n])

    def carry(self, host):
        if host in GATHER_ON:
            return [self._gather(GATHER_ON[host])]
        return [_scatter_exchange(self.p[n]) for n in SCATTER_ON.get(host, ())]

    def took(self, host, outs):
        if host in GATHER_ON:
            self.full[GATHER_ON[host]], = outs
        for n, r2 in zip(SCATTER_ON.get(host, ()), outs):
            self.r2[n] = r2

    def grad(self, n, g):
        rows, cols = self.shards[n].shape
        self.grads[n] = g
        self.r1[n] = _rs_pair(g, COL_SHARDED[n], rows, cols, "rs_pair_" + n)
        self.p[n] = _rs_add(g, self.r1[n], self.c_id, self.s_id, COL_SHARDED[n], rows, cols, "rs_add_" + n)
        if not any(n in ns for ns in SCATTER_ON.values()):
            self.r2[n], = _run_exchange(_scatter_exchange(self.p[n]), "rs_scatter_" + n)

    def reduced(self, n):
        rows, cols = self.shards[n].shape
        t = _rs_total(self.grads[n], self.r1[n], self.r2[n], self.c_id, self.s_id, COL_SHARDED[n], rows, cols,
                      "rs_total_" + n)
        return _rs_exchange(t, "rs_exchange_" + n)


def _adamw_vals(w, g, m, v):
    m = ADAM_B1 * m + (1.0 - ADAM_B1) * g
    v = ADAM_B2 * v + (1.0 - ADAM_B2) * (g * g)
    m_hat = m / (1.0 - ADAM_B1 ** ADAM_STEP)
    v_hat = v / (1.0 - ADAM_B2 ** ADAM_STEP)
    delta = -ADAM_LR * (m_hat / (jnp.sqrt(v_hat) + ADAM_EPS) + ADAM_WD * w)
    return delta, m, v


def _adamw(w, g, m, v, name):
    rows, cols = w.shape
    tr = max(d for d in _divisors(rows, 8) if d * cols * 4 <= 2 * 2 ** 20)

    def kern(w_ref, g_ref, m_ref, v_ref, d_ref, mo_ref, vo_ref):
        d_ref[...], mo_ref[...], vo_ref[...] = _adamw_vals(w_ref[...], g_ref[...], m_ref[...], v_ref[...])

    spec = pl.BlockSpec((tr, cols), lambda i: (i, 0))
    return pl.pallas_call(
        kern, name=name, grid=(rows // tr,), in_specs=[spec] * 4, out_specs=[spec] * 3,
        out_shape=[jax.ShapeDtypeStruct((rows, cols), F32)] * 3,
        compiler_params=_cp(("parallel",)),
    )(w, g, m, v)


def _pack(arrs):
    flat = []
    for a in arrs:
        a = a.reshape(-1)
        flat.append(jnp.pad(a, (0, (-a.shape[0]) % 1024)))
    return jnp.concatenate(flat).reshape(-1, 128)


def _unpack(slab, shapes):
    out, off = [], 0
    flat = slab.reshape(slab.shape[:-2] + (-1,))
    for shp in shapes:
        n = math.prod(shp)
        out.append(flat[..., off:off + n].reshape(slab.shape[:-2] + tuple(shp)))
        off += n + (-n) % 1024
    return out


SMALL = ("meta_tokens", "attn_pre_norm_w", "ret_gn_w", "sb_norm_w", "attn_post_norm_w", "ffn_pre_norm_w",
         "conv_w", "conv_b", "ffn_post_norm_w")
ORDER = ("meta_tokens", "attn_pre_norm_w", "w_in", "ret_gn_w", "sb_norm_w", "w_out", "attn_post_norm_w",
         "ffn_pre_norm_w", "w_up", "conv_w", "conv_b", "w_down", "ffn_post_norm_w")


def kernel(x, meta_tokens, attn_pre_norm_w, w_in, ret_gn_w, sb_norm_w, w_out, attn_post_norm_w, ffn_pre_norm_w, w_up, conv_w, conv_b, w_down, ffn_post_norm_w, loss_target, m_meta_tokens, m_attn_pre_norm_w, m_w_in, m_ret_gn_w, m_sb_norm_w, m_w_out, m_attn_post_norm_w, m_ffn_pre_norm_w, m_w_up, m_conv_w, m_conv_b, m_w_down, m_ffn_post_norm_w, v_meta_tokens, v_attn_pre_norm_w, v_w_in, v_ret_gn_w, v_sb_norm_w, v_w_out, v_attn_post_norm_w, v_ffn_pre_norm_w, v_w_up, v_conv_w, v_conv_b, v_w_down, v_ffn_post_norm_w):
    args = dict(locals())
    w = {n: args[n] for n in ORDER}
    m = {n: args["m_" + n] for n in ORDER}
    v = {n: args["v_" + n] for n in ORDER}
    xi, yi, ci = _coords()
    shard_id = 2 * xi + yi
    c_id = ci.astype(jnp.int32).reshape(1)
    s_id = shard_id.astype(jnp.int32).reshape(1)
    d = x.shape[-1]

    mshape, cshape = w["meta_tokens"].shape, w["conv_w"][0].shape
    got = _unpack(_gather4_small(_pack([w["meta_tokens"], w["conv_w"][0]])), [mshape, cshape])
    meta_full = jnp.moveaxis(got[0], 0, 1).reshape(N_META, d)
    conv_w_full = jnp.moveaxis(got[1], 0, 1).reshape(CONV_W, -1)

    net = _Sharded({n: w[n][0] for n in BIG}, c_id, s_id)
    grad_x, g_small = _local_step(
        x[0], loss_target[0], meta_full, net, conv_w_full, w["conv_b"], w["attn_pre_norm_w"], w["ret_gn_w"],
        w["sb_norm_w"], w["attn_post_norm_w"], w["ffn_pre_norm_w"], w["ffn_post_norm_w"])

    names = ("loss", "meta", "pre1", "gn", "sb", "post1", "pre2", "conv_w", "conv_b", "post2")
    tot = _unpack(_allreduce8_small(_pack([g_small[n] for n in names])), [g_small[n].shape for n in names])
    tot = dict(zip(names, tot))
    loss = tot["loss"][0, 0]
    mcols, ccols = mshape[1], cshape[1]
    grads = {
        "meta_tokens": lax.dynamic_slice_in_dim(tot["meta"], shard_id * mcols, mcols, axis=1),
        "attn_pre_norm_w": tot["pre1"], "ret_gn_w": tot["gn"], "sb_norm_w": tot["sb"],
        "attn_post_norm_w": tot["post1"], "ffn_pre_norm_w": tot["pre2"],
        "conv_w": lax.dynamic_slice_in_dim(tot["conv_w"], shard_id * ccols, ccols, axis=1)[None],
        "conv_b": tot["conv_b"], "ffn_post_norm_w": tot["post2"],
    }

    for n in BIG:
        grads[n] = net.reduced(n)[None]

    delta, new_m, new_v = {}, {}, {}
    for n in BIG:
        dl, mo, vo = _adamw(w[n][0], grads[n][0], m[n][0], v[n][0], "adamw_" + n)
        delta[n], new_m[n], new_v[n] = dl[None], mo[None], vo[None]
    shapes = [w[n].shape for n in SMALL]
    packed = [_pack([src[n] for n in SMALL]) for src in (w, grads, m, v)]
    outs = _adamw(*packed, "adamw_small")
    for dst, slab in zip((delta, new_m, new_v), outs):
        for n, a in zip(SMALL, _unpack(slab, shapes)):
            dst[n] = a

    return (loss, grad_x[None], *[grads[n] for n in ORDER], *[delta[n] for n in ORDER],
            *[new_m[n] for n in ORDER], *[new_v[n] for n in ORDER])
```

```python
import functools
import math

import jax
import jax.numpy as jnp
from jax import lax
from jax.experimental import pallas as pl
from jax.experimental.pallas import tpu as pltpu

F32 = jnp.float32
BF16 = jnp.bfloat16
MESH = pl.DeviceIdType.MESH

EPS = 1e-6
ROPE_BASE = 10000.0
N_META = 16
CHUNK = 128
N_PAD = CHUNK - N_META
RET_DK = 256
SB_DH = 128
CONV_W = 3

ADAM_LR = 0.001
ADAM_B1 = 0.9
ADAM_B2 = 0.999
ADAM_EPS = 1e-08
ADAM_WD = 0.01
ADAM_STEP = 10

V7X_VMEM_BYTES = 64 * 2 ** 20
VMEM_LIMIT = V7X_VMEM_BYTES - 8 * 2 ** 20
MM_BUDGET = 40 * 2 ** 20
V7X_BF16_FLOPS = 0.9e15
V7X_HBM_BPS = 2.0e12
GRID_STEP_S = 0.25e-6

NN = (((1,), (0,)), ((), ()))
NT = (((1,), (1,)), ((), ()))
TN = (((0,), (0,)), ((), ()))


def _cp(sem, vmem=VMEM_LIMIT):
    return pltpu.CompilerParams(dimension_semantics=sem, vmem_limit_bytes=vmem)


def _dot(a, b, dims=NN):
    return lax.dot_general(a, b, dims, preferred_element_type=F32)


def _sigmoid(x):
    return 1.0 / (1.0 + jnp.exp(-x))


def _divisors(n, align):
    return [d for d in range(align, n + 1, align) if n % d == 0]


def _mm_tiles(mode, m, n, k, out_bytes):
    best = None
    tms = [d for d in _divisors(m, 128 if mode == "tn" else 16) if d >= 128]
    tns = [d for d in _divisors(n, 256)] or [d for d in _divisors(n, 128)]
    tks = [d for d in _divisors(k, 128) if d >= 128]
    flops = 2.0 * m * n * k
    for tm in tms:
        for tn in tns:
            if tm * tn > 2112 * 1024:
                continue
            for tk in tks:
                nk = k // tk
                foot = 4 * (tm * tk + tk * tn) + 2 * tm * tn * out_bytes
                if nk > 1:
                    foot += 4 * tm * tn
                if foot > MM_BUDGET:
                    continue
                steps = (m // tm) * (n // tn) * nk
                for swap in (False, True):
                    if nk > 1:
                        traffic = 2.0 * (n // tn) * m * k + 2.0 * (m // tm) * n * k
                    elif swap:
                        traffic = 2.0 * n * k + 2.0 * (n // tn) * m * k
                    else:
                        traffic = 2.0 * m * k + 2.0 * (m // tm) * n * k
                    traffic += out_bytes * m * n
                    t = max(flops / V7X_BF16_FLOPS, traffic / V7X_HBM_BPS) + steps * GRID_STEP_S
                    if best is None or t < best[0]:
                        best = (t, tm, tn, tk, swap)
    assert best is not None, (mode, m, n, k)
    return best[1:]


class _Exchange:
    def __init__(self, operands, out_shapes, aliases, nsem, start, finish):
        self.operands, self.out_shapes, self.aliases, self.nsem = operands, out_shapes, aliases, nsem
        self.start, self.finish = start, finish


def _carried(exchanges, in_base, out_base):
    ins = [a for e in exchanges for a in e.operands]
    outs = [s for e in exchanges for s in e.out_shapes]
    sems = [pltpu.SemaphoreType.DMA((e.nsem,)) for e in exchanges for _ in (0, 1)]
    alias, i, o = {}, in_base, out_base
    for e in exchanges:
        alias.update({i + k: o + v for k, v in e.aliases.items()})
        i, o = i + len(e.operands), o + len(e.out_shapes)
    return ins, outs, sems, alias


def _run_carried(exchanges, phase, in_refs, out_refs, sem_refs):
    i = o = 0
    for n, e in enumerate(exchanges):
        ni, no = len(e.operands), len(e.out_shapes)
        getattr(e, phase)(in_refs[i:i + ni], out_refs[o:o + no], sem_refs[2 * n], sem_refs[2 * n + 1])
        i, o = i + ni, o + no


def _run_exchange(e, name):
    ins, outs, sems, alias = _carried([e], 0, 0)

    def kern(*refs):
        in_refs, out_refs, sem_refs = refs[:len(ins)], refs[len(ins):len(ins) + len(outs)], refs[len(ins) + len(outs):]
        _run_carried([e], "start", in_refs, out_refs, sem_refs)
        _run_carried([e], "finish", in_refs, out_refs, sem_refs)

    return pl.pallas_call(
        kern, name=name, in_specs=[ANY] * len(ins), out_specs=[ANY] * len(outs), out_shape=outs,
        input_output_aliases=alias, scratch_shapes=sems,
    )(*ins)


def _matmul(a, b, mode, out_dtype, name, carry=()):
    if mode == "nn":
        (m, k), (k2, n) = a.shape, b.shape
    elif mode == "nt":
        (m, k), (n, k2) = a.shape, b.shape
    else:
        (k, m), (k2, n) = a.shape, b.shape
    assert k == k2 and a.dtype == BF16 and b.dtype == BF16
    tm, tn, tk, swap = _mm_tiles(mode, m, n, k, jnp.dtype(out_dtype).itemsize)
    nk = k // tk
    dims = {"nn": NN, "nt": NT, "tn": TN}[mode]

    def ij(g0, g1):
        return (g1, g0) if swap else (g0, g1)

    if mode == "tn":
        a_spec = pl.BlockSpec((tk, tm), lambda g0, g1, kk: (kk, ij(g0, g1)[0]))
    else:
        a_spec = pl.BlockSpec((tm, tk), lambda g0, g1, kk: (ij(g0, g1)[0], kk))
    if mode == "nt":
        b_spec = pl.BlockSpec((tn, tk), lambda g0, g1, kk: (ij(g0, g1)[1], kk))
    else:
        b_spec = pl.BlockSpec((tk, tn), lambda g0, g1, kk: (kk, ij(g0, g1)[1]))
    o_spec = pl.BlockSpec((tm, tn), lambda g0, g1, kk: ij(g0, g1))

    grid = (n // tn, m // tm, nk) if swap else (m // tm, n // tn, nk)
    carry = list(carry)
    x_in, x_out, x_sems, alias = _carried(carry, 2, 1)
    acc_shapes = [pltpu.VMEM((tm, tn), F32)] if nk > 1 else []

    def kern(a_ref, b_ref, *rest):
        x_in_refs, o_ref = rest[:len(x_in)], rest[len(x_in)]
        x_out_refs = rest[len(x_in) + 1:len(x_in) + 1 + len(x_out)]
        tail = rest[len(x_in) + 1 + len(x_out):]
        acc, sem_refs = tail[:len(acc_shapes)], tail[len(acc_shapes):]
        pid = [pl.program_id(ax) for ax in range(3)]
        if carry:
            @pl.when((pid[0] == 0) & (pid[1] == 0) & (pid[2] == 0))
            def _():
                _run_carried(carry, "start", x_in_refs, x_out_refs, sem_refs)

        prod = _dot(a_ref[...], b_ref[...], dims)
        if nk == 1:
            o_ref[...] = prod.astype(out_dtype)
        else:
            kk = pid[2]

            @pl.when(kk == 0)
            def _():
                acc[0][...] = prod

            @pl.when(kk > 0)
            def _():
                acc[0][...] += prod

            @pl.when(kk == nk - 1)
            def _():
                o_ref[...] = acc[0][...].astype(out_dtype)

        if carry:
            @pl.when((pid[0] == grid[0] - 1) & (pid[1] == grid[1] - 1) & (pid[2] == grid[2] - 1))
            def _():
                _run_carried(carry, "finish", x_in_refs, x_out_refs, sem_refs)

    sem = ("arbitrary",) * 3 if carry else ("parallel", "parallel", "arbitrary")
    res = pl.pallas_call(
        kern, name=name, grid=grid, in_specs=[a_spec, b_spec] + [ANY] * len(x_in),
        out_specs=[o_spec] + [ANY] * len(x_out),
        out_shape=[jax.ShapeDtypeStruct((m, n), out_dtype)] + x_out,
        input_output_aliases=alias, scratch_shapes=acc_shapes + x_sems,
        compiler_params=_cp(sem),
    )(a, b, *x_in)
    return (res[0], res[1:]) if carry else res[0]


def _rms_fwd(x, w):
    r = lax.rsqrt(jnp.mean(x * x, axis=-1, keepdims=True) + EPS)
    return x * r * w


def _rms_bwd(x, w, g):
    r = lax.rsqrt(jnp.mean(x * x, axis=-1, keepdims=True) + EPS)
    gw = g * w
    dx = r * gw - x * (r * r * r * jnp.mean(gw * x, axis=-1, keepdims=True))
    return dx, g * (x * r)


def _row_spec(d):
    return pl.BlockSpec((CHUNK, d), lambda i: (i, 0))


def _vec_spec(d):
    return pl.BlockSpec((1, d), lambda i: (0, 0))


def _prenorm(h0, w):
    tp, d = h0.shape

    def kern(h_ref, w_ref, o_ref):
        o_ref[...] = _rms_fwd(h_ref[...], w_ref[...]).astype(BF16)

    return pl.pallas_call(
        kern, name="prenorm1", grid=(tp // CHUNK,),
        in_specs=[_row_spec(d), _vec_spec(d)], out_specs=_row_spec(d),
        out_shape=jax.ShapeDtypeStruct((tp, d), BF16),
        compiler_params=_cp(("parallel",)),
    )(h0, w)


def _mid_fwd(h0, a, w_post, w_pre):
    tp, d = h0.shape

    def kern(h_ref, a_ref, wp_ref, wq_ref, h1_ref, n2_ref):
        h1 = h_ref[...] + _rms_fwd(a_ref[...], wp_ref[...])
        h1_ref[...] = h1
        n2_ref[...] = _rms_fwd(h1, wq_ref[...]).astype(BF16)

    return pl.pallas_call(
        kern, name="mid_fwd", grid=(tp // CHUNK,),
        in_specs=[_row_spec(d), _row_spec(d), _vec_spec(d), _vec_spec(d)],
        out_specs=[_row_spec(d), _row_spec(d)],
        out_shape=[jax.ShapeDtypeStruct((tp, d), F32), jax.ShapeDtypeStruct((tp, d), BF16)],
        compiler_params=_cp(("parallel",)),
    )(h0, a, w_post, w_pre)


def _loss_bwd(h1, f, tgt, w_post):
    tp, d = h1.shape

    def kern(h_ref, f_ref, t_ref, w_ref, dy_ref, df_ref, loss_ref, dw_ref):
        i = pl.program_id(0)

        @pl.when(i == 0)
        def _():
            dy_ref[...] = jnp.zeros_like(dy_ref)
            df_ref[...] = jnp.zeros_like(df_ref)
            loss_ref[...] = jnp.zeros_like(loss_ref)
            dw_ref[...] = jnp.zeros_like(dw_ref)

        @pl.when(i > 0)
        def _():
            fv = f_ref[...]
            w = w_ref[...]
            err = h_ref[...] + _rms_fwd(fv, w) - t_ref[...]
            loss_ref[...] += 0.5 * jnp.sum(jnp.mean(err * err, axis=-1, keepdims=True))
            dy = err * (1.0 / d)
            dy_ref[...] = dy
            dfv, dwr = _rms_bwd(fv, w, dy)
            df_ref[...] = dfv.astype(BF16)
            dw_ref[...] += jnp.sum(dwr, axis=0, keepdims=True)

    return pl.pallas_call(
        kern, name="loss_bwd", grid=(tp // CHUNK,),
        in_specs=[_row_spec(d), _row_spec(d),
                  pl.BlockSpec((CHUNK, d), lambda i: (jnp.maximum(i - 1, 0), 0)), _vec_spec(d)],
        out_specs=[_row_spec(d), _row_spec(d), pl.BlockSpec((1, 128), lambda i: (0, 0)), _vec_spec(d)],
        out_shape=[jax.ShapeDtypeStruct((tp, d), F32), jax.ShapeDtypeStruct((tp, d), BF16),
                   jax.ShapeDtypeStruct((1, 128), F32), jax.ShapeDtypeStruct((1, d), F32)],
        compiler_params=_cp(("arbitrary",)),
    )(h1, f, tgt, w_post)


def _mid_bwd(dy, dn2, h1, a, w_pre, w_post):
    tp, d = h1.shape

    def kern(dy_ref, dn_ref, h_ref, a_ref, wq_ref, wp_ref, dh_ref, da_ref, dwq_ref, dwp_ref):
        @pl.when(pl.program_id(0) == 0)
        def _():
            dwq_ref[...] = jnp.zeros_like(dwq_ref)
            dwp_ref[...] = jnp.zeros_like(dwp_ref)

        dx, dwq = _rms_bwd(h_ref[...], wq_ref[...], dn_ref[...])
        dh = dy_ref[...] + dx
        dh_ref[...] = dh
        da, dwp = _rms_bwd(a_ref[...], wp_ref[...], dh)
        da_ref[...] = da.astype(BF16)
        dwq_ref[...] += jnp.sum(dwq, axis=0, keepdims=True)
        dwp_ref[...] += jnp.sum(dwp, axis=0, keepdims=True)

    return pl.pallas_call(
        kern, name="mid_bwd", grid=(tp // CHUNK,),
        in_specs=[_row_spec(d)] * 4 + [_vec_spec(d)] * 2,
        out_specs=[_row_spec(d), _row_spec(d), _vec_spec(d), _vec_spec(d)],
        out_shape=[jax.ShapeDtypeStruct((tp, d), F32), jax.ShapeDtypeStruct((tp, d), BF16),
                   jax.ShapeDtypeStruct((1, d), F32), jax.ShapeDtypeStruct((1, d), F32)],
        compiler_params=_cp(("arbitrary",)),
    )(dy, dn2, h1, a, w_pre, w_post)


def _pre_bwd(dh1, dn1, h0, w_pre):
    tp, d = h0.shape
    s = tp - CHUNK

    def kern(dh_ref, dn_ref, h_ref, w_ref, gx_ref, gm_ref, dw_ref):
        i = pl.program_id(0)
        dx, dwr = _rms_bwd(h_ref[...], w_ref[...], dn_ref[...])
        dh0 = dh_ref[...] + dx
        gx_ref[...] = dh0

        @pl.when(i == 0)
        def _():
            gm_ref[...] = dh0[N_PAD:, :]
            dw_ref[...] = jnp.zeros_like(dw_ref)

        dw_ref[...] += jnp.sum(dwr, axis=0, keepdims=True)

    return pl.pallas_call(
        kern, name="pre_bwd", grid=(tp // CHUNK,),
        in_specs=[_row_spec(d)] * 3 + [_vec_spec(d)],
        out_specs=[pl.BlockSpec((CHUNK, d), lambda i: (jnp.maximum(i - 1, 0), 0)),
                   pl.BlockSpec((N_META, d), lambda i: (0, 0)), _vec_spec(d)],
        out_shape=[jax.ShapeDtypeStruct((s, d), F32), jax.ShapeDtypeStruct((N_META, d), F32),
                   jax.ShapeDtypeStruct((1, d), F32)],
        compiler_params=_cp(("arbitrary",)),
    )(dh1, dn1, h0, w_pre)


def _ffn_cols(dff):
    return dff // 2 if dff % 256 == 0 else dff


def _ffn_fwd(gu, conv_w, conv_b):
    tp, two_dff = gu.shape
    dff = two_dff // 2
    tc = _ffn_cols(dff)
    nj = dff // tc
    r8 = CHUNK // 8

    def kern(g_ref, gp_ref, u_ref, w_ref, b_ref, o_ref):
        i = pl.program_id(1)
        prev = gp_ref[...] * (i > 0).astype(F32)
        ext = jnp.concatenate([prev, g_ref[...]], axis=0)
        w = w_ref[...]
        conv = (b_ref[...] + w[0:1] * pltpu.roll(ext, 2, 0)[8:] + w[1:2] * pltpu.roll(ext, 1, 0)[8:]
                + w[2:3] * ext[8:])
        o_ref[...] = (conv * _sigmoid(conv) * u_ref[...]).astype(BF16)

    return pl.pallas_call(
        kern, name="ffn_fwd", grid=(nj, tp // CHUNK),
        in_specs=[pl.BlockSpec((CHUNK, tc), lambda j, i: (i, j)),
                  pl.BlockSpec((8, tc), lambda j, i: (jnp.maximum(i * r8 - 1, 0), j)),
                  pl.BlockSpec((CHUNK, tc), lambda j, i: (i, j + nj)),
                  pl.BlockSpec((CONV_W, tc), lambda j, i: (0, j)),
                  pl.BlockSpec((1, tc), lambda j, i: (0, j))],
        out_specs=pl.BlockSpec((CHUNK, tc), lambda j, i: (i, j)),
        out_shape=jax.ShapeDtypeStruct((tp, dff), BF16),
        compiler_params=_cp(("parallel", "parallel")),
    )(gu, gu, gu, conv_w, conv_b)


def _ffn_bwd(gu, dact, conv_w, conv_b):
    tp, two_dff = gu.shape
    dff = two_dff // 2
    tc = _ffn_cols(dff)
    nj = dff // tc
    ni = tp // CHUNK
    r8 = CHUNK // 8

    def kern(g_ref, gp_ref, gn_ref, u_ref, un_ref, d_ref, dn_ref, w_ref, b_ref, dg_ref, du_ref, st_ref):
        i = pl.program_id(1)

        @pl.when(i == 0)
        def _():
            st_ref[...] = jnp.zeros_like(st_ref)

        first = (i > 0).astype(F32)
        last = (i < ni - 1).astype(F32)
        gate = g_ref[...]
        ext = jnp.concatenate([gp_ref[...] * first, gate, gn_ref[...]], axis=0)
        w = w_ref[...]
        r1 = pltpu.roll(ext, 1, 0)
        r2 = pltpu.roll(ext, 2, 0)
        conv = (b_ref[...] + w[0:1] * r2 + w[1:2] * r1 + w[2:3] * ext)[8:]
        up = jnp.concatenate([u_ref[...], un_ref[...]], axis=0)
        da = jnp.concatenate([d_ref[...], dn_ref[...] * last], axis=0)
        sg = _sigmoid(conv)
        dc = da * up * (sg * (1.0 + conv * (1.0 - sg)))
        du_ref[...] = (d_ref[...] * (conv * sg)[:CHUNK]).astype(BF16)
        n = CHUNK + 8
        dgate = w[2:3] * dc + w[1:2] * pltpu.roll(dc, n - 1, 0) + w[0:1] * pltpu.roll(dc, n - 2, 0)
        dg_ref[...] = dgate[:CHUNK].astype(BF16)
        dcm = dc[:CHUNK]
        s0 = jnp.sum(dcm * r2[8:8 + CHUNK], axis=0, keepdims=True)
        s1 = jnp.sum(dcm * r1[8:8 + CHUNK], axis=0, keepdims=True)
        s2 = jnp.sum(dcm * gate, axis=0, keepdims=True)
        s3 = jnp.sum(dcm, axis=0, keepdims=True)
        row = lax.broadcasted_iota(jnp.int32, (8, tc), 0)
        st_ref[...] += jnp.where(row == 0, s0, jnp.where(row == 1, s1, jnp.where(row == 2, s2,
                                 jnp.where(row == 3, s3, 0.0))))

    main = lambda off: pl.BlockSpec((CHUNK, tc), lambda j, i: (i, j + off))
    nxt = lambda off: pl.BlockSpec((8, tc), lambda j, i: (jnp.minimum((i + 1) * r8, ni * r8 - 1), j + off))
    return pl.pallas_call(
        kern, name="ffn_bwd", grid=(nj, ni),
        in_specs=[main(0), pl.BlockSpec((8, tc), lambda j, i: (jnp.maximum(i * r8 - 1, 0), j)), nxt(0),
                  main(nj), nxt(nj), main(0), nxt(0),
                  pl.BlockSpec((CONV_W, tc), lambda j, i: (0, j)), pl.BlockSpec((1, tc), lambda j, i: (0, j))],
        out_specs=[main(0), main(0), pl.BlockSpec((8, tc), lambda j, i: (0, j))],
        out_shape=[jax.ShapeDtypeStruct((tp, dff), BF16), jax.ShapeDtypeStruct((tp, dff), BF16),
                   jax.ShapeDtypeStruct((8, dff), F32)],
        compiler_params=_cp(("parallel", "arbitrary")),
    )(gu, gu, gu, gu, gu, dact, dact, conv_w, conv_b)


def _rot(x, cs, sn):
    x1, x2 = x[:, :128], x[:, 128:]
    return jnp.concatenate([x1 * cs - x2 * sn, x1 * sn + x2 * cs], axis=1)


def _rot_t(x, cs, sn):
    x1, x2 = x[:, :128], x[:, 128:]
    return jnp.concatenate([x1 * cs + x2 * sn, x2 * cs - x1 * sn], axis=1)


def _ret_tables(rh):
    lg = jnp.log(1.0 - 2.0 ** (-5.0 - jnp.arange(rh, dtype=F32)))
    idx = jnp.arange(CHUNK, dtype=F32)
    diff = idx[:, None] - idx[None, :]
    intra = jnp.where(diff[None] >= 0, jnp.exp(jnp.maximum(diff, 0.0)[None] * lg[:, None, None]), 0.0)
    qdec = jnp.exp((idx[None, :] + 1.0) * lg[:, None])[..., None]
    kdec = jnp.exp((CHUNK - 1.0 - idx[None, :]) * lg[:, None])[..., None]
    cdec = jnp.exp(CHUNK * lg)[:, None, None]
    return intra, qdec, kdec, cdec


def _ret_specs(rh, nc, rev):
    cc = (lambda c: nc - 1 - c) if rev else (lambda c: c)
    col = lambda sec: pl.BlockSpec((CHUNK, RET_DK), lambda h, c: (cc(c), sec * rh + h))
    tab = [pl.BlockSpec((CHUNK, 128), lambda h, c: (cc(c), 0))] * 2
    dec = [pl.BlockSpec((1, CHUNK, CHUNK), lambda h, c: (h, 0, 0)),
           pl.BlockSpec((1, CHUNK, 1), lambda h, c: (h, 0, 0)),
           pl.BlockSpec((1, CHUNK, 1), lambda h, c: (h, 0, 0)),
           pl.BlockSpec((1, 1, 1), lambda h, c: (h, 0, 0))]
    hw = pl.BlockSpec((1, RET_DK), lambda h, c: (0, h))
    hcol = pl.BlockSpec((CHUNK, RET_DK), lambda h, c: (cc(c), h))
    st = pl.BlockSpec((1, 1, RET_DK, RET_DK), lambda h, c: (h, cc(c), 0, 0))
    return col, tab, dec, hw, hcol, st


def _ret_fwd(proj, cos, sin, tables, gnw, rh):
    tp = proj.shape[0]
    nc = tp // CHUNK
    col, tab, dec, hw, hcol, st = _ret_specs(rh, nc, False)

    def kern(q_ref, k_ref, v_ref, g_ref, cos_ref, sin_ref, in_ref, qd_ref, kd_ref, cd_ref, w_ref,
             out_ref, ry_ref, st_ref, state):
        @pl.when(pl.program_id(1) == 0)
        def _():
            state[...] = jnp.zeros_like(state)

        cs, sn = cos_ref[...], sin_ref[...]
        q = (_rot(q_ref[...], cs, sn) * (RET_DK ** -0.5)).astype(BF16)
        kf = _rot(k_ref[...], cs, sn)
        k = kf.astype(BF16)
        v = v_ref[...].astype(BF16)
        s_old = state[...]
        s_b = s_old.astype(BF16)
        st_ref[0, 0] = s_b
        sc = _dot(q, k, NT) * in_ref[0]
        ry = _dot(sc.astype(BF16), v) + _dot(q, s_b) * qd_ref[0]
        state[...] = s_old * cd_ref[0] + _dot((kf * kd_ref[0]).astype(BF16), v, TN)
        ry_ref[...] = ry
        g = g_ref[...]
        out_ref[...] = (g * _sigmoid(g) * _rms_fwd(ry, w_ref[...])).astype(BF16)

    return pl.pallas_call(
        kern, name="ret_fwd", grid=(rh, nc),
        in_specs=[col(0), col(1), col(2), col(3)] + tab + dec + [hw],
        out_specs=[hcol, hcol, st],
        out_shape=[jax.ShapeDtypeStruct((tp, rh * RET_DK), BF16), jax.ShapeDtypeStruct((tp, rh * RET_DK), F32),
                   jax.ShapeDtypeStruct((rh, nc, RET_DK, RET_DK), BF16)],
        scratch_shapes=[pltpu.VMEM((RET_DK, RET_DK), F32)],
        compiler_params=_cp(("parallel", "arbitrary")),
    )(proj, proj, proj, proj, cos, sin, *tables, gnw)


def _ret_bwd(proj, dmix, ry_all, states, cos, sin, tables, gnw, rh):
    tp = proj.shape[0]
    nc = tp // CHUNK
    col, tab, dec, hw, hcol, st = _ret_specs(rh, nc, True)

    def kern(q_ref, k_ref, v_ref, g_ref, cos_ref, sin_ref, in_ref, qd_ref, kd_ref, cd_ref, w_ref,
             do_ref, ry_ref, st_ref, dq_ref, dk_ref, dv_ref, dg_ref, dw_ref, ds):
        @pl.when(pl.program_id(1) == 0)
        def _():
            ds[...] = jnp.zeros_like(ds)
            dw_ref[...] = jnp.zeros_like(dw_ref)

        cs, sn = cos_ref[...], sin_ref[...]
        qf = _rot(q_ref[...], cs, sn) * (RET_DK ** -0.5)
        q = qf.astype(BF16)
        kf = _rot(k_ref[...], cs, sn)
        k = kf.astype(BF16)
        vf = v_ref[...]
        v = vf.astype(BF16)
        g = g_ref[...]
        ry = ry_ref[...]
        w = w_ref[...]
        dout = do_ref[...]
        sg = _sigmoid(g)
        dhn = dout * (g * sg)
        dry, dwr = _rms_bwd(ry, w, dhn)
        dg_ref[...] = (dout * _rms_fwd(ry, w) * (sg * (1.0 + g * (1.0 - sg)))).astype(BF16)
        dw_ref[...] += jnp.sum(dwr, axis=0, keepdims=True)

        dmat = in_ref[0]
        qd, kd = qd_ref[0], kd_ref[0]
        dyb = dry.astype(BF16)
        p = (_dot(q, k, NT) * dmat).astype(BF16)
        dp = (_dot(dyb, v, NT) * dmat).astype(BF16)
        ady = (dry * qd).astype(BF16)
        ds_old = ds[...]
        ds_b = ds_old.astype(BF16)
        dq = _dot(dp, k) + _dot(ady, st_ref[0, 0], NT)
        dk = _dot(dp, q, TN) + _dot(v, ds_b, NT) * kd
        dv = _dot(p, dyb, TN) + _dot((kf * kd).astype(BF16), ds_b)
        ds[...] = ds_old * cd_ref[0] + _dot(q, ady, TN)
        dq_ref[...] = _rot_t(dq * (RET_DK ** -0.5), cs, sn).astype(BF16)
        dk_ref[...] = _rot_t(dk, cs, sn).astype(BF16)
        dv_ref[...] = dv.astype(BF16)

    rw = rh * RET_DK
    outs = pl.pallas_call(
        kern, name="ret_bwd", grid=(rh, nc),
        in_specs=[col(0), col(1), col(2), col(3)] + tab + dec + [hw, hcol, hcol, st],
        out_specs=[hcol, hcol, hcol, hcol, hw],
        out_shape=[jax.ShapeDtypeStruct((tp, rw), BF16)] * 4 + [jax.ShapeDtypeStruct((1, rw), F32)],
        scratch_shapes=[pltpu.VMEM((RET_DK, RET_DK), F32)],
        compiler_params=_cp(("parallel", "arbitrary")),
    )(proj, proj, proj, proj, cos, sin, *tables, gnw, dmix, ry_all, states)
    return outs


def _sb_tile(tp):
    return 3 * CHUNK if tp % (3 * CHUNK) == 0 else CHUNK


def _sb_block(q, k, qpos, kb, scale):
    z = _dot(q, k, NT) * scale
    kpos = kb * CHUNK + lax.broadcasted_iota(jnp.int32, qpos.shape, 1)
    mask = (kpos < qpos) & (kpos >= N_PAD)
    t = jnp.log(1.0 + jnp.exp(-jnp.abs(z)))
    lb = jnp.minimum(z, 0.0) - t
    lk = jnp.where(mask, -jnp.maximum(z, 0.0) - t, 0.0)
    return mask, lb, lk


def _tri_sum(x, tri):
    hi = x.astype(BF16)
    lo = (x - hi.astype(F32)).astype(BF16)
    return _dot(hi, tri) + _dot(lo, tri)


def _tri(strict_upper):
    r = lax.broadcasted_iota(jnp.int32, (CHUNK, CHUNK), 0)
    c = lax.broadcasted_iota(jnp.int32, (CHUNK, CHUNK), 1)
    return ((r > c) if strict_upper else (r < c)).astype(BF16)


def _sb_fwd(proj, sbw, sh, col0, carry=()):
    tp = proj.shape[0]
    tq = _sb_tile(tp)
    nsub, nq = tq // CHUNK, tp // tq
    assert tp // CHUNK <= 128
    scale = 1.0 / math.sqrt(SB_DH)
    hq = pl.BlockSpec((tq, SB_DH), lambda h, i: (i, h))

    carry = list(carry)
    x_in, x_out, x_sems, alias = _carried(carry, 4, 3)

    def kern(q_ref, k_ref, v_ref, w_ref, *rest):
        x_in_refs, rest = rest[:len(x_in)], rest[len(x_in):]
        out_ref, sy_ref, ao_ref = rest[:3]
        x_out_refs, (a_run, k16, v16), sem_refs = rest[3:3 + len(x_out)], rest[3 + len(x_out):6 + len(x_out)], rest[6 + len(x_out):]
        h, i = pl.program_id(0), pl.program_id(1)
        if carry:
            @pl.when((h == 0) & (i == 0))
            def _():
                _run_carried(carry, "start", x_in_refs, x_out_refs, sem_refs)

        @pl.when(i == 0)
        def _():
            k16[...] = k_ref[...].astype(BF16)
            v16[...] = v_ref[...].astype(BF16)

        q = q_ref[...].astype(BF16)
        upper = _tri(True)
        lane = lax.broadcasted_iota(jnp.int32, (tq, CHUNK), 1)
        qpos = i * tq + lax.broadcasted_iota(jnp.int32, (tq, CHUNK), 0)
        a_run[...] = jnp.zeros_like(a_run)
        sy_ref[...] = jnp.zeros_like(sy_ref)
        ao_ref[...] = jnp.zeros_like(ao_ref)

        def body(jj, carry):
            a, acc, at = a_run[...], sy_ref[...], ao_ref[0]
            for sub in reversed(range(nsub)):
                kb = (i - jj) * nsub + sub
                rows = pl.ds(pl.multiple_of(kb * CHUNK, CHUNK), CHUNK)
                mask, lb, lk = _sb_block(q, k16[rows, :], qpos, kb, scale)
                wgt = jnp.where(mask, jnp.exp(lb + a + _tri_sum(lk, upper)), 0.0)
                acc = acc + _dot(wgt.astype(BF16), v16[rows, :])
                at = jnp.where(lane == kb, a, at)
                a = a + jnp.sum(lk, axis=1, keepdims=True)
            a_run[...], sy_ref[...], ao_ref[0] = a, acc, at
            return carry

        lax.fori_loop(0, i + 1, body, 0)
        out_ref[...] = _rms_fwd(sy_ref[...], w_ref[...]).astype(BF16)
        if carry:
            @pl.when((h == sh - 1) & (i == nq - 1))
            def _():
                _run_carried(carry, "finish", x_in_refs, x_out_refs, sem_refs)

    kv = lambda sec: pl.BlockSpec((tp, SB_DH), lambda h, i: (0, col0 + sec * sh + h))
    res = pl.pallas_call(
        kern, name="sb_fwd", grid=(sh, nq),
        in_specs=[pl.BlockSpec((tq, SB_DH), lambda h, i: (i, col0 + h)), kv(1), kv(2),
                  pl.BlockSpec((1, SB_DH), lambda h, i: (0, h))] + [ANY] * len(x_in),
        out_specs=[hq, hq, pl.BlockSpec((1, tq, 128), lambda h, i: (h, i, 0))] + [ANY] * len(x_out),
        out_shape=[jax.ShapeDtypeStruct((tp, sh * SB_DH), BF16), jax.ShapeDtypeStruct((tp, sh * SB_DH), F32),
                   jax.ShapeDtypeStruct((sh, tp, 128), F32)] + x_out,
        input_output_aliases=alias,
        scratch_shapes=[pltpu.VMEM((tq, CHUNK), F32), pltpu.VMEM((tp, SB_DH), BF16),
                        pltpu.VMEM((tp, SB_DH), BF16)] + x_sems,
        compiler_params=_cp(("arbitrary", "arbitrary") if carry else ("parallel", "arbitrary")),
    )(proj, proj, proj, sbw, *x_in)
    return res[:3], res[3:]


def _sb_bwd(proj, dmix, sy_all, aoff, sbw, sh, col0, dcol0, carry=()):
    tp = proj.shape[0]
    tq = _sb_tile(tp)
    nsub, nq = tq // CHUNK, tp // tq
    scale = 1.0 / math.sqrt(SB_DH)
    hq = pl.BlockSpec((tq, SB_DH), lambda h, i: (i, h))
    carry = list(carry)
    x_in, x_out, x_sems, alias = _carried(carry, 7, 4)

    def kern(q_ref, k_ref, v_ref, w_ref, do_ref, sy_ref, ao_ref, *rest):
        x_in_refs, rest = rest[:len(x_in)], rest[len(x_in):]
        dq_ref, dk_ref, dv_ref, dw_ref = rest[:4]
        x_out_refs, rest = rest[4:4 + len(x_out)], rest[4 + len(x_out):]
        (dk_acc, dv_acc, dq_acc, e_run, k16, v16), sem_refs = rest[:6], rest[6:]
        h, i = pl.program_id(0), pl.program_id(1)
        if carry:
            @pl.when((h == 0) & (i == 0))
            def _():
                _run_carried(carry, "start", x_in_refs, x_out_refs, sem_refs)

        @pl.when(i == 0)
        def _():
            dk_acc[...] = jnp.zeros_like(dk_acc)
            dv_acc[...] = jnp.zeros_like(dv_acc)
            dw_ref[...] = jnp.zeros_like(dw_ref)
            k16[...] = k_ref[...].astype(BF16)
            v16[...] = v_ref[...].astype(BF16)

        qf = q_ref[...]
        q = qf.astype(BF16)
        dsy, dwr = _rms_bwd(sy_ref[...], w_ref[...], do_ref[...])
        dw_ref[...] += jnp.sum(dwr, axis=0, keepdims=True)
        dsy_b = dsy.astype(BF16)
        q_t = qf.T.astype(BF16)
        dsy_t = dsy.T.astype(BF16)
        atile = ao_ref[0]
        upper = _tri(True)
        lower = _tri(False)
        lane = lax.broadcasted_iota(jnp.int32, (tq, CHUNK), 1)
        qpos = i * tq + lax.broadcasted_iota(jnp.int32, (tq, CHUNK), 0)

        e_run[...] = jnp.zeros_like(e_run)
        dq_acc[...] = jnp.zeros_like(dq_acc)

        def body(jj, carry):
            e_prev, dq = e_run[...], dq_acc[...]
            for sub in range(nsub):
                kb = jj * nsub + sub
                rows = pl.ds(pl.multiple_of(kb * CHUNK, CHUNK), CHUNK)
                k, v = k16[rows, :], v16[rows, :]
                mask, lb, lk = _sb_block(q, k, qpos, kb, scale)
                a = jnp.sum(jnp.where(lane == kb, atile, 0.0), axis=1, keepdims=True)
                wgt = jnp.where(mask, jnp.exp(lb + a + _tri_sum(lk, upper)), 0.0)
                e = wgt * _dot(dsy_b, v, NT)
                dv_acc[kb] += _dot(dsy_t, wgt.astype(BF16))
                sig = jnp.exp(lb)
                e_all = e_prev + _tri_sum(e, lower)
                dz = (jnp.where(mask, e * (1.0 - sig) - e_all * sig, 0.0) * scale).astype(BF16)
                dk_acc[kb] += _dot(q_t, dz)
                dq = dq + _dot(dz, k)
                e_prev = e_prev + jnp.sum(e, axis=1, keepdims=True)
            e_run[...], dq_acc[...] = e_prev, dq
            return carry

        lax.fori_loop(0, i + 1, body, 0)
        dq_ref[...] = dq_acc[...].astype(BF16)

        @pl.when(i == nq - 1)
        def _():
            def untranspose(kb, c):
                rows = pl.ds(pl.multiple_of(kb * CHUNK, CHUNK), CHUNK)
                dk_ref[rows, :] = dk_acc[kb].T.astype(BF16)
                dv_ref[rows, :] = dv_acc[kb].T.astype(BF16)
                return c

            lax.fori_loop(0, tp // CHUNK, untranspose, 0)

        if carry:
            @pl.when((h == sh - 1) & (i == nq - 1))
            def _():
                _run_carried(carry, "finish", x_in_refs, x_out_refs, sem_refs)

    kv = lambda sec: pl.BlockSpec((tp, SB_DH), lambda h, i: (0, col0 + sec * sh + h))
    hfull = pl.BlockSpec((tp, SB_DH), lambda h, i: (0, h))
    sw = sh * SB_DH
    acc_t = pltpu.VMEM((tp // CHUNK, SB_DH, CHUNK), F32)
    res = pl.pallas_call(
        kern, name="sb_bwd", grid=(sh, nq),
        in_specs=[pl.BlockSpec((tq, SB_DH), lambda h, i: (i, col0 + h)), kv(1), kv(2),
                  pl.BlockSpec((1, SB_DH), lambda h, i: (0, h)),
                  pl.BlockSpec((tq, SB_DH), lambda h, i: (i, dcol0 + h)), hq,
                  pl.BlockSpec((1, tq, 128), lambda h, i: (h, i, 0))] + [ANY] * len(x_in),
        out_specs=[hq, hfull, hfull, pl.BlockSpec((1, SB_DH), lambda h, i: (0, h))] + [ANY] * len(x_out),
        out_shape=[jax.ShapeDtypeStruct((tp, sw), BF16)] * 3 + [jax.ShapeDtypeStruct((1, sw), F32)] + x_out,
        input_output_aliases=alias,
        scratch_shapes=[acc_t, acc_t, pltpu.VMEM((tq, SB_DH), F32), pltpu.VMEM((tq, CHUNK), F32),
                        pltpu.VMEM((tp, SB_DH), BF16), pltpu.VMEM((tp, SB_DH), BF16)] + x_sems,
        compiler_params=_cp(("arbitrary", "arbitrary") if carry else ("parallel", "arbitrary")),
    )(proj, proj, proj, sbw, dmix, sy_all, aoff, *x_in)
    return res[:4], res[4:]


def _local_step(x, tgt, meta, net, conv_w, conv_b, pre1_w, gn_w, sb_w, post1_w, pre2_w, post2_w):
    s, d = x.shape

    def mm(host, a, b, mode, dtype):
        carry = net.carry(host)
        out = _matmul(a, b, mode, dtype, host, carry)
        if carry:
            out, extra = out
            net.took(host, extra)
        return out

    tp = s + CHUNK
    rh, sh = d // 512, d // 256
    rw = rh * RET_DK
    h0 = jnp.concatenate([jnp.zeros((N_PAD, d), F32), meta, x], axis=0)
    pos = jnp.arange(tp, dtype=F32) - N_PAD
    inv = ROPE_BASE ** (-jnp.arange(128, dtype=F32) / 128)
    ang = pos[:, None] * inv[None, :]
    cos, sin = jnp.cos(ang), jnp.sin(ang)
    tables = _ret_tables(rh)
    sb_col0 = 4 * rw // SB_DH

    n1 = _prenorm(h0, pre1_w)
    proj = mm("mm_proj", n1, net.weight("w_in"), "nn", F32)
    ret_out, ry, states = _ret_fwd(proj, cos, sin, tables, gn_w, rh)
    (sb_out, sy, aoff), extra = _sb_fwd(proj, sb_w, sh, sb_col0, net.carry("sb_fwd"))
    net.took("sb_fwd", extra)
    mixed = jnp.concatenate([ret_out, sb_out], axis=1)
    a = mm("mm_out", mixed, net.weight("w_out"), "nn", F32)
    h1, n2 = _mid_fwd(h0, a, post1_w, pre2_w)
    gu = mm("mm_up", n2, net.weight("w_up"), "nn", F32)
    act = _ffn_fwd(gu, conv_w, conv_b)
    f = mm("mm_down", act, net.weight("w_down"), "nn", F32)

    dy, d_f, loss, dw_post2 = _loss_bwd(h1, f, tgt, post2_w)
    d_act = mm("mm_dact", d_f, net.weight("w_down"), "nt", F32)
    net.grad("w_down", mm("mm_gdown", act, d_f, "tn", BF16))
    d_gate, d_up, ffn_stats = _ffn_bwd(gu, d_act, conv_w, conv_b)
    d_gu = jnp.concatenate([d_gate, d_up], axis=1)
    d_n2 = mm("mm_dn2", d_gu, net.weight("w_up"), "nt", F32)
    net.grad("w_up", mm("mm_gup", n2, d_gu, "tn", BF16))
    dh1, d_a, dw_pre2, dw_post1 = _mid_bwd(dy, d_n2, h1, a, pre2_w, post1_w)
    d_mix = mm("mm_dmix", d_a, net.weight("w_out"), "nt", F32)
    net.grad("w_out", mm("mm_gout", mixed, d_a, "tn", BF16))
    d_rq, d_rk, d_rv, d_rg, dw_gn = _ret_bwd(proj, d_mix, ry, states, cos, sin, tables, gn_w, rh)
    (d_sq, d_sk, d_sv, dw_sb), extra = _sb_bwd(proj, d_mix, sy, aoff, sb_w, sh, sb_col0, rw // SB_DH,
                                               net.carry("sb_bwd"))
    net.took("sb_bwd", extra)
    d_proj = jnp.concatenate([d_rq, d_rk, d_rv, d_rg, d_sq, d_sk, d_sv], axis=1)
    net.grad("w_in", mm("mm_gin", n1, d_proj, "tn", BF16))
    d_n1 = mm("mm_dn1", d_proj, net.weight("w_in"), "nt", F32)
    grad_x, g_meta, dw_pre1 = _pre_bwd(dh1, d_n1, h0, pre1_w)

    small = dict(loss=loss, meta=g_meta, pre1=dw_pre1, gn=dw_gn, sb=dw_sb, post1=dw_post1, pre2=dw_pre2,
                 conv_w=ffn_stats[0:3], conv_b=ffn_stats[3:4], post2=dw_post2)
    return grad_x, small


def _coords():
    return lax.axis_index("x"), lax.axis_index("y"), lax.axis_index("c")


def _other_chips(x, y):
    return [(1 - x, y), (x, 1 - y), (1 - x, 1 - y)]


ANY = pl.BlockSpec(memory_space=pl.ANY)
VM = pl.BlockSpec(memory_space=pltpu.VMEM)


def _gather4_small(v):
    r = v.shape[0]

    def kern(v_ref, o_ref, send, recv):
        x, y, c = _coords()
        o_ref[2 * x + y] = v_ref[...]
        cps = [pltpu.make_async_remote_copy(v_ref, o_ref.at[2 * x + y], send.at[j], recv.at[j],
                                            device_id=(px, py, c), device_id_type=MESH)
               for j, (px, py) in enumerate(_other_chips(x, y))]
        for cp in cps:
            cp.start()
        for cp in cps:
            cp.wait()

    return pl.pallas_call(
        kern, name="gather_small", in_specs=[VM], out_specs=VM,
        out_shape=jax.ShapeDtypeStruct((4, r, 128), F32),
        scratch_shapes=[pltpu.SemaphoreType.DMA((3,)), pltpu.SemaphoreType.DMA((3,))],
    )(v)


def _allreduce8_small(v):
    r = v.shape[0]
    flips = [(fx, fy, fc) for fx in (0, 1) for fy in (0, 1) for fc in (0, 1)][1:]

    def kern(v_ref, o_ref, buf, send, recv):
        x, y, c = _coords()
        me = 4 * x + 2 * y + c
        buf[me] = v_ref[...]
        cps = [pltpu.make_async_remote_copy(v_ref, buf.at[me], send.at[j], recv.at[j],
                                            device_id=(x ^ fx, y ^ fy, c ^ fc), device_id_type=MESH)
               for j, (fx, fy, fc) in enumerate(flips)]
        for cp in cps:
            cp.start()
        for cp in cps:
            cp.wait()
        tot = buf[0]
        for j in range(1, 8):
            tot = tot + buf[j]
        o_ref[...] = tot

    return pl.pallas_call(
        kern, name="allreduce_small", in_specs=[VM], out_specs=VM,
        out_shape=jax.ShapeDtypeStruct((r, 128), F32),
        scratch_shapes=[pltpu.VMEM((8, r, 128), F32), pltpu.SemaphoreType.DMA((7,)),
                        pltpu.SemaphoreType.DMA((7,))],
    )(v)


def _piece(ref, col_sharded, rows, cols, s, hc):
    half = rows // 2
    if col_sharded:
        return ref.at[pl.ds(pl.multiple_of(hc * half, 16), half), pl.ds(pl.multiple_of(s * cols, 128), cols)]
    return ref.at[pl.ds(pl.multiple_of(s * rows + hc * half, 16), half), :]


def _cast_place(w32, c_id, s_id, col_sharded, name):
    rows, cols = w32.shape
    full_shape = (rows, 4 * cols) if col_sharded else (4 * rows, cols)
    tr = max(d for d in _divisors(rows, 16) if d * cols * 4 <= 4 * 2 ** 20)
    nt = rows // tr

    def kern(c_ref, s_ref, w_ref, o_ref):
        o_ref[...] = w_ref[...].astype(BF16)

    if col_sharded:
        o_spec = pl.BlockSpec((tr, cols), lambda t, c_ref, s_ref: (t, s_ref[0]))
    else:
        o_spec = pl.BlockSpec((tr, cols), lambda t, c_ref, s_ref: (s_ref[0] * nt + t, 0))
    return pl.pallas_call(
        kern, name=name,
        grid_spec=pltpu.PrefetchScalarGridSpec(
            num_scalar_prefetch=2, grid=(nt,),
            in_specs=[pl.BlockSpec((tr, cols), lambda t, c_ref, s_ref: (t, 0))], out_specs=o_spec),
        out_shape=jax.ShapeDtypeStruct(full_shape, BF16),
        compiler_params=_cp(("parallel",)),
    )(c_id, s_id, w32)


def _gather_exchange(full, rows, cols, col_sharded):
    def copies(w_ref, o_ref, send, recv):
        x, y, c = _coords()
        s = 2 * x + y
        sib = (x, y, 1 - c)
        chips = _other_chips(x, y)
        pc = functools.partial(_piece, o_ref, col_sharded, rows, cols)
        mine = _piece(w_ref, col_sharded, rows, cols, s, c)
        first = [pltpu.make_async_remote_copy(mine, pc(s, c), send.at[j], recv.at[j],
                                              device_id=(px, py, c), device_id_type=MESH)
                 for j, (px, py) in enumerate(chips)]
        passed = [pltpu.make_async_remote_copy(pc(2 * px + py, c), pc(2 * px + py, c), send.at[3 + j], recv.at[3 + j],
                                               device_id=sib, device_id_type=MESH)
                  for j, (px, py) in enumerate(chips)]
        from_sib = [pltpu.make_async_remote_copy(pc(2 * px + py, 1 - c), pc(2 * px + py, 1 - c), send.at[3 + j],
                                                 recv.at[3 + j], device_id=sib, device_id_type=MESH)
                    for j, (px, py) in enumerate(chips)]
        return first, passed, from_sib

    def start(ins, outs, send, recv):
        for cp in copies(ins[0], outs[0], send, recv)[0]:
            cp.start()

    def finish(ins, outs, send, recv):
        first, passed, from_sib = copies(ins[0], outs[0], send, recv)
        for j in range(3):
            first[j].wait_recv()
            passed[j].start()
        for cp in from_sib:
            cp.wait_recv()
        for cp in first + passed:
            cp.wait_send()

    return _Exchange([full], [jax.ShapeDtypeStruct(full.shape, BF16)], {0: 0}, 6, start, finish)


def _pair_exchange(g, col_sharded, rows, cols):
    half = rows // 2

    def copies(g_ref, o_ref, send, recv):
        x, y, c = _coords()
        return [pltpu.make_async_remote_copy(_piece(g_ref, col_sharded, rows, cols, s, 1 - c), o_ref.at[s],
                                             send.at[s], recv.at[s], device_id=(x, y, 1 - c), device_id_type=MESH)
                for s in range(4)]

    def start(ins, outs, send, recv):
        for cp in copies(ins[0], outs[0], send, recv):
            cp.start()

    def finish(ins, outs, send, recv):
        for cp in copies(ins[0], outs[0], send, recv):
            cp.wait()

    return _Exchange([g], [jax.ShapeDtypeStruct((4, half, cols), BF16)], {}, 4, start, finish)


def _half_tiles(half, cols):
    tr = max(d for d in _divisors(half, 16) if d * cols * 4 <= 4 * 2 ** 20)
    return tr, half // tr


def _half_spec(col_sharded, tr, nt, cols, own):
    which = (lambda s, s_ref: s_ref[0]) if own else (lambda s, s_ref: s)
    if col_sharded:
        return pl.BlockSpec((tr, cols), lambda s, t, c_ref, s_ref: (c_ref[0] * nt + t, which(s, s_ref)))
    return pl.BlockSpec((tr, cols), lambda s, t, c_ref, s_ref: ((2 * which(s, s_ref) + c_ref[0]) * nt + t, 0))


def _rs_add(g, r1, c_id, s_id, col_sharded, rows, cols, name):
    half = rows // 2
    tr, nt = _half_tiles(half, cols)

    def kern(c_ref, s_ref, g_ref, r_ref, o_ref):
        o_ref[0] = (g_ref[...].astype(F32) + r_ref[0].astype(F32)).astype(BF16)

    slab = pl.BlockSpec((1, tr, cols), lambda s, t, c_ref, s_ref: (s, t, 0))
    return pl.pallas_call(
        kern, name=name,
        grid_spec=pltpu.PrefetchScalarGridSpec(
            num_scalar_prefetch=2, grid=(4, nt),
            in_specs=[_half_spec(col_sharded, tr, nt, cols, False), slab], out_specs=slab),
        out_shape=jax.ShapeDtypeStruct((4, half, cols), BF16),
        compiler_params=_cp(("parallel", "parallel")),
    )(c_id, s_id, g, r1)


def _scatter_exchange(p):
    _, half, cols = p.shape

    def copies(p_ref, o_ref, send, recv):
        x, y, c = _coords()
        return [pltpu.make_async_remote_copy(p_ref.at[2 * px + py], o_ref.at[j], send.at[j], recv.at[j],
                                             device_id=(px, py, c), device_id_type=MESH)
                for j, (px, py) in enumerate(_other_chips(x, y))]

    def start(ins, outs, send, recv):
        for cp in copies(ins[0], outs[0], send, recv):
            cp.start()

    def finish(ins, outs, send, recv):
        for cp in copies(ins[0], outs[0], send, recv):
            cp.wait()

    return _Exchange([p], [jax.ShapeDtypeStruct((3, half, cols), BF16)], {}, 3, start, finish)


def _rs_total(g, r1, r2, c_id, s_id, col_sharded, rows, cols, name):
    half = rows // 2
    tr, nt = _half_tiles(half, cols)

    def kern(c_ref, s_ref, g_ref, r1_ref, r2_ref, o_ref):
        tot = g_ref[...].astype(F32) + r1_ref[0].astype(F32)
        for j in range(3):
            tot = tot + r2_ref[j].astype(F32)
        o_ref[...] = tot

    return pl.pallas_call(
        kern, name=name,
        grid_spec=pltpu.PrefetchScalarGridSpec(
            num_scalar_prefetch=2, grid=(1, nt),
            in_specs=[_half_spec(col_sharded, tr, nt, cols, True),
                      pl.BlockSpec((1, tr, cols), lambda s, t, c_ref, s_ref: (s_ref[0], t, 0)),
                      pl.BlockSpec((3, tr, cols), lambda s, t, c_ref, s_ref: (0, t, 0))],
            out_specs=pl.BlockSpec((tr, cols), lambda s, t, c_ref, s_ref: (c_ref[0] * nt + t, 0))),
        out_shape=jax.ShapeDtypeStruct((rows, cols), F32),
        compiler_params=_cp(("parallel", "parallel")),
    )(c_id, s_id, g, r1, r2)


def _rs_exchange(t, name):
    rows, cols = t.shape
    half = rows // 2

    def kern(t_ref, o_ref, send, recv):
        x, y, c = _coords()
        mine = pl.ds(pl.multiple_of(c * half, 8), half)
        cp = pltpu.make_async_remote_copy(t_ref.at[mine, :], o_ref.at[mine, :], send, recv,
                                          device_id=(x, y, 1 - c), device_id_type=MESH)
        cp.start()
        cp.wait()

    return pl.pallas_call(
        kern, name=name, in_specs=[ANY], out_specs=ANY, input_output_aliases={0: 0},
        out_shape=jax.ShapeDtypeStruct((rows, cols), F32),
        scratch_shapes=[pltpu.SemaphoreType.DMA, pltpu.SemaphoreType.DMA],
    )(t)


class _Whole:
    def __init__(self, full):
        self.full, self.grads = dict(full), {}

    def weight(self, n):
        return self.full[n]

    def carry(self, host):
        return []

    def took(self, host, outs):
        pass

    def grad(self, n, g):
        self.grads[n] = g


BIG = ("w_in", "w_out", "w_up", "w_down")
COL_SHARDED = dict(w_in=True, w_out=False, w_up=True, w_down=False)
GATHER_ON = dict(mm_proj="w_out", sb_fwd="w_up", mm_up="w_down")
PAIR_ON = dict(mm_dn2="w_down", mm_dmix="w_up")
SCATTER_ON = dict(mm_gup=("w_down",), sb_bwd=("w_up", "w_out"), mm_dn1=("w_in",))


class _Sharded(_Whole):
    def __init__(self, shards, c_id, s_id):
        self.shards, self.c_id, self.s_id = shards, c_id, s_id
        self.full, self.grads, self.r1, self.p, self.r2 = {}, {}, {}, {}, {}
        self.placed = {n: _cast_place(shards[n], c_id, s_id, COL_SHARDED[n], "place_" + n) for n in BIG}
        self.full["w_in"], = _run_exchange(self._gather("w_in"), "gather_w_in")

    def _gather(self, n):
        return _gather_exchange(self.placed[n], *self.shards[n].shape, COL_SHARDED[n])

    def _pair(self, n):
        return _pair_exchange(self.grads[n], COL_SHARDED[n], *self.shards[n].shape)

    def _paired(self, n, r1):
        rows, cols = self.shards[n].shape
        self.r1[n] = r1
        self.p[n] = _rs_add(self.grads[n], r1, self.c_id, self.s_id, COL_SHARDED[n], rows, cols, "rs_add_" + n)

    def carry(self, host):
        if host in GATHER_ON:
            return [self._gather(GATHER_ON[host])]
        if host in PAIR_ON:
            return [self._pair(PAIR_ON[host])]
        return [_scatter_exchange(self.p[n]) for n in SCATTER_ON.get(host, ())]

    def took(self, host, outs):
        if host in GATHER_ON:
            self.full[GATHER_ON[host]], = outs
        if host in PAIR_ON:
            self._paired(PAIR_ON[host], outs[0])
        for n, r2 in zip(SCATTER_ON.get(host, ()), outs):
            self.r2[n] = r2

    def grad(self, n, g):
        self.grads[n] = g
        if n not in PAIR_ON.values():
            self._paired(n, _run_exchange(self._pair(n), "rs_pair_" + n)[0])

    def reduced(self, n):
        rows, cols = self.shards[n].shape
        t = _rs_total(self.grads[n], self.r1[n], self.r2[n], self.c_id, self.s_id, COL_SHARDED[n], rows, cols,
                      "rs_total_" + n)
        return _rs_exchange(t, "rs_exchange_" + n)


def _adamw_vals(w, g, m, v):
    m = ADAM_B1 * m + (1.0 - ADAM_B1) * g
    v = ADAM_B2 * v + (1.0 - ADAM_B2) * (g * g)
    m_hat = m / (1.0 - ADAM_B1 ** ADAM_STEP)
    v_hat = v / (1.0 - ADAM_B2 ** ADAM_STEP)
    delta = -ADAM_LR * (m_hat / (jnp.sqrt(v_hat) + ADAM_EPS) + ADAM_WD * w)
    return delta, m, v


def _adamw(w, g, m, v, name):
    rows, cols = w.shape
    tr = max(d for d in _divisors(rows, 8) if d * cols * 4 <= 2 * 2 ** 20)

    def kern(w_ref, g_ref, m_ref, v_ref, d_ref, mo_ref, vo_ref):
        d_ref[...], mo_ref[...], vo_ref[...] = _adamw_vals(w_ref[...], g_ref[...], m_ref[...], v_ref[...])

    spec = pl.BlockSpec((tr, cols), lambda i: (i, 0))
    return pl.pallas_call(
        kern, name=name, grid=(rows // tr,), in_specs=[spec] * 4, out_specs=[spec] * 3,
        out_shape=[jax.ShapeDtypeStruct((rows, cols), F32)] * 3,
        compiler_params=_cp(("parallel",)),
    )(w, g, m, v)


def _pack(arrs):
    flat = []
    for a in arrs:
        a = a.reshape(-1)
        flat.append(jnp.pad(a, (0, (-a.shape[0]) % 1024)))
    return jnp.concatenate(flat).reshape(-1, 128)


def _unpack(slab, shapes):
    out, off = [], 0
    flat = slab.reshape(slab.shape[:-2] + (-1,))
    for shp in shapes:
        n = math.prod(shp)
        out.append(flat[..., off:off + n].reshape(slab.shape[:-2] + tuple(shp)))
        off += n + (-n) % 1024
    return out


SMALL = ("meta_tokens", "attn_pre_norm_w", "ret_gn_w", "sb_norm_w", "attn_post_norm_w", "ffn_pre_norm_w",
         "conv_w", "conv_b", "ffn_post_norm_w")
ORDER = ("meta_tokens", "attn_pre_norm_w", "w_in", "ret_gn_w", "sb_norm_w", "w_out", "attn_post_norm_w",
         "ffn_pre_norm_w", "w_up", "conv_w", "conv_b", "w_down", "ffn_post_norm_w")


def kernel(x, meta_tokens, attn_pre_norm_w, w_in, ret_gn_w, sb_norm_w, w_out, attn_post_norm_w, ffn_pre_norm_w, w_up, conv_w, conv_b, w_down, ffn_post_norm_w, loss_target, m_meta_tokens, m_attn_pre_norm_w, m_w_in, m_ret_gn_w, m_sb_norm_w, m_w_out, m_attn_post_norm_w, m_ffn_pre_norm_w, m_w_up, m_conv_w, m_conv_b, m_w_down, m_ffn_post_norm_w, v_meta_tokens, v_attn_pre_norm_w, v_w_in, v_ret_gn_w, v_sb_norm_w, v_w_out, v_attn_post_norm_w, v_ffn_pre_norm_w, v_w_up, v_conv_w, v_conv_b, v_w_down, v_ffn_post_norm_w):
    args = dict(locals())
    w = {n: args[n] for n in ORDER}
    m = {n: args["m_" + n] for n in ORDER}
    v = {n: args["v_" + n] for n in ORDER}
    xi, yi, ci = _coords()
    shard_id = 2 * xi + yi
    c_id = ci.astype(jnp.int32).reshape(1)
    s_id = shard_id.astype(jnp.int32).reshape(1)
    d = x.shape[-1]

    mshape, cshape = w["meta_tokens"].shape, w["conv_w"][0].shape
    got = _unpack(_gather4_small(_pack([w["meta_tokens"], w["conv_w"][0]])), [mshape, cshape])
    meta_full = jnp.moveaxis(got[0], 0, 1).reshape(N_META, d)
    conv_w_full = jnp.moveaxis(got[1], 0, 1).reshape(CONV_W, -1)

    net = _Sharded({n: w[n][0] for n in BIG}, c_id, s_id)
    grad_x, g_small = _local_step(
        x[0], loss_target[0], meta_full, net, conv_w_full, w["conv_b"], w["attn_pre_norm_w"], w["ret_gn_w"],
        w["sb_norm_w"], w["attn_post_norm_w"], w["ffn_pre_norm_w"], w["ffn_post_norm_w"])

    names = ("loss", "meta", "pre1", "gn", "sb", "post1", "pre2", "conv_w", "conv_b", "post2")
    tot = _unpack(_allreduce8_small(_pack([g_small[n] for n in names])), [g_small[n].shape for n in names])
    tot = dict(zip(names, tot))
    loss = tot["loss"][0, 0]
    mcols, ccols = mshape[1], cshape[1]
    grads = {
        "meta_tokens": lax.dynamic_slice_in_dim(tot["meta"], shard_id * mcols, mcols, axis=1),
        "attn_pre_norm_w": tot["pre1"], "ret_gn_w": tot["gn"], "sb_norm_w": tot["sb"],
        "attn_post_norm_w": tot["post1"], "ffn_pre_norm_w": tot["pre2"],
        "conv_w": lax.dynamic_slice_in_dim(tot["conv_w"], shard_id * ccols, ccols, axis=1)[None],
        "conv_b": tot["conv_b"], "ffn_post_norm_w": tot["post2"],
    }

    for n in BIG:
        grads[n] = net.reduced(n)[None]

    delta, new_m, new_v = {}, {}, {}
    for n in BIG:
        dl, mo, vo = _adamw(w[n][0], grads[n][0], m[n][0], v[n][0], "adamw_" + n)
        delta[n], new_m[n], new_v[n] = dl[None], mo[None], vo[None]
    shapes = [w[n].shape for n in SMALL]
    packed = [_pack([src[n] for n in SMALL]) for src in (w, grads, m, v)]
    outs = _adamw(*packed, "adamw_small")
    for dst, slab in zip((delta, new_m, new_v), outs):
        for n, a in zip(SMALL, _unpack(slab, shapes)):
            dst[n] = a

    return (loss, grad_x[None], *[grads[n] for n in ORDER], *[delta[n] for n in ORDER],
            *[new_m[n] for n in ORDER], *[new_v[n] for n in ORDER])
```

```python
import functools
import math

import jax
import jax.numpy as jnp
from jax import lax
from jax.experimental import pallas as pl
from jax.experimental.pallas import tpu as pltpu

F32 = jnp.float32
BF16 = jnp.bfloat16
MESH = pl.DeviceIdType.MESH

EPS = 1e-6
ROPE_BASE = 10000.0
N_META = 16
CHUNK = 128
N_PAD = CHUNK - N_META
RET_DK = 256
SB_DH = 128
CONV_W = 3

ADAM_LR = 0.001
ADAM_B1 = 0.9
ADAM_B2 = 0.999
ADAM_EPS = 1e-08
ADAM_WD = 0.01
ADAM_STEP = 10

V7X_VMEM_BYTES = 64 * 2 ** 20
VMEM_LIMIT = V7X_VMEM_BYTES - 8 * 2 ** 20
MM_BUDGET = 40 * 2 ** 20
V7X_BF16_FLOPS = 0.9e15
V7X_HBM_BPS = 2.0e12
GRID_STEP_S = 0.6e-6
ACC_BPS = 2.0e13

NN = (((1,), (0,)), ((), ()))
NT = (((1,), (1,)), ((), ()))
TN = (((0,), (0,)), ((), ()))


def _cp(sem, vmem=VMEM_LIMIT):
    return pltpu.CompilerParams(dimension_semantics=sem, vmem_limit_bytes=vmem)


def _dot(a, b, dims=NN):
    return lax.dot_general(a, b, dims, preferred_element_type=F32)


def _sigmoid(x):
    return 1.0 / (1.0 + jnp.exp(-x))


def _divisors(n, align):
    return [d for d in range(align, n + 1, align) if n % d == 0]


def _mm_tiles(mode, m, n, k, out_bytes):
    best = None
    tms = [d for d in _divisors(m, 128 if mode == "tn" else 16) if d >= 128]
    tns = [d for d in _divisors(n, 256)] or [d for d in _divisors(n, 128)]
    tks = [d for d in _divisors(k, 128) if d >= 128]
    flops = 2.0 * m * n * k
    for tm in tms:
        for tn in tns:
            if tm * tn > 2112 * 1024:
                continue
            for tk in tks:
                nk = k // tk
                foot = 4 * (tm * tk + tk * tn) + 2 * tm * tn * out_bytes
                if nk > 1:
                    foot += 4 * tm * tn
                if foot > MM_BUDGET:
                    continue
                steps = (m // tm) * (n // tn) * nk
                for swap in (False, True):
                    if nk > 1:
                        traffic = 2.0 * (n // tn) * m * k + 2.0 * (m // tm) * n * k
                    elif swap:
                        traffic = 2.0 * n * k + 2.0 * (n // tn) * m * k
                    else:
                        traffic = 2.0 * m * k + 2.0 * (m // tm) * n * k
                    traffic += out_bytes * m * n
                    t = max(flops / V7X_BF16_FLOPS, traffic / V7X_HBM_BPS) + steps * GRID_STEP_S
                    if nk > 1:
                        t += 12.0 * m * n * nk / ACC_BPS
                    if best is None or t < best[0]:
                        best = (t, tm, tn, tk, swap)
    assert best is not None, (mode, m, n, k)
    return best[1:]


class _Exchange:
    def __init__(self, operands, out_shapes, aliases, nsem, start, finish):
        self.operands, self.out_shapes, self.aliases, self.nsem = operands, out_shapes, aliases, nsem
        self.start, self.finish = start, finish


def _carried(exchanges, in_base, out_base):
    ins = [a for e in exchanges for a in e.operands]
    outs = [s for e in exchanges for s in e.out_shapes]
    sems = [pltpu.SemaphoreType.DMA((e.nsem,)) for e in exchanges for _ in (0, 1)]
    alias, i, o = {}, in_base, out_base
    for e in exchanges:
        alias.update({i + k: o + v for k, v in e.aliases.items()})
        i, o = i + len(e.operands), o + len(e.out_shapes)
    return ins, outs, sems, alias


def _run_carried(exchanges, phase, in_refs, out_refs, sem_refs):
    i = o = 0
    for n, e in enumerate(exchanges):
        ni, no = len(e.operands), len(e.out_shapes)
        getattr(e, phase)(in_refs[i:i + ni], out_refs[o:o + no], sem_refs[2 * n], sem_refs[2 * n + 1])
        i, o = i + ni, o + no


def _run_exchange(e, name):
    ins, outs, sems, alias = _carried([e], 0, 0)

    def kern(*refs):
        in_refs, out_refs, sem_refs = refs[:len(ins)], refs[len(ins):len(ins) + len(outs)], refs[len(ins) + len(outs):]
        _run_carried([e], "start", in_refs, out_refs, sem_refs)
        _run_carried([e], "finish", in_refs, out_refs, sem_refs)

    return pl.pallas_call(
        kern, name=name, in_specs=[ANY] * len(ins), out_specs=[ANY] * len(outs), out_shape=outs,
        input_output_aliases=alias, scratch_shapes=sems,
    )(*ins)


def _matmul(a, b, mode, out_dtype, name, carry=()):
    if mode == "nn":
        (m, k), (k2, n) = a.shape, b.shape
    elif mode == "nt":
        (m, k), (n, k2) = a.shape, b.shape
    else:
        (k, m), (k2, n) = a.shape, b.shape
    assert k == k2 and a.dtype == BF16 and b.dtype == BF16
    tm, tn, tk, swap = _mm_tiles(mode, m, n, k, jnp.dtype(out_dtype).itemsize)
    nk = k // tk
    dims = {"nn": NN, "nt": NT, "tn": TN}[mode]

    def ij(g0, g1):
        return (g1, g0) if swap else (g0, g1)

    if mode == "tn":
        a_spec = pl.BlockSpec((tk, tm), lambda g0, g1, kk: (kk, ij(g0, g1)[0]))
    else:
        a_spec = pl.BlockSpec((tm, tk), lambda g0, g1, kk: (ij(g0, g1)[0], kk))
    if mode == "nt":
        b_spec = pl.BlockSpec((tn, tk), lambda g0, g1, kk: (ij(g0, g1)[1], kk))
    else:
        b_spec = pl.BlockSpec((tk, tn), lambda g0, g1, kk: (kk, ij(g0, g1)[1]))
    o_spec = pl.BlockSpec((tm, tn), lambda g0, g1, kk: ij(g0, g1))

    grid = (n // tn, m // tm, nk) if swap else (m // tm, n // tn, nk)
    carry = list(carry)
    x_in, x_out, x_sems, alias = _carried(carry, 2, 1)
    acc_shapes = [pltpu.VMEM((tm, tn), F32)] if nk > 1 else []

    def kern(a_ref, b_ref, *rest):
        x_in_refs, o_ref = rest[:len(x_in)], rest[len(x_in)]
        x_out_refs = rest[len(x_in) + 1:len(x_in) + 1 + len(x_out)]
        tail = rest[len(x_in) + 1 + len(x_out):]
        acc, sem_refs = tail[:len(acc_shapes)], tail[len(acc_shapes):]
        pid = [pl.program_id(ax) for ax in range(3)]
        if carry:
            @pl.when((pid[0] == 0) & (pid[1] == 0) & (pid[2] == 0))
            def _():
                _run_carried(carry, "start", x_in_refs, x_out_refs, sem_refs)

        prod = _dot(a_ref[...], b_ref[...], dims)
        if nk == 1:
            o_ref[...] = prod.astype(out_dtype)
        else:
            kk = pid[2]

            @pl.when(kk == 0)
            def _():
                acc[0][...] = prod

            @pl.when(kk > 0)
            def _():
                acc[0][...] += prod

            @pl.when(kk == nk - 1)
            def _():
                o_ref[...] = acc[0][...].astype(out_dtype)

        if carry:
            @pl.when((pid[0] == grid[0] - 1) & (pid[1] == grid[1] - 1) & (pid[2] == grid[2] - 1))
            def _():
                _run_carried(carry, "finish", x_in_refs, x_out_refs, sem_refs)

    sem = ("arbitrary",) * 3 if carry else ("parallel", "parallel", "arbitrary")
    res = pl.pallas_call(
        kern, name=name, grid=grid, in_specs=[a_spec, b_spec] + [ANY] * len(x_in),
        out_specs=[o_spec] + [ANY] * len(x_out),
        out_shape=[jax.ShapeDtypeStruct((m, n), out_dtype)] + x_out,
        input_output_aliases=alias, scratch_shapes=acc_shapes + x_sems,
        compiler_params=_cp(sem),
    )(a, b, *x_in)
    return (res[0], res[1:]) if carry else res[0]


def _rms_fwd(x, w):
    r = lax.rsqrt(jnp.mean(x * x, axis=-1, keepdims=True) + EPS)
    return x * r * w


def _rms_bwd(x, w, g):
    r = lax.rsqrt(jnp.mean(x * x, axis=-1, keepdims=True) + EPS)
    gw = g * w
    dx = r * gw - x * (r * r * r * jnp.mean(gw * x, axis=-1, keepdims=True))
    return dx, g * (x * r)


def _row_spec(d):
    return pl.BlockSpec((CHUNK, d), lambda i: (i, 0))


def _vec_spec(d):
    return pl.BlockSpec((1, d), lambda i: (0, 0))


def _prenorm(h0, w):
    tp, d = h0.shape

    def kern(h_ref, w_ref, o_ref):
        o_ref[...] = _rms_fwd(h_ref[...], w_ref[...]).astype(BF16)

    return pl.pallas_call(
        kern, name="prenorm1", grid=(tp // CHUNK,),
        in_specs=[_row_spec(d), _vec_spec(d)], out_specs=_row_spec(d),
        out_shape=jax.ShapeDtypeStruct((tp, d), BF16),
        compiler_params=_cp(("parallel",)),
    )(h0, w)


def _mid_fwd(h0, a, w_post, w_pre):
    tp, d = h0.shape

    def kern(h_ref, a_ref, wp_ref, wq_ref, h1_ref, n2_ref):
        h1 = h_ref[...] + _rms_fwd(a_ref[...], wp_ref[...])
        h1_ref[...] = h1
        n2_ref[...] = _rms_fwd(h1, wq_ref[...]).astype(BF16)

    return pl.pallas_call(
        kern, name="mid_fwd", grid=(tp // CHUNK,),
        in_specs=[_row_spec(d), _row_spec(d), _vec_spec(d), _vec_spec(d)],
        out_specs=[_row_spec(d), _row_spec(d)],
        out_shape=[jax.ShapeDtypeStruct((tp, d), F32), jax.ShapeDtypeStruct((tp, d), BF16)],
        compiler_params=_cp(("parallel",)),
    )(h0, a, w_post, w_pre)


def _loss_bwd(h1, f, tgt, w_post):
    tp, d = h1.shape

    def kern(h_ref, f_ref, t_ref, w_ref, dy_ref, df_ref, loss_ref, dw_ref):
        i = pl.program_id(0)

        @pl.when(i == 0)
        def _():
            dy_ref[...] = jnp.zeros_like(dy_ref)
            df_ref[...] = jnp.zeros_like(df_ref)
            loss_ref[...] = jnp.zeros_like(loss_ref)
            dw_ref[...] = jnp.zeros_like(dw_ref)

        @pl.when(i > 0)
        def _():
            fv = f_ref[...]
            w = w_ref[...]
            err = h_ref[...] + _rms_fwd(fv, w) - t_ref[...]
            loss_ref[...] += 0.5 * jnp.sum(jnp.mean(err * err, axis=-1, keepdims=True))
            dy = err * (1.0 / d)
            dy_ref[...] = dy
            dfv, dwr = _rms_bwd(fv, w, dy)
            df_ref[...] = dfv.astype(BF16)
            dw_ref[...] += jnp.sum(dwr, axis=0, keepdims=True)

    return pl.pallas_call(
        kern, name="loss_bwd", grid=(tp // CHUNK,),
        in_specs=[_row_spec(d), _row_spec(d),
                  pl.BlockSpec((CHUNK, d), lambda i: (jnp.maximum(i - 1, 0), 0)), _vec_spec(d)],
        out_specs=[_row_spec(d), _row_spec(d), pl.BlockSpec((1, 128), lambda i: (0, 0)), _vec_spec(d)],
        out_shape=[jax.ShapeDtypeStruct((tp, d), F32), jax.ShapeDtypeStruct((tp, d), BF16),
                   jax.ShapeDtypeStruct((1, 128), F32), jax.ShapeDtypeStruct((1, d), F32)],
        compiler_params=_cp(("arbitrary",)),
    )(h1, f, tgt, w_post)


def _mid_bwd(dy, dn2, h1, a, w_pre, w_post):
    tp, d = h1.shape

    def kern(dy_ref, dn_ref, h_ref, a_ref, wq_ref, wp_ref, dh_ref, da_ref, dwq_ref, dwp_ref):
        @pl.when(pl.program_id(0) == 0)
        def _():
            dwq_ref[...] = jnp.zeros_like(dwq_ref)
            dwp_ref[...] = jnp.zeros_like(dwp_ref)

        dx, dwq = _rms_bwd(h_ref[...], wq_ref[...], dn_ref[...])
        dh = dy_ref[...] + dx
        dh_ref[...] = dh
        da, dwp = _rms_bwd(a_ref[...], wp_ref[...], dh)
        da_ref[...] = da.astype(BF16)
        dwq_ref[...] += jnp.sum(dwq, axis=0, keepdims=True)
        dwp_ref[...] += jnp.sum(dwp, axis=0, keepdims=True)

    return pl.pallas_call(
        kern, name="mid_bwd", grid=(tp // CHUNK,),
        in_specs=[_row_spec(d)] * 4 + [_vec_spec(d)] * 2,
        out_specs=[_row_spec(d), _row_spec(d), _vec_spec(d), _vec_spec(d)],
        out_shape=[jax.ShapeDtypeStruct((tp, d), F32), jax.ShapeDtypeStruct((tp, d), BF16),
                   jax.ShapeDtypeStruct((1, d), F32), jax.ShapeDtypeStruct((1, d), F32)],
        compiler_params=_cp(("arbitrary",)),
    )(dy, dn2, h1, a, w_pre, w_post)


def _pre_bwd(dh1, dn1, h0, w_pre):
    tp, d = h0.shape
    s = tp - CHUNK

    def kern(dh_ref, dn_ref, h_ref, w_ref, gx_ref, gm_ref, dw_ref):
        i = pl.program_id(0)
        dx, dwr = _rms_bwd(h_ref[...], w_ref[...], dn_ref[...])
        dh0 = dh_ref[...] + dx
        gx_ref[...] = dh0

        @pl.when(i == 0)
        def _():
            gm_ref[...] = dh0[N_PAD:, :]
            dw_ref[...] = jnp.zeros_like(dw_ref)

        dw_ref[...] += jnp.sum(dwr, axis=0, keepdims=True)

    return pl.pallas_call(
        kern, name="pre_bwd", grid=(tp // CHUNK,),
        in_specs=[_row_spec(d)] * 3 + [_vec_spec(d)],
        out_specs=[pl.BlockSpec((CHUNK, d), lambda i: (jnp.maximum(i - 1, 0), 0)),
                   pl.BlockSpec((N_META, d), lambda i: (0, 0)), _vec_spec(d)],
        out_shape=[jax.ShapeDtypeStruct((s, d), F32), jax.ShapeDtypeStruct((N_META, d), F32),
                   jax.ShapeDtypeStruct((1, d), F32)],
        compiler_params=_cp(("arbitrary",)),
    )(dh1, dn1, h0, w_pre)


def _ffn_cols(dff):
    return dff // 2 if dff % 256 == 0 else dff


def _ffn_fwd(gu, conv_w, conv_b):
    tp, two_dff = gu.shape
    dff = two_dff // 2
    tc = _ffn_cols(dff)
    nj = dff // tc
    r8 = CHUNK // 8

    def kern(g_ref, gp_ref, u_ref, w_ref, b_ref, o_ref):
        i = pl.program_id(1)
        prev = gp_ref[...] * (i > 0).astype(F32)
        ext = jnp.concatenate([prev, g_ref[...]], axis=0)
        w = w_ref[...]
        conv = (b_ref[...] + w[0:1] * pltpu.roll(ext, 2, 0)[8:] + w[1:2] * pltpu.roll(ext, 1, 0)[8:]
                + w[2:3] * ext[8:])
        o_ref[...] = (conv * _sigmoid(conv) * u_ref[...]).astype(BF16)

    return pl.pallas_call(
        kern, name="ffn_fwd", grid=(nj, tp // CHUNK),
        in_specs=[pl.BlockSpec((CHUNK, tc), lambda j, i: (i, j)),
                  pl.BlockSpec((8, tc), lambda j, i: (jnp.maximum(i * r8 - 1, 0), j)),
                  pl.BlockSpec((CHUNK, tc), lambda j, i: (i, j + nj)),
                  pl.BlockSpec((CONV_W, tc), lambda j, i: (0, j)),
                  pl.BlockSpec((1, tc), lambda j, i: (0, j))],
        out_specs=pl.BlockSpec((CHUNK, tc), lambda j, i: (i, j)),
        out_shape=jax.ShapeDtypeStruct((tp, dff), BF16),
        compiler_params=_cp(("parallel", "parallel")),
    )(gu, gu, gu, conv_w, conv_b)


def _ffn_bwd(gu, dact, conv_w, conv_b):
    tp, two_dff = gu.shape
    dff = two_dff // 2
    tc = _ffn_cols(dff)
    nj = dff // tc
    ni = tp // CHUNK
    r8 = CHUNK // 8

    def kern(g_ref, gp_ref, gn_ref, u_ref, un_ref, d_ref, dn_ref, w_ref, b_ref, dg_ref, du_ref, st_ref):
        i = pl.program_id(1)

        @pl.when(i == 0)
        def _():
            st_ref[...] = jnp.zeros_like(st_ref)

        first = (i > 0).astype(F32)
        last = (i < ni - 1).astype(F32)
        gate = g_ref[...]
        ext = jnp.concatenate([gp_ref[...] * first, gate, gn_ref[...]], axis=0)
        w = w_ref[...]
        r1 = pltpu.roll(ext, 1, 0)
        r2 = pltpu.roll(ext, 2, 0)
        conv = (b_ref[...] + w[0:1] * r2 + w[1:2] * r1 + w[2:3] * ext)[8:]
        up = jnp.concatenate([u_ref[...], un_ref[...]], axis=0)
        da = jnp.concatenate([d_ref[...], dn_ref[...] * last], axis=0)
        sg = _sigmoid(conv)
        dc = da * up * (sg * (1.0 + conv * (1.0 - sg)))
        du_ref[...] = (d_ref[...] * (conv * sg)[:CHUNK]).astype(BF16)
        n = CHUNK + 8
        dgate = w[2:3] * dc + w[1:2] * pltpu.roll(dc, n - 1, 0) + w[0:1] * pltpu.roll(dc, n - 2, 0)
        dg_ref[...] = dgate[:CHUNK].astype(BF16)
        dcm = dc[:CHUNK]
        s0 = jnp.sum(dcm * r2[8:8 + CHUNK], axis=0, keepdims=True)
        s1 = jnp.sum(dcm * r1[8:8 + CHUNK], axis=0, keepdims=True)
        s2 = jnp.sum(dcm * gate, axis=0, keepdims=True)
        s3 = jnp.sum(dcm, axis=0, keepdims=True)
        row = lax.broadcasted_iota(jnp.int32, (8, tc), 0)
        st_ref[...] += jnp.where(row == 0, s0, jnp.where(row == 1, s1, jnp.where(row == 2, s2,
                                 jnp.where(row == 3, s3, 0.0))))

    main = lambda off: pl.BlockSpec((CHUNK, tc), lambda j, i: (i, j + off))
    nxt = lambda off: pl.BlockSpec((8, tc), lambda j, i: (jnp.minimum((i + 1) * r8, ni * r8 - 1), j + off))
    return pl.pallas_call(
        kern, name="ffn_bwd", grid=(nj, ni),
        in_specs=[main(0), pl.BlockSpec((8, tc), lambda j, i: (jnp.maximum(i * r8 - 1, 0), j)), nxt(0),
                  main(nj), nxt(nj), main(0), nxt(0),
                  pl.BlockSpec((CONV_W, tc), lambda j, i: (0, j)), pl.BlockSpec((1, tc), lambda j, i: (0, j))],
        out_specs=[main(0), main(0), pl.BlockSpec((8, tc), lambda j, i: (0, j))],
        out_shape=[jax.ShapeDtypeStruct((tp, dff), BF16), jax.ShapeDtypeStruct((tp, dff), BF16),
                   jax.ShapeDtypeStruct((8, dff), F32)],
        compiler_params=_cp(("parallel", "arbitrary")),
    )(gu, gu, gu, gu, gu, dact, dact, conv_w, conv_b)


def _rot(x, cs, sn):
    x1, x2 = x[:, :128], x[:, 128:]
    return jnp.concatenate([x1 * cs - x2 * sn, x1 * sn + x2 * cs], axis=1)


def _rot_t(x, cs, sn):
    x1, x2 = x[:, :128], x[:, 128:]
    return jnp.concatenate([x1 * cs + x2 * sn, x2 * cs - x1 * sn], axis=1)


def _ret_tables(rh):
    lg = jnp.log(1.0 - 2.0 ** (-5.0 - jnp.arange(rh, dtype=F32)))
    idx = jnp.arange(CHUNK, dtype=F32)
    diff = idx[:, None] - idx[None, :]
    intra = jnp.where(diff[None] >= 0, jnp.exp(jnp.maximum(diff, 0.0)[None] * lg[:, None, None]), 0.0)
    qdec = jnp.exp((idx[None, :] + 1.0) * lg[:, None])[..., None]
    kdec = jnp.exp((CHUNK - 1.0 - idx[None, :]) * lg[:, None])[..., None]
    cdec = jnp.exp(CHUNK * lg)[:, None, None]
    return intra, qdec, kdec, cdec


def _ret_specs(rh, nc, rev):
    cc = (lambda c: nc - 1 - c) if rev else (lambda c: c)
    col = lambda sec: pl.BlockSpec((CHUNK, RET_DK), lambda h, c: (cc(c), sec * rh + h))
    tab = [pl.BlockSpec((CHUNK, 128), lambda h, c: (cc(c), 0))] * 2
    dec = [pl.BlockSpec((1, CHUNK, CHUNK), lambda h, c: (h, 0, 0)),
           pl.BlockSpec((1, CHUNK, 1), lambda h, c: (h, 0, 0)),
           pl.BlockSpec((1, CHUNK, 1), lambda h, c: (h, 0, 0)),
           pl.BlockSpec((1, 1, 1), lambda h, c: (h, 0, 0))]
    hw = pl.BlockSpec((1, RET_DK), lambda h, c: (0, h))
    hcol = pl.BlockSpec((CHUNK, RET_DK), lambda h, c: (cc(c), h))
    st = pl.BlockSpec((1, 1, RET_DK, RET_DK), lambda h, c: (h, cc(c), 0, 0))
    return col, tab, dec, hw, hcol, st


def _ret_fwd(proj, cos, sin, tables, gnw, rh):
    tp = proj.shape[0]
    nc = tp // CHUNK
    col, tab, dec, hw, hcol, st = _ret_specs(rh, nc, False)

    def kern(q_ref, k_ref, v_ref, g_ref, cos_ref, sin_ref, in_ref, qd_ref, kd_ref, cd_ref, w_ref,
             out_ref, ry_ref, st_ref, state):
        @pl.when(pl.program_id(1) == 0)
        def _():
            state[...] = jnp.zeros_like(state)

        cs, sn = cos_ref[...], sin_ref[...]
        q = (_rot(q_ref[...], cs, sn) * (RET_DK ** -0.5)).astype(BF16)
        kf = _rot(k_ref[...], cs, sn)
        k = kf.astype(BF16)
        v = v_ref[...].astype(BF16)
        s_old = state[...]
        s_b = s_old.astype(BF16)
        st_ref[0, 0] = s_b
        sc = _dot(q, k, NT) * in_ref[0]
        ry = _dot(sc.astype(BF16), v) + _dot(q, s_b) * qd_ref[0]
        state[...] = s_old * cd_ref[0] + _dot((kf * kd_ref[0]).astype(BF16), v, TN)
        ry_ref[...] = ry
        g = g_ref[...]
        out_ref[...] = (g * _sigmoid(g) * _rms_fwd(ry, w_ref[...])).astype(BF16)

    return pl.pallas_call(
        kern, name="ret_fwd", grid=(rh, nc),
        in_specs=[col(0), col(1), col(2), col(3)] + tab + dec + [hw],
        out_specs=[hcol, hcol, st],
        out_shape=[jax.ShapeDtypeStruct((tp, rh * RET_DK), BF16), jax.ShapeDtypeStruct((tp, rh * RET_DK), F32),
                   jax.ShapeDtypeStruct((rh, nc, RET_DK, RET_DK), BF16)],
        scratch_shapes=[pltpu.VMEM((RET_DK, RET_DK), F32)],
        compiler_params=_cp(("parallel", "arbitrary")),
    )(proj, proj, proj, proj, cos, sin, *tables, gnw)


def _ret_bwd(proj, dmix, ry_all, states, cos, sin, tables, gnw, rh):
    tp = proj.shape[0]
    nc = tp // CHUNK
    col, tab, dec, hw, hcol, st = _ret_specs(rh, nc, True)

    def kern(q_ref, k_ref, v_ref, g_ref, cos_ref, sin_ref, in_ref, qd_ref, kd_ref, cd_ref, w_ref,
             do_ref, ry_ref, st_ref, dq_ref, dk_ref, dv_ref, dg_ref, dw_ref, ds):
        @pl.when(pl.program_id(1) == 0)
        def _():
            ds[...] = jnp.zeros_like(ds)
            dw_ref[...] = jnp.zeros_like(dw_ref)

        cs, sn = cos_ref[...], sin_ref[...]
        qf = _rot(q_ref[...], cs, sn) * (RET_DK ** -0.5)
        q = qf.astype(BF16)
        kf = _rot(k_ref[...], cs, sn)
        k = kf.astype(BF16)
        vf = v_ref[...]
        v = vf.astype(BF16)
        g = g_ref[...]
        ry = ry_ref[...]
        w = w_ref[...]
        dout = do_ref[...]
        sg = _sigmoid(g)
        dhn = dout * (g * sg)
        dry, dwr = _rms_bwd(ry, w, dhn)
        dg_ref[...] = (dout * _rms_fwd(ry, w) * (sg * (1.0 + g * (1.0 - sg)))).astype(BF16)
        dw_ref[...] += jnp.sum(dwr, axis=0, keepdims=True)

        dmat = in_ref[0]
        qd, kd = qd_ref[0], kd_ref[0]
        dyb = dry.astype(BF16)
        p = (_dot(q, k, NT) * dmat).astype(BF16)
        dp = (_dot(dyb, v, NT) * dmat).astype(BF16)
        ady = (dry * qd).astype(BF16)
        ds_old = ds[...]
        ds_b = ds_old.astype(BF16)
        dq = _dot(dp, k) + _dot(ady, st_ref[0, 0], NT)
        dk = _dot(dp, q, TN) + _dot(v, ds_b, NT) * kd
        dv = _dot(p, dyb, TN) + _dot((kf * kd).astype(BF16), ds_b)
        ds[...] = ds_old * cd_ref[0] + _dot(q, ady, TN)
        dq_ref[...] = _rot_t(dq * (RET_DK ** -0.5), cs, sn).astype(BF16)
        dk_ref[...] = _rot_t(dk, cs, sn).astype(BF16)
        dv_ref[...] = dv.astype(BF16)

    rw = rh * RET_DK
    outs = pl.pallas_call(
        kern, name="ret_bwd", grid=(rh, nc),
        in_specs=[col(0), col(1), col(2), col(3)] + tab + dec + [hw, hcol, hcol, st],
        out_specs=[hcol, hcol, hcol, hcol, hw],
        out_shape=[jax.ShapeDtypeStruct((tp, rw), BF16)] * 4 + [jax.ShapeDtypeStruct((1, rw), F32)],
        scratch_shapes=[pltpu.VMEM((RET_DK, RET_DK), F32)],
        compiler_params=_cp(("parallel", "arbitrary")),
    )(proj, proj, proj, proj, cos, sin, *tables, gnw, dmix, ry_all, states)
    return outs


def _sb_tile(tp):
    return 3 * CHUNK if tp % (3 * CHUNK) == 0 else CHUNK


def _sb_block(q, k, qpos, kb, scale, masked):
    z = _dot(q, k, NT) * scale
    t = jnp.log(1.0 + jnp.exp(-jnp.abs(z)))
    lb = jnp.minimum(z, 0.0) - t
    lk = -jnp.maximum(z, 0.0) - t
    if not masked:
        return None, lb, lk
    kpos = kb * CHUNK + lax.broadcasted_iota(jnp.int32, qpos.shape, 1)
    mask = (kpos < qpos) & (kpos >= N_PAD)
    return mask, lb, jnp.where(mask, lk, 0.0)


def _keep(mask, x):
    return x if mask is None else jnp.where(mask, x, 0.0)


def _sb_trips(i, trip):
    lax.fori_loop(0, 1, trip(True), 0)
    lax.fori_loop(1, i, trip(False), 0)
    lax.fori_loop(jnp.maximum(i, 1), i + 1, trip(True), 0)


def _tri_sum(x, tri):
    hi = x.astype(BF16)
    lo = (x - hi.astype(F32)).astype(BF16)
    if tri.shape[0] == CHUNK:
        return _dot(hi, tri) + _dot(lo, tri)
    return _dot(jnp.concatenate([hi, lo], axis=1), tri)


def _tri(strict_upper, copies=1):
    r = lax.broadcasted_iota(jnp.int32, (copies * CHUNK, CHUNK), 0) % CHUNK
    c = lax.broadcasted_iota(jnp.int32, (copies * CHUNK, CHUNK), 1)
    return ((r > c) if strict_upper else (r < c)).astype(BF16)


def _sb_fwd(proj, sbw, sh, col0, carry=()):
    tp = proj.shape[0]
    tq = _sb_tile(tp)
    nsub, nq = tq // CHUNK, tp // tq
    assert tp // CHUNK <= 128
    scale = 1.0 / math.sqrt(SB_DH)
    hq = pl.BlockSpec((tq, SB_DH), lambda h, i: (i, h))

    carry = list(carry)
    x_in, x_out, x_sems, alias = _carried(carry, 4, 3)

    def kern(q_ref, k_ref, v_ref, w_ref, *rest):
        x_in_refs, rest = rest[:len(x_in)], rest[len(x_in):]
        out_ref, sy_ref, ao_ref = rest[:3]
        x_out_refs, (a_run, k16, v16), sem_refs = rest[3:3 + len(x_out)], rest[3 + len(x_out):6 + len(x_out)], rest[6 + len(x_out):]
        h, i = pl.program_id(0), pl.program_id(1)
        if carry:
            @pl.when((h == 0) & (i == 0))
            def _():
                _run_carried(carry, "start", x_in_refs, x_out_refs, sem_refs)

        @pl.when(i == 0)
        def _():
            k16[...] = k_ref[...].astype(BF16)
            v16[...] = v_ref[...].astype(BF16)

        q = q_ref[...].astype(BF16)
        upper = _tri(True)
        lane = lax.broadcasted_iota(jnp.int32, (tq, CHUNK), 1)
        qpos = i * tq + lax.broadcasted_iota(jnp.int32, (tq, CHUNK), 0)
        a_run[...] = jnp.zeros_like(a_run)
        sy_ref[...] = jnp.zeros_like(sy_ref)
        ao_ref[...] = jnp.zeros_like(ao_ref)

        def trip(masked):
            def body(jj, carry):
                a, acc, at = a_run[...], sy_ref[...], ao_ref[0]
                for sub in reversed(range(nsub)):
                    kb = (i - jj) * nsub + sub
                    rows = pl.ds(pl.multiple_of(kb * CHUNK, CHUNK), CHUNK)
                    mask, lb, lk = _sb_block(q, k16[rows, :], qpos, kb, scale, masked)
                    wgt = _keep(mask, jnp.exp(lb + a + _tri_sum(lk, upper)))
                    acc = acc + _dot(wgt.astype(BF16), v16[rows, :])
                    at = jnp.where(lane == kb, a, at)
                    a = a + jnp.sum(lk, axis=1, keepdims=True)
                a_run[...], sy_ref[...], ao_ref[0] = a, acc, at
                return carry
            return body

        _sb_trips(i, trip)
        out_ref[...] = _rms_fwd(sy_ref[...], w_ref[...]).astype(BF16)
        if carry:
            @pl.when((h == sh - 1) & (i == nq - 1))
            def _():
                _run_carried(carry, "finish", x_in_refs, x_out_refs, sem_refs)

    kv = lambda sec: pl.BlockSpec((tp, SB_DH), lambda h, i: (0, col0 + sec * sh + h))
    res = pl.pallas_call(
        kern, name="sb_fwd", grid=(sh, nq),
        in_specs=[pl.BlockSpec((tq, SB_DH), lambda h, i: (i, col0 + h)), kv(1), kv(2),
                  pl.BlockSpec((1, SB_DH), lambda h, i: (0, h))] + [ANY] * len(x_in),
        out_specs=[hq, hq, pl.BlockSpec((1, tq, 128), lambda h, i: (h, i, 0))] + [ANY] * len(x_out),
        out_shape=[jax.ShapeDtypeStruct((tp, sh * SB_DH), BF16), jax.ShapeDtypeStruct((tp, sh * SB_DH), F32),
                   jax.ShapeDtypeStruct((sh, tp, 128), F32)] + x_out,
        input_output_aliases=alias,
        scratch_shapes=[pltpu.VMEM((tq, CHUNK), F32), pltpu.VMEM((tp, SB_DH), BF16),
                        pltpu.VMEM((tp, SB_DH), BF16)] + x_sems,
        compiler_params=_cp(("arbitrary", "arbitrary") if carry else ("parallel", "arbitrary")),
    )(proj, proj, proj, sbw, *x_in)
    return res[:3], res[3:]


def _sb_bwd(proj, dmix, sy_all, aoff, sbw, sh, col0, dcol0, carry=()):
    tp = proj.shape[0]
    tq = _sb_tile(tp)
    nsub, nq = tq // CHUNK, tp // tq
    scale = 1.0 / math.sqrt(SB_DH)
    hq = pl.BlockSpec((tq, SB_DH), lambda h, i: (i, h))
    carry = list(carry)
    x_in, x_out, x_sems, alias = _carried(carry, 7, 4)

    def kern(q_ref, k_ref, v_ref, w_ref, do_ref, sy_ref, ao_ref, *rest):
        x_in_refs, rest = rest[:len(x_in)], rest[len(x_in):]
        dq_ref, dk_ref, dv_ref, dw_ref = rest[:4]
        x_out_refs, rest = rest[4:4 + len(x_out)], rest[4 + len(x_out):]
        (dk_acc, dv_acc, dq_acc, e_run, k16, v16), sem_refs = rest[:6], rest[6:]
        h, i = pl.program_id(0), pl.program_id(1)
        if carry:
            @pl.when((h == 0) & (i == 0))
            def _():
                _run_carried(carry, "start", x_in_refs, x_out_refs, sem_refs)

        @pl.when(i == 0)
        def _():
            dk_acc[...] = jnp.zeros_like(dk_acc)
            dv_acc[...] = jnp.zeros_like(dv_acc)
            dw_ref[...] = jnp.zeros_like(dw_ref)
            k16[...] = k_ref[...].astype(BF16)
            v16[...] = v_ref[...].astype(BF16)

        qf = q_ref[...]
        q = qf.astype(BF16)
        dsy, dwr = _rms_bwd(sy_ref[...], w_ref[...], do_ref[...])
        dw_ref[...] += jnp.sum(dwr, axis=0, keepdims=True)
        dsy_b = dsy.astype(BF16)
        q_t = qf.T.astype(BF16)
        dsy_t = dsy.T.astype(BF16)
        atile = ao_ref[0]
        upper = _tri(True, 2)
        lower = _tri(False, 2)
        lane = lax.broadcasted_iota(jnp.int32, (tq, CHUNK), 1)
        qpos = i * tq + lax.broadcasted_iota(jnp.int32, (tq, CHUNK), 0)

        e_run[...] = jnp.zeros_like(e_run)
        dq_acc[...] = jnp.zeros_like(dq_acc)

        def trip(masked):
            def body(jj, carry):
                e_prev, dq = e_run[...], dq_acc[...]
                for sub in range(nsub):
                    kb = jj * nsub + sub
                    rows = pl.ds(pl.multiple_of(kb * CHUNK, CHUNK), CHUNK)
                    k, v = k16[rows, :], v16[rows, :]
                    mask, lb, lk = _sb_block(q, k, qpos, kb, scale, masked)
                    a = jnp.sum(jnp.where(lane == kb, atile, 0.0), axis=1, keepdims=True)
                    wgt = _keep(mask, jnp.exp(lb + a + _tri_sum(lk, upper)))
                    e = wgt * _dot(dsy_b, v, NT)
                    dv_acc[kb] += _dot(dsy_t, wgt.astype(BF16))
                    sig = jnp.exp(lb)
                    e_all = e_prev + _tri_sum(e, lower)
                    dz = (_keep(mask, e - sig * (e + e_all)) * scale).astype(BF16)
                    dk_acc[kb] += _dot(q_t, dz)
                    dq = dq + _dot(dz, k)
                    e_prev = e_prev + jnp.sum(e, axis=1, keepdims=True)
                e_run[...], dq_acc[...] = e_prev, dq
                return carry
            return body

        _sb_trips(i, trip)
        dq_ref[...] = dq_acc[...].astype(BF16)

        @pl.when(i == nq - 1)
        def _():
            def untranspose(kb, c):
                rows = pl.ds(pl.multiple_of(kb * CHUNK, CHUNK), CHUNK)
                dk_ref[rows, :] = dk_acc[kb].T.astype(BF16)
                dv_ref[rows, :] = dv_acc[kb].T.astype(BF16)
                return c

            lax.fori_loop(0, tp // CHUNK, untranspose, 0)

        if carry:
            @pl.when((h == sh - 1) & (i == nq - 1))
            def _():
                _run_carried(carry, "finish", x_in_refs, x_out_refs, sem_refs)

    kv = lambda sec: pl.BlockSpec((tp, SB_DH), lambda h, i: (0, col0 + sec * sh + h))
    hfull = pl.BlockSpec((tp, SB_DH), lambda h, i: (0, h))
    sw = sh * SB_DH
    acc_t = pltpu.VMEM((tp // CHUNK, SB_DH, CHUNK), F32)
    res = pl.pallas_call(
        kern, name="sb_bwd", grid=(sh, nq),
        in_specs=[pl.BlockSpec((tq, SB_DH), lambda h, i: (i, col0 + h)), kv(1), kv(2),
                  pl.BlockSpec((1, SB_DH), lambda h, i: (0, h)),
                  pl.BlockSpec((tq, SB_DH), lambda h, i: (i, dcol0 + h)), hq,
                  pl.BlockSpec((1, tq, 128), lambda h, i: (h, i, 0))] + [ANY] * len(x_in),
        out_specs=[hq, hfull, hfull, pl.BlockSpec((1, SB_DH), lambda h, i: (0, h))] + [ANY] * len(x_out),
        out_shape=[jax.ShapeDtypeStruct((tp, sw), BF16)] * 3 + [jax.ShapeDtypeStruct((1, sw), F32)] + x_out,
        input_output_aliases=alias,
        scratch_shapes=[acc_t, acc_t, pltpu.VMEM((tq, SB_DH), F32), pltpu.VMEM((tq, CHUNK), F32),
                        pltpu.VMEM((tp, SB_DH), BF16), pltpu.VMEM((tp, SB_DH), BF16)] + x_sems,
        compiler_params=_cp(("arbitrary", "arbitrary") if carry else ("parallel", "arbitrary")),
    )(proj, proj, proj, sbw, dmix, sy_all, aoff, *x_in)
    return res[:4], res[4:]


def _local_step(x, tgt, meta, net, conv_w, conv_b, pre1_w, gn_w, sb_w, post1_w, pre2_w, post2_w):
    s, d = x.shape

    def mm(host, a, b, mode, dtype):
        carry = net.carry(host)
        out = _matmul(a, b, mode, dtype, host, carry)
        if carry:
            out, extra = out
            net.took(host, extra)
        return out

    tp = s + CHUNK
    rh, sh = d // 512, d // 256
    rw = rh * RET_DK
    h0 = jnp.concatenate([jnp.zeros((N_PAD, d), F32), meta, x], axis=0)
    pos = jnp.arange(tp, dtype=F32) - N_PAD
    inv = ROPE_BASE ** (-jnp.arange(128, dtype=F32) / 128)
    ang = pos[:, None] * inv[None, :]
    cos, sin = jnp.cos(ang), jnp.sin(ang)
    tables = _ret_tables(rh)
    sb_col0 = 4 * rw // SB_DH

    n1 = _prenorm(h0, pre1_w)
    proj = mm("mm_proj", n1, net.weight("w_in"), "nn", F32)
    ret_out, ry, states = _ret_fwd(proj, cos, sin, tables, gn_w, rh)
    (sb_out, sy, aoff), extra = _sb_fwd(proj, sb_w, sh, sb_col0, net.carry("sb_fwd"))
    net.took("sb_fwd", extra)
    mixed = jnp.concatenate([ret_out, sb_out], axis=1)
    a = mm("mm_out", mixed, net.weight("w_out"), "nn", F32)
    h1, n2 = _mid_fwd(h0, a, post1_w, pre2_w)
    gu = mm("mm_up", n2, net.weight("w_up"), "nn", F32)
    act = _ffn_fwd(gu, conv_w, conv_b)
    f = mm("mm_down", act, net.weight("w_down"), "nn", F32)

    dy, d_f, loss, dw_post2 = _loss_bwd(h1, f, tgt, post2_w)
    d_act = mm("mm_dact", d_f, net.weight("w_down"), "nt", F32)
    net.grad("w_down", mm("mm_gdown", act, d_f, "tn", BF16))
    d_gate, d_up, ffn_stats = _ffn_bwd(gu, d_act, conv_w, conv_b)
    d_gu = jnp.concatenate([d_gate, d_up], axis=1)
    d_n2 = mm("mm_dn2", d_gu, net.weight("w_up"), "nt", F32)
    net.grad("w_up", mm("mm_gup", n2, d_gu, "tn", BF16))
    dh1, d_a, dw_pre2, dw_post1 = _mid_bwd(dy, d_n2, h1, a, pre2_w, post1_w)
    d_mix = mm("mm_dmix", d_a, net.weight("w_out"), "nt", F32)
    net.grad("w_out", mm("mm_gout", mixed, d_a, "tn", BF16))
    d_rq, d_rk, d_rv, d_rg, dw_gn = _ret_bwd(proj, d_mix, ry, states, cos, sin, tables, gn_w, rh)
    (d_sq, d_sk, d_sv, dw_sb), extra = _sb_bwd(proj, d_mix, sy, aoff, sb_w, sh, sb_col0, rw // SB_DH,
                                               net.carry("sb_bwd"))
    net.took("sb_bwd", extra)
    d_proj = jnp.concatenate([d_rq, d_rk, d_rv, d_rg, d_sq, d_sk, d_sv], axis=1)
    net.grad("w_in", mm("mm_gin", n1, d_proj, "tn", BF16))
    d_n1 = mm("mm_dn1", d_proj, net.weight("w_in"), "nt", F32)
    grad_x, g_meta, dw_pre1 = _pre_bwd(dh1, d_n1, h0, pre1_w)

    small = dict(loss=loss, meta=g_meta, pre1=dw_pre1, gn=dw_gn, sb=dw_sb, post1=dw_post1, pre2=dw_pre2,
                 conv_w=ffn_stats[0:3], conv_b=ffn_stats[3:4], post2=dw_post2)
    return grad_x, small


def _coords():
    return lax.axis_index("x"), lax.axis_index("y"), lax.axis_index("c")


def _other_chips(x, y):
    return [(1 - x, y), (x, 1 - y), (1 - x, 1 - y)]


ANY = pl.BlockSpec(memory_space=pl.ANY)
VM = pl.BlockSpec(memory_space=pltpu.VMEM)


def _gather4_small(v):
    r = v.shape[0]

    def kern(v_ref, o_ref, send, recv):
        x, y, c = _coords()
        o_ref[2 * x + y] = v_ref[...]
        cps = [pltpu.make_async_remote_copy(v_ref, o_ref.at[2 * x + y], send.at[j], recv.at[j],
                                            device_id=(px, py, c), device_id_type=MESH)
               for j, (px, py) in enumerate(_other_chips(x, y))]
        for cp in cps:
            cp.start()
        for cp in cps:
            cp.wait()

    return pl.pallas_call(
        kern, name="gather_small", in_specs=[VM], out_specs=VM,
        out_shape=jax.ShapeDtypeStruct((4, r, 128), F32),
        scratch_shapes=[pltpu.SemaphoreType.DMA((3,)), pltpu.SemaphoreType.DMA((3,))],
    )(v)


def _allreduce8_small(v):
    r = v.shape[0]
    flips = [(fx, fy, fc) for fx in (0, 1) for fy in (0, 1) for fc in (0, 1)][1:]

    def kern(v_ref, o_ref, buf, send, recv):
        x, y, c = _coords()
        me = 4 * x + 2 * y + c
        buf[me] = v_ref[...]
        cps = [pltpu.make_async_remote_copy(v_ref, buf.at[me], send.at[j], recv.at[j],
                                            device_id=(x ^ fx, y ^ fy, c ^ fc), device_id_type=MESH)
               for j, (fx, fy, fc) in enumerate(flips)]
        for cp in cps:
            cp.start()
        for cp in cps:
            cp.wait()
        tot = buf[0]
        for j in range(1, 8):
            tot = tot + buf[j]
        o_ref[...] = tot

    return pl.pallas_call(
        kern, name="allreduce_small", in_specs=[VM], out_specs=VM,
        out_shape=jax.ShapeDtypeStruct((r, 128), F32),
        scratch_shapes=[pltpu.VMEM((8, r, 128), F32), pltpu.SemaphoreType.DMA((7,)),
                        pltpu.SemaphoreType.DMA((7,))],
    )(v)


def _piece(ref, col_sharded, rows, cols, s, hc):
    half = rows // 2
    if col_sharded:
        return ref.at[pl.ds(pl.multiple_of(hc * half, 16), half), pl.ds(pl.multiple_of(s * cols, 128), cols)]
    return ref.at[pl.ds(pl.multiple_of(s * rows + hc * half, 16), half), :]


def _cast_place(w32, c_id, s_id, col_sharded, name):
    rows, cols = w32.shape
    full_shape = (rows, 4 * cols) if col_sharded else (4 * rows, cols)
    tr = max(d for d in _divisors(rows, 16) if d * cols * 4 <= 4 * 2 ** 20)
    nt = rows // tr

    def kern(c_ref, s_ref, w_ref, o_ref):
        o_ref[...] = w_ref[...].astype(BF16)

    if col_sharded:
        o_spec = pl.BlockSpec((tr, cols), lambda t, c_ref, s_ref: (t, s_ref[0]))
    else:
        o_spec = pl.BlockSpec((tr, cols), lambda t, c_ref, s_ref: (s_ref[0] * nt + t, 0))
    return pl.pallas_call(
        kern, name=name,
        grid_spec=pltpu.PrefetchScalarGridSpec(
            num_scalar_prefetch=2, grid=(nt,),
            in_specs=[pl.BlockSpec((tr, cols), lambda t, c_ref, s_ref: (t, 0))], out_specs=o_spec),
        out_shape=jax.ShapeDtypeStruct(full_shape, BF16),
        compiler_params=_cp(("parallel",)),
    )(c_id, s_id, w32)


def _gather_exchange(full, rows, cols, col_sharded):
    def copies(w_ref, o_ref, send, recv):
        x, y, c = _coords()
        s = 2 * x + y
        sib = (x, y, 1 - c)
        chips = _other_chips(x, y)
        pc = functools.partial(_piece, o_ref, col_sharded, rows, cols)
        mine = _piece(w_ref, col_sharded, rows, cols, s, c)
        first = [pltpu.make_async_remote_copy(mine, pc(s, c), send.at[j], recv.at[j],
                                              device_id=(px, py, c), device_id_type=MESH)
                 for j, (px, py) in enumerate(chips)]
        passed = [pltpu.make_async_remote_copy(pc(2 * px + py, c), pc(2 * px + py, c), send.at[3 + j], recv.at[3 + j],
                                               device_id=sib, device_id_type=MESH)
                  for j, (px, py) in enumerate(chips)]
        from_sib = [pltpu.make_async_remote_copy(pc(2 * px + py, 1 - c), pc(2 * px + py, 1 - c), send.at[3 + j],
                                                 recv.at[3 + j], device_id=sib, device_id_type=MESH)
                    for j, (px, py) in enumerate(chips)]
        return first, passed, from_sib

    def start(ins, outs, send, recv):
        for cp in copies(ins[0], outs[0], send, recv)[0]:
            cp.start()

    def finish(ins, outs, send, recv):
        first, passed, from_sib = copies(ins[0], outs[0], send, recv)
        for j in range(3):
            first[j].wait_recv()
            passed[j].start()
        for cp in from_sib:
            cp.wait_recv()
        for cp in first + passed:
            cp.wait_send()

    return _Exchange([full], [jax.ShapeDtypeStruct(full.shape, BF16)], {0: 0}, 6, start, finish)


def _pair_exchange(g, col_sharded, rows, cols):
    half = rows // 2

    def copies(g_ref, o_ref, send, recv):
        x, y, c = _coords()
        return [pltpu.make_async_remote_copy(_piece(g_ref, col_sharded, rows, cols, s, 1 - c), o_ref.at[s],
                                             send.at[s], recv.at[s], device_id=(x, y, 1 - c), device_id_type=MESH)
                for s in range(4)]

    def start(ins, outs, send, recv):
        for cp in copies(ins[0], outs[0], send, recv):
            cp.start()

    def finish(ins, outs, send, recv):
        for cp in copies(ins[0], outs[0], send, recv):
            cp.wait()

    return _Exchange([g], [jax.ShapeDtypeStruct((4, half, cols), BF16)], {}, 4, start, finish)


def _half_tiles(half, cols):
    tr = max(d for d in _divisors(half, 16) if d * cols * 4 <= 4 * 2 ** 20)
    return tr, half // tr


def _half_spec(col_sharded, tr, nt, cols, own):
    which = (lambda s, s_ref: s_ref[0]) if own else (lambda s, s_ref: (s_ref[0] + 1 + s) % 4)
    if col_sharded:
        return pl.BlockSpec((tr, cols), lambda s, t, c_ref, s_ref: (c_ref[0] * nt + t, which(s, s_ref)))
    return pl.BlockSpec((tr, cols), lambda s, t, c_ref, s_ref: ((2 * which(s, s_ref) + c_ref[0]) * nt + t, 0))


def _rs_add(g, r1, c_id, s_id, col_sharded, rows, cols, name):
    half = rows // 2
    tr, nt = _half_tiles(half, cols)

    def kern(c_ref, s_ref, g_ref, r_ref, o_ref):
        o_ref[0] = (g_ref[...].astype(F32) + r_ref[0].astype(F32)).astype(BF16)

    slab = pl.BlockSpec((1, tr, cols), lambda s, t, c_ref, s_ref: ((s_ref[0] + 1 + s) % 4, t, 0))
    return pl.pallas_call(
        kern, name=name,
        grid_spec=pltpu.PrefetchScalarGridSpec(
            num_scalar_prefetch=2, grid=(3, nt),
            in_specs=[_half_spec(col_sharded, tr, nt, cols, False), slab], out_specs=slab),
        out_shape=jax.ShapeDtypeStruct((4, half, cols), BF16),
        compiler_params=_cp(("parallel", "parallel")),
    )(c_id, s_id, g, r1)


def _scatter_exchange(p):
    _, half, cols = p.shape

    def copies(p_ref, o_ref, send, recv):
        x, y, c = _coords()
        return [pltpu.make_async_remote_copy(p_ref.at[2 * px + py], o_ref.at[j], send.at[j], recv.at[j],
                                             device_id=(px, py, c), device_id_type=MESH)
                for j, (px, py) in enumerate(_other_chips(x, y))]

    def start(ins, outs, send, recv):
        for cp in copies(ins[0], outs[0], send, recv):
            cp.start()

    def finish(ins, outs, send, recv):
        for cp in copies(ins[0], outs[0], send, recv):
            cp.wait()

    return _Exchange([p], [jax.ShapeDtypeStruct((3, half, cols), BF16)], {}, 3, start, finish)


def _rs_total(g, r1, r2, c_id, s_id, col_sharded, rows, cols, name):
    half = rows // 2
    tr, nt = _half_tiles(half, cols)

    def kern(c_ref, s_ref, g_ref, r1_ref, r2_ref, o_ref):
        tot = g_ref[...].astype(F32) + r1_ref[0].astype(F32)
        for j in range(3):
            tot = tot + r2_ref[j].astype(F32)
        o_ref[...] = tot

    return pl.pallas_call(
        kern, name=name,
        grid_spec=pltpu.PrefetchScalarGridSpec(
            num_scalar_prefetch=2, grid=(1, nt),
            in_specs=[_half_spec(col_sharded, tr, nt, cols, True),
                      pl.BlockSpec((1, tr, cols), lambda s, t, c_ref, s_ref: (s_ref[0], t, 0)),
                      pl.BlockSpec((3, tr, cols), lambda s, t, c_ref, s_ref: (0, t, 0))],
            out_specs=pl.BlockSpec((tr, cols), lambda s, t, c_ref, s_ref: (c_ref[0] * nt + t, 0))),
        out_shape=jax.ShapeDtypeStruct((rows, cols), F32),
        compiler_params=_cp(("parallel", "parallel")),
    )(c_id, s_id, g, r1, r2)


def _rs_exchange(t, name):
    rows, cols = t.shape
    half = rows // 2

    def kern(t_ref, o_ref, send, recv):
        x, y, c = _coords()
        mine = pl.ds(pl.multiple_of(c * half, 8), half)
        cp = pltpu.make_async_remote_copy(t_ref.at[mine, :], o_ref.at[mine, :], send, recv,
                                          device_id=(x, y, 1 - c), device_id_type=MESH)
        cp.start()
        cp.wait()

    return pl.pallas_call(
        kern, name=name, in_specs=[ANY], out_specs=ANY, input_output_aliases={0: 0},
        out_shape=jax.ShapeDtypeStruct((rows, cols), F32),
        scratch_shapes=[pltpu.SemaphoreType.DMA, pltpu.SemaphoreType.DMA],
    )(t)


class _Whole:
    def __init__(self, full):
        self.full, self.grads = dict(full), {}

    def weight(self, n):
        return self.full[n]

    def carry(self, host):
        return []

    def took(self, host, outs):
        pass

    def grad(self, n, g):
        self.grads[n] = g


BIG = ("w_in", "w_out", "w_up", "w_down")
COL_SHARDED = dict(w_in=True, w_out=False, w_up=True, w_down=False)
GATHER_ON = dict(mm_proj="w_out", sb_fwd="w_up", mm_up="w_down")
PAIR_ON = dict(mm_dn2="w_down", mm_dmix="w_up")
SCATTER_ON = dict(mm_gup=("w_down",), sb_bwd=("w_up", "w_out"), mm_dn1=("w_in",))


class _Sharded(_Whole):
    def __init__(self, shards, c_id, s_id):
        self.shards, self.c_id, self.s_id = shards, c_id, s_id
        self.full, self.grads, self.r1, self.p, self.r2 = {}, {}, {}, {}, {}
        self.placed = {n: _cast_place(shards[n], c_id, s_id, COL_SHARDED[n], "place_" + n) for n in BIG}
        self.full["w_in"], = _run_exchange(self._gather("w_in"), "gather_w_in")

    def _gather(self, n):
        return _gather_exchange(self.placed[n], *self.shards[n].shape, COL_SHARDED[n])

    def _pair(self, n):
        return _pair_exchange(self.grads[n], COL_SHARDED[n], *self.shards[n].shape)

    def _paired(self, n, r1):
        rows, cols = self.shards[n].shape
        self.r1[n] = r1
        self.p[n] = _rs_add(self.grads[n], r1, self.c_id, self.s_id, COL_SHARDED[n], rows, cols, "rs_add_" + n)

    def carry(self, host):
        if host in GATHER_ON:
            return [self._gather(GATHER_ON[host])]
        if host in PAIR_ON:
            return [self._pair(PAIR_ON[host])]
        return [_scatter_exchange(self.p[n]) for n in SCATTER_ON.get(host, ())]

    def took(self, host, outs):
        if host in GATHER_ON:
            self.full[GATHER_ON[host]], = outs
        if host in PAIR_ON:
            self._paired(PAIR_ON[host], outs[0])
        for n, r2 in zip(SCATTER_ON.get(host, ()), outs):
            self.r2[n] = r2

    def grad(self, n, g):
        self.grads[n] = g
        if n not in PAIR_ON.values():
            self._paired(n, _run_exchange(self._pair(n), "rs_pair_" + n)[0])

    def reduced(self, n):
        rows, cols = self.shards[n].shape
        t = _rs_total(self.grads[n], self.r1[n], self.r2[n], self.c_id, self.s_id, COL_SHARDED[n], rows, cols,
                      "rs_total_" + n)
        return _rs_exchange(t, "rs_exchange_" + n)


def _adamw_vals(w, g, m, v):
    m = ADAM_B1 * m + (1.0 - ADAM_B1) * g
    v = ADAM_B2 * v + (1.0 - ADAM_B2) * (g * g)
    m_hat = m / (1.0 - ADAM_B1 ** ADAM_STEP)
    v_hat = v / (1.0 - ADAM_B2 ** ADAM_STEP)
    delta = -ADAM_LR * (m_hat / (jnp.sqrt(v_hat) + ADAM_EPS) + ADAM_WD * w)
    return delta, m, v


def _adamw(w, g, m, v, name):
    rows, cols = w.shape
    tr = max(d for d in _divisors(rows, 8) if d * cols * 4 <= 2 * 2 ** 20)

    def kern(w_ref, g_ref, m_ref, v_ref, d_ref, mo_ref, vo_ref, go_ref):
        g = g_ref[...]
        d_ref[...], mo_ref[...], vo_ref[...] = _adamw_vals(w_ref[...], g, m_ref[...], v_ref[...])
        go_ref[...] = g

    spec = pl.BlockSpec((tr, cols), lambda i: (i, 0))
    return pl.pallas_call(
        kern, name=name, grid=(rows // tr,), in_specs=[spec] * 4, out_specs=[spec] * 4,
        out_shape=[jax.ShapeDtypeStruct((rows, cols), F32)] * 4,
        compiler_params=_cp(("parallel",)),
    )(w, g, m, v)


def _pack(arrs):
    flat = []
    for a in arrs:
        a = a.reshape(-1)
        flat.append(jnp.pad(a, (0, (-a.shape[0]) % 1024)))
    return jnp.concatenate(flat).reshape(-1, 128)


def _unpack(slab, shapes):
    out, off = [], 0
    flat = slab.reshape(slab.shape[:-2] + (-1,))
    for shp in shapes:
        n = math.prod(shp)
        out.append(flat[..., off:off + n].reshape(slab.shape[:-2] + tuple(shp)))
        off += n + (-n) % 1024
    return out


SMALL = ("meta_tokens", "attn_pre_norm_w", "ret_gn_w", "sb_norm_w", "attn_post_norm_w", "ffn_pre_norm_w",
         "conv_w", "conv_b", "ffn_post_norm_w")
ORDER = ("meta_tokens", "attn_pre_norm_w", "w_in", "ret_gn_w", "sb_norm_w", "w_out", "attn_post_norm_w",
         "ffn_pre_norm_w", "w_up", "conv_w", "conv_b", "w_down", "ffn_post_norm_w")


def kernel(x, meta_tokens, attn_pre_norm_w, w_in, ret_gn_w, sb_norm_w, w_out, attn_post_norm_w, ffn_pre_norm_w, w_up, conv_w, conv_b, w_down, ffn_post_norm_w, loss_target, m_meta_tokens, m_attn_pre_norm_w, m_w_in, m_ret_gn_w, m_sb_norm_w, m_w_out, m_attn_post_norm_w, m_ffn_pre_norm_w, m_w_up, m_conv_w, m_conv_b, m_w_down, m_ffn_post_norm_w, v_meta_tokens, v_attn_pre_norm_w, v_w_in, v_ret_gn_w, v_sb_norm_w, v_w_out, v_attn_post_norm_w, v_ffn_pre_norm_w, v_w_up, v_conv_w, v_conv_b, v_w_down, v_ffn_post_norm_w):
    args = dict(locals())
    w = {n: args[n] for n in ORDER}
    m = {n: args["m_" + n] for n in ORDER}
    v = {n: args["v_" + n] for n in ORDER}
    xi, yi, ci = _coords()
    shard_id = 2 * xi + yi
    c_id = ci.astype(jnp.int32).reshape(1)
    s_id = shard_id.astype(jnp.int32).reshape(1)
    d = x.shape[-1]

    mshape, cshape = w["meta_tokens"].shape, w["conv_w"][0].shape
    got = _unpack(_gather4_small(_pack([w["meta_tokens"], w["conv_w"][0]])), [mshape, cshape])
    meta_full = jnp.moveaxis(got[0], 0, 1).reshape(N_META, d)
    conv_w_full = jnp.moveaxis(got[1], 0, 1).reshape(CONV_W, -1)

    net = _Sharded({n: w[n][0] for n in BIG}, c_id, s_id)
    grad_x, g_small = _local_step(
        x[0], loss_target[0], meta_full, net, conv_w_full, w["conv_b"], w["attn_pre_norm_w"], w["ret_gn_w"],
        w["sb_norm_w"], w["attn_post_norm_w"], w["ffn_pre_norm_w"], w["ffn_post_norm_w"])

    names = ("loss", "meta", "pre1", "gn", "sb", "post1", "pre2", "conv_w", "conv_b", "post2")
    tot = _unpack(_allreduce8_small(_pack([g_small[n] for n in names])), [g_small[n].shape for n in names])
    tot = dict(zip(names, tot))
    loss = tot["loss"][0, 0]
    mcols, ccols = mshape[1], cshape[1]
    grads = {
        "meta_tokens": lax.dynamic_slice_in_dim(tot["meta"], shard_id * mcols, mcols, axis=1),
        "attn_pre_norm_w": tot["pre1"], "ret_gn_w": tot["gn"], "sb_norm_w": tot["sb"],
        "attn_post_norm_w": tot["post1"], "ffn_pre_norm_w": tot["pre2"],
        "conv_w": lax.dynamic_slice_in_dim(tot["conv_w"], shard_id * ccols, ccols, axis=1)[None],
        "conv_b": tot["conv_b"], "ffn_post_norm_w": tot["post2"],
    }

    delta, new_m, new_v = {}, {}, {}
    for n in BIG:
        dl, mo, vo, g = _adamw(w[n][0], net.reduced(n), m[n][0], v[n][0], "adamw_" + n)
        delta[n], new_m[n], new_v[n], grads[n] = dl[None], mo[None], vo[None], g[None]
    shapes = [w[n].shape for n in SMALL]
    packed = [_pack([src[n] for n in SMALL]) for src in (w, grads, m, v)]
    outs = _adamw(*packed, "adamw_small")[:3]
    for dst, slab in zip((delta, new_m, new_v), outs):
        for n, a in zip(SMALL, _unpack(slab, shapes)):
            dst[n] = a

    return (loss, grad_x[None], *[grads[n] for n in ORDER], *[delta[n] for n in ORDER],
            *[new_m[n] for n in ORDER], *[new_v[n] for n in ORDER])
```

```python
import functools
import math

import jax
import jax.numpy as jnp
from jax import lax
from jax.experimental import pallas as pl
from jax.experimental.pallas import tpu as pltpu

F32 = jnp.float32
BF16 = jnp.bfloat16
MESH = pl.DeviceIdType.MESH

EPS = 1e-6
ROPE_BASE = 10000.0
N_META = 16
CHUNK = 128
N_PAD = CHUNK - N_META
RET_DK = 256
SB_DH = 128
CONV_W = 3

ADAM_LR = 0.001
ADAM_B1 = 0.9
ADAM_B2 = 0.999
ADAM_EPS = 1e-08
ADAM_WD = 0.01
ADAM_STEP = 10

V7X_VMEM_BYTES = 64 * 2 ** 20
VMEM_LIMIT = V7X_VMEM_BYTES - 8 * 2 ** 20
MM_BUDGET = 40 * 2 ** 20
V7X_BF16_FLOPS = 0.9e15
V7X_HBM_BPS = 2.0e12
GRID_STEP_S = 0.6e-6
ACC_BPS = 2.0e13

NN = (((1,), (0,)), ((), ()))
NT = (((1,), (1,)), ((), ()))
TN = (((0,), (0,)), ((), ()))


def _cp(sem, vmem=VMEM_LIMIT):
    return pltpu.CompilerParams(dimension_semantics=sem, vmem_limit_bytes=vmem)


def _dot(a, b, dims=NN):
    return lax.dot_general(a, b, dims, preferred_element_type=F32)


def _sigmoid(x):
    return 1.0 / (1.0 + jnp.exp(-x))


def _divisors(n, align):
    return [d for d in range(align, n + 1, align) if n % d == 0]


def _mm_tiles(mode, m, n, k, out_bytes):
    best = None
    tms = [d for d in _divisors(m, 128 if mode == "tn" else 16) if d >= 128]
    tns = [d for d in _divisors(n, 256)] or [d for d in _divisors(n, 128)]
    tks = [d for d in _divisors(k, 128) if d >= 128]
    flops = 2.0 * m * n * k
    for tm in tms:
        for tn in tns:
            if tm * tn > 2112 * 1024:
                continue
            for tk in tks:
                nk = k // tk
                foot = 4 * (tm * tk + tk * tn) + 2 * tm * tn * out_bytes
                if nk > 1:
                    foot += 4 * tm * tn
                if foot > MM_BUDGET:
                    continue
                steps = (m // tm) * (n // tn) * nk
                for swap in (False, True):
                    if nk > 1:
                        traffic = 2.0 * (n // tn) * m * k + 2.0 * (m // tm) * n * k
                    elif swap:
                        traffic = 2.0 * n * k + 2.0 * (n // tn) * m * k
                    else:
                        traffic = 2.0 * m * k + 2.0 * (m // tm) * n * k
                    traffic += out_bytes * m * n
                    t = max(flops / V7X_BF16_FLOPS, traffic / V7X_HBM_BPS) + steps * GRID_STEP_S
                    if nk > 1:
                        t += 12.0 * m * n * nk / ACC_BPS
                    if best is None or t < best[0]:
                        best = (t, tm, tn, tk, swap)
    assert best is not None, (mode, m, n, k)
    return best[1:]


class _Exchange:
    def __init__(self, operands, out_shapes, aliases, nsem, start, finish):
        self.operands, self.out_shapes, self.aliases, self.nsem = operands, out_shapes, aliases, nsem
        self.start, self.finish = start, finish


def _carried(exchanges, in_base, out_base):
    ins = [a for e in exchanges for a in e.operands]
    outs = [s for e in exchanges for s in e.out_shapes]
    sems = [pltpu.SemaphoreType.DMA((e.nsem,)) for e in exchanges for _ in (0, 1)]
    alias, i, o = {}, in_base, out_base
    for e in exchanges:
        alias.update({i + k: o + v for k, v in e.aliases.items()})
        i, o = i + len(e.operands), o + len(e.out_shapes)
    return ins, outs, sems, alias


def _run_carried(exchanges, phase, in_refs, out_refs, sem_refs):
    i = o = 0
    for n, e in enumerate(exchanges):
        ni, no = len(e.operands), len(e.out_shapes)
        getattr(e, phase)(in_refs[i:i + ni], out_refs[o:o + no], sem_refs[2 * n], sem_refs[2 * n + 1])
        i, o = i + ni, o + no


def _run_exchange(e, name):
    ins, outs, sems, alias = _carried([e], 0, 0)

    def kern(*refs):
        in_refs, out_refs, sem_refs = refs[:len(ins)], refs[len(ins):len(ins) + len(outs)], refs[len(ins) + len(outs):]
        _run_carried([e], "start", in_refs, out_refs, sem_refs)
        _run_carried([e], "finish", in_refs, out_refs, sem_refs)

    return pl.pallas_call(
        kern, name=name, in_specs=[ANY] * len(ins), out_specs=[ANY] * len(outs), out_shape=outs,
        input_output_aliases=alias, scratch_shapes=sems,
    )(*ins)


def _matmul(a, b, mode, out_dtype, name, carry=()):
    if mode == "nn":
        (m, k), (k2, n) = a.shape, b.shape
    elif mode == "nt":
        (m, k), (n, k2) = a.shape, b.shape
    else:
        (k, m), (k2, n) = a.shape, b.shape
    assert k == k2 and a.dtype == BF16 and b.dtype == BF16
    tm, tn, tk, swap = _mm_tiles(mode, m, n, k, jnp.dtype(out_dtype).itemsize)
    nk = k // tk
    dims = {"nn": NN, "nt": NT, "tn": TN}[mode]

    def ij(g0, g1):
        return (g1, g0) if swap else (g0, g1)

    if mode == "tn":
        a_spec = pl.BlockSpec((tk, tm), lambda g0, g1, kk: (kk, ij(g0, g1)[0]))
    else:
        a_spec = pl.BlockSpec((tm, tk), lambda g0, g1, kk: (ij(g0, g1)[0], kk))
    if mode == "nt":
        b_spec = pl.BlockSpec((tn, tk), lambda g0, g1, kk: (ij(g0, g1)[1], kk))
    else:
        b_spec = pl.BlockSpec((tk, tn), lambda g0, g1, kk: (kk, ij(g0, g1)[1]))
    o_spec = pl.BlockSpec((tm, tn), lambda g0, g1, kk: ij(g0, g1))

    grid = (n // tn, m // tm, nk) if swap else (m // tm, n // tn, nk)
    carry = list(carry)
    x_in, x_out, x_sems, alias = _carried(carry, 2, 1)
    acc_shapes = [pltpu.VMEM((tm, tn), F32)] if nk > 1 else []

    def kern(a_ref, b_ref, *rest):
        x_in_refs, o_ref = rest[:len(x_in)], rest[len(x_in)]
        x_out_refs = rest[len(x_in) + 1:len(x_in) + 1 + len(x_out)]
        tail = rest[len(x_in) + 1 + len(x_out):]
        acc, sem_refs = tail[:len(acc_shapes)], tail[len(acc_shapes):]
        pid = [pl.program_id(ax) for ax in range(3)]
        if carry:
            @pl.when((pid[0] == 0) & (pid[1] == 0) & (pid[2] == 0))
            def _():
                _run_carried(carry, "start", x_in_refs, x_out_refs, sem_refs)

        prod = _dot(a_ref[...], b_ref[...], dims)
        if nk == 1:
            o_ref[...] = prod.astype(out_dtype)
        else:
            kk = pid[2]

            @pl.when(kk == 0)
            def _():
                acc[0][...] = prod

            @pl.when(kk > 0)
            def _():
                acc[0][...] += prod

            @pl.when(kk == nk - 1)
            def _():
                o_ref[...] = acc[0][...].astype(out_dtype)

        if carry:
            @pl.when((pid[0] == grid[0] - 1) & (pid[1] == grid[1] - 1) & (pid[2] == grid[2] - 1))
            def _():
                _run_carried(carry, "finish", x_in_refs, x_out_refs, sem_refs)

    sem = ("arbitrary",) * 3 if carry else ("parallel", "parallel", "arbitrary")
    res = pl.pallas_call(
        kern, name=name, grid=grid, in_specs=[a_spec, b_spec] + [ANY] * len(x_in),
        out_specs=[o_spec] + [ANY] * len(x_out),
        out_shape=[jax.ShapeDtypeStruct((m, n), out_dtype)] + x_out,
        input_output_aliases=alias, scratch_shapes=acc_shapes + x_sems,
        compiler_params=_cp(sem),
    )(a, b, *x_in)
    return (res[0], res[1:]) if carry else res[0]


def _rms_fwd(x, w):
    r = lax.rsqrt(jnp.mean(x * x, axis=-1, keepdims=True) + EPS)
    return x * r * w


def _rms_bwd(x, w, g):
    r = lax.rsqrt(jnp.mean(x * x, axis=-1, keepdims=True) + EPS)
    gw = g * w
    dx = r * gw - x * (r * r * r * jnp.mean(gw * x, axis=-1, keepdims=True))
    return dx, g * (x * r)


def _row_spec(d):
    return pl.BlockSpec((CHUNK, d), lambda i: (i, 0))


def _vec_spec(d):
    return pl.BlockSpec((1, d), lambda i: (0, 0))


def _prenorm(h0, w):
    tp, d = h0.shape

    def kern(h_ref, w_ref, o_ref):
        o_ref[...] = _rms_fwd(h_ref[...], w_ref[...]).astype(BF16)

    return pl.pallas_call(
        kern, name="prenorm1", grid=(tp // CHUNK,),
        in_specs=[_row_spec(d), _vec_spec(d)], out_specs=_row_spec(d),
        out_shape=jax.ShapeDtypeStruct((tp, d), BF16),
        compiler_params=_cp(("parallel",)),
    )(h0, w)


def _mid_fwd(h0, a, w_post, w_pre):
    tp, d = h0.shape

    def kern(h_ref, a_ref, wp_ref, wq_ref, h1_ref, n2_ref):
        h1 = h_ref[...] + _rms_fwd(a_ref[...], wp_ref[...])
        h1_ref[...] = h1
        n2_ref[...] = _rms_fwd(h1, wq_ref[...]).astype(BF16)

    return pl.pallas_call(
        kern, name="mid_fwd", grid=(tp // CHUNK,),
        in_specs=[_row_spec(d), _row_spec(d), _vec_spec(d), _vec_spec(d)],
        out_specs=[_row_spec(d), _row_spec(d)],
        out_shape=[jax.ShapeDtypeStruct((tp, d), F32), jax.ShapeDtypeStruct((tp, d), BF16)],
        compiler_params=_cp(("parallel",)),
    )(h0, a, w_post, w_pre)


def _loss_bwd(h1, f, tgt, w_post):
    tp, d = h1.shape

    def kern(h_ref, f_ref, t_ref, w_ref, dy_ref, df_ref, loss_ref, dw_ref):
        i = pl.program_id(0)

        @pl.when(i == 0)
        def _():
            dy_ref[...] = jnp.zeros_like(dy_ref)
            df_ref[...] = jnp.zeros_like(df_ref)
            loss_ref[...] = jnp.zeros_like(loss_ref)
            dw_ref[...] = jnp.zeros_like(dw_ref)

        @pl.when(i > 0)
        def _():
            fv = f_ref[...]
            w = w_ref[...]
            err = h_ref[...] + _rms_fwd(fv, w) - t_ref[...]
            loss_ref[...] += 0.5 * jnp.sum(jnp.mean(err * err, axis=-1, keepdims=True))
            dy = err * (1.0 / d)
            dy_ref[...] = dy
            dfv, dwr = _rms_bwd(fv, w, dy)
            df_ref[...] = dfv.astype(BF16)
            dw_ref[...] += jnp.sum(dwr, axis=0, keepdims=True)

    return pl.pallas_call(
        kern, name="loss_bwd", grid=(tp // CHUNK,),
        in_specs=[_row_spec(d), _row_spec(d),
                  pl.BlockSpec((CHUNK, d), lambda i: (jnp.maximum(i - 1, 0), 0)), _vec_spec(d)],
        out_specs=[_row_spec(d), _row_spec(d), pl.BlockSpec((1, 128), lambda i: (0, 0)), _vec_spec(d)],
        out_shape=[jax.ShapeDtypeStruct((tp, d), F32), jax.ShapeDtypeStruct((tp, d), BF16),
                   jax.ShapeDtypeStruct((1, 128), F32), jax.ShapeDtypeStruct((1, d), F32)],
        compiler_params=_cp(("arbitrary",)),
    )(h1, f, tgt, w_post)


def _mid_bwd(dy, dn2, h1, a, w_pre, w_post):
    tp, d = h1.shape

    def kern(dy_ref, dn_ref, h_ref, a_ref, wq_ref, wp_ref, dh_ref, da_ref, dwq_ref, dwp_ref):
        @pl.when(pl.program_id(0) == 0)
        def _():
            dwq_ref[...] = jnp.zeros_like(dwq_ref)
            dwp_ref[...] = jnp.zeros_like(dwp_ref)

        dx, dwq = _rms_bwd(h_ref[...], wq_ref[...], dn_ref[...])
        dh = dy_ref[...] + dx
        dh_ref[...] = dh
        da, dwp = _rms_bwd(a_ref[...], wp_ref[...], dh)
        da_ref[...] = da.astype(BF16)
        dwq_ref[...] += jnp.sum(dwq, axis=0, keepdims=True)
        dwp_ref[...] += jnp.sum(dwp, axis=0, keepdims=True)

    return pl.pallas_call(
        kern, name="mid_bwd", grid=(tp // CHUNK,),
        in_specs=[_row_spec(d)] * 4 + [_vec_spec(d)] * 2,
        out_specs=[_row_spec(d), _row_spec(d), _vec_spec(d), _vec_spec(d)],
        out_shape=[jax.ShapeDtypeStruct((tp, d), F32), jax.ShapeDtypeStruct((tp, d), BF16),
                   jax.ShapeDtypeStruct((1, d), F32), jax.ShapeDtypeStruct((1, d), F32)],
        compiler_params=_cp(("arbitrary",)),
    )(dy, dn2, h1, a, w_pre, w_post)


def _pre_bwd(dh1, dn1, h0, w_pre):
    tp, d = h0.shape
    s = tp - CHUNK

    def kern(dh_ref, dn_ref, h_ref, w_ref, gx_ref, gm_ref, dw_ref):
        i = pl.program_id(0)
        dx, dwr = _rms_bwd(h_ref[...], w_ref[...], dn_ref[...])
        dh0 = dh_ref[...] + dx
        gx_ref[...] = dh0

        @pl.when(i == 0)
        def _():
            gm_ref[...] = dh0[N_PAD:, :]
            dw_ref[...] = jnp.zeros_like(dw_ref)

        dw_ref[...] += jnp.sum(dwr, axis=0, keepdims=True)

    return pl.pallas_call(
        kern, name="pre_bwd", grid=(tp // CHUNK,),
        in_specs=[_row_spec(d)] * 3 + [_vec_spec(d)],
        out_specs=[pl.BlockSpec((CHUNK, d), lambda i: (jnp.maximum(i - 1, 0), 0)),
                   pl.BlockSpec((N_META, d), lambda i: (0, 0)), _vec_spec(d)],
        out_shape=[jax.ShapeDtypeStruct((s, d), F32), jax.ShapeDtypeStruct((N_META, d), F32),
                   jax.ShapeDtypeStruct((1, d), F32)],
        compiler_params=_cp(("arbitrary",)),
    )(dh1, dn1, h0, w_pre)


def _ffn_cols(dff):
    return dff // 2 if dff % 256 == 0 else dff


HALO = 16


def _ffn_fwd(gu, conv_w, conv_b):
    tp, two_dff = gu.shape
    dff = two_dff // 2
    tc = _ffn_cols(dff)
    nj = dff // tc
    r8 = CHUNK // HALO

    def kern(g_ref, gp_ref, u_ref, w_ref, b_ref, o_ref):
        i = pl.program_id(1)
        prev = gp_ref[...].astype(F32) * (i > 0).astype(F32)
        ext = jnp.concatenate([prev, g_ref[...].astype(F32)], axis=0)
        w = w_ref[...]
        conv = (b_ref[...] + w[0:1] * pltpu.roll(ext, 2, 0)[HALO:] + w[1:2] * pltpu.roll(ext, 1, 0)[HALO:]
                + w[2:3] * ext[HALO:])
        o_ref[...] = (conv * _sigmoid(conv) * u_ref[...].astype(F32)).astype(BF16)

    return pl.pallas_call(
        kern, name="ffn_fwd", grid=(nj, tp // CHUNK),
        in_specs=[pl.BlockSpec((CHUNK, tc), lambda j, i: (i, j)),
                  pl.BlockSpec((HALO, tc), lambda j, i: (jnp.maximum(i * r8 - 1, 0), j)),
                  pl.BlockSpec((CHUNK, tc), lambda j, i: (i, j + nj)),
                  pl.BlockSpec((CONV_W, tc), lambda j, i: (0, j)),
                  pl.BlockSpec((1, tc), lambda j, i: (0, j))],
        out_specs=pl.BlockSpec((CHUNK, tc), lambda j, i: (i, j)),
        out_shape=jax.ShapeDtypeStruct((tp, dff), BF16),
        compiler_params=_cp(("parallel", "parallel")),
    )(gu, gu, gu, conv_w, conv_b)


def _ffn_bwd(gu, dact, conv_w, conv_b):
    tp, two_dff = gu.shape
    dff = two_dff // 2
    tc = _ffn_cols(dff)
    nj = dff // tc
    ni = tp // CHUNK
    r8 = CHUNK // HALO

    def kern(g_ref, gp_ref, gn_ref, u_ref, un_ref, d_ref, dn_ref, w_ref, b_ref, dg_ref, du_ref, st_ref):
        i = pl.program_id(1)

        @pl.when(i == 0)
        def _():
            st_ref[...] = jnp.zeros_like(st_ref)

        first = (i > 0).astype(F32)
        last = (i < ni - 1).astype(F32)
        gate = g_ref[...].astype(F32)
        d_main = d_ref[...].astype(F32)
        ext = jnp.concatenate([gp_ref[...].astype(F32) * first, gate, gn_ref[...].astype(F32)], axis=0)
        w = w_ref[...]
        r1 = pltpu.roll(ext, 1, 0)
        r2 = pltpu.roll(ext, 2, 0)
        conv = (b_ref[...] + w[0:1] * r2 + w[1:2] * r1 + w[2:3] * ext)[HALO:]
        up = jnp.concatenate([u_ref[...].astype(F32), un_ref[...].astype(F32)], axis=0)
        da = jnp.concatenate([d_main, dn_ref[...].astype(F32) * last], axis=0)
        sg = _sigmoid(conv)
        dc = da * up * (sg * (1.0 + conv * (1.0 - sg)))
        du_ref[...] = (d_main * (conv * sg)[:CHUNK]).astype(BF16)
        n = CHUNK + HALO
        dgate = w[2:3] * dc + w[1:2] * pltpu.roll(dc, n - 1, 0) + w[0:1] * pltpu.roll(dc, n - 2, 0)
        dg_ref[...] = dgate[:CHUNK].astype(BF16)
        dcm = dc[:CHUNK]
        s0 = jnp.sum(dcm * r2[HALO:HALO + CHUNK], axis=0, keepdims=True)
        s1 = jnp.sum(dcm * r1[HALO:HALO + CHUNK], axis=0, keepdims=True)
        s2 = jnp.sum(dcm * gate, axis=0, keepdims=True)
        s3 = jnp.sum(dcm, axis=0, keepdims=True)
        row = lax.broadcasted_iota(jnp.int32, (8, tc), 0)
        st_ref[...] += jnp.where(row == 0, s0, jnp.where(row == 1, s1, jnp.where(row == 2, s2,
                                 jnp.where(row == 3, s3, 0.0))))

    main = lambda off: pl.BlockSpec((CHUNK, tc), lambda j, i: (i, j + off))
    nxt = lambda off: pl.BlockSpec((HALO, tc), lambda j, i: (jnp.minimum((i + 1) * r8, ni * r8 - 1), j + off))
    return pl.pallas_call(
        kern, name="ffn_bwd", grid=(nj, ni),
        in_specs=[main(0), pl.BlockSpec((HALO, tc), lambda j, i: (jnp.maximum(i * r8 - 1, 0), j)), nxt(0),
                  main(nj), nxt(nj), main(0), nxt(0),
                  pl.BlockSpec((CONV_W, tc), lambda j, i: (0, j)), pl.BlockSpec((1, tc), lambda j, i: (0, j))],
        out_specs=[main(0), main(0), pl.BlockSpec((8, tc), lambda j, i: (0, j))],
        out_shape=[jax.ShapeDtypeStruct((tp, dff), BF16), jax.ShapeDtypeStruct((tp, dff), BF16),
                   jax.ShapeDtypeStruct((8, dff), F32)],
        compiler_params=_cp(("parallel", "arbitrary")),
    )(gu, gu, gu, gu, gu, dact, dact, conv_w, conv_b)


def _rot(x, cs, sn):
    x1, x2 = x[:, :128], x[:, 128:]
    return jnp.concatenate([x1 * cs - x2 * sn, x1 * sn + x2 * cs], axis=1)


def _rot_t(x, cs, sn):
    x1, x2 = x[:, :128], x[:, 128:]
    return jnp.concatenate([x1 * cs + x2 * sn, x2 * cs - x1 * sn], axis=1)


def _ret_tables(rh):
    lg = jnp.log(1.0 - 2.0 ** (-5.0 - jnp.arange(rh, dtype=F32)))
    idx = jnp.arange(CHUNK, dtype=F32)
    diff = idx[:, None] - idx[None, :]
    intra = jnp.where(diff[None] >= 0, jnp.exp(jnp.maximum(diff, 0.0)[None] * lg[:, None, None]), 0.0)
    qdec = jnp.exp((idx[None, :] + 1.0) * lg[:, None])[..., None]
    kdec = jnp.exp((CHUNK - 1.0 - idx[None, :]) * lg[:, None])[..., None]
    cdec = jnp.exp(CHUNK * lg)[:, None, None]
    return intra, qdec, kdec, cdec


def _ret_specs(rh, nc, rev):
    hp = max(d for d in (8, 4, 2, 1) if rh % d == 0)
    ng, wd = rh // hp, hp * RET_DK
    cc = (lambda c: nc - 1 - c) if rev else (lambda c: c)
    col = lambda sec: pl.BlockSpec((CHUNK, wd), lambda h, c: (cc(c), sec * ng + h))
    tab = [pl.BlockSpec((CHUNK, 128), lambda h, c: (cc(c), 0))] * 2
    dec = [pl.BlockSpec((hp, CHUNK, CHUNK), lambda h, c: (h, 0, 0)),
           pl.BlockSpec((hp, CHUNK, 1), lambda h, c: (h, 0, 0)),
           pl.BlockSpec((hp, CHUNK, 1), lambda h, c: (h, 0, 0)),
           pl.BlockSpec((hp, 1, 1), lambda h, c: (h, 0, 0))]
    hw = pl.BlockSpec((1, wd), lambda h, c: (0, h))
    hcol = pl.BlockSpec((CHUNK, wd), lambda h, c: (cc(c), h))
    st = pl.BlockSpec((hp, 1, RET_DK, RET_DK), lambda h, c: (h, cc(c), 0, 0))
    heads = [(hh, slice(hh * RET_DK, (hh + 1) * RET_DK)) for hh in range(hp)]
    return heads, ng, col, tab, dec, hw, hcol, st


def _ret_fwd(proj, cos, sin, tables, gnw, rh):
    tp = proj.shape[0]
    nc = tp // CHUNK
    heads, ng, col, tab, dec, hw, hcol, st = _ret_specs(rh, nc, False)

    def kern(q_ref, k_ref, v_ref, g_ref, cos_ref, sin_ref, in_ref, qd_ref, kd_ref, cd_ref, w_ref,
             out_ref, ry_ref, st_ref, state):
        @pl.when(pl.program_id(1) == 0)
        def _():
            state[...] = jnp.zeros_like(state)

        cs, sn = cos_ref[...], sin_ref[...]
        for hh, sl in heads:
            q = (_rot(q_ref[:, sl], cs, sn) * (RET_DK ** -0.5)).astype(BF16)
            kf = _rot(k_ref[:, sl], cs, sn)
            k = kf.astype(BF16)
            v = v_ref[:, sl].astype(BF16)
            s_old = state[hh]
            s_b = s_old.astype(BF16)
            st_ref[hh, 0] = s_b
            sc = _dot(q, k, NT) * in_ref[hh]
            ry = _dot(sc.astype(BF16), v) + _dot(q, s_b) * qd_ref[hh]
            state[hh] = s_old * cd_ref[hh] + _dot((kf * kd_ref[hh]).astype(BF16), v, TN)
            ry_ref[:, sl] = ry
            g = g_ref[:, sl]
            out_ref[:, sl] = (g * _sigmoid(g) * _rms_fwd(ry, w_ref[:, sl])).astype(BF16)

    return pl.pallas_call(
        kern, name="ret_fwd", grid=(ng, nc),
        in_specs=[col(0), col(1), col(2), col(3)] + tab + dec + [hw],
        out_specs=[hcol, hcol, st],
        out_shape=[jax.ShapeDtypeStruct((tp, rh * RET_DK), BF16), jax.ShapeDtypeStruct((tp, rh * RET_DK), F32),
                   jax.ShapeDtypeStruct((rh, nc, RET_DK, RET_DK), BF16)],
        scratch_shapes=[pltpu.VMEM((len(heads), RET_DK, RET_DK), F32)],
        compiler_params=_cp(("parallel", "arbitrary")),
    )(proj, proj, proj, proj, cos, sin, *tables, gnw)


def _ret_bwd(proj, dmix, ry_all, states, cos, sin, tables, gnw, rh):
    tp = proj.shape[0]
    nc = tp // CHUNK
    heads, ng, col, tab, dec, hw, hcol, st = _ret_specs(rh, nc, True)

    def kern(q_ref, k_ref, v_ref, g_ref, cos_ref, sin_ref, in_ref, qd_ref, kd_ref, cd_ref, w_ref,
             do_ref, ry_ref, st_ref, dq_ref, dk_ref, dv_ref, dg_ref, dw_ref, ds):
        @pl.when(pl.program_id(1) == 0)
        def _():
            ds[...] = jnp.zeros_like(ds)
            dw_ref[...] = jnp.zeros_like(dw_ref)

        cs, sn = cos_ref[...], sin_ref[...]
        for hh, sl in heads:
            q = (_rot(q_ref[:, sl], cs, sn) * (RET_DK ** -0.5)).astype(BF16)
            kf = _rot(k_ref[:, sl], cs, sn)
            k = kf.astype(BF16)
            v = v_ref[:, sl].astype(BF16)
            g = g_ref[:, sl]
            ry = ry_ref[:, sl]
            w = w_ref[:, sl]
            dout = do_ref[:, sl]
            sg = _sigmoid(g)
            dhn = dout * (g * sg)
            dry, dwr = _rms_bwd(ry, w, dhn)
            dg_ref[:, sl] = (dout * _rms_fwd(ry, w) * (sg * (1.0 + g * (1.0 - sg)))).astype(BF16)
            dw_ref[:, sl] += jnp.sum(dwr, axis=0, keepdims=True)

            dmat = in_ref[hh]
            qd, kd = qd_ref[hh], kd_ref[hh]
            dyb = dry.astype(BF16)
            p = (_dot(q, k, NT) * dmat).astype(BF16)
            dp = (_dot(dyb, v, NT) * dmat).astype(BF16)
            ady = (dry * qd).astype(BF16)
            ds_old = ds[hh]
            ds_b = ds_old.astype(BF16)
            dq = _dot(dp, k) + _dot(ady, st_ref[hh, 0], NT)
            dk = _dot(dp, q, TN) + _dot(v, ds_b, NT) * kd
            dv = _dot(p, dyb, TN) + _dot((kf * kd).astype(BF16), ds_b)
            ds[hh] = ds_old * cd_ref[hh] + _dot(q, ady, TN)
            dq_ref[:, sl] = _rot_t(dq * (RET_DK ** -0.5), cs, sn).astype(BF16)
            dk_ref[:, sl] = _rot_t(dk, cs, sn).astype(BF16)
            dv_ref[:, sl] = dv.astype(BF16)

    rw = rh * RET_DK
    outs = pl.pallas_call(
        kern, name="ret_bwd", grid=(ng, nc),
        in_specs=[col(0), col(1), col(2), col(3)] + tab + dec + [hw, hcol, hcol, st],
        out_specs=[hcol, hcol, hcol, hcol, hw],
        out_shape=[jax.ShapeDtypeStruct((tp, rw), BF16)] * 4 + [jax.ShapeDtypeStruct((1, rw), F32)],
        scratch_shapes=[pltpu.VMEM((len(heads), RET_DK, RET_DK), F32)],
        compiler_params=_cp(("parallel", "arbitrary")),
    )(proj, proj, proj, proj, cos, sin, *tables, gnw, dmix, ry_all, states)
    return outs


def _sb_tile(tp):
    return 3 * CHUNK if tp % (3 * CHUNK) == 0 else CHUNK


def _sb_block(q, k, qpos, kb, scale, masked):
    z = _dot(q, k, NT) * scale
    t = jnp.log(1.0 + jnp.exp(-jnp.abs(z)))
    lb = jnp.minimum(z, 0.0) - t
    lk = -jnp.maximum(z, 0.0) - t
    if not masked:
        return None, lb, lk
    kpos = kb * CHUNK + lax.broadcasted_iota(jnp.int32, qpos.shape, 1)
    mask = (kpos < qpos) & (kpos >= N_PAD)
    return mask, lb, jnp.where(mask, lk, 0.0)


def _keep(mask, x):
    return x if mask is None else jnp.where(mask, x, 0.0)


def _sb_trips(i, trip):
    lax.fori_loop(0, 1, trip(True), 0)
    lax.fori_loop(1, i, trip(False), 0)
    lax.fori_loop(jnp.maximum(i, 1), i + 1, trip(True), 0)


def _tri_sum(x, tri):
    hi = x.astype(BF16)
    lo = (x - hi.astype(F32)).astype(BF16)
    if tri.shape[0] == CHUNK:
        return _dot(hi, tri) + _dot(lo, tri)
    return _dot(jnp.concatenate([hi, lo], axis=1), tri)


def _tri(strict_upper, copies=1):
    r = lax.broadcasted_iota(jnp.int32, (copies * CHUNK, CHUNK), 0) % CHUNK
    c = lax.broadcasted_iota(jnp.int32, (copies * CHUNK, CHUNK), 1)
    return ((r > c) if strict_upper else (r < c)).astype(BF16)


def _sb_fwd(proj, sbw, sh, col0, carry=()):
    tp = proj.shape[0]
    tq = _sb_tile(tp)
    nsub, nq = tq // CHUNK, tp // tq
    assert tp // CHUNK <= 128
    scale = 1.0 / math.sqrt(SB_DH)
    hq = pl.BlockSpec((tq, SB_DH), lambda h, i: (i, h))

    carry = list(carry)
    x_in, x_out, x_sems, alias = _carried(carry, 4, 3)

    def kern(q_ref, k_ref, v_ref, w_ref, *rest):
        x_in_refs, rest = rest[:len(x_in)], rest[len(x_in):]
        out_ref, sy_ref, ao_ref = rest[:3]
        x_out_refs, (a_run, k16, v16), sem_refs = rest[3:3 + len(x_out)], rest[3 + len(x_out):6 + len(x_out)], rest[6 + len(x_out):]
        h, i = pl.program_id(0), pl.program_id(1)
        if carry:
            @pl.when((h == 0) & (i == 0))
            def _():
                _run_carried(carry, "start", x_in_refs, x_out_refs, sem_refs)

        @pl.when(i == 0)
        def _():
            k16[...] = k_ref[...].astype(BF16)
            v16[...] = v_ref[...].astype(BF16)

        q = q_ref[...].astype(BF16)
        upper = _tri(True)
        lane = lax.broadcasted_iota(jnp.int32, (tq, CHUNK), 1)
        qpos = i * tq + lax.broadcasted_iota(jnp.int32, (tq, CHUNK), 0)
        a_run[...] = jnp.zeros_like(a_run)
        sy_ref[...] = jnp.zeros_like(sy_ref)
        ao_ref[...] = jnp.zeros_like(ao_ref)

        def trip(masked):
            def body(jj, carry):
                a, acc, at = a_run[...], sy_ref[...], ao_ref[0]
                for sub in reversed(range(nsub)):
                    kb = (i - jj) * nsub + sub
                    rows = pl.ds(pl.multiple_of(kb * CHUNK, CHUNK), CHUNK)
                    mask, lb, lk = _sb_block(q, k16[rows, :], qpos, kb, scale, masked)
                    wgt = _keep(mask, jnp.exp(lb + a + _tri_sum(lk, upper)))
                    acc = acc + _dot(wgt.astype(BF16), v16[rows, :])
                    at = jnp.where(lane == kb, a, at)
                    a = a + jnp.sum(lk, axis=1, keepdims=True)
                a_run[...], sy_ref[...], ao_ref[0] = a, acc, at
                return carry
            return body

        _sb_trips(i, trip)
        out_ref[...] = _rms_fwd(sy_ref[...], w_ref[...]).astype(BF16)
        if carry:
            @pl.when((h == sh - 1) & (i == nq - 1))
            def _():
                _run_carried(carry, "finish", x_in_refs, x_out_refs, sem_refs)

    kv = lambda sec: pl.BlockSpec((tp, SB_DH), lambda h, i: (0, col0 + sec * sh + h))
    res = pl.pallas_call(
        kern, name="sb_fwd", grid=(sh, nq),
        in_specs=[pl.BlockSpec((tq, SB_DH), lambda h, i: (i, col0 + h)), kv(1), kv(2),
                  pl.BlockSpec((1, SB_DH), lambda h, i: (0, h))] + [ANY] * len(x_in),
        out_specs=[hq, hq, pl.BlockSpec((1, tq, 128), lambda h, i: (h, i, 0))] + [ANY] * len(x_out),
        out_shape=[jax.ShapeDtypeStruct((tp, sh * SB_DH), BF16), jax.ShapeDtypeStruct((tp, sh * SB_DH), F32),
                   jax.ShapeDtypeStruct((sh, tp, 128), F32)] + x_out,
        input_output_aliases=alias,
        scratch_shapes=[pltpu.VMEM((tq, CHUNK), F32), pltpu.VMEM((tp, SB_DH), BF16),
                        pltpu.VMEM((tp, SB_DH), BF16)] + x_sems,
        compiler_params=_cp(("arbitrary", "arbitrary") if carry else ("parallel", "arbitrary")),
    )(proj, proj, proj, sbw, *x_in)
    return res[:3], res[3:]


def _sb_bwd(proj, dmix, sy_all, aoff, sbw, sh, col0, dcol0, carry=()):
    tp = proj.shape[0]
    tq = _sb_tile(tp)
    nsub, nq = tq // CHUNK, tp // tq
    scale = 1.0 / math.sqrt(SB_DH)
    hq = pl.BlockSpec((tq, SB_DH), lambda h, i: (i, h))
    carry = list(carry)
    x_in, x_out, x_sems, alias = _carried(carry, 7, 4)

    def kern(q_ref, k_ref, v_ref, w_ref, do_ref, sy_ref, ao_ref, *rest):
        x_in_refs, rest = rest[:len(x_in)], rest[len(x_in):]
        dq_ref, dk_ref, dv_ref, dw_ref = rest[:4]
        x_out_refs, rest = rest[4:4 + len(x_out)], rest[4 + len(x_out):]
        (dk_acc, dv_acc, dq_acc, e_run, k16, v16), sem_refs = rest[:6], rest[6:]
        h, i = pl.program_id(0), pl.program_id(1)
        if carry:
            @pl.when((h == 0) & (i == 0))
            def _():
                _run_carried(carry, "start", x_in_refs, x_out_refs, sem_refs)

        @pl.when(i == 0)
        def _():
            dk_acc[...] = jnp.zeros_like(dk_acc)
            dv_acc[...] = jnp.zeros_like(dv_acc)
            dw_ref[...] = jnp.zeros_like(dw_ref)
            k16[...] = k_ref[...].astype(BF16)
            v16[...] = v_ref[...].astype(BF16)

        qf = q_ref[...]
        q = qf.astype(BF16)
        dsy, dwr = _rms_bwd(sy_ref[...], w_ref[...], do_ref[...])
        dw_ref[...] += jnp.sum(dwr, axis=0, keepdims=True)
        dsy_b = dsy.astype(BF16)
        q_t = qf.T.astype(BF16)
        dsy_t = dsy.T.astype(BF16)
        atile = ao_ref[0]
        upper = _tri(True, 2)
        lower = _tri(False, 2)
        lane = lax.broadcasted_iota(jnp.int32, (tq, CHUNK), 1)
        qpos = i * tq + lax.broadcasted_iota(jnp.int32, (tq, CHUNK), 0)

        e_run[...] = jnp.zeros_like(e_run)
        dq_acc[...] = jnp.zeros_like(dq_acc)

        def trip(masked):
            def body(jj, carry):
                e_prev, dq = e_run[...], dq_acc[...]
                for sub in range(nsub):
                    kb = jj * nsub + sub
                    rows = pl.ds(pl.multiple_of(kb * CHUNK, CHUNK), CHUNK)
                    k, v = k16[rows, :], v16[rows, :]
                    mask, lb, lk = _sb_block(q, k, qpos, kb, scale, masked)
                    a = jnp.sum(jnp.where(lane == kb, atile, 0.0), axis=1, keepdims=True)
                    wgt = _keep(mask, jnp.exp(lb + a + _tri_sum(lk, upper)))
                    e = wgt * _dot(dsy_b, v, NT)
                    dv_acc[kb] += _dot(dsy_t, wgt.astype(BF16))
                    sig = jnp.exp(lb)
                    e_all = e_prev + _tri_sum(e, lower)
                    dz = (_keep(mask, e - sig * (e + e_all)) * scale).astype(BF16)
                    dk_acc[kb] += _dot(q_t, dz)
                    dq = dq + _dot(dz, k)
                    e_prev = e_prev + jnp.sum(e, axis=1, keepdims=True)
                e_run[...], dq_acc[...] = e_prev, dq
                return carry
            return body

        _sb_trips(i, trip)
        dq_ref[...] = dq_acc[...].astype(BF16)

        @pl.when(i == nq - 1)
        def _():
            def untranspose(kb, c):
                rows = pl.ds(pl.multiple_of(kb * CHUNK, CHUNK), CHUNK)
                dk_ref[rows, :] = dk_acc[kb].T.astype(BF16)
                dv_ref[rows, :] = dv_acc[kb].T.astype(BF16)
                return c

            lax.fori_loop(0, tp // CHUNK, untranspose, 0)

        if carry:
            @pl.when((h == sh - 1) & (i == nq - 1))
            def _():
                _run_carried(carry, "finish", x_in_refs, x_out_refs, sem_refs)

    kv = lambda sec: pl.BlockSpec((tp, SB_DH), lambda h, i: (0, col0 + sec * sh + h))
    hfull = pl.BlockSpec((tp, SB_DH), lambda h, i: (0, h))
    sw = sh * SB_DH
    acc_t = pltpu.VMEM((tp // CHUNK, SB_DH, CHUNK), F32)
    res = pl.pallas_call(
        kern, name="sb_bwd", grid=(sh, nq),
        in_specs=[pl.BlockSpec((tq, SB_DH), lambda h, i: (i, col0 + h)), kv(1), kv(2),
                  pl.BlockSpec((1, SB_DH), lambda h, i: (0, h)),
                  pl.BlockSpec((tq, SB_DH), lambda h, i: (i, dcol0 + h)), hq,
                  pl.BlockSpec((1, tq, 128), lambda h, i: (h, i, 0))] + [ANY] * len(x_in),
        out_specs=[hq, hfull, hfull, pl.BlockSpec((1, SB_DH), lambda h, i: (0, h))] + [ANY] * len(x_out),
        out_shape=[jax.ShapeDtypeStruct((tp, sw), BF16)] * 3 + [jax.ShapeDtypeStruct((1, sw), F32)] + x_out,
        input_output_aliases=alias,
        scratch_shapes=[acc_t, acc_t, pltpu.VMEM((tq, SB_DH), F32), pltpu.VMEM((tq, CHUNK), F32),
                        pltpu.VMEM((tp, SB_DH), BF16), pltpu.VMEM((tp, SB_DH), BF16)] + x_sems,
        compiler_params=_cp(("arbitrary", "arbitrary") if carry else ("parallel", "arbitrary")),
    )(proj, proj, proj, sbw, dmix, sy_all, aoff, *x_in)
    return res[:4], res[4:]


def _local_step(x, tgt, meta, net, conv_w, conv_b, pre1_w, gn_w, sb_w, post1_w, pre2_w, post2_w):
    s, d = x.shape

    def mm(host, a, b, mode, dtype):
        carry = net.carry(host)
        out = _matmul(a, b, mode, dtype, host, carry)
        if carry:
            out, extra = out
            net.took(host, extra)
        return out

    tp = s + CHUNK
    rh, sh = d // 512, d // 256
    rw = rh * RET_DK
    h0 = jnp.concatenate([jnp.zeros((N_PAD, d), F32), meta, x], axis=0)
    pos = jnp.arange(tp, dtype=F32) - N_PAD
    inv = ROPE_BASE ** (-jnp.arange(128, dtype=F32) / 128)
    ang = pos[:, None] * inv[None, :]
    cos, sin = jnp.cos(ang), jnp.sin(ang)
    tables = _ret_tables(rh)
    sb_col0 = 4 * rw // SB_DH

    n1 = _prenorm(h0, pre1_w)
    proj = mm("mm_proj", n1, net.weight("w_in"), "nn", F32)
    ret_out, ry, states = _ret_fwd(proj, cos, sin, tables, gn_w, rh)
    (sb_out, sy, aoff), extra = _sb_fwd(proj, sb_w, sh, sb_col0, net.carry("sb_fwd"))
    net.took("sb_fwd", extra)
    mixed = jnp.concatenate([ret_out, sb_out], axis=1)
    a = mm("mm_out", mixed, net.weight("w_out"), "nn", F32)
    h1, n2 = _mid_fwd(h0, a, post1_w, pre2_w)
    gu = mm("mm_up", n2, net.weight("w_up"), "nn", BF16)
    act = _ffn_fwd(gu, conv_w, conv_b)
    f = mm("mm_down", act, net.weight("w_down"), "nn", F32)

    dy, d_f, loss, dw_post2 = _loss_bwd(h1, f, tgt, post2_w)
    d_act = mm("mm_dact", d_f, net.weight("w_down"), "nt", BF16)
    net.grad("w_down", mm("mm_gdown", act, d_f, "tn", BF16))
    d_gate, d_up, ffn_stats = _ffn_bwd(gu, d_act, conv_w, conv_b)
    d_gu = jnp.concatenate([d_gate, d_up], axis=1)
    d_n2 = mm("mm_dn2", d_gu, net.weight("w_up"), "nt", F32)
    net.grad("w_up", mm("mm_gup", n2, d_gu, "tn", BF16))
    dh1, d_a, dw_pre2, dw_post1 = _mid_bwd(dy, d_n2, h1, a, pre2_w, post1_w)
    d_mix = mm("mm_dmix", d_a, net.weight("w_out"), "nt", F32)
    net.grad("w_out", mm("mm_gout", mixed, d_a, "tn", BF16))
    d_rq, d_rk, d_rv, d_rg, dw_gn = _ret_bwd(proj, d_mix, ry, states, cos, sin, tables, gn_w, rh)
    (d_sq, d_sk, d_sv, dw_sb), extra = _sb_bwd(proj, d_mix, sy, aoff, sb_w, sh, sb_col0, rw // SB_DH,
                                               net.carry("sb_bwd"))
    net.took("sb_bwd", extra)
    d_proj = jnp.concatenate([d_rq, d_rk, d_rv, d_rg, d_sq, d_sk, d_sv], axis=1)
    net.grad("w_in", mm("mm_gin", n1, d_proj, "tn", BF16))
    d_n1 = mm("mm_dn1", d_proj, net.weight("w_in"), "nt", F32)
    grad_x, g_meta, dw_pre1 = _pre_bwd(dh1, d_n1, h0, pre1_w)

    small = dict(loss=loss, meta=g_meta, pre1=dw_pre1, gn=dw_gn, sb=dw_sb, post1=dw_post1, pre2=dw_pre2,
                 conv_w=ffn_stats[0:3], conv_b=ffn_stats[3:4], post2=dw_post2)
    return grad_x, small


def _coords():
    return lax.axis_index("x"), lax.axis_index("y"), lax.axis_index("c")


def _other_chips(x, y):
    return [(1 - x, y), (x, 1 - y), (1 - x, 1 - y)]


ANY = pl.BlockSpec(memory_space=pl.ANY)
VM = pl.BlockSpec(memory_space=pltpu.VMEM)


def _gather4_small(v):
    r = v.shape[0]

    def kern(v_ref, o_ref, send, recv):
        x, y, c = _coords()
        o_ref[2 * x + y] = v_ref[...]
        cps = [pltpu.make_async_remote_copy(v_ref, o_ref.at[2 * x + y], send.at[j], recv.at[j],
                                            device_id=(px, py, c), device_id_type=MESH)
               for j, (px, py) in enumerate(_other_chips(x, y))]
        for cp in cps:
            cp.start()
        for cp in cps:
            cp.wait()

    return pl.pallas_call(
        kern, name="gather_small", in_specs=[VM], out_specs=VM,
        out_shape=jax.ShapeDtypeStruct((4, r, 128), F32),
        scratch_shapes=[pltpu.SemaphoreType.DMA((3,)), pltpu.SemaphoreType.DMA((3,))],
    )(v)


def _allreduce8_small(v):
    r = v.shape[0]
    flips = [(fx, fy, fc) for fx in (0, 1) for fy in (0, 1) for fc in (0, 1)][1:]

    def kern(v_ref, o_ref, buf, send, recv):
        x, y, c = _coords()
        me = 4 * x + 2 * y + c
        buf[me] = v_ref[...]
        cps = [pltpu.make_async_remote_copy(v_ref, buf.at[me], send.at[j], recv.at[j],
                                            device_id=(x ^ fx, y ^ fy, c ^ fc), device_id_type=MESH)
               for j, (fx, fy, fc) in enumerate(flips)]
        for cp in cps:
            cp.start()
        for cp in cps:
            cp.wait()
        tot = buf[0]
        for j in range(1, 8):
            tot = tot + buf[j]
        o_ref[...] = tot

    return pl.pallas_call(
        kern, name="allreduce_small", in_specs=[VM], out_specs=VM,
        out_shape=jax.ShapeDtypeStruct((r, 128), F32),
        scratch_shapes=[pltpu.VMEM((8, r, 128), F32), pltpu.SemaphoreType.DMA((7,)),
                        pltpu.SemaphoreType.DMA((7,))],
    )(v)


def _piece(ref, col_sharded, rows, cols, s, hc):
    half = rows // 2
    if col_sharded:
        return ref.at[pl.ds(pl.multiple_of(hc * half, 16), half), pl.ds(pl.multiple_of(s * cols, 128), cols)]
    return ref.at[pl.ds(pl.multiple_of(s * rows + hc * half, 16), half), :]


def _cast_place(w32, c_id, s_id, col_sharded, name):
    rows, cols = w32.shape
    full_shape = (rows, 4 * cols) if col_sharded else (4 * rows, cols)
    tr = max(d for d in _divisors(rows, 16) if d * cols * 4 <= 4 * 2 ** 20)
    nt = rows // tr

    def kern(c_ref, s_ref, w_ref, o_ref):
        o_ref[...] = w_ref[...].astype(BF16)

    if col_sharded:
        o_spec = pl.BlockSpec((tr, cols), lambda t, c_ref, s_ref: (t, s_ref[0]))
    else:
        o_spec = pl.BlockSpec((tr, cols), lambda t, c_ref, s_ref: (s_ref[0] * nt + t, 0))
    return pl.pallas_call(
        kern, name=name,
        grid_spec=pltpu.PrefetchScalarGridSpec(
            num_scalar_prefetch=2, grid=(nt,),
            in_specs=[pl.BlockSpec((tr, cols), lambda t, c_ref, s_ref: (t, 0))], out_specs=o_spec),
        out_shape=jax.ShapeDtypeStruct(full_shape, BF16),
        compiler_params=_cp(("parallel",)),
    )(c_id, s_id, w32)


def _gather_exchange(full, rows, cols, col_sharded):
    def copies(w_ref, o_ref, send, recv):
        x, y, c = _coords()
        s = 2 * x + y
        sib = (x, y, 1 - c)
        chips = _other_chips(x, y)
        pc = functools.partial(_piece, o_ref, col_sharded, rows, cols)
        mine = _piece(w_ref, col_sharded, rows, cols, s, c)
        first = [pltpu.make_async_remote_copy(mine, pc(s, c), send.at[j], recv.at[j],
                                              device_id=(px, py, c), device_id_type=MESH)
                 for j, (px, py) in enumerate(chips)]
        passed = [pltpu.make_async_remote_copy(pc(2 * px + py, c), pc(2 * px + py, c), send.at[3 + j], recv.at[3 + j],
                                               device_id=sib, device_id_type=MESH)
                  for j, (px, py) in enumerate(chips)]
        from_sib = [pltpu.make_async_remote_copy(pc(2 * px + py, 1 - c), pc(2 * px + py, 1 - c), send.at[3 + j],
                                                 recv.at[3 + j], device_id=sib, device_id_type=MESH)
                    for j, (px, py) in enumerate(chips)]
        return first, passed, from_sib

    def start(ins, outs, send, recv):
        for cp in copies(ins[0], outs[0], send, recv)[0]:
            cp.start()

    def finish(ins, outs, send, recv):
        first, passed, from_sib = copies(ins[0], outs[0], send, recv)
        for j in range(3):
            first[j].wait_recv()
            passed[j].start()
        for cp in from_sib:
            cp.wait_recv()
        for cp in first + passed:
            cp.wait_send()

    return _Exchange([full], [jax.ShapeDtypeStruct(full.shape, BF16)], {0: 0}, 6, start, finish)


def _pair_exchange(g, col_sharded, rows, cols):
    half = rows // 2

    def copies(g_ref, o_ref, send, recv):
        x, y, c = _coords()
        return [pltpu.make_async_remote_copy(_piece(g_ref, col_sharded, rows, cols, s, 1 - c), o_ref.at[s],
                                             send.at[s], recv.at[s], device_id=(x, y, 1 - c), device_id_type=MESH)
                for s in range(4)]

    def start(ins, outs, send, recv):
        for cp in copies(ins[0], outs[0], send, recv):
            cp.start()

    def finish(ins, outs, send, recv):
        for cp in copies(ins[0], outs[0], send, recv):
            cp.wait()

    return _Exchange([g], [jax.ShapeDtypeStruct((4, half, cols), BF16)], {}, 4, start, finish)


def _half_tiles(half, cols):
    tr = max(d for d in _divisors(half, 16) if d * cols * 4 <= 4 * 2 ** 20)
    return tr, half // tr


def _half_spec(col_sharded, tr, nt, cols, own):
    which = (lambda s, s_ref: s_ref[0]) if own else (lambda s, s_ref: (s_ref[0] + 1 + s) % 4)
    if col_sharded:
        return pl.BlockSpec((tr, cols), lambda s, t, c_ref, s_ref: (c_ref[0] * nt + t, which(s, s_ref)))
    return pl.BlockSpec((tr, cols), lambda s, t, c_ref, s_ref: ((2 * which(s, s_ref) + c_ref[0]) * nt + t, 0))


def _rs_add(g, r1, c_id, s_id, col_sharded, rows, cols, name):
    half = rows // 2
    tr, nt = _half_tiles(half, cols)

    def kern(c_ref, s_ref, g_ref, r_ref, o_ref):
        o_ref[0] = (g_ref[...].astype(F32) + r_ref[0].astype(F32)).astype(BF16)

    slab = pl.BlockSpec((1, tr, cols), lambda s, t, c_ref, s_ref: ((s_ref[0] + 1 + s) % 4, t, 0))
    return pl.pallas_call(
        kern, name=name,
        grid_spec=pltpu.PrefetchScalarGridSpec(
            num_scalar_prefetch=2, grid=(3, nt),
            in_specs=[_half_spec(col_sharded, tr, nt, cols, False), slab], out_specs=slab),
        out_shape=jax.ShapeDtypeStruct((4, half, cols), BF16),
        compiler_params=_cp(("parallel", "parallel")),
    )(c_id, s_id, g, r1)


def _scatter_exchange(p):
    _, half, cols = p.shape

    def copies(p_ref, o_ref, send, recv):
        x, y, c = _coords()
        return [pltpu.make_async_remote_copy(p_ref.at[2 * px + py], o_ref.at[j], send.at[j], recv.at[j],
                                             device_id=(px, py, c), device_id_type=MESH)
                for j, (px, py) in enumerate(_other_chips(x, y))]

    def start(ins, outs, send, recv):
        for cp in copies(ins[0], outs[0], send, recv):
            cp.start()

    def finish(ins, outs, send, recv):
        for cp in copies(ins[0], outs[0], send, recv):
            cp.wait()

    return _Exchange([p], [jax.ShapeDtypeStruct((3, half, cols), BF16)], {}, 3, start, finish)


def _rs_total(g, r1, r2, c_id, s_id, col_sharded, rows, cols, name):
    half = rows // 2
    tr, nt = _half_tiles(half, cols)

    def kern(c_ref, s_ref, g_ref, r1_ref, r2_ref, o_ref):
        tot = g_ref[...].astype(F32) + r1_ref[0].astype(F32)
        for j in range(3):
            tot = tot + r2_ref[j].astype(F32)
        o_ref[...] = tot

    return pl.pallas_call(
        kern, name=name,
        grid_spec=pltpu.PrefetchScalarGridSpec(
            num_scalar_prefetch=2, grid=(1, nt),
            in_specs=[_half_spec(col_sharded, tr, nt, cols, True),
                      pl.BlockSpec((1, tr, cols), lambda s, t, c_ref, s_ref: (s_ref[0], t, 0)),
                      pl.BlockSpec((3, tr, cols), lambda s, t, c_ref, s_ref: (0, t, 0))],
            out_specs=pl.BlockSpec((tr, cols), lambda s, t, c_ref, s_ref: (c_ref[0] * nt + t, 0))),
        out_shape=jax.ShapeDtypeStruct((rows, cols), F32),
        compiler_params=_cp(("parallel", "parallel")),
    )(c_id, s_id, g, r1, r2)


def _rs_exchange(t, name):
    rows, cols = t.shape
    half = rows // 2

    def kern(t_ref, o_ref, send, recv):
        x, y, c = _coords()
        mine = pl.ds(pl.multiple_of(c * half, 8), half)
        cp = pltpu.make_async_remote_copy(t_ref.at[mine, :], o_ref.at[mine, :], send, recv,
                                          device_id=(x, y, 1 - c), device_id_type=MESH)
        cp.start()
        cp.wait()

    return pl.pallas_call(
        kern, name=name, in_specs=[ANY], out_specs=ANY, input_output_aliases={0: 0},
        out_shape=jax.ShapeDtypeStruct((rows, cols), F32),
        scratch_shapes=[pltpu.SemaphoreType.DMA, pltpu.SemaphoreType.DMA],
    )(t)


class _Whole:
    def __init__(self, full):
        self.full, self.grads = dict(full), {}

    def weight(self, n):
        return self.full[n]

    def carry(self, host):
        return []

    def took(self, host, outs):
        pass

    def grad(self, n, g):
        self.grads[n] = g


BIG = ("w_in", "w_out", "w_up", "w_down")
COL_SHARDED = dict(w_in=True, w_out=False, w_up=True, w_down=False)
GATHER_ON = dict(mm_proj="w_out", sb_fwd="w_up", mm_up="w_down")
PAIR_ON = dict(mm_dn2="w_down", mm_dmix="w_up")
SCATTER_ON = dict(mm_gup=("w_down",), sb_bwd=("w_up", "w_out"), mm_dn1=("w_in",))


class _Sharded(_Whole):
    def __init__(self, shards, c_id, s_id):
        self.shards, self.c_id, self.s_id = shards, c_id, s_id
        self.full, self.grads, self.r1, self.p, self.r2 = {}, {}, {}, {}, {}
        self.placed = {n: _cast_place(shards[n], c_id, s_id, COL_SHARDED[n], "place_" + n) for n in BIG}
        self.full["w_in"], = _run_exchange(self._gather("w_in"), "gather_w_in")

    def _gather(self, n):
        return _gather_exchange(self.placed[n], *self.shards[n].shape, COL_SHARDED[n])

    def _pair(self, n):
        return _pair_exchange(self.grads[n], COL_SHARDED[n], *self.shards[n].shape)

    def _paired(self, n, r1):
        rows, cols = self.shards[n].shape
        self.r1[n] = r1
        self.p[n] = _rs_add(self.grads[n], r1, self.c_id, self.s_id, COL_SHARDED[n], rows, cols, "rs_add_" + n)

    def carry(self, host):
        if host in GATHER_ON:
            return [self._gather(GATHER_ON[host])]
        if host in PAIR_ON:
            return [self._pair(PAIR_ON[host])]
        return [_scatter_exchange(self.p[n]) for n in SCATTER_ON.get(host, ())]

    def took(self, host, outs):
        if host in GATHER_ON:
            self.full[GATHER_ON[host]], = outs
        if host in PAIR_ON:
            self._paired(PAIR_ON[host], outs[0])
        for n, r2 in zip(SCATTER_ON.get(host, ()), outs):
            self.r2[n] = r2

    def grad(self, n, g):
        self.grads[n] = g
        if n not in PAIR_ON.values():
            self._paired(n, _run_exchange(self._pair(n), "rs_pair_" + n)[0])

    def reduced(self, n):
        rows, cols = self.shards[n].shape
        t = _rs_total(self.grads[n], self.r1[n], self.r2[n], self.c_id, self.s_id, COL_SHARDED[n], rows, cols,
                      "rs_total_" + n)
        return _rs_exchange(t, "rs_exchange_" + n)


def _adamw_vals(w, g, m, v):
    m = ADAM_B1 * m + (1.0 - ADAM_B1) * g
    v = ADAM_B2 * v + (1.0 - ADAM_B2) * (g * g)
    m_hat = m / (1.0 - ADAM_B1 ** ADAM_STEP)
    v_hat = v / (1.0 - ADAM_B2 ** ADAM_STEP)
    delta = -ADAM_LR * (m_hat / (jnp.sqrt(v_hat) + ADAM_EPS) + ADAM_WD * w)
    return delta, m, v


def _adamw(w, g, m, v, name):
    rows, cols = w.shape
    tr = max(d for d in _divisors(rows, 8) if d * cols * 4 <= 2 * 2 ** 20)

    def kern(w_ref, g_ref, m_ref, v_ref, d_ref, mo_ref, vo_ref, go_ref):
        g = g_ref[...]
        d_ref[...], mo_ref[...], vo_ref[...] = _adamw_vals(w_ref[...], g, m_ref[...], v_ref[...])
        go_ref[...] = g

    spec = pl.BlockSpec((tr, cols), lambda i: (i, 0))
    return pl.pallas_call(
        kern, name=name, grid=(rows // tr,), in_specs=[spec] * 4, out_specs=[spec] * 4,
        out_shape=[jax.ShapeDtypeStruct((rows, cols), F32)] * 4,
        compiler_params=_cp(("parallel",)),
    )(w, g, m, v)


def _pack(arrs):
    flat = []
    for a in arrs:
        a = a.reshape(-1)
        flat.append(jnp.pad(a, (0, (-a.shape[0]) % 1024)))
    return jnp.concatenate(flat).reshape(-1, 128)


def _unpack(slab, shapes):
    out, off = [], 0
    flat = slab.reshape(slab.shape[:-2] + (-1,))
    for shp in shapes:
        n = math.prod(shp)
        out.append(flat[..., off:off + n].reshape(slab.shape[:-2] + tuple(shp)))
        off += n + (-n) % 1024
    return out


SMALL = ("meta_tokens", "attn_pre_norm_w", "ret_gn_w", "sb_norm_w", "attn_post_norm_w", "ffn_pre_norm_w",
         "conv_w", "conv_b", "ffn_post_norm_w")
ORDER = ("meta_tokens", "attn_pre_norm_w", "w_in", "ret_gn_w", "sb_norm_w", "w_out", "attn_post_norm_w",
         "ffn_pre_norm_w", "w_up", "conv_w", "conv_b", "w_down", "ffn_post_norm_w")


def kernel(x, meta_tokens, attn_pre_norm_w, w_in, ret_gn_w, sb_norm_w, w_out, attn_post_norm_w, ffn_pre_norm_w, w_up, conv_w, conv_b, w_down, ffn_post_norm_w, loss_target, m_meta_tokens, m_attn_pre_norm_w, m_w_in, m_ret_gn_w, m_sb_norm_w, m_w_out, m_attn_post_norm_w, m_ffn_pre_norm_w, m_w_up, m_conv_w, m_conv_b, m_w_down, m_ffn_post_norm_w, v_meta_tokens, v_attn_pre_norm_w, v_w_in, v_ret_gn_w, v_sb_norm_w, v_w_out, v_attn_post_norm_w, v_ffn_pre_norm_w, v_w_up, v_conv_w, v_conv_b, v_w_down, v_ffn_post_norm_w):
    args = dict(locals())
    w = {n: args[n] for n in ORDER}
    m = {n: args["m_" + n] for n in ORDER}
    v = {n: args["v_" + n] for n in ORDER}
    xi, yi, ci = _coords()
    shard_id = 2 * xi + yi
    c_id = ci.astype(jnp.int32).reshape(1)
    s_id = shard_id.astype(jnp.int32).reshape(1)
    d = x.shape[-1]

    mshape, cshape = w["meta_tokens"].shape, w["conv_w"][0].shape
    got = _unpack(_gather4_small(_pack([w["meta_tokens"], w["conv_w"][0]])), [mshape, cshape])
    meta_full = jnp.moveaxis(got[0], 0, 1).reshape(N_META, d)
    conv_w_full = jnp.moveaxis(got[1], 0, 1).reshape(CONV_W, -1)

    net = _Sharded({n: w[n][0] for n in BIG}, c_id, s_id)
    grad_x, g_small = _local_step(
        x[0], loss_target[0], meta_full, net, conv_w_full, w["conv_b"], w["attn_pre_norm_w"], w["ret_gn_w"],
        w["sb_norm_w"], w["attn_post_norm_w"], w["ffn_pre_norm_w"], w["ffn_post_norm_w"])

    names = ("loss", "meta", "pre1", "gn", "sb", "post1", "pre2", "conv_w", "conv_b", "post2")
    tot = _unpack(_allreduce8_small(_pack([g_small[n] for n in names])), [g_small[n].shape for n in names])
    tot = dict(zip(names, tot))
    loss = tot["loss"][0, 0]
    mcols, ccols = mshape[1], cshape[1]
    grads = {
        "meta_tokens": lax.dynamic_slice_in_dim(tot["meta"], shard_id * mcols, mcols, axis=1),
        "attn_pre_norm_w": tot["pre1"], "ret_gn_w": tot["gn"], "sb_norm_w": tot["sb"],
        "attn_post_norm_w": tot["post1"], "ffn_pre_norm_w": tot["pre2"],
        "conv_w": lax.dynamic_slice_in_dim(tot["conv_w"], shard_id * ccols, ccols, axis=1)[None],
        "conv_b": tot["conv_b"], "ffn_post_norm_w": tot["post2"],
    }

    delta, new_m, new_v = {}, {}, {}
    for n in BIG:
        dl, mo, vo, g = _adamw(w[n][0], net.reduced(n), m[n][0], v[n][0], "adamw_" + n)
        delta[n], new_m[n], new_v[n], grads[n] = dl[None], mo[None], vo[None], g[None]
    shapes = [w[n].shape for n in SMALL]
    packed = [_pack([src[n] for n in SMALL]) for src in (w, grads, m, v)]
    outs = _adamw(*packed, "adamw_small")[:3]
    for dst, slab in zip((delta, new_m, new_v), outs):
        for n, a in zip(SMALL, _unpack(slab, shapes)):
            dst[n] = a

    return (loss, grad_x[None], *[grads[n] for n in ORDER], *[delta[n] for n in ORDER],
            *[new_m[n] for n in ORDER], *[new_v[n] for n in ORDER])
```

```python
import functools
import math

import jax
import jax.numpy as jnp
from jax import lax
from jax.experimental import pallas as pl
from jax.experimental.pallas import tpu as pltpu

F32 = jnp.float32
BF16 = jnp.bfloat16
MESH = pl.DeviceIdType.MESH

EPS = 1e-6
ROPE_BASE = 10000.0
N_META = 16
CHUNK = 128
N_PAD = CHUNK - N_META
RET_DK = 256
SB_DH = 128
SB_HEADS_PER_STEP = 1
CONV_W = 3

ADAM_LR = 0.001
ADAM_B1 = 0.9
ADAM_B2 = 0.999
ADAM_EPS = 1e-08
ADAM_WD = 0.01
ADAM_STEP = 10

V7X_VMEM_BYTES = 64 * 2 ** 20
VMEM_LIMIT = V7X_VMEM_BYTES - 8 * 2 ** 20
MM_BUDGET = 40 * 2 ** 20
V7X_BF16_FLOPS = 0.9e15
V7X_HBM_BPS = 2.0e12
GRID_STEP_S = 0.6e-6
ACC_BPS = 2.0e13

NN = (((1,), (0,)), ((), ()))
NT = (((1,), (1,)), ((), ()))
TN = (((0,), (0,)), ((), ()))


def _cp(sem, vmem=VMEM_LIMIT):
    return pltpu.CompilerParams(dimension_semantics=sem, vmem_limit_bytes=vmem)


def _dot(a, b, dims=NN):
    return lax.dot_general(a, b, dims, preferred_element_type=F32)


def _sigmoid(x):
    return 1.0 / (1.0 + jnp.exp(-x))


def _divisors(n, align):
    return [d for d in range(align, n + 1, align) if n % d == 0]


def _mm_tiles(mode, m, n, k, out_bytes):
    best = None
    tms = [d for d in _divisors(m, 128 if mode == "tn" else 16) if d >= 128]
    tns = [d for d in _divisors(n, 256)] or [d for d in _divisors(n, 128)]
    tks = [d for d in _divisors(k, 128) if d >= 128]
    flops = 2.0 * m * n * k
    for tm in tms:
        for tn in tns:
            if tm * tn > 2112 * 1024:
                continue
            for tk in tks:
                nk = k // tk
                foot = 4 * (tm * tk + tk * tn) + 2 * tm * tn * out_bytes
                if nk > 1:
                    foot += 4 * tm * tn
                if foot > MM_BUDGET:
                    continue
                steps = (m // tm) * (n // tn) * nk
                for swap in (False, True):
                    if nk > 1:
                        traffic = 2.0 * (n // tn) * m * k + 2.0 * (m // tm) * n * k
                    elif swap:
                        traffic = 2.0 * n * k + 2.0 * (n // tn) * m * k
                    else:
                        traffic = 2.0 * m * k + 2.0 * (m // tm) * n * k
                    traffic += out_bytes * m * n
                    t = max(flops / V7X_BF16_FLOPS, traffic / V7X_HBM_BPS) + steps * GRID_STEP_S
                    if nk > 1:
                        t += 12.0 * m * n * nk / ACC_BPS
                    if best is None or t < best[0]:
                        best = (t, tm, tn, tk, swap)
    assert best is not None, (mode, m, n, k)
    return best[1:]


class _Exchange:
    def __init__(self, operands, out_shapes, aliases, nsem, start, finish, relay=None, relay_at=1.0):
        self.operands, self.out_shapes, self.aliases, self.nsem = operands, out_shapes, aliases, nsem
        self.start, self.finish, self.relay, self.relay_at = start, finish, relay, relay_at


def _carried(exchanges, in_base, out_base):
    ins = [a for e in exchanges for a in e.operands]
    outs = [s for e in exchanges for s in e.out_shapes]
    sems = [pltpu.SemaphoreType.DMA((e.nsem,)) for e in exchanges for _ in (0, 1)]
    alias, i, o = {}, in_base, out_base
    for e in exchanges:
        alias.update({i + k: o + v for k, v in e.aliases.items()})
        i, o = i + len(e.operands), o + len(e.out_shapes)
    return ins, outs, sems, alias


def _run_carried(exchanges, phase, in_refs, out_refs, sem_refs, pick=None):
    i = o = 0
    for n, e in enumerate(exchanges):
        ni, no = len(e.operands), len(e.out_shapes)
        if getattr(e, phase) is not None and pick in (None, n):
            getattr(e, phase)(in_refs[i:i + ni], out_refs[o:o + no], sem_refs[2 * n], sem_refs[2 * n + 1])
        i, o = i + ni, o + no


def _carried_before(exchanges, step, nsteps, in_refs, out_refs, sem_refs):
    if not exchanges:
        return

    @pl.when(step == 0)
    def _():
        _run_carried(exchanges, "start", in_refs, out_refs, sem_refs)

    for n, e in enumerate(exchanges):
        if e.relay is not None:
            @pl.when(step == min(nsteps - 1, int(e.relay_at * nsteps)))
            def _(n=n):
                _run_carried(exchanges, "relay", in_refs, out_refs, sem_refs, pick=n)


def _carried_after(exchanges, step, nsteps, in_refs, out_refs, sem_refs):
    if not exchanges:
        return

    @pl.when(step == nsteps - 1)
    def _():
        _run_carried(exchanges, "finish", in_refs, out_refs, sem_refs)


def _run_exchange(e, name):
    ins, outs, sems, alias = _carried([e], 0, 0)

    def kern(*refs):
        in_refs, out_refs, sem_refs = refs[:len(ins)], refs[len(ins):len(ins) + len(outs)], refs[len(ins) + len(outs):]
        for phase in ("start", "relay", "finish"):
            _run_carried([e], phase, in_refs, out_refs, sem_refs)

    return pl.pallas_call(
        kern, name=name, in_specs=[ANY] * len(ins), out_specs=[ANY] * len(outs), out_shape=outs,
        input_output_aliases=alias, scratch_shapes=sems,
    )(*ins)


def _matmul(a, b, mode, out_dtype, name, carry=()):
    if mode == "nn":
        (m, k), (k2, n) = a.shape, b.shape
    elif mode == "nt":
        (m, k), (n, k2) = a.shape, b.shape
    else:
        (k, m), (k2, n) = a.shape, b.shape
    assert k == k2 and a.dtype == BF16 and b.dtype == BF16
    tm, tn, tk, swap = _mm_tiles(mode, m, n, k, jnp.dtype(out_dtype).itemsize)
    nk = k // tk
    dims = {"nn": NN, "nt": NT, "tn": TN}[mode]

    def ij(g0, g1):
        return (g1, g0) if swap else (g0, g1)

    if mode == "tn":
        a_spec = pl.BlockSpec((tk, tm), lambda g0, g1, kk: (kk, ij(g0, g1)[0]))
    else:
        a_spec = pl.BlockSpec((tm, tk), lambda g0, g1, kk: (ij(g0, g1)[0], kk))
    if mode == "nt":
        b_spec = pl.BlockSpec((tn, tk), lambda g0, g1, kk: (ij(g0, g1)[1], kk))
    else:
        b_spec = pl.BlockSpec((tk, tn), lambda g0, g1, kk: (kk, ij(g0, g1)[1]))
    o_spec = pl.BlockSpec((tm, tn), lambda g0, g1, kk: ij(g0, g1))

    grid = (n // tn, m // tm, nk) if swap else (m // tm, n // tn, nk)
    carry = list(carry)
    x_in, x_out, x_sems, alias = _carried(carry, 2, 1)
    acc_shapes = [pltpu.VMEM((tm, tn), F32)] if nk > 1 else []

    def kern(a_ref, b_ref, *rest):
        x_in_refs, o_ref = rest[:len(x_in)], rest[len(x_in)]
        x_out_refs = rest[len(x_in) + 1:len(x_in) + 1 + len(x_out)]
        tail = rest[len(x_in) + 1 + len(x_out):]
        acc, sem_refs = tail[:len(acc_shapes)], tail[len(acc_shapes):]
        pid = [pl.program_id(ax) for ax in range(3)]
        step, nsteps = (pid[0] * grid[1] + pid[1]) * grid[2] + pid[2], grid[0] * grid[1] * grid[2]
        _carried_before(carry, step, nsteps, x_in_refs, x_out_refs, sem_refs)
        prod = _dot(a_ref[...], b_ref[...], dims)
        if nk == 1:
            o_ref[...] = prod.astype(out_dtype)
        else:
            kk = pid[2]

            @pl.when(kk == 0)
            def _():
                acc[0][...] = prod

            @pl.when(kk > 0)
            def _():
                acc[0][...] += prod

            @pl.when(kk == nk - 1)
            def _():
                o_ref[...] = acc[0][...].astype(out_dtype)

        _carried_after(carry, step, nsteps, x_in_refs, x_out_refs, sem_refs)

    sem = ("arbitrary",) * 3 if carry else ("parallel", "parallel", "arbitrary")
    res = pl.pallas_call(
        kern, name=name, grid=grid, in_specs=[a_spec, b_spec] + [ANY] * len(x_in),
        out_specs=[o_spec] + [ANY] * len(x_out),
        out_shape=[jax.ShapeDtypeStruct((m, n), out_dtype)] + x_out,
        input_output_aliases=alias, scratch_shapes=acc_shapes + x_sems,
        compiler_params=_cp(sem),
    )(a, b, *x_in)
    return (res[0], res[1:]) if carry else res[0]


def _rms_fwd(x, w):
    r = lax.rsqrt(jnp.mean(x * x, axis=-1, keepdims=True) + EPS)
    return x * r * w


def _rms_bwd(x, w, g):
    r = lax.rsqrt(jnp.mean(x * x, axis=-1, keepdims=True) + EPS)
    gw = g * w
    dx = r * gw - x * (r * r * r * jnp.mean(gw * x, axis=-1, keepdims=True))
    return dx, g * (x * r)


def _row_spec(d):
    return pl.BlockSpec((CHUNK, d), lambda i: (i, 0))


def _vec_spec(d):
    return pl.BlockSpec((1, d), lambda i: (0, 0))


def _prenorm(h0, w):
    tp, d = h0.shape

    def kern(h_ref, w_ref, o_ref):
        o_ref[...] = _rms_fwd(h_ref[...], w_ref[...]).astype(BF16)

    return pl.pallas_call(
        kern, name="prenorm1", grid=(tp // CHUNK,),
        in_specs=[_row_spec(d), _vec_spec(d)], out_specs=_row_spec(d),
        out_shape=jax.ShapeDtypeStruct((tp, d), BF16),
        compiler_params=_cp(("parallel",)),
    )(h0, w)


def _mid_fwd(h0, a, w_post, w_pre):
    tp, d = h0.shape

    def kern(h_ref, a_ref, wp_ref, wq_ref, h1_ref, n2_ref):
        h1 = h_ref[...] + _rms_fwd(a_ref[...], wp_ref[...])
        h1_ref[...] = h1
        n2_ref[...] = _rms_fwd(h1, wq_ref[...]).astype(BF16)

    return pl.pallas_call(
        kern, name="mid_fwd", grid=(tp // CHUNK,),
        in_specs=[_row_spec(d), _row_spec(d), _vec_spec(d), _vec_spec(d)],
        out_specs=[_row_spec(d), _row_spec(d)],
        out_shape=[jax.ShapeDtypeStruct((tp, d), F32), jax.ShapeDtypeStruct((tp, d), BF16)],
        compiler_params=_cp(("parallel",)),
    )(h0, a, w_post, w_pre)


def _loss_bwd(h1, f, tgt, w_post):
    tp, d = h1.shape

    def kern(h_ref, f_ref, t_ref, w_ref, dy_ref, df_ref, loss_ref, dw_ref):
        i = pl.program_id(0)

        @pl.when(i == 0)
        def _():
            dy_ref[...] = jnp.zeros_like(dy_ref)
            df_ref[...] = jnp.zeros_like(df_ref)
            loss_ref[...] = jnp.zeros_like(loss_ref)
            dw_ref[...] = jnp.zeros_like(dw_ref)

        @pl.when(i > 0)
        def _():
            fv = f_ref[...]
            w = w_ref[...]
            err = h_ref[...] + _rms_fwd(fv, w) - t_ref[...]
            loss_ref[...] += 0.5 * jnp.sum(jnp.mean(err * err, axis=-1, keepdims=True))
            dy = err * (1.0 / d)
            dy_ref[...] = dy
            dfv, dwr = _rms_bwd(fv, w, dy)
            df_ref[...] = dfv.astype(BF16)
            dw_ref[...] += jnp.sum(dwr, axis=0, keepdims=True)

    return pl.pallas_call(
        kern, name="loss_bwd", grid=(tp // CHUNK,),
        in_specs=[_row_spec(d), _row_spec(d),
                  pl.BlockSpec((CHUNK, d), lambda i: (jnp.maximum(i - 1, 0), 0)), _vec_spec(d)],
        out_specs=[_row_spec(d), _row_spec(d), pl.BlockSpec((1, 128), lambda i: (0, 0)), _vec_spec(d)],
        out_shape=[jax.ShapeDtypeStruct((tp, d), F32), jax.ShapeDtypeStruct((tp, d), BF16),
                   jax.ShapeDtypeStruct((1, 128), F32), jax.ShapeDtypeStruct((1, d), F32)],
        compiler_params=_cp(("arbitrary",)),
    )(h1, f, tgt, w_post)


def _mid_bwd(dy, dn2, h1, a, w_pre, w_post):
    tp, d = h1.shape

    def kern(dy_ref, dn_ref, h_ref, a_ref, wq_ref, wp_ref, dh_ref, da_ref, dwq_ref, dwp_ref):
        @pl.when(pl.program_id(0) == 0)
        def _():
            dwq_ref[...] = jnp.zeros_like(dwq_ref)
            dwp_ref[...] = jnp.zeros_like(dwp_ref)

        dx, dwq = _rms_bwd(h_ref[...], wq_ref[...], dn_ref[...])
        dh = dy_ref[...] + dx
        dh_ref[...] = dh
        da, dwp = _rms_bwd(a_ref[...], wp_ref[...], dh)
        da_ref[...] = da.astype(BF16)
        dwq_ref[...] += jnp.sum(dwq, axis=0, keepdims=True)
        dwp_ref[...] += jnp.sum(dwp, axis=0, keepdims=True)

    return pl.pallas_call(
        kern, name="mid_bwd", grid=(tp // CHUNK,),
        in_specs=[_row_spec(d)] * 4 + [_vec_spec(d)] * 2,
        out_specs=[_row_spec(d), _row_spec(d), _vec_spec(d), _vec_spec(d)],
        out_shape=[jax.ShapeDtypeStruct((tp, d), F32), jax.ShapeDtypeStruct((tp, d), BF16),
                   jax.ShapeDtypeStruct((1, d), F32), jax.ShapeDtypeStruct((1, d), F32)],
        compiler_params=_cp(("arbitrary",)),
    )(dy, dn2, h1, a, w_pre, w_post)


def _pre_bwd(dh1, dn1, h0, w_pre):
    tp, d = h0.shape
    s = tp - CHUNK

    def kern(dh_ref, dn_ref, h_ref, w_ref, gx_ref, gm_ref, dw_ref):
        i = pl.program_id(0)
        dx, dwr = _rms_bwd(h_ref[...], w_ref[...], dn_ref[...])
        dh0 = dh_ref[...] + dx
        gx_ref[...] = dh0

        @pl.when(i == 0)
        def _():
            gm_ref[...] = dh0[N_PAD:, :]
            dw_ref[...] = jnp.zeros_like(dw_ref)

        dw_ref[...] += jnp.sum(dwr, axis=0, keepdims=True)

    return pl.pallas_call(
        kern, name="pre_bwd", grid=(tp // CHUNK,),
        in_specs=[_row_spec(d)] * 3 + [_vec_spec(d)],
        out_specs=[pl.BlockSpec((CHUNK, d), lambda i: (jnp.maximum(i - 1, 0), 0)),
                   pl.BlockSpec((N_META, d), lambda i: (0, 0)), _vec_spec(d)],
        out_shape=[jax.ShapeDtypeStruct((s, d), F32), jax.ShapeDtypeStruct((N_META, d), F32),
                   jax.ShapeDtypeStruct((1, d), F32)],
        compiler_params=_cp(("arbitrary",)),
    )(dh1, dn1, h0, w_pre)


def _ffn_cols(dff):
    return dff // 2 if dff % 256 == 0 else dff


HALO = 16


def _ffn_fwd(gu, conv_w, conv_b):
    tp, two_dff = gu.shape
    dff = two_dff // 2
    tc = _ffn_cols(dff)
    nj = dff // tc
    r8 = CHUNK // HALO

    def kern(g_ref, gp_ref, u_ref, w_ref, b_ref, o_ref):
        i = pl.program_id(1)
        prev = gp_ref[...].astype(F32) * (i > 0).astype(F32)
        ext = jnp.concatenate([prev, g_ref[...].astype(F32)], axis=0)
        w = w_ref[...]
        conv = (b_ref[...] + w[0:1] * pltpu.roll(ext, 2, 0)[HALO:] + w[1:2] * pltpu.roll(ext, 1, 0)[HALO:]
                + w[2:3] * ext[HALO:])
        o_ref[...] = (conv * _sigmoid(conv) * u_ref[...].astype(F32)).astype(BF16)

    return pl.pallas_call(
        kern, name="ffn_fwd", grid=(nj, tp // CHUNK),
        in_specs=[pl.BlockSpec((CHUNK, tc), lambda j, i: (i, j)),
                  pl.BlockSpec((HALO, tc), lambda j, i: (jnp.maximum(i * r8 - 1, 0), j)),
                  pl.BlockSpec((CHUNK, tc), lambda j, i: (i, j + nj)),
                  pl.BlockSpec((CONV_W, tc), lambda j, i: (0, j)),
                  pl.BlockSpec((1, tc), lambda j, i: (0, j))],
        out_specs=pl.BlockSpec((CHUNK, tc), lambda j, i: (i, j)),
        out_shape=jax.ShapeDtypeStruct((tp, dff), BF16),
        compiler_params=_cp(("parallel", "parallel")),
    )(gu, gu, gu, conv_w, conv_b)


def _ffn_bwd(gu, dact, conv_w, conv_b):
    tp, two_dff = gu.shape
    dff = two_dff // 2
    tc = _ffn_cols(dff)
    nj = dff // tc
    ni = tp // CHUNK
    r8 = CHUNK // HALO

    def kern(g_ref, gp_ref, gn_ref, u_ref, un_ref, d_ref, dn_ref, w_ref, b_ref, dg_ref, du_ref, st_ref):
        i = pl.program_id(1)

        @pl.when(i == 0)
        def _():
            st_ref[...] = jnp.zeros_like(st_ref)

        first = (i > 0).astype(F32)
        last = (i < ni - 1).astype(F32)
        gate = g_ref[...].astype(F32)
        d_main = d_ref[...].astype(F32)
        ext = jnp.concatenate([gp_ref[...].astype(F32) * first, gate, gn_ref[...].astype(F32)], axis=0)
        w = w_ref[...]
        r1 = pltpu.roll(ext, 1, 0)
        r2 = pltpu.roll(ext, 2, 0)
        conv = (b_ref[...] + w[0:1] * r2 + w[1:2] * r1 + w[2:3] * ext)[HALO:]
        up = jnp.concatenate([u_ref[...].astype(F32), un_ref[...].astype(F32)], axis=0)
        da = jnp.concatenate([d_main, dn_ref[...].astype(F32) * last], axis=0)
        sg = _sigmoid(conv)
        dc = da * up * (sg * (1.0 + conv * (1.0 - sg)))
        du_ref[...] = (d_main * (conv * sg)[:CHUNK]).astype(BF16)
        n = CHUNK + HALO
        dgate = w[2:3] * dc + w[1:2] * pltpu.roll(dc, n - 1, 0) + w[0:1] * pltpu.roll(dc, n - 2, 0)
        dg_ref[...] = dgate[:CHUNK].astype(BF16)
        dcm = dc[:CHUNK]
        s0 = jnp.sum(dcm * r2[HALO:HALO + CHUNK], axis=0, keepdims=True)
        s1 = jnp.sum(dcm * r1[HALO:HALO + CHUNK], axis=0, keepdims=True)
        s2 = jnp.sum(dcm * gate, axis=0, keepdims=True)
        s3 = jnp.sum(dcm, axis=0, keepdims=True)
        row = lax.broadcasted_iota(jnp.int32, (8, tc), 0)
        st_ref[...] += jnp.where(row == 0, s0, jnp.where(row == 1, s1, jnp.where(row == 2, s2,
                                 jnp.where(row == 3, s3, 0.0))))

    main = lambda off: pl.BlockSpec((CHUNK, tc), lambda j, i: (i, j + off))
    nxt = lambda off: pl.BlockSpec((HALO, tc), lambda j, i: (jnp.minimum((i + 1) * r8, ni * r8 - 1), j + off))
    return pl.pallas_call(
        kern, name="ffn_bwd", grid=(nj, ni),
        in_specs=[main(0), pl.BlockSpec((HALO, tc), lambda j, i: (jnp.maximum(i * r8 - 1, 0), j)), nxt(0),
                  main(nj), nxt(nj), main(0), nxt(0),
                  pl.BlockSpec((CONV_W, tc), lambda j, i: (0, j)), pl.BlockSpec((1, tc), lambda j, i: (0, j))],
        out_specs=[main(0), main(0), pl.BlockSpec((8, tc), lambda j, i: (0, j))],
        out_shape=[jax.ShapeDtypeStruct((tp, dff), BF16), jax.ShapeDtypeStruct((tp, dff), BF16),
                   jax.ShapeDtypeStruct((8, dff), F32)],
        compiler_params=_cp(("parallel", "arbitrary")),
    )(gu, gu, gu, gu, gu, dact, dact, conv_w, conv_b)


def _rot(x, cs, sn):
    x1, x2 = x[:, :128], x[:, 128:]
    return jnp.concatenate([x1 * cs - x2 * sn, x1 * sn + x2 * cs], axis=1)


def _rot_t(x, cs, sn):
    x1, x2 = x[:, :128], x[:, 128:]
    return jnp.concatenate([x1 * cs + x2 * sn, x2 * cs - x1 * sn], axis=1)


def _ret_tables(rh):
    lg = jnp.log(1.0 - 2.0 ** (-5.0 - jnp.arange(rh, dtype=F32)))
    idx = jnp.arange(CHUNK, dtype=F32)
    diff = idx[:, None] - idx[None, :]
    intra = jnp.where(diff[None] >= 0, jnp.exp(jnp.maximum(diff, 0.0)[None] * lg[:, None, None]), 0.0)
    qdec = jnp.exp((idx[None, :] + 1.0) * lg[:, None])[..., None]
    kdec = jnp.exp((CHUNK - 1.0 - idx[None, :]) * lg[:, None])[..., None]
    cdec = jnp.exp(CHUNK * lg)[:, None, None]
    return intra, qdec, kdec, cdec


def _ret_specs(rh, nc, rev):
    hp = max(d for d in (8, 4, 2, 1) if rh % d == 0)
    ng, wd = rh // hp, hp * RET_DK
    cc = (lambda c: nc - 1 - c) if rev else (lambda c: c)
    col = lambda sec: pl.BlockSpec((CHUNK, wd), lambda h, c: (cc(c), sec * ng + h))
    tab = [pl.BlockSpec((CHUNK, 128), lambda h, c: (cc(c), 0))] * 2
    dec = [pl.BlockSpec((hp, CHUNK, CHUNK), lambda h, c: (h, 0, 0)),
           pl.BlockSpec((hp, CHUNK, 1), lambda h, c: (h, 0, 0)),
           pl.BlockSpec((hp, CHUNK, 1), lambda h, c: (h, 0, 0)),
           pl.BlockSpec((hp, 1, 1), lambda h, c: (h, 0, 0))]
    hw = pl.BlockSpec((1, wd), lambda h, c: (0, h))
    hcol = pl.BlockSpec((CHUNK, wd), lambda h, c: (cc(c), h))
    st = pl.BlockSpec((hp, 1, RET_DK, RET_DK), lambda h, c: (h, cc(c), 0, 0))
    heads = [(hh, slice(hh * RET_DK, (hh + 1) * RET_DK)) for hh in range(hp)]
    return heads, ng, col, tab, dec, hw, hcol, st


def _ret_fwd(proj, cos, sin, tables, gnw, rh):
    tp = proj.shape[0]
    nc = tp // CHUNK
    heads, ng, col, tab, dec, hw, hcol, st = _ret_specs(rh, nc, False)

    def kern(q_ref, k_ref, v_ref, g_ref, cos_ref, sin_ref, in_ref, qd_ref, kd_ref, cd_ref, w_ref,
             out_ref, ry_ref, st_ref, state):
        @pl.when(pl.program_id(1) == 0)
        def _():
            state[...] = jnp.zeros_like(state)

        cs, sn = cos_ref[...], sin_ref[...]
        for hh, sl in heads:
            q = (_rot(q_ref[:, sl], cs, sn) * (RET_DK ** -0.5)).astype(BF16)
            kf = _rot(k_ref[:, sl], cs, sn)
            k = kf.astype(BF16)
            v = v_ref[:, sl].astype(BF16)
            s_old = state[hh]
            s_b = s_old.astype(BF16)
            st_ref[hh, 0] = s_b
            sc = _dot(q, k, NT) * in_ref[hh]
            ry = _dot(sc.astype(BF16), v) + _dot(q, s_b) * qd_ref[hh]
            state[hh] = s_old * cd_ref[hh] + _dot((kf * kd_ref[hh]).astype(BF16), v, TN)
            ry_ref[:, sl] = ry
            g = g_ref[:, sl]
            out_ref[:, sl] = (g * _sigmoid(g) * _rms_fwd(ry, w_ref[:, sl])).astype(BF16)

    return pl.pallas_call(
        kern, name="ret_fwd", grid=(ng, nc),
        in_specs=[col(0), col(1), col(2), col(3)] + tab + dec + [hw],
        out_specs=[hcol, hcol, st],
        out_shape=[jax.ShapeDtypeStruct((tp, rh * RET_DK), BF16), jax.ShapeDtypeStruct((tp, rh * RET_DK), F32),
                   jax.ShapeDtypeStruct((rh, nc, RET_DK, RET_DK), BF16)],
        scratch_shapes=[pltpu.VMEM((len(heads), RET_DK, RET_DK), F32)],
        compiler_params=_cp(("parallel", "arbitrary")),
    )(proj, proj, proj, proj, cos, sin, *tables, gnw)


def _ret_bwd(proj, dmix, ry_all, states, cos, sin, tables, gnw, rh):
    tp = proj.shape[0]
    nc = tp // CHUNK
    heads, ng, col, tab, dec, hw, hcol, st = _ret_specs(rh, nc, True)

    def kern(q_ref, k_ref, v_ref, g_ref, cos_ref, sin_ref, in_ref, qd_ref, kd_ref, cd_ref, w_ref,
             do_ref, ry_ref, st_ref, dq_ref, dk_ref, dv_ref, dg_ref, dw_ref, ds):
        @pl.when(pl.program_id(1) == 0)
        def _():
            ds[...] = jnp.zeros_like(ds)
            dw_ref[...] = jnp.zeros_like(dw_ref)

        cs, sn = cos_ref[...], sin_ref[...]
        for hh, sl in heads:
            q = (_rot(q_ref[:, sl], cs, sn) * (RET_DK ** -0.5)).astype(BF16)
            kf = _rot(k_ref[:, sl], cs, sn)
            k = kf.astype(BF16)
            v = v_ref[:, sl].astype(BF16)
            g = g_ref[:, sl]
            ry = ry_ref[:, sl]
            w = w_ref[:, sl]
            dout = do_ref[:, sl]
            sg = _sigmoid(g)
            dhn = dout * (g * sg)
            dry, dwr = _rms_bwd(ry, w, dhn)
            dg_ref[:, sl] = (dout * _rms_fwd(ry, w) * (sg * (1.0 + g * (1.0 - sg)))).astype(BF16)
            dw_ref[:, sl] += jnp.sum(dwr, axis=0, keepdims=True)

            dmat = in_ref[hh]
            qd, kd = qd_ref[hh], kd_ref[hh]
            dyb = dry.astype(BF16)
            p = (_dot(q, k, NT) * dmat).astype(BF16)
            dp = (_dot(dyb, v, NT) * dmat).astype(BF16)
            ady = (dry * qd).astype(BF16)
            ds_old = ds[hh]
            ds_b = ds_old.astype(BF16)
            dq = _dot(dp, k) + _dot(ady, st_ref[hh, 0], NT)
            dk = _dot(dp, q, TN) + _dot(v, ds_b, NT) * kd
            dv = _dot(p, dyb, TN) + _dot((kf * kd).astype(BF16), ds_b)
            ds[hh] = ds_old * cd_ref[hh] + _dot(q, ady, TN)
            dq_ref[:, sl] = _rot_t(dq * (RET_DK ** -0.5), cs, sn).astype(BF16)
            dk_ref[:, sl] = _rot_t(dk, cs, sn).astype(BF16)
            dv_ref[:, sl] = dv.astype(BF16)

    rw = rh * RET_DK
    outs = pl.pallas_call(
        kern, name="ret_bwd", grid=(ng, nc),
        in_specs=[col(0), col(1), col(2), col(3)] + tab + dec + [hw, hcol, hcol, st],
        out_specs=[hcol, hcol, hcol, hcol, hw],
        out_shape=[jax.ShapeDtypeStruct((tp, rw), BF16)] * 4 + [jax.ShapeDtypeStruct((1, rw), F32)],
        scratch_shapes=[pltpu.VMEM((len(heads), RET_DK, RET_DK), F32)],
        compiler_params=_cp(("parallel", "arbitrary")),
    )(proj, proj, proj, proj, cos, sin, *tables, gnw, dmix, ry_all, states)
    return outs


def _sb_tile(tp):
    return 3 * CHUNK if tp % (3 * CHUNK) == 0 else CHUNK


def _sb_heads(sh, col0):
    hp = SB_HEADS_PER_STEP if sh % SB_HEADS_PER_STEP == 0 and col0 % SB_HEADS_PER_STEP == 0 else 1
    heads = [(hh, slice(hh * SB_DH, (hh + 1) * SB_DH)) for hh in range(hp)]
    return heads, sh // hp, hp * SB_DH, col0 // hp


def _sb_block(q, k, qpos, kb, scale, masked):
    z = _dot(q, k, NT) * scale
    t = jnp.log(1.0 + jnp.exp(-jnp.abs(z)))
    lb = jnp.minimum(z, 0.0) - t
    lk = -jnp.maximum(z, 0.0) - t
    if not masked:
        return None, lb, lk
    kpos = kb * CHUNK + lax.broadcasted_iota(jnp.int32, qpos.shape, 1)
    mask = (kpos < qpos) & (kpos >= N_PAD)
    return mask, lb, jnp.where(mask, lk, 0.0)


def _keep(mask, x):
    return x if mask is None else jnp.where(mask, x, 0.0)


def _sb_trips(i, trip):
    lax.fori_loop(0, 1, trip(True), 0)
    lax.fori_loop(1, i, trip(False), 0)
    lax.fori_loop(jnp.maximum(i, 1), i + 1, trip(True), 0)


def _tri_sum(x, tri):
    hi = x.astype(BF16)
    lo = (x - hi.astype(F32)).astype(BF16)
    if tri.shape[0] == CHUNK:
        return _dot(hi, tri) + _dot(lo, tri)
    return _dot(jnp.concatenate([hi, lo], axis=1), tri)


def _tri(strict_upper, copies=1):
    r = lax.broadcasted_iota(jnp.int32, (copies * CHUNK, CHUNK), 0) % CHUNK
    c = lax.broadcasted_iota(jnp.int32, (copies * CHUNK, CHUNK), 1)
    return ((r > c) if strict_upper else (r < c)).astype(BF16)


def _sb_fwd(proj, sbw, sh, col0, carry=()):
    tp = proj.shape[0]
    tq = _sb_tile(tp)
    nsub, nq = tq // CHUNK, tp // tq
    assert tp // CHUNK <= 128
    scale = 1.0 / math.sqrt(SB_DH)
    heads, ng, wd, cb = _sb_heads(sh, col0)
    hq = pl.BlockSpec((tq, wd), lambda h, i: (i, h))

    carry = list(carry)
    x_in, x_out, x_sems, alias = _carried(carry, 4, 3)

    def kern(q_ref, k_ref, v_ref, w_ref, *rest):
        x_in_refs, rest = rest[:len(x_in)], rest[len(x_in):]
        out_ref, sy_ref, ao_ref = rest[:3]
        x_out_refs, (a_run, k16, v16), sem_refs = rest[3:3 + len(x_out)], rest[3 + len(x_out):6 + len(x_out)], rest[6 + len(x_out):]
        h, i = pl.program_id(0), pl.program_id(1)
        _carried_before(carry, h * nq + i, ng * nq, x_in_refs, x_out_refs, sem_refs)

        @pl.when(i == 0)
        def _():
            k16[...] = k_ref[...].astype(BF16)
            v16[...] = v_ref[...].astype(BF16)

        qs = [q_ref[:, sl].astype(BF16) for _, sl in heads]
        upper = _tri(True)
        lane = lax.broadcasted_iota(jnp.int32, (tq, CHUNK), 1)
        qpos = i * tq + lax.broadcasted_iota(jnp.int32, (tq, CHUNK), 0)
        a_run[...] = jnp.zeros_like(a_run)
        sy_ref[...] = jnp.zeros_like(sy_ref)
        ao_ref[...] = jnp.zeros_like(ao_ref)

        def trip(masked):
            def body(jj, carry):
                for hh, sl in heads:
                    a, acc, at = a_run[hh], sy_ref[:, sl], ao_ref[hh]
                    for sub in reversed(range(nsub)):
                        kb = (i - jj) * nsub + sub
                        rows = pl.ds(pl.multiple_of(kb * CHUNK, CHUNK), CHUNK)
                        mask, lb, lk = _sb_block(qs[hh], k16[rows, sl], qpos, kb, scale, masked)
                        wgt = _keep(mask, jnp.exp(lb + a + _tri_sum(lk, upper)))
                        acc = acc + _dot(wgt.astype(BF16), v16[rows, sl])
                        at = jnp.where(lane == kb, a, at)
                        a = a + jnp.sum(lk, axis=1, keepdims=True)
                    a_run[hh], sy_ref[:, sl], ao_ref[hh] = a, acc, at
                return carry
            return body

        _sb_trips(i, trip)
        for _, sl in heads:
            out_ref[:, sl] = _rms_fwd(sy_ref[:, sl], w_ref[:, sl]).astype(BF16)
        _carried_after(carry, h * nq + i, ng * nq, x_in_refs, x_out_refs, sem_refs)

    kv = lambda sec: pl.BlockSpec((tp, wd), lambda h, i: (0, cb + sec * ng + h))
    res = pl.pallas_call(
        kern, name="sb_fwd", grid=(ng, nq),
        in_specs=[pl.BlockSpec((tq, wd), lambda h, i: (i, cb + h)), kv(1), kv(2),
                  pl.BlockSpec((1, wd), lambda h, i: (0, h))] + [ANY] * len(x_in),
        out_specs=[hq, hq, pl.BlockSpec((len(heads), tq, 128), lambda h, i: (h, i, 0))] + [ANY] * len(x_out),
        out_shape=[jax.ShapeDtypeStruct((tp, sh * SB_DH), BF16), jax.ShapeDtypeStruct((tp, sh * SB_DH), F32),
                   jax.ShapeDtypeStruct((sh, tp, 128), F32)] + x_out,
        input_output_aliases=alias,
        scratch_shapes=[pltpu.VMEM((len(heads), tq, CHUNK), F32), pltpu.VMEM((tp, wd), BF16),
                        pltpu.VMEM((tp, wd), BF16)] + x_sems,
        compiler_params=_cp(("arbitrary", "arbitrary") if carry else ("parallel", "arbitrary")),
    )(proj, proj, proj, sbw, *x_in)
    return res[:3], res[3:]


def _sb_bwd(proj, dmix, sy_all, aoff, sbw, sh, col0, dcol0, carry=()):
    tp = proj.shape[0]
    tq = _sb_tile(tp)
    nsub, nq = tq // CHUNK, tp // tq
    scale = 1.0 / math.sqrt(SB_DH)
    hq = pl.BlockSpec((tq, SB_DH), lambda h, i: (i, h))
    carry = list(carry)
    x_in, x_out, x_sems, alias = _carried(carry, 7, 4)

    def kern(q_ref, k_ref, v_ref, w_ref, do_ref, sy_ref, ao_ref, *rest):
        x_in_refs, rest = rest[:len(x_in)], rest[len(x_in):]
        dq_ref, dk_ref, dv_ref, dw_ref = rest[:4]
        x_out_refs, rest = rest[4:4 + len(x_out)], rest[4 + len(x_out):]
        (dk_acc, dv_acc, dq_acc, e_run, k16, v16), sem_refs = rest[:6], rest[6:]
        h, i = pl.program_id(0), pl.program_id(1)
        _carried_before(carry, h * nq + i, sh * nq, x_in_refs, x_out_refs, sem_refs)

        @pl.when(i == 0)
        def _():
            dk_acc[...] = jnp.zeros_like(dk_acc)
            dv_acc[...] = jnp.zeros_like(dv_acc)
            dw_ref[...] = jnp.zeros_like(dw_ref)
            k16[...] = k_ref[...].astype(BF16)
            v16[...] = v_ref[...].astype(BF16)

        qf = q_ref[...]
        q = qf.astype(BF16)
        dsy, dwr = _rms_bwd(sy_ref[...], w_ref[...], do_ref[...])
        dw_ref[...] += jnp.sum(dwr, axis=0, keepdims=True)
        dsy_b = dsy.astype(BF16)
        q_t = qf.T.astype(BF16)
        dsy_t = dsy.T.astype(BF16)
        atile = ao_ref[0]
        upper = _tri(True, 2)
        lower = _tri(False, 2)
        lane = lax.broadcasted_iota(jnp.int32, (tq, CHUNK), 1)
        qpos = i * tq + lax.broadcasted_iota(jnp.int32, (tq, CHUNK), 0)

        e_run[...] = jnp.zeros_like(e_run)
        dq_acc[...] = jnp.zeros_like(dq_acc)

        def trip(masked):
            def body(jj, carry):
                e_prev, dq = e_run[...], dq_acc[...]
                for sub in range(nsub):
                    kb = jj * nsub + sub
                    rows = pl.ds(pl.multiple_of(kb * CHUNK, CHUNK), CHUNK)
                    k, v = k16[rows, :], v16[rows, :]
                    mask, lb, lk = _sb_block(q, k, qpos, kb, scale, masked)
                    a = jnp.sum(jnp.where(lane == kb, atile, 0.0), axis=1, keepdims=True)
                    wgt = _keep(mask, jnp.exp(lb + a + _tri_sum(lk, upper)))
                    e = wgt * _dot(dsy_b, v, NT)
                    dv_acc[kb] += _dot(dsy_t, wgt.astype(BF16))
                    sig = jnp.exp(lb)
                    e_all = e_prev + _tri_sum(e, lower)
                    dz = (_keep(mask, e - sig * (e + e_all)) * scale).astype(BF16)
                    dk_acc[kb] += _dot(q_t, dz)
                    dq = dq + _dot(dz, k)
                    e_prev = e_prev + jnp.sum(e, axis=1, keepdims=True)
                e_run[...], dq_acc[...] = e_prev, dq
                return carry
            return body

        _sb_trips(i, trip)
        dq_ref[...] = dq_acc[...].astype(BF16)

        @pl.when(i == nq - 1)
        def _():
            def untranspose(kb, c):
                rows = pl.ds(pl.multiple_of(kb * CHUNK, CHUNK), CHUNK)
                dk_ref[rows, :] = dk_acc[kb].T.astype(BF16)
                dv_ref[rows, :] = dv_acc[kb].T.astype(BF16)
                return c

            lax.fori_loop(0, tp // CHUNK, untranspose, 0)

        _carried_after(carry, h * nq + i, sh * nq, x_in_refs, x_out_refs, sem_refs)

    kv = lambda sec: pl.BlockSpec((tp, SB_DH), lambda h, i: (0, col0 + sec * sh + h))
    hfull = pl.BlockSpec((tp, SB_DH), lambda h, i: (0, h))
    sw = sh * SB_DH
    acc_t = pltpu.VMEM((tp // CHUNK, SB_DH, CHUNK), F32)
    res = pl.pallas_call(
        kern, name="sb_bwd", grid=(sh, nq),
        in_specs=[pl.BlockSpec((tq, SB_DH), lambda h, i: (i, col0 + h)), kv(1), kv(2),
                  pl.BlockSpec((1, SB_DH), lambda h, i: (0, h)),
                  pl.BlockSpec((tq, SB_DH), lambda h, i: (i, dcol0 + h)), hq,
                  pl.BlockSpec((1, tq, 128), lambda h, i: (h, i, 0))] + [ANY] * len(x_in),
        out_specs=[hq, hfull, hfull, pl.BlockSpec((1, SB_DH), lambda h, i: (0, h))] + [ANY] * len(x_out),
        out_shape=[jax.ShapeDtypeStruct((tp, sw), BF16)] * 3 + [jax.ShapeDtypeStruct((1, sw), F32)] + x_out,
        input_output_aliases=alias,
        scratch_shapes=[acc_t, acc_t, pltpu.VMEM((tq, SB_DH), F32), pltpu.VMEM((tq, CHUNK), F32),
                        pltpu.VMEM((tp, SB_DH), BF16), pltpu.VMEM((tp, SB_DH), BF16)] + x_sems,
        compiler_params=_cp(("arbitrary", "arbitrary") if carry else ("parallel", "arbitrary")),
    )(proj, proj, proj, sbw, dmix, sy_all, aoff, *x_in)
    return res[:4], res[4:]


def _local_step(x, tgt, meta, net, conv_w, conv_b, pre1_w, gn_w, sb_w, post1_w, pre2_w, post2_w):
    s, d = x.shape

    def mm(host, a, b, mode, dtype):
        carry = net.carry(host)
        out = _matmul(a, b, mode, dtype, host, carry)
        if carry:
            out, extra = out
            net.took(host, extra)
        return out

    tp = s + CHUNK
    rh, sh = d // 512, d // 256
    rw = rh * RET_DK
    h0 = jnp.concatenate([jnp.zeros((N_PAD, d), F32), meta, x], axis=0)
    pos = jnp.arange(tp, dtype=F32) - N_PAD
    inv = ROPE_BASE ** (-jnp.arange(128, dtype=F32) / 128)
    ang = pos[:, None] * inv[None, :]
    cos, sin = jnp.cos(ang), jnp.sin(ang)
    tables = _ret_tables(rh)
    sb_col0 = 4 * rw // SB_DH

    n1 = _prenorm(h0, pre1_w)
    proj = mm("mm_proj", n1, net.weight("w_in"), "nn", F32)
    ret_out, ry, states = _ret_fwd(proj, cos, sin, tables, gn_w, rh)
    (sb_out, sy, aoff), extra = _sb_fwd(proj, sb_w, sh, sb_col0, net.carry("sb_fwd"))
    net.took("sb_fwd", extra)
    mixed = jnp.concatenate([ret_out, sb_out], axis=1)
    a = mm("mm_out", mixed, net.weight("w_out"), "nn", F32)
    h1, n2 = _mid_fwd(h0, a, post1_w, pre2_w)
    gu = mm("mm_up", n2, net.weight("w_up"), "nn", BF16)
    act = _ffn_fwd(gu, conv_w, conv_b)
    f = mm("mm_down", act, net.weight("w_down"), "nn", F32)

    dy, d_f, loss, dw_post2 = _loss_bwd(h1, f, tgt, post2_w)
    d_act = mm("mm_dact", d_f, net.weight("w_down"), "nt", BF16)
    net.grad("w_down", mm("mm_gdown", act, d_f, "tn", BF16))
    d_gate, d_up, ffn_stats = _ffn_bwd(gu, d_act, conv_w, conv_b)
    d_gu = jnp.concatenate([d_gate, d_up], axis=1)
    d_n2 = mm("mm_dn2", d_gu, net.weight("w_up"), "nt", F32)
    net.grad("w_up", mm("mm_gup", n2, d_gu, "tn", BF16))
    dh1, d_a, dw_pre2, dw_post1 = _mid_bwd(dy, d_n2, h1, a, pre2_w, post1_w)
    d_mix = mm("mm_dmix", d_a, net.weight("w_out"), "nt", F32)
    net.grad("w_out", mm("mm_gout", mixed, d_a, "tn", BF16))
    d_rq, d_rk, d_rv, d_rg, dw_gn = _ret_bwd(proj, d_mix, ry, states, cos, sin, tables, gn_w, rh)
    (d_sq, d_sk, d_sv, dw_sb), extra = _sb_bwd(proj, d_mix, sy, aoff, sb_w, sh, sb_col0, rw // SB_DH,
                                               net.carry("sb_bwd"))
    net.took("sb_bwd", extra)
    d_proj = jnp.concatenate([d_rq, d_rk, d_rv, d_rg, d_sq, d_sk, d_sv], axis=1)
    net.grad("w_in", mm("mm_gin", n1, d_proj, "tn", BF16))
    d_n1 = mm("mm_dn1", d_proj, net.weight("w_in"), "nt", F32)
    grad_x, g_meta, dw_pre1 = _pre_bwd(dh1, d_n1, h0, pre1_w)

    small = dict(loss=loss, meta=g_meta, pre1=dw_pre1, gn=dw_gn, sb=dw_sb, post1=dw_post1, pre2=dw_pre2,
                 conv_w=ffn_stats[0:3], conv_b=ffn_stats[3:4], post2=dw_post2)
    return grad_x, small


def _coords():
    return lax.axis_index("x"), lax.axis_index("y"), lax.axis_index("c")


def _other_chips(x, y):
    return [(1 - x, y), (x, 1 - y), (1 - x, 1 - y)]


ANY = pl.BlockSpec(memory_space=pl.ANY)
VM = pl.BlockSpec(memory_space=pltpu.VMEM)


def _gather4_small(v):
    r = v.shape[0]

    def kern(v_ref, o_ref, send, recv):
        x, y, c = _coords()
        o_ref[2 * x + y] = v_ref[...]
        cps = [pltpu.make_async_remote_copy(v_ref, o_ref.at[2 * x + y], send.at[j], recv.at[j],
                                            device_id=(px, py, c), device_id_type=MESH)
               for j, (px, py) in enumerate(_other_chips(x, y))]
        for cp in cps:
            cp.start()
        for cp in cps:
            cp.wait()

    return pl.pallas_call(
        kern, name="gather_small", in_specs=[VM], out_specs=VM,
        out_shape=jax.ShapeDtypeStruct((4, r, 128), F32),
        scratch_shapes=[pltpu.SemaphoreType.DMA((3,)), pltpu.SemaphoreType.DMA((3,))],
    )(v)


def _allreduce8_small(v):
    r = v.shape[0]
    flips = [(fx, fy, fc) for fx in (0, 1) for fy in (0, 1) for fc in (0, 1)][1:]

    def kern(v_ref, o_ref, buf, send, recv):
        x, y, c = _coords()
        me = 4 * x + 2 * y + c
        buf[me] = v_ref[...]
        cps = [pltpu.make_async_remote_copy(v_ref, buf.at[me], send.at[j], recv.at[j],
                                            device_id=(x ^ fx, y ^ fy, c ^ fc), device_id_type=MESH)
               for j, (fx, fy, fc) in enumerate(flips)]
        for cp in cps:
            cp.start()
        for cp in cps:
            cp.wait()
        tot = buf[0]
        for j in range(1, 8):
            tot = tot + buf[j]
        o_ref[...] = tot

    return pl.pallas_call(
        kern, name="allreduce_small", in_specs=[VM], out_specs=VM,
        out_shape=jax.ShapeDtypeStruct((r, 128), F32),
        scratch_shapes=[pltpu.VMEM((8, r, 128), F32), pltpu.SemaphoreType.DMA((7,)),
                        pltpu.SemaphoreType.DMA((7,))],
    )(v)


def _piece(ref, col_sharded, rows, cols, s, hc):
    half = rows // 2
    if col_sharded:
        return ref.at[pl.ds(pl.multiple_of(hc * half, 16), half), pl.ds(pl.multiple_of(s * cols, 128), cols)]
    return ref.at[pl.ds(pl.multiple_of(s * rows + hc * half, 16), half), :]


def _cast_place(w32, c_id, s_id, col_sharded, name):
    rows, cols = w32.shape
    full_shape = (rows, 4 * cols) if col_sharded else (4 * rows, cols)
    tr = max(d for d in _divisors(rows, 16) if d * cols * 4 <= 4 * 2 ** 20)
    nt = rows // tr

    def kern(c_ref, s_ref, w_ref, o_ref):
        o_ref[...] = w_ref[...].astype(BF16)

    if col_sharded:
        o_spec = pl.BlockSpec((tr, cols), lambda t, c_ref, s_ref: (t, s_ref[0]))
    else:
        o_spec = pl.BlockSpec((tr, cols), lambda t, c_ref, s_ref: (s_ref[0] * nt + t, 0))
    return pl.pallas_call(
        kern, name=name,
        grid_spec=pltpu.PrefetchScalarGridSpec(
            num_scalar_prefetch=2, grid=(nt,),
            in_specs=[pl.BlockSpec((tr, cols), lambda t, c_ref, s_ref: (t, 0))], out_specs=o_spec),
        out_shape=jax.ShapeDtypeStruct(full_shape, BF16),
        compiler_params=_cp(("parallel",)),
    )(c_id, s_id, w32)


def _gather_exchange(full, rows, cols, col_sharded, relay_at=1.0):
    def copies(w_ref, o_ref, send, recv):
        x, y, c = _coords()
        s = 2 * x + y
        sib = (x, y, 1 - c)
        chips = _other_chips(x, y)
        pc = functools.partial(_piece, o_ref, col_sharded, rows, cols)
        mine = _piece(w_ref, col_sharded, rows, cols, s, c)
        first = [pltpu.make_async_remote_copy(mine, pc(s, c), send.at[j], recv.at[j],
                                              device_id=(px, py, c), device_id_type=MESH)
                 for j, (px, py) in enumerate(chips)]
        passed = [pltpu.make_async_remote_copy(pc(2 * px + py, c), pc(2 * px + py, c), send.at[3 + j], recv.at[3 + j],
                                               device_id=sib, device_id_type=MESH)
                  for j, (px, py) in enumerate(chips)]
        from_sib = [pltpu.make_async_remote_copy(pc(2 * px + py, 1 - c), pc(2 * px + py, 1 - c), send.at[3 + j],
                                                 recv.at[3 + j], device_id=sib, device_id_type=MESH)
                    for j, (px, py) in enumerate(chips)]
        return first, passed, from_sib

    def start(ins, outs, send, recv):
        for cp in copies(ins[0], outs[0], send, recv)[0]:
            cp.start()

    def relay(ins, outs, send, recv):
        first, passed, _ = copies(ins[0], outs[0], send, recv)
        for j in range(3):
            first[j].wait_recv()
            passed[j].start()

    def finish(ins, outs, send, recv):
        first, passed, from_sib = copies(ins[0], outs[0], send, recv)
        for cp in from_sib:
            cp.wait_recv()
        for cp in first + passed:
            cp.wait_send()

    return _Exchange([full], [jax.ShapeDtypeStruct(full.shape, BF16)], {0: 0}, 6, start, finish, relay, relay_at)


def _pair_exchange(g, col_sharded, rows, cols):
    half = rows // 2

    def copies(g_ref, o_ref, send, recv):
        x, y, c = _coords()
        return [pltpu.make_async_remote_copy(_piece(g_ref, col_sharded, rows, cols, s, 1 - c), o_ref.at[s],
                                             send.at[s], recv.at[s], device_id=(x, y, 1 - c), device_id_type=MESH)
                for s in range(4)]

    def start(ins, outs, send, recv):
        for cp in copies(ins[0], outs[0], send, recv):
            cp.start()

    def finish(ins, outs, send, recv):
        for cp in copies(ins[0], outs[0], send, recv):
            cp.wait()

    return _Exchange([g], [jax.ShapeDtypeStruct((4, half, cols), BF16)], {}, 4, start, finish)


def _half_tiles(half, cols):
    tr = max(d for d in _divisors(half, 16) if d * cols * 4 <= 4 * 2 ** 20)
    return tr, half // tr


def _half_spec(col_sharded, tr, nt, cols, own):
    which = (lambda s, s_ref: s_ref[0]) if own else (lambda s, s_ref: (s_ref[0] + 1 + s) % 4)
    if col_sharded:
        return pl.BlockSpec((tr, cols), lambda s, t, c_ref, s_ref: (c_ref[0] * nt + t, which(s, s_ref)))
    return pl.BlockSpec((tr, cols), lambda s, t, c_ref, s_ref: ((2 * which(s, s_ref) + c_ref[0]) * nt + t, 0))


def _rs_add(g, r1, c_id, s_id, col_sharded, rows, cols, name):
    half = rows // 2
    tr, nt = _half_tiles(half, cols)

    def kern(c_ref, s_ref, g_ref, r_ref, o_ref):
        o_ref[0] = (g_ref[...].astype(F32) + r_ref[0].astype(F32)).astype(BF16)

    slab = pl.BlockSpec((1, tr, cols), lambda s, t, c_ref, s_ref: ((s_ref[0] + 1 + s) % 4, t, 0))
    return pl.pallas_call(
        kern, name=name,
        grid_spec=pltpu.PrefetchScalarGridSpec(
            num_scalar_prefetch=2, grid=(3, nt),
            in_specs=[_half_spec(col_sharded, tr, nt, cols, False), slab], out_specs=slab),
        out_shape=jax.ShapeDtypeStruct((4, half, cols), BF16),
        compiler_params=_cp(("parallel", "parallel")),
    )(c_id, s_id, g, r1)


def _scatter_exchange(p):
    _, half, cols = p.shape

    def copies(p_ref, o_ref, send, recv):
        x, y, c = _coords()
        return [pltpu.make_async_remote_copy(p_ref.at[2 * px + py], o_ref.at[j], send.at[j], recv.at[j],
                                             device_id=(px, py, c), device_id_type=MESH)
                for j, (px, py) in enumerate(_other_chips(x, y))]

    def start(ins, outs, send, recv):
        for cp in copies(ins[0], outs[0], send, recv):
            cp.start()

    def finish(ins, outs, send, recv):
        for cp in copies(ins[0], outs[0], send, recv):
            cp.wait()

    return _Exchange([p], [jax.ShapeDtypeStruct((3, half, cols), BF16)], {}, 3, start, finish)


def _rs_total(g, r1, r2, c_id, s_id, col_sharded, rows, cols, name):
    half = rows // 2
    tr, nt = _half_tiles(half, cols)

    def kern(c_ref, s_ref, g_ref, r1_ref, r2_ref, o_ref):
        tot = g_ref[...].astype(F32) + r1_ref[0].astype(F32)
        for j in range(3):
            tot = tot + r2_ref[j].astype(F32)
        o_ref[...] = tot

    return pl.pallas_call(
        kern, name=name,
        grid_spec=pltpu.PrefetchScalarGridSpec(
            num_scalar_prefetch=2, grid=(1, nt),
            in_specs=[_half_spec(col_sharded, tr, nt, cols, True),
                      pl.BlockSpec((1, tr, cols), lambda s, t, c_ref, s_ref: (s_ref[0], t, 0)),
                      pl.BlockSpec((3, tr, cols), lambda s, t, c_ref, s_ref: (0, t, 0))],
            out_specs=pl.BlockSpec((tr, cols), lambda s, t, c_ref, s_ref: (c_ref[0] * nt + t, 0))),
        out_shape=jax.ShapeDtypeStruct((rows, cols), F32),
        compiler_params=_cp(("parallel", "parallel")),
    )(c_id, s_id, g, r1, r2)


def _rs_exchange(t, name):
    rows, cols = t.shape
    half = rows // 2

    def kern(t_ref, o_ref, send, recv):
        x, y, c = _coords()
        mine = pl.ds(pl.multiple_of(c * half, 8), half)
        cp = pltpu.make_async_remote_copy(t_ref.at[mine, :], o_ref.at[mine, :], send, recv,
                                          device_id=(x, y, 1 - c), device_id_type=MESH)
        cp.start()
        cp.wait()

    return pl.pallas_call(
        kern, name=name, in_specs=[ANY], out_specs=ANY, input_output_aliases={0: 0},
        out_shape=jax.ShapeDtypeStruct((rows, cols), F32),
        scratch_shapes=[pltpu.SemaphoreType.DMA, pltpu.SemaphoreType.DMA],
    )(t)


class _Whole:
    def __init__(self, full):
        self.full, self.grads = dict(full), {}

    def weight(self, n):
        return self.full[n]

    def carry(self, host):
        return []

    def took(self, host, outs):
        pass

    def grad(self, n, g):
        self.grads[n] = g


BIG = ("w_in", "w_out", "w_up", "w_down")
COL_SHARDED = dict(w_in=True, w_out=False, w_up=True, w_down=False)
GATHER_ON = dict(mm_proj="w_out", sb_fwd="w_up", mm_up="w_down")
RELAY_AT = dict(mm_proj=0.6, sb_fwd=0.85, mm_up=0.7)
PAIR_ON = dict(mm_dn2="w_down", mm_dmix="w_up")
SCATTER_ON = dict(mm_gup=("w_down",), sb_bwd=("w_up", "w_out"), mm_dn1=("w_in",))


class _Sharded(_Whole):
    def __init__(self, shards, c_id, s_id):
        self.shards, self.c_id, self.s_id = shards, c_id, s_id
        self.full, self.grads, self.r1, self.p, self.r2 = {}, {}, {}, {}, {}
        self.placed = {n: _cast_place(shards[n], c_id, s_id, COL_SHARDED[n], "place_" + n) for n in BIG}
        self.full["w_in"], = _run_exchange(self._gather("w_in"), "gather_w_in")

    def _gather(self, n, relay_at=1.0):
        return _gather_exchange(self.placed[n], *self.shards[n].shape, COL_SHARDED[n], relay_at)

    def _pair(self, n):
        return _pair_exchange(self.grads[n], COL_SHARDED[n], *self.shards[n].shape)

    def _paired(self, n, r1):
        rows, cols = self.shards[n].shape
        self.r1[n] = r1
        self.p[n] = _rs_add(self.grads[n], r1, self.c_id, self.s_id, COL_SHARDED[n], rows, cols, "rs_add_" + n)

    def carry(self, host):
        if host in GATHER_ON:
            return [self._gather(GATHER_ON[host], RELAY_AT[host])]
        if host in PAIR_ON:
            return [self._pair(PAIR_ON[host])]
        return [_scatter_exchange(self.p[n]) for n in SCATTER_ON.get(host, ())]

    def took(self, host, outs):
        if host in GATHER_ON:
            self.full[GATHER_ON[host]], = outs
        if host in PAIR_ON:
            self._paired(PAIR_ON[host], outs[0])
        for n, r2 in zip(SCATTER_ON.get(host, ()), outs):
            self.r2[n] = r2

    def grad(self, n, g):
        self.grads[n] = g
        if n not in PAIR_ON.values():
            self._paired(n, _run_exchange(self._pair(n), "rs_pair_" + n)[0])

    def reduced(self, n):
        rows, cols = self.shards[n].shape
        t = _rs_total(self.grads[n], self.r1[n], self.r2[n], self.c_id, self.s_id, COL_SHARDED[n], rows, cols,
                      "rs_total_" + n)
        return _rs_exchange(t, "rs_exchange_" + n)


def _adamw_vals(w, g, m, v):
    m = ADAM_B1 * m + (1.0 - ADAM_B1) * g
    v = ADAM_B2 * v + (1.0 - ADAM_B2) * (g * g)
    m_hat = m / (1.0 - ADAM_B1 ** ADAM_STEP)
    v_hat = v / (1.0 - ADAM_B2 ** ADAM_STEP)
    delta = -ADAM_LR * (m_hat / (jnp.sqrt(v_hat) + ADAM_EPS) + ADAM_WD * w)
    return delta, m, v


def _adamw(w, g, m, v, name):
    rows, cols = w.shape
    tr = max(d for d in _divisors(rows, 8) if d * cols * 4 <= 2 * 2 ** 20)

    def kern(w_ref, g_ref, m_ref, v_ref, d_ref, mo_ref, vo_ref, go_ref):
        g = g_ref[...]
        d_ref[...], mo_ref[...], vo_ref[...] = _adamw_vals(w_ref[...], g, m_ref[...], v_ref[...])
        go_ref[...] = g

    spec = pl.BlockSpec((tr, cols), lambda i: (i, 0))
    return pl.pallas_call(
        kern, name=name, grid=(rows // tr,), in_specs=[spec] * 4, out_specs=[spec] * 4,
        out_shape=[jax.ShapeDtypeStruct((rows, cols), F32)] * 4,
        compiler_params=_cp(("parallel",)),
    )(w, g, m, v)


def _pack(arrs):
    flat = []
    for a in arrs:
        a = a.reshape(-1)
        flat.append(jnp.pad(a, (0, (-a.shape[0]) % 1024)))
    return jnp.concatenate(flat).reshape(-1, 128)


def _unpack(slab, shapes):
    out, off = [], 0
    flat = slab.reshape(slab.shape[:-2] + (-1,))
    for shp in shapes:
        n = math.prod(shp)
        out.append(flat[..., off:off + n].reshape(slab.shape[:-2] + tuple(shp)))
        off += n + (-n) % 1024
    return out


SMALL = ("meta_tokens", "attn_pre_norm_w", "ret_gn_w", "sb_norm_w", "attn_post_norm_w", "ffn_pre_norm_w",
         "conv_w", "conv_b", "ffn_post_norm_w")
ORDER = ("meta_tokens", "attn_pre_norm_w", "w_in", "ret_gn_w", "sb_norm_w", "w_out", "attn_post_norm_w",
         "ffn_pre_norm_w", "w_up", "conv_w", "conv_b", "w_down", "ffn_post_norm_w")


def kernel(x, meta_tokens, attn_pre_norm_w, w_in, ret_gn_w, sb_norm_w, w_out, attn_post_norm_w, ffn_pre_norm_w, w_up, conv_w, conv_b, w_down, ffn_post_norm_w, loss_target, m_meta_tokens, m_attn_pre_norm_w, m_w_in, m_ret_gn_w, m_sb_norm_w, m_w_out, m_attn_post_norm_w, m_ffn_pre_norm_w, m_w_up, m_conv_w, m_conv_b, m_w_down, m_ffn_post_norm_w, v_meta_tokens, v_attn_pre_norm_w, v_w_in, v_ret_gn_w, v_sb_norm_w, v_w_out, v_attn_post_norm_w, v_ffn_pre_norm_w, v_w_up, v_conv_w, v_conv_b, v_w_down, v_ffn_post_norm_w):
    args = dict(locals())
    w = {n: args[n] for n in ORDER}
    m = {n: args["m_" + n] for n in ORDER}
    v = {n: args["v_" + n] for n in ORDER}
    xi, yi, ci = _coords()
    shard_id = 2 * xi + yi
    c_id = ci.astype(jnp.int32).reshape(1)
    s_id = shard_id.astype(jnp.int32).reshape(1)
    d = x.shape[-1]

    mshape, cshape = w["meta_tokens"].shape, w["conv_w"][0].shape
    got = _unpack(_gather4_small(_pack([w["meta_tokens"], w["conv_w"][0]])), [mshape, cshape])
    meta_full = jnp.moveaxis(got[0], 0, 1).reshape(N_META, d)
    conv_w_full = jnp.moveaxis(got[1], 0, 1).reshape(CONV_W, -1)

    net = _Sharded({n: w[n][0] for n in BIG}, c_id, s_id)
    grad_x, g_small = _local_step(
        x[0], loss_target[0], meta_full, net, conv_w_full, w["conv_b"], w["attn_pre_norm_w"], w["ret_gn_w"],
        w["sb_norm_w"], w["attn_post_norm_w"], w["ffn_pre_norm_w"], w["ffn_post_norm_w"])

    names = ("loss", "meta", "pre1", "gn", "sb", "post1", "pre2", "conv_w", "conv_b", "post2")
    tot = _unpack(_allreduce8_small(_pack([g_small[n] for n in names])), [g_small[n].shape for n in names])
    tot = dict(zip(names, tot))
    loss = tot["loss"][0, 0]
    mcols, ccols = mshape[1], cshape[1]
    grads = {
        "meta_tokens": lax.dynamic_slice_in_dim(tot["meta"], shard_id * mcols, mcols, axis=1),
        "attn_pre_norm_w": tot["pre1"], "ret_gn_w": tot["gn"], "sb_norm_w": tot["sb"],
        "attn_post_norm_w": tot["post1"], "ffn_pre_norm_w": tot["pre2"],
        "conv_w": lax.dynamic_slice_in_dim(tot["conv_w"], shard_id * ccols, ccols, axis=1)[None],
        "conv_b": tot["conv_b"], "ffn_post_norm_w": tot["post2"],
    }

    delta, new_m, new_v = {}, {}, {}
    for n in BIG:
        dl, mo, vo, g = _adamw(w[n][0], net.reduced(n), m[n][0], v[n][0], "adamw_" + n)
        delta[n], new_m[n], new_v[n], grads[n] = dl[None], mo[None], vo[None], g[None]
    shapes = [w[n].shape for n in SMALL]
    packed = [_pack([src[n] for n in SMALL]) for src in (w, grads, m, v)]
    outs = _adamw(*packed, "adamw_small")[:3]
    for dst, slab in zip((delta, new_m, new_v), outs):
        for n, a in zip(SMALL, _unpack(slab, shapes)):
            dst[n] = a

    return (loss, grad_x[None], *[grads[n] for n in ORDER], *[delta[n] for n in ORDER],
            *[new_m[n] for n in ORDER], *[new_v[n] for n in ORDER])
```

```python
import functools
import math

import jax
import jax.numpy as jnp
from jax import lax
from jax.experimental import pallas as pl
from jax.experimental.pallas import tpu as pltpu

F32 = jnp.float32
BF16 = jnp.bfloat16
MESH = pl.DeviceIdType.MESH

EPS = 1e-6
ROPE_BASE = 10000.0
N_META = 16
CHUNK = 128
N_PAD = CHUNK - N_META
RET_DK = 256
SB_DH = 128
SB_HEADS_PER_STEP = 1
CONV_W = 3

ADAM_LR = 0.001
ADAM_B1 = 0.9
ADAM_B2 = 0.999
ADAM_EPS = 1e-08
ADAM_WD = 0.01
ADAM_STEP = 10

V7X_VMEM_BYTES = 64 * 2 ** 20
VMEM_LIMIT = V7X_VMEM_BYTES - 8 * 2 ** 20
MM_BUDGET = 40 * 2 ** 20
V7X_BF16_FLOPS = 0.9e15
V7X_HBM_BPS = 2.0e12
GRID_STEP_S = 0.6e-6
ACC_BPS = 2.0e13

NN = (((1,), (0,)), ((), ()))
NT = (((1,), (1,)), ((), ()))
TN = (((0,), (0,)), ((), ()))


def _cp(sem, vmem=VMEM_LIMIT):
    return pltpu.CompilerParams(dimension_semantics=sem, vmem_limit_bytes=vmem)


def _dot(a, b, dims=NN):
    return lax.dot_general(a, b, dims, preferred_element_type=F32)


def _sigmoid(x):
    return 1.0 / (1.0 + jnp.exp(-x))


def _divisors(n, align):
    return [d for d in range(align, n + 1, align) if n % d == 0]


def _mm_tiles(mode, m, n, k, out_bytes):
    best = None
    tms = [d for d in _divisors(m, 128 if mode == "tn" else 16) if d >= 128]
    tns = [d for d in _divisors(n, 256)] or [d for d in _divisors(n, 128)]
    tks = [d for d in _divisors(k, 128) if d >= 128]
    flops = 2.0 * m * n * k
    for tm in tms:
        for tn in tns:
            if tm * tn > 2112 * 1024:
                continue
            for tk in tks:
                nk = k // tk
                foot = 4 * (tm * tk + tk * tn) + 2 * tm * tn * out_bytes
                if nk > 1:
                    foot += 4 * tm * tn
                if foot > MM_BUDGET:
                    continue
                steps = (m // tm) * (n // tn) * nk
                for swap in (False, True):
                    if nk > 1:
                        traffic = 2.0 * (n // tn) * m * k + 2.0 * (m // tm) * n * k
                    elif swap:
                        traffic = 2.0 * n * k + 2.0 * (n // tn) * m * k
                    else:
                        traffic = 2.0 * m * k + 2.0 * (m // tm) * n * k
                    traffic += out_bytes * m * n
                    t = max(flops / V7X_BF16_FLOPS, traffic / V7X_HBM_BPS) + steps * GRID_STEP_S
                    if nk > 1:
                        t += 12.0 * m * n * nk / ACC_BPS
                    if best is None or t < best[0]:
                        best = (t, tm, tn, tk, swap)
    assert best is not None, (mode, m, n, k)
    return best[1:]


class _Exchange:
    def __init__(self, operands, out_shapes, aliases, nsem, start, finish, relay=None, relay_at=1.0):
        self.operands, self.out_shapes, self.aliases, self.nsem = operands, out_shapes, aliases, nsem
        self.start, self.finish, self.relay, self.relay_at = start, finish, relay, relay_at


def _carried(exchanges, in_base, out_base):
    ins = [a for e in exchanges for a in e.operands]
    outs = [s for e in exchanges for s in e.out_shapes]
    sems = [pltpu.SemaphoreType.DMA((e.nsem,)) for e in exchanges for _ in (0, 1)]
    alias, i, o = {}, in_base, out_base
    for e in exchanges:
        alias.update({i + k: o + v for k, v in e.aliases.items()})
        i, o = i + len(e.operands), o + len(e.out_shapes)
    return ins, outs, sems, alias


def _run_carried(exchanges, phase, in_refs, out_refs, sem_refs, pick=None):
    i = o = 0
    for n, e in enumerate(exchanges):
        ni, no = len(e.operands), len(e.out_shapes)
        if getattr(e, phase) is not None and pick in (None, n):
            getattr(e, phase)(in_refs[i:i + ni], out_refs[o:o + no], sem_refs[2 * n], sem_refs[2 * n + 1])
        i, o = i + ni, o + no


def _carried_before(exchanges, step, nsteps, in_refs, out_refs, sem_refs):
    if not exchanges:
        return

    @pl.when(step == 0)
    def _():
        _run_carried(exchanges, "start", in_refs, out_refs, sem_refs)

    for n, e in enumerate(exchanges):
        if e.relay is not None:
            @pl.when(step == min(nsteps - 1, int(e.relay_at * nsteps)))
            def _(n=n):
                _run_carried(exchanges, "relay", in_refs, out_refs, sem_refs, pick=n)


def _carried_after(exchanges, step, nsteps, in_refs, out_refs, sem_refs):
    if not exchanges:
        return

    @pl.when(step == nsteps - 1)
    def _():
        _run_carried(exchanges, "finish", in_refs, out_refs, sem_refs)


def _run_exchange(e, name):
    ins, outs, sems, alias = _carried([e], 0, 0)

    def kern(*refs):
        in_refs, out_refs, sem_refs = refs[:len(ins)], refs[len(ins):len(ins) + len(outs)], refs[len(ins) + len(outs):]
        for phase in ("start", "relay", "finish"):
            _run_carried([e], phase, in_refs, out_refs, sem_refs)

    return pl.pallas_call(
        kern, name=name, in_specs=[ANY] * len(ins), out_specs=[ANY] * len(outs), out_shape=outs,
        input_output_aliases=alias, scratch_shapes=sems,
    )(*ins)


def _matmul(a, b, mode, out_dtype, name, carry=()):
    if mode == "nn":
        (m, k), (k2, n) = a.shape, b.shape
    elif mode == "nt":
        (m, k), (n, k2) = a.shape, b.shape
    else:
        (k, m), (k2, n) = a.shape, b.shape
    assert k == k2 and a.dtype == BF16 and b.dtype == BF16
    tm, tn, tk, swap = _mm_tiles(mode, m, n, k, jnp.dtype(out_dtype).itemsize)
    nk = k // tk
    dims = {"nn": NN, "nt": NT, "tn": TN}[mode]

    def ij(g0, g1):
        return (g1, g0) if swap else (g0, g1)

    if mode == "tn":
        a_spec = pl.BlockSpec((tk, tm), lambda g0, g1, kk: (kk, ij(g0, g1)[0]))
    else:
        a_spec = pl.BlockSpec((tm, tk), lambda g0, g1, kk: (ij(g0, g1)[0], kk))
    if mode == "nt":
        b_spec = pl.BlockSpec((tn, tk), lambda g0, g1, kk: (ij(g0, g1)[1], kk))
    else:
        b_spec = pl.BlockSpec((tk, tn), lambda g0, g1, kk: (kk, ij(g0, g1)[1]))
    o_spec = pl.BlockSpec((tm, tn), lambda g0, g1, kk: ij(g0, g1))

    grid = (n // tn, m // tm, nk) if swap else (m // tm, n // tn, nk)
    carry = list(carry)
    x_in, x_out, x_sems, alias = _carried(carry, 2, 1)
    acc_shapes = [pltpu.VMEM((tm, tn), F32)] if nk > 1 else []

    def kern(a_ref, b_ref, *rest):
        x_in_refs, o_ref = rest[:len(x_in)], rest[len(x_in)]
        x_out_refs = rest[len(x_in) + 1:len(x_in) + 1 + len(x_out)]
        tail = rest[len(x_in) + 1 + len(x_out):]
        acc, sem_refs = tail[:len(acc_shapes)], tail[len(acc_shapes):]
        pid = [pl.program_id(ax) for ax in range(3)]
        step, nsteps = (pid[0] * grid[1] + pid[1]) * grid[2] + pid[2], grid[0] * grid[1] * grid[2]
        _carried_before(carry, step, nsteps, x_in_refs, x_out_refs, sem_refs)
        prod = _dot(a_ref[...], b_ref[...], dims)
        if nk == 1:
            o_ref[...] = prod.astype(out_dtype)
        else:
            kk = pid[2]

            @pl.when(kk == 0)
            def _():
                acc[0][...] = prod

            @pl.when(kk > 0)
            def _():
                acc[0][...] += prod

            @pl.when(kk == nk - 1)
            def _():
                o_ref[...] = acc[0][...].astype(out_dtype)

        _carried_after(carry, step, nsteps, x_in_refs, x_out_refs, sem_refs)

    sem = ("arbitrary",) * 3 if carry else ("parallel", "parallel", "arbitrary")
    res = pl.pallas_call(
        kern, name=name, grid=grid, in_specs=[a_spec, b_spec] + [ANY] * len(x_in),
        out_specs=[o_spec] + [ANY] * len(x_out),
        out_shape=[jax.ShapeDtypeStruct((m, n), out_dtype)] + x_out,
        input_output_aliases=alias, scratch_shapes=acc_shapes + x_sems,
        compiler_params=_cp(sem),
    )(a, b, *x_in)
    return (res[0], res[1:]) if carry else res[0]


def _rms_fwd(x, w):
    r = lax.rsqrt(jnp.mean(x * x, axis=-1, keepdims=True) + EPS)
    return x * r * w


def _rms_bwd(x, w, g):
    r = lax.rsqrt(jnp.mean(x * x, axis=-1, keepdims=True) + EPS)
    gw = g * w
    dx = r * gw - x * (r * r * r * jnp.mean(gw * x, axis=-1, keepdims=True))
    return dx, g * (x * r)


def _row_spec(d):
    return pl.BlockSpec((CHUNK, d), lambda i: (i, 0))


def _vec_spec(d):
    return pl.BlockSpec((1, d), lambda i: (0, 0))


def _prenorm(h0, w, carry=()):
    tp, d = h0.shape
    carry = list(carry)
    x_in, x_out, x_sems, alias = _carried(carry, 2, 1)
    nsteps = tp // CHUNK

    def kern(h_ref, w_ref, *rest):
        x_in_refs, o_ref = rest[:len(x_in)], rest[len(x_in)]
        x_out_refs, sem_refs = rest[len(x_in) + 1:len(x_in) + 1 + len(x_out)], rest[len(x_in) + 1 + len(x_out):]
        _carried_before(carry, pl.program_id(0), nsteps, x_in_refs, x_out_refs, sem_refs)
        o_ref[...] = _rms_fwd(h_ref[...], w_ref[...]).astype(BF16)
        _carried_after(carry, pl.program_id(0), nsteps, x_in_refs, x_out_refs, sem_refs)

    res = pl.pallas_call(
        kern, name="prenorm1", grid=(nsteps,),
        in_specs=[_row_spec(d), _vec_spec(d)] + [ANY] * len(x_in), out_specs=[_row_spec(d)] + [ANY] * len(x_out),
        out_shape=[jax.ShapeDtypeStruct((tp, d), BF16)] + x_out,
        input_output_aliases=alias, scratch_shapes=x_sems,
        compiler_params=_cp(("arbitrary",) if carry else ("parallel",)),
    )(h0, w, *x_in)
    return res[0], res[1:]


def _mid_fwd(h0, a, w_post, w_pre):
    tp, d = h0.shape

    def kern(h_ref, a_ref, wp_ref, wq_ref, h1_ref, n2_ref):
        h1 = h_ref[...] + _rms_fwd(a_ref[...], wp_ref[...])
        h1_ref[...] = h1
        n2_ref[...] = _rms_fwd(h1, wq_ref[...]).astype(BF16)

    return pl.pallas_call(
        kern, name="mid_fwd", grid=(tp // CHUNK,),
        in_specs=[_row_spec(d), _row_spec(d), _vec_spec(d), _vec_spec(d)],
        out_specs=[_row_spec(d), _row_spec(d)],
        out_shape=[jax.ShapeDtypeStruct((tp, d), F32), jax.ShapeDtypeStruct((tp, d), BF16)],
        compiler_params=_cp(("parallel",)),
    )(h0, a, w_post, w_pre)


def _loss_bwd(h1, f, tgt, w_post):
    tp, d = h1.shape

    def kern(h_ref, f_ref, t_ref, w_ref, dy_ref, df_ref, loss_ref, dw_ref):
        i = pl.program_id(0)

        @pl.when(i == 0)
        def _():
            dy_ref[...] = jnp.zeros_like(dy_ref)
            df_ref[...] = jnp.zeros_like(df_ref)
            loss_ref[...] = jnp.zeros_like(loss_ref)
            dw_ref[...] = jnp.zeros_like(dw_ref)

        @pl.when(i > 0)
        def _():
            fv = f_ref[...]
            w = w_ref[...]
            err = h_ref[...] + _rms_fwd(fv, w) - t_ref[...]
            loss_ref[...] += 0.5 * jnp.sum(jnp.mean(err * err, axis=-1, keepdims=True))
            dy = err * (1.0 / d)
            dy_ref[...] = dy
            dfv, dwr = _rms_bwd(fv, w, dy)
            df_ref[...] = dfv.astype(BF16)
            dw_ref[...] += jnp.sum(dwr, axis=0, keepdims=True)

    return pl.pallas_call(
        kern, name="loss_bwd", grid=(tp // CHUNK,),
        in_specs=[_row_spec(d), _row_spec(d),
                  pl.BlockSpec((CHUNK, d), lambda i: (jnp.maximum(i - 1, 0), 0)), _vec_spec(d)],
        out_specs=[_row_spec(d), _row_spec(d), pl.BlockSpec((1, 128), lambda i: (0, 0)), _vec_spec(d)],
        out_shape=[jax.ShapeDtypeStruct((tp, d), F32), jax.ShapeDtypeStruct((tp, d), BF16),
                   jax.ShapeDtypeStruct((1, 128), F32), jax.ShapeDtypeStruct((1, d), F32)],
        compiler_params=_cp(("arbitrary",)),
    )(h1, f, tgt, w_post)


def _mid_bwd(dy, dn2, h1, a, w_pre, w_post):
    tp, d = h1.shape

    def kern(dy_ref, dn_ref, h_ref, a_ref, wq_ref, wp_ref, dh_ref, da_ref, dwq_ref, dwp_ref):
        @pl.when(pl.program_id(0) == 0)
        def _():
            dwq_ref[...] = jnp.zeros_like(dwq_ref)
            dwp_ref[...] = jnp.zeros_like(dwp_ref)

        dx, dwq = _rms_bwd(h_ref[...], wq_ref[...], dn_ref[...])
        dh = dy_ref[...] + dx
        dh_ref[...] = dh
        da, dwp = _rms_bwd(a_ref[...], wp_ref[...], dh)
        da_ref[...] = da.astype(BF16)
        dwq_ref[...] += jnp.sum(dwq, axis=0, keepdims=True)
        dwp_ref[...] += jnp.sum(dwp, axis=0, keepdims=True)

    return pl.pallas_call(
        kern, name="mid_bwd", grid=(tp // CHUNK,),
        in_specs=[_row_spec(d)] * 4 + [_vec_spec(d)] * 2,
        out_specs=[_row_spec(d), _row_spec(d), _vec_spec(d), _vec_spec(d)],
        out_shape=[jax.ShapeDtypeStruct((tp, d), F32), jax.ShapeDtypeStruct((tp, d), BF16),
                   jax.ShapeDtypeStruct((1, d), F32), jax.ShapeDtypeStruct((1, d), F32)],
        compiler_params=_cp(("arbitrary",)),
    )(dy, dn2, h1, a, w_pre, w_post)


def _pre_bwd(dh1, dn1, h0, w_pre):
    tp, d = h0.shape
    s = tp - CHUNK

    def kern(dh_ref, dn_ref, h_ref, w_ref, gx_ref, gm_ref, dw_ref):
        i = pl.program_id(0)
        dx, dwr = _rms_bwd(h_ref[...], w_ref[...], dn_ref[...])
        dh0 = dh_ref[...] + dx
        gx_ref[...] = dh0

        @pl.when(i == 0)
        def _():
            gm_ref[...] = dh0[N_PAD:, :]
            dw_ref[...] = jnp.zeros_like(dw_ref)

        dw_ref[...] += jnp.sum(dwr, axis=0, keepdims=True)

    return pl.pallas_call(
        kern, name="pre_bwd", grid=(tp // CHUNK,),
        in_specs=[_row_spec(d)] * 3 + [_vec_spec(d)],
        out_specs=[pl.BlockSpec((CHUNK, d), lambda i: (jnp.maximum(i - 1, 0), 0)),
                   pl.BlockSpec((N_META, d), lambda i: (0, 0)), _vec_spec(d)],
        out_shape=[jax.ShapeDtypeStruct((s, d), F32), jax.ShapeDtypeStruct((N_META, d), F32),
                   jax.ShapeDtypeStruct((1, d), F32)],
        compiler_params=_cp(("arbitrary",)),
    )(dh1, dn1, h0, w_pre)


def _ffn_cols(dff):
    return dff // 2 if dff % 256 == 0 else dff


HALO = 16


def _ffn_fwd(gu, conv_w, conv_b):
    tp, two_dff = gu.shape
    dff = two_dff // 2
    tc = _ffn_cols(dff)
    nj = dff // tc
    r8 = CHUNK // HALO

    def kern(g_ref, gp_ref, u_ref, w_ref, b_ref, o_ref):
        i = pl.program_id(1)
        prev = gp_ref[...].astype(F32) * (i > 0).astype(F32)
        ext = jnp.concatenate([prev, g_ref[...].astype(F32)], axis=0)
        w = w_ref[...]
        conv = (b_ref[...] + w[0:1] * pltpu.roll(ext, 2, 0)[HALO:] + w[1:2] * pltpu.roll(ext, 1, 0)[HALO:]
                + w[2:3] * ext[HALO:])
        o_ref[...] = (conv * _sigmoid(conv) * u_ref[...].astype(F32)).astype(BF16)

    return pl.pallas_call(
        kern, name="ffn_fwd", grid=(nj, tp // CHUNK),
        in_specs=[pl.BlockSpec((CHUNK, tc), lambda j, i: (i, j)),
                  pl.BlockSpec((HALO, tc), lambda j, i: (jnp.maximum(i * r8 - 1, 0), j)),
                  pl.BlockSpec((CHUNK, tc), lambda j, i: (i, j + nj)),
                  pl.BlockSpec((CONV_W, tc), lambda j, i: (0, j)),
                  pl.BlockSpec((1, tc), lambda j, i: (0, j))],
        out_specs=pl.BlockSpec((CHUNK, tc), lambda j, i: (i, j)),
        out_shape=jax.ShapeDtypeStruct((tp, dff), BF16),
        compiler_params=_cp(("parallel", "parallel")),
    )(gu, gu, gu, conv_w, conv_b)


def _ffn_bwd(gu, dact, conv_w, conv_b):
    tp, two_dff = gu.shape
    dff = two_dff // 2
    tc = _ffn_cols(dff)
    nj = dff // tc
    ni = tp // CHUNK
    r8 = CHUNK // HALO

    def kern(g_ref, gp_ref, gn_ref, u_ref, un_ref, d_ref, dn_ref, w_ref, b_ref, dg_ref, du_ref, st_ref):
        i = pl.program_id(1)

        @pl.when(i == 0)
        def _():
            st_ref[...] = jnp.zeros_like(st_ref)

        first = (i > 0).astype(F32)
        last = (i < ni - 1).astype(F32)
        gate = g_ref[...].astype(F32)
        d_main = d_ref[...].astype(F32)
        ext = jnp.concatenate([gp_ref[...].astype(F32) * first, gate, gn_ref[...].astype(F32)], axis=0)
        w = w_ref[...]
        r1 = pltpu.roll(ext, 1, 0)
        r2 = pltpu.roll(ext, 2, 0)
        conv = (b_ref[...] + w[0:1] * r2 + w[1:2] * r1 + w[2:3] * ext)[HALO:]
        up = jnp.concatenate([u_ref[...].astype(F32), un_ref[...].astype(F32)], axis=0)
        da = jnp.concatenate([d_main, dn_ref[...].astype(F32) * last], axis=0)
        sg = _sigmoid(conv)
        dc = da * up * (sg * (1.0 + conv * (1.0 - sg)))
        du_ref[...] = (d_main * (conv * sg)[:CHUNK]).astype(BF16)
        n = CHUNK + HALO
        dgate = w[2:3] * dc + w[1:2] * pltpu.roll(dc, n - 1, 0) + w[0:1] * pltpu.roll(dc, n - 2, 0)
        dg_ref[...] = dgate[:CHUNK].astype(BF16)
        dcm = dc[:CHUNK]
        s0 = jnp.sum(dcm * r2[HALO:HALO + CHUNK], axis=0, keepdims=True)
        s1 = jnp.sum(dcm * r1[HALO:HALO + CHUNK], axis=0, keepdims=True)
        s2 = jnp.sum(dcm * gate, axis=0, keepdims=True)
        s3 = jnp.sum(dcm, axis=0, keepdims=True)
        row = lax.broadcasted_iota(jnp.int32, (8, tc), 0)
        st_ref[...] += jnp.where(row == 0, s0, jnp.where(row == 1, s1, jnp.where(row == 2, s2,
                                 jnp.where(row == 3, s3, 0.0))))

    main = lambda off: pl.BlockSpec((CHUNK, tc), lambda j, i: (i, j + off))
    nxt = lambda off: pl.BlockSpec((HALO, tc), lambda j, i: (jnp.minimum((i + 1) * r8, ni * r8 - 1), j + off))
    return pl.pallas_call(
        kern, name="ffn_bwd", grid=(nj, ni),
        in_specs=[main(0), pl.BlockSpec((HALO, tc), lambda j, i: (jnp.maximum(i * r8 - 1, 0), j)), nxt(0),
                  main(nj), nxt(nj), main(0), nxt(0),
                  pl.BlockSpec((CONV_W, tc), lambda j, i: (0, j)), pl.BlockSpec((1, tc), lambda j, i: (0, j))],
        out_specs=[main(0), main(0), pl.BlockSpec((8, tc), lambda j, i: (0, j))],
        out_shape=[jax.ShapeDtypeStruct((tp, dff), BF16), jax.ShapeDtypeStruct((tp, dff), BF16),
                   jax.ShapeDtypeStruct((8, dff), F32)],
        compiler_params=_cp(("parallel", "arbitrary")),
    )(gu, gu, gu, gu, gu, dact, dact, conv_w, conv_b)


def _rot(x, cs, sn):
    x1, x2 = x[:, :128], x[:, 128:]
    return jnp.concatenate([x1 * cs - x2 * sn, x1 * sn + x2 * cs], axis=1)


def _rot_t(x, cs, sn):
    x1, x2 = x[:, :128], x[:, 128:]
    return jnp.concatenate([x1 * cs + x2 * sn, x2 * cs - x1 * sn], axis=1)


def _ret_tables(rh):
    lg = jnp.log(1.0 - 2.0 ** (-5.0 - jnp.arange(rh, dtype=F32)))
    idx = jnp.arange(CHUNK, dtype=F32)
    diff = idx[:, None] - idx[None, :]
    intra = jnp.where(diff[None] >= 0, jnp.exp(jnp.maximum(diff, 0.0)[None] * lg[:, None, None]), 0.0)
    qdec = jnp.exp((idx[None, :] + 1.0) * lg[:, None])[..., None]
    kdec = jnp.exp((CHUNK - 1.0 - idx[None, :]) * lg[:, None])[..., None]
    cdec = jnp.exp(CHUNK * lg)[:, None, None]
    return intra, qdec, kdec, cdec


def _ret_specs(rh, nc, rev):
    hp = max(d for d in (8, 4, 2, 1) if rh % d == 0)
    ng, wd = rh // hp, hp * RET_DK
    cc = (lambda c: nc - 1 - c) if rev else (lambda c: c)
    col = lambda sec: pl.BlockSpec((CHUNK, wd), lambda h, c: (cc(c), sec * ng + h))
    tab = [pl.BlockSpec((CHUNK, 128), lambda h, c: (cc(c), 0))] * 2
    dec = [pl.BlockSpec((hp, CHUNK, CHUNK), lambda h, c: (h, 0, 0)),
           pl.BlockSpec((hp, CHUNK, 1), lambda h, c: (h, 0, 0)),
           pl.BlockSpec((hp, CHUNK, 1), lambda h, c: (h, 0, 0)),
           pl.BlockSpec((hp, 1, 1), lambda h, c: (h, 0, 0))]
    hw = pl.BlockSpec((1, wd), lambda h, c: (0, h))
    hcol = pl.BlockSpec((CHUNK, wd), lambda h, c: (cc(c), h))
    st = pl.BlockSpec((hp, 1, RET_DK, RET_DK), lambda h, c: (h, cc(c), 0, 0))
    heads = [(hh, slice(hh * RET_DK, (hh + 1) * RET_DK)) for hh in range(hp)]
    return heads, ng, col, tab, dec, hw, hcol, st


def _ret_fwd(proj, cos, sin, tables, gnw, rh):
    tp = proj.shape[0]
    nc = tp // CHUNK
    heads, ng, col, tab, dec, hw, hcol, st = _ret_specs(rh, nc, False)

    def kern(q_ref, k_ref, v_ref, g_ref, cos_ref, sin_ref, in_ref, qd_ref, kd_ref, cd_ref, w_ref,
             out_ref, ry_ref, st_ref, state):
        @pl.when(pl.program_id(1) == 0)
        def _():
            state[...] = jnp.zeros_like(state)

        cs, sn = cos_ref[...], sin_ref[...]
        for hh, sl in heads:
            q = (_rot(q_ref[:, sl], cs, sn) * (RET_DK ** -0.5)).astype(BF16)
            kf = _rot(k_ref[:, sl], cs, sn)
            k = kf.astype(BF16)
            v = v_ref[:, sl].astype(BF16)
            s_old = state[hh]
            s_b = s_old.astype(BF16)
            st_ref[hh, 0] = s_b
            sc = _dot(q, k, NT) * in_ref[hh]
            ry = _dot(sc.astype(BF16), v) + _dot(q, s_b) * qd_ref[hh]
            state[hh] = s_old * cd_ref[hh] + _dot((kf * kd_ref[hh]).astype(BF16), v, TN)
            ry_ref[:, sl] = ry
            g = g_ref[:, sl]
            out_ref[:, sl] = (g * _sigmoid(g) * _rms_fwd(ry, w_ref[:, sl])).astype(BF16)

    return pl.pallas_call(
        kern, name="ret_fwd", grid=(ng, nc),
        in_specs=[col(0), col(1), col(2), col(3)] + tab + dec + [hw],
        out_specs=[hcol, hcol, st],
        out_shape=[jax.ShapeDtypeStruct((tp, rh * RET_DK), BF16), jax.ShapeDtypeStruct((tp, rh * RET_DK), F32),
                   jax.ShapeDtypeStruct((rh, nc, RET_DK, RET_DK), BF16)],
        scratch_shapes=[pltpu.VMEM((len(heads), RET_DK, RET_DK), F32)],
        compiler_params=_cp(("parallel", "arbitrary")),
    )(proj, proj, proj, proj, cos, sin, *tables, gnw)


def _ret_bwd(proj, dmix, ry_all, states, cos, sin, tables, gnw, rh):
    tp = proj.shape[0]
    nc = tp // CHUNK
    heads, ng, col, tab, dec, hw, hcol, st = _ret_specs(rh, nc, True)

    def kern(q_ref, k_ref, v_ref, g_ref, cos_ref, sin_ref, in_ref, qd_ref, kd_ref, cd_ref, w_ref,
             do_ref, ry_ref, st_ref, dq_ref, dk_ref, dv_ref, dg_ref, dw_ref, ds):
        @pl.when(pl.program_id(1) == 0)
        def _():
            ds[...] = jnp.zeros_like(ds)
            dw_ref[...] = jnp.zeros_like(dw_ref)

        cs, sn = cos_ref[...], sin_ref[...]
        for hh, sl in heads:
            q = (_rot(q_ref[:, sl], cs, sn) * (RET_DK ** -0.5)).astype(BF16)
            kf = _rot(k_ref[:, sl], cs, sn)
            k = kf.astype(BF16)
            v = v_ref[:, sl].astype(BF16)
            g = g_ref[:, sl]
            ry = ry_ref[:, sl]
            w = w_ref[:, sl]
            dout = do_ref[:, sl]
            sg = _sigmoid(g)
            dhn = dout * (g * sg)
            dry, dwr = _rms_bwd(ry, w, dhn)
            dg_ref[:, sl] = (dout * _rms_fwd(ry, w) * (sg * (1.0 + g * (1.0 - sg)))).astype(BF16)
            dw_ref[:, sl] += jnp.sum(dwr, axis=0, keepdims=True)

            dmat = in_ref[hh]
            qd, kd = qd_ref[hh], kd_ref[hh]
            dyb = dry.astype(BF16)
            p = (_dot(q, k, NT) * dmat).astype(BF16)
            dp = (_dot(dyb, v, NT) * dmat).astype(BF16)
            ady = (dry * qd).astype(BF16)
            ds_old = ds[hh]
            ds_b = ds_old.astype(BF16)
            dq = _dot(dp, k) + _dot(ady, st_ref[hh, 0], NT)
            dk = _dot(dp, q, TN) + _dot(v, ds_b, NT) * kd
            dv = _dot(p, dyb, TN) + _dot((kf * kd).astype(BF16), ds_b)
            ds[hh] = ds_old * cd_ref[hh] + _dot(q, ady, TN)
            dq_ref[:, sl] = _rot_t(dq * (RET_DK ** -0.5), cs, sn).astype(BF16)
            dk_ref[:, sl] = _rot_t(dk, cs, sn).astype(BF16)
            dv_ref[:, sl] = dv.astype(BF16)

    rw = rh * RET_DK
    outs = pl.pallas_call(
        kern, name="ret_bwd", grid=(ng, nc),
        in_specs=[col(0), col(1), col(2), col(3)] + tab + dec + [hw, hcol, hcol, st],
        out_specs=[hcol, hcol, hcol, hcol, hw],
        out_shape=[jax.ShapeDtypeStruct((tp, rw), BF16)] * 4 + [jax.ShapeDtypeStruct((1, rw), F32)],
        scratch_shapes=[pltpu.VMEM((len(heads), RET_DK, RET_DK), F32)],
        compiler_params=_cp(("parallel", "arbitrary")),
    )(proj, proj, proj, proj, cos, sin, *tables, gnw, dmix, ry_all, states)
    return outs


def _sb_tile(tp):
    return 3 * CHUNK if tp % (3 * CHUNK) == 0 else CHUNK


def _sb_heads(sh, col0):
    hp = SB_HEADS_PER_STEP if sh % SB_HEADS_PER_STEP == 0 and col0 % SB_HEADS_PER_STEP == 0 else 1
    heads = [(hh, slice(hh * SB_DH, (hh + 1) * SB_DH)) for hh in range(hp)]
    return heads, sh // hp, hp * SB_DH, col0 // hp


def _sb_block(q, k, qpos, kb, scale, masked):
    z = _dot(q, k, NT) * scale
    t = jnp.log(1.0 + jnp.exp(-jnp.abs(z)))
    lb = jnp.minimum(z, 0.0) - t
    lk = -jnp.maximum(z, 0.0) - t
    if not masked:
        return None, lb, lk
    kpos = kb * CHUNK + lax.broadcasted_iota(jnp.int32, qpos.shape, 1)
    mask = (kpos < qpos) & (kpos >= N_PAD)
    return mask, lb, jnp.where(mask, lk, 0.0)


def _keep(mask, x):
    return x if mask is None else jnp.where(mask, x, 0.0)


def _sb_trips(i, trip):
    lax.fori_loop(0, 1, trip(True), 0)
    lax.fori_loop(1, i, trip(False), 0)
    lax.fori_loop(jnp.maximum(i, 1), i + 1, trip(True), 0)


def _tri_sum(x, tri):
    hi = x.astype(BF16)
    lo = (x - hi.astype(F32)).astype(BF16)
    if tri.shape[0] == CHUNK:
        return _dot(hi, tri) + _dot(lo, tri)
    return _dot(jnp.concatenate([hi, lo], axis=1), tri)


def _tri(strict_upper, copies=1):
    r = lax.broadcasted_iota(jnp.int32, (copies * CHUNK, CHUNK), 0) % CHUNK
    c = lax.broadcasted_iota(jnp.int32, (copies * CHUNK, CHUNK), 1)
    return ((r > c) if strict_upper else (r < c)).astype(BF16)


def _sb_fwd(proj, sbw, sh, col0, carry=()):
    tp = proj.shape[0]
    tq = _sb_tile(tp)
    nsub, nq = tq // CHUNK, tp // tq
    assert tp // CHUNK <= 128
    scale = 1.0 / math.sqrt(SB_DH)
    heads, ng, wd, cb = _sb_heads(sh, col0)
    hq = pl.BlockSpec((tq, wd), lambda h, i: (i, h))

    carry = list(carry)
    x_in, x_out, x_sems, alias = _carried(carry, 4, 3)

    def kern(q_ref, k_ref, v_ref, w_ref, *rest):
        x_in_refs, rest = rest[:len(x_in)], rest[len(x_in):]
        out_ref, sy_ref, ao_ref = rest[:3]
        x_out_refs, (a_run, k16, v16), sem_refs = rest[3:3 + len(x_out)], rest[3 + len(x_out):6 + len(x_out)], rest[6 + len(x_out):]
        h, i = pl.program_id(0), pl.program_id(1)
        _carried_before(carry, h * nq + i, ng * nq, x_in_refs, x_out_refs, sem_refs)

        @pl.when(i == 0)
        def _():
            k16[...] = k_ref[...].astype(BF16)
            v16[...] = v_ref[...].astype(BF16)

        qs = [q_ref[:, sl].astype(BF16) for _, sl in heads]
        upper = _tri(True)
        lane = lax.broadcasted_iota(jnp.int32, (tq, CHUNK), 1)
        qpos = i * tq + lax.broadcasted_iota(jnp.int32, (tq, CHUNK), 0)
        a_run[...] = jnp.zeros_like(a_run)
        sy_ref[...] = jnp.zeros_like(sy_ref)
        ao_ref[...] = jnp.zeros_like(ao_ref)

        def trip(masked):
            def body(jj, carry):
                for hh, sl in heads:
                    a, acc, at = a_run[hh], sy_ref[:, sl], ao_ref[hh]
                    for sub in reversed(range(nsub)):
                        kb = (i - jj) * nsub + sub
                        rows = pl.ds(pl.multiple_of(kb * CHUNK, CHUNK), CHUNK)
                        mask, lb, lk = _sb_block(qs[hh], k16[rows, sl], qpos, kb, scale, masked)
                        wgt = _keep(mask, jnp.exp(lb + a + _tri_sum(lk, upper)))
                        acc = acc + _dot(wgt.astype(BF16), v16[rows, sl])
                        at = jnp.where(lane == kb, a, at)
                        a = a + jnp.sum(lk, axis=1, keepdims=True)
                    a_run[hh], sy_ref[:, sl], ao_ref[hh] = a, acc, at
                return carry
            return body

        _sb_trips(i, trip)
        for _, sl in heads:
            out_ref[:, sl] = _rms_fwd(sy_ref[:, sl], w_ref[:, sl]).astype(BF16)
        _carried_after(carry, h * nq + i, ng * nq, x_in_refs, x_out_refs, sem_refs)

    kv = lambda sec: pl.BlockSpec((tp, wd), lambda h, i: (0, cb + sec * ng + h))
    res = pl.pallas_call(
        kern, name="sb_fwd", grid=(ng, nq),
        in_specs=[pl.BlockSpec((tq, wd), lambda h, i: (i, cb + h)), kv(1), kv(2),
                  pl.BlockSpec((1, wd), lambda h, i: (0, h))] + [ANY] * len(x_in),
        out_specs=[hq, hq, pl.BlockSpec((len(heads), tq, 128), lambda h, i: (h, i, 0))] + [ANY] * len(x_out),
        out_shape=[jax.ShapeDtypeStruct((tp, sh * SB_DH), BF16), jax.ShapeDtypeStruct((tp, sh * SB_DH), F32),
                   jax.ShapeDtypeStruct((sh, tp, 128), F32)] + x_out,
        input_output_aliases=alias,
        scratch_shapes=[pltpu.VMEM((len(heads), tq, CHUNK), F32), pltpu.VMEM((tp, wd), BF16),
                        pltpu.VMEM((tp, wd), BF16)] + x_sems,
        compiler_params=_cp(("arbitrary", "arbitrary") if carry else ("parallel", "arbitrary")),
    )(proj, proj, proj, sbw, *x_in)
    return res[:3], res[3:]


def _sb_bwd(proj, dmix, sy_all, aoff, sbw, sh, col0, dcol0, carry=()):
    tp = proj.shape[0]
    tq = _sb_tile(tp)
    nsub, nq = tq // CHUNK, tp // tq
    scale = 1.0 / math.sqrt(SB_DH)
    hq = pl.BlockSpec((tq, SB_DH), lambda h, i: (i, h))
    carry = list(carry)
    x_in, x_out, x_sems, alias = _carried(carry, 7, 4)

    def kern(q_ref, k_ref, v_ref, w_ref, do_ref, sy_ref, ao_ref, *rest):
        x_in_refs, rest = rest[:len(x_in)], rest[len(x_in):]
        dq_ref, dk_ref, dv_ref, dw_ref = rest[:4]
        x_out_refs, rest = rest[4:4 + len(x_out)], rest[4 + len(x_out):]
        (dk_acc, dv_acc, dq_acc, e_run, k16, v16), sem_refs = rest[:6], rest[6:]
        h, i = pl.program_id(0), pl.program_id(1)
        _carried_before(carry, h * nq + i, sh * nq, x_in_refs, x_out_refs, sem_refs)

        @pl.when(i == 0)
        def _():
            dk_acc[...] = jnp.zeros_like(dk_acc)
            dv_acc[...] = jnp.zeros_like(dv_acc)
            dw_ref[...] = jnp.zeros_like(dw_ref)
            k16[...] = k_ref[...].astype(BF16)
            v16[...] = v_ref[...].astype(BF16)

        qf = q_ref[...]
        q = qf.astype(BF16)
        dsy, dwr = _rms_bwd(sy_ref[...], w_ref[...], do_ref[...])
        dw_ref[...] += jnp.sum(dwr, axis=0, keepdims=True)
        dsy_b = dsy.astype(BF16)
        q_t = qf.T.astype(BF16)
        dsy_t = dsy.T.astype(BF16)
        atile = ao_ref[0]
        upper = _tri(True, 2)
        lower = _tri(False, 2)
        lane = lax.broadcasted_iota(jnp.int32, (tq, CHUNK), 1)
        qpos = i * tq + lax.broadcasted_iota(jnp.int32, (tq, CHUNK), 0)

        e_run[...] = jnp.zeros_like(e_run)
        dq_acc[...] = jnp.zeros_like(dq_acc)

        def trip(masked):
            def body(jj, carry):
                e_prev, dq = e_run[...], dq_acc[...]
                for sub in range(nsub):
                    kb = jj * nsub + sub
                    rows = pl.ds(pl.multiple_of(kb * CHUNK, CHUNK), CHUNK)
                    k, v = k16[rows, :], v16[rows, :]
                    mask, lb, lk = _sb_block(q, k, qpos, kb, scale, masked)
                    a = jnp.sum(jnp.where(lane == kb, atile, 0.0), axis=1, keepdims=True)
                    wgt = _keep(mask, jnp.exp(lb + a + _tri_sum(lk, upper)))
                    e = wgt * _dot(dsy_b, v, NT)
                    dv_acc[kb] += _dot(dsy_t, wgt.astype(BF16))
                    sig = jnp.exp(lb)
                    e_all = e_prev + _tri_sum(e, lower)
                    dz = (_keep(mask, e - sig * (e + e_all)) * scale).astype(BF16)
                    dk_acc[kb] += _dot(q_t, dz)
                    dq = dq + _dot(dz, k)
                    e_prev = e_prev + jnp.sum(e, axis=1, keepdims=True)
                e_run[...], dq_acc[...] = e_prev, dq
                return carry
            return body

        _sb_trips(i, trip)
        dq_ref[...] = dq_acc[...].astype(BF16)

        @pl.when(i == nq - 1)
        def _():
            def untranspose(kb, c):
                rows = pl.ds(pl.multiple_of(kb * CHUNK, CHUNK), CHUNK)
                dk_ref[rows, :] = dk_acc[kb].T.astype(BF16)
                dv_ref[rows, :] = dv_acc[kb].T.astype(BF16)
                return c

            lax.fori_loop(0, tp // CHUNK, untranspose, 0)

        _carried_after(carry, h * nq + i, sh * nq, x_in_refs, x_out_refs, sem_refs)

    kv = lambda sec: pl.BlockSpec((tp, SB_DH), lambda h, i: (0, col0 + sec * sh + h))
    hfull = pl.BlockSpec((tp, SB_DH), lambda h, i: (0, h))
    sw = sh * SB_DH
    acc_t = pltpu.VMEM((tp // CHUNK, SB_DH, CHUNK), F32)
    res = pl.pallas_call(
        kern, name="sb_bwd", grid=(sh, nq),
        in_specs=[pl.BlockSpec((tq, SB_DH), lambda h, i: (i, col0 + h)), kv(1), kv(2),
                  pl.BlockSpec((1, SB_DH), lambda h, i: (0, h)),
                  pl.BlockSpec((tq, SB_DH), lambda h, i: (i, dcol0 + h)), hq,
                  pl.BlockSpec((1, tq, 128), lambda h, i: (h, i, 0))] + [ANY] * len(x_in),
        out_specs=[hq, hfull, hfull, pl.BlockSpec((1, SB_DH), lambda h, i: (0, h))] + [ANY] * len(x_out),
        out_shape=[jax.ShapeDtypeStruct((tp, sw), BF16)] * 3 + [jax.ShapeDtypeStruct((1, sw), F32)] + x_out,
        input_output_aliases=alias,
        scratch_shapes=[acc_t, acc_t, pltpu.VMEM((tq, SB_DH), F32), pltpu.VMEM((tq, CHUNK), F32),
                        pltpu.VMEM((tp, SB_DH), BF16), pltpu.VMEM((tp, SB_DH), BF16)] + x_sems,
        compiler_params=_cp(("arbitrary", "arbitrary") if carry else ("parallel", "arbitrary")),
    )(proj, proj, proj, sbw, dmix, sy_all, aoff, *x_in)
    return res[:4], res[4:]


def _local_step(x, tgt, meta, net, conv_w, conv_b, pre1_w, gn_w, sb_w, post1_w, pre2_w, post2_w):
    s, d = x.shape

    def mm(host, a, b, mode, dtype):
        carry = net.carry(host)
        out = _matmul(a, b, mode, dtype, host, carry)
        if carry:
            out, extra = out
            net.took(host, extra)
        return out

    tp = s + CHUNK
    rh, sh = d // 512, d // 256
    rw = rh * RET_DK
    h0 = jnp.concatenate([jnp.zeros((N_PAD, d), F32), meta, x], axis=0)
    pos = jnp.arange(tp, dtype=F32) - N_PAD
    inv = ROPE_BASE ** (-jnp.arange(128, dtype=F32) / 128)
    ang = pos[:, None] * inv[None, :]
    cos, sin = jnp.cos(ang), jnp.sin(ang)
    tables = _ret_tables(rh)
    sb_col0 = 4 * rw // SB_DH

    n1, extra = _prenorm(h0, pre1_w, net.carry("prenorm1"))
    net.took("prenorm1", extra)
    proj = mm("mm_proj", n1, net.weight("w_in"), "nn", F32)
    ret_out, ry, states = _ret_fwd(proj, cos, sin, tables, gn_w, rh)
    (sb_out, sy, aoff), extra = _sb_fwd(proj, sb_w, sh, sb_col0, net.carry("sb_fwd"))
    net.took("sb_fwd", extra)
    mixed = jnp.concatenate([ret_out, sb_out], axis=1)
    a = mm("mm_out", mixed, net.weight("w_out"), "nn", F32)
    h1, n2 = _mid_fwd(h0, a, post1_w, pre2_w)
    gu = mm("mm_up", n2, net.weight("w_up"), "nn", BF16)
    act = _ffn_fwd(gu, conv_w, conv_b)
    f = mm("mm_down", act, net.weight("w_down"), "nn", F32)

    dy, d_f, loss, dw_post2 = _loss_bwd(h1, f, tgt, post2_w)
    d_act = mm("mm_dact", d_f, net.weight("w_down"), "nt", BF16)
    net.grad("w_down", mm("mm_gdown", act, d_f, "tn", BF16))
    d_gate, d_up, ffn_stats = _ffn_bwd(gu, d_act, conv_w, conv_b)
    d_gu = jnp.concatenate([d_gate, d_up], axis=1)
    d_n2 = mm("mm_dn2", d_gu, net.weight("w_up"), "nt", F32)
    net.grad("w_up", mm("mm_gup", n2, d_gu, "tn", BF16))
    dh1, d_a, dw_pre2, dw_post1 = _mid_bwd(dy, d_n2, h1, a, pre2_w, post1_w)
    d_mix = mm("mm_dmix", d_a, net.weight("w_out"), "nt", F32)
    net.grad("w_out", mm("mm_gout", mixed, d_a, "tn", BF16))
    d_rq, d_rk, d_rv, d_rg, dw_gn = _ret_bwd(proj, d_mix, ry, states, cos, sin, tables, gn_w, rh)
    (d_sq, d_sk, d_sv, dw_sb), extra = _sb_bwd(proj, d_mix, sy, aoff, sb_w, sh, sb_col0, rw // SB_DH,
                                               net.carry("sb_bwd"))
    net.took("sb_bwd", extra)
    d_proj = jnp.concatenate([d_rq, d_rk, d_rv, d_rg, d_sq, d_sk, d_sv], axis=1)
    net.grad("w_in", mm("mm_gin", n1, d_proj, "tn", BF16))
    d_n1 = mm("mm_dn1", d_proj, net.weight("w_in"), "nt", F32)
    grad_x, g_meta, dw_pre1 = _pre_bwd(dh1, d_n1, h0, pre1_w)

    small = dict(loss=loss, meta=g_meta, pre1=dw_pre1, gn=dw_gn, sb=dw_sb, post1=dw_post1, pre2=dw_pre2,
                 conv_w=ffn_stats[0:3], conv_b=ffn_stats[3:4], post2=dw_post2)
    return grad_x, small


def _coords():
    return lax.axis_index("x"), lax.axis_index("y"), lax.axis_index("c")


def _other_chips(x, y):
    return [(1 - x, y), (x, 1 - y), (1 - x, 1 - y)]


ANY = pl.BlockSpec(memory_space=pl.ANY)
VM = pl.BlockSpec(memory_space=pltpu.VMEM)


def _gather4_small(v):
    r = v.shape[0]

    def kern(v_ref, o_ref, send, recv):
        x, y, c = _coords()
        o_ref[2 * x + y] = v_ref[...]
        cps = [pltpu.make_async_remote_copy(v_ref, o_ref.at[2 * x + y], send.at[j], recv.at[j],
                                            device_id=(px, py, c), device_id_type=MESH)
               for j, (px, py) in enumerate(_other_chips(x, y))]
        for cp in cps:
            cp.start()
        for cp in cps:
            cp.wait()

    return pl.pallas_call(
        kern, name="gather_small", in_specs=[VM], out_specs=VM,
        out_shape=jax.ShapeDtypeStruct((4, r, 128), F32),
        scratch_shapes=[pltpu.SemaphoreType.DMA((3,)), pltpu.SemaphoreType.DMA((3,))],
    )(v)


def _allreduce8_small(v):
    r = v.shape[0]
    flips = [(fx, fy, fc) for fx in (0, 1) for fy in (0, 1) for fc in (0, 1)][1:]

    def kern(v_ref, o_ref, buf, send, recv):
        x, y, c = _coords()
        me = 4 * x + 2 * y + c
        buf[me] = v_ref[...]
        cps = [pltpu.make_async_remote_copy(v_ref, buf.at[me], send.at[j], recv.at[j],
                                            device_id=(x ^ fx, y ^ fy, c ^ fc), device_id_type=MESH)
               for j, (fx, fy, fc) in enumerate(flips)]
        for cp in cps:
            cp.start()
        for cp in cps:
            cp.wait()
        tot = buf[0]
        for j in range(1, 8):
            tot = tot + buf[j]
        o_ref[...] = tot

    return pl.pallas_call(
        kern, name="allreduce_small", in_specs=[VM], out_specs=VM,
        out_shape=jax.ShapeDtypeStruct((r, 128), F32),
        scratch_shapes=[pltpu.VMEM((8, r, 128), F32), pltpu.SemaphoreType.DMA((7,)),
                        pltpu.SemaphoreType.DMA((7,))],
    )(v)


def _piece(ref, col_sharded, rows, cols, s, hc):
    half = rows // 2
    if col_sharded:
        return ref.at[pl.ds(pl.multiple_of(hc * half, 16), half), pl.ds(pl.multiple_of(s * cols, 128), cols)]
    return ref.at[pl.ds(pl.multiple_of(s * rows + hc * half, 16), half), :]


def _cast_place(w32, c_id, s_id, col_sharded, name):
    rows, cols = w32.shape
    full_shape = (rows, 4 * cols) if col_sharded else (4 * rows, cols)
    tr = max(d for d in _divisors(rows, 16) if d * cols * 4 <= 4 * 2 ** 20)
    nt = rows // tr

    def kern(c_ref, s_ref, w_ref, o_ref):
        o_ref[...] = w_ref[...].astype(BF16)

    if col_sharded:
        o_spec = pl.BlockSpec((tr, cols), lambda t, c_ref, s_ref: (t, s_ref[0]))
    else:
        o_spec = pl.BlockSpec((tr, cols), lambda t, c_ref, s_ref: (s_ref[0] * nt + t, 0))
    return pl.pallas_call(
        kern, name=name,
        grid_spec=pltpu.PrefetchScalarGridSpec(
            num_scalar_prefetch=2, grid=(nt,),
            in_specs=[pl.BlockSpec((tr, cols), lambda t, c_ref, s_ref: (t, 0))], out_specs=o_spec),
        out_shape=jax.ShapeDtypeStruct(full_shape, BF16),
        compiler_params=_cp(("parallel",)),
    )(c_id, s_id, w32)


def _gather_exchange(full, rows, cols, col_sharded, relay_at=1.0):
    def copies(w_ref, o_ref, send, recv):
        x, y, c = _coords()
        s = 2 * x + y
        sib = (x, y, 1 - c)
        chips = _other_chips(x, y)
        pc = functools.partial(_piece, o_ref, col_sharded, rows, cols)
        mine = _piece(w_ref, col_sharded, rows, cols, s, c)
        first = [pltpu.make_async_remote_copy(mine, pc(s, c), send.at[j], recv.at[j],
                                              device_id=(px, py, c), device_id_type=MESH)
                 for j, (px, py) in enumerate(chips)]
        passed = [pltpu.make_async_remote_copy(pc(2 * px + py, c), pc(2 * px + py, c), send.at[3 + j], recv.at[3 + j],
                                               device_id=sib, device_id_type=MESH)
                  for j, (px, py) in enumerate(chips)]
        from_sib = [pltpu.make_async_remote_copy(pc(2 * px + py, 1 - c), pc(2 * px + py, 1 - c), send.at[3 + j],
                                                 recv.at[3 + j], device_id=sib, device_id_type=MESH)
                    for j, (px, py) in enumerate(chips)]
        return first, passed, from_sib

    def start(ins, outs, send, recv):
        for cp in copies(ins[0], outs[0], send, recv)[0]:
            cp.start()

    def relay(ins, outs, send, recv):
        first, passed, _ = copies(ins[0], outs[0], send, recv)
        for j in range(3):
            first[j].wait_recv()
            passed[j].start()

    def finish(ins, outs, send, recv):
        first, passed, from_sib = copies(ins[0], outs[0], send, recv)
        for cp in from_sib:
            cp.wait_recv()
        for cp in first + passed:
            cp.wait_send()

    return _Exchange([full], [jax.ShapeDtypeStruct(full.shape, BF16)], {0: 0}, 6, start, finish, relay, relay_at)


def _pair_exchange(g, col_sharded, rows, cols):
    half = rows // 2

    def copies(g_ref, o_ref, send, recv):
        x, y, c = _coords()
        return [pltpu.make_async_remote_copy(_piece(g_ref, col_sharded, rows, cols, s, 1 - c), o_ref.at[s],
                                             send.at[s], recv.at[s], device_id=(x, y, 1 - c), device_id_type=MESH)
                for s in range(4)]

    def start(ins, outs, send, recv):
        for cp in copies(ins[0], outs[0], send, recv):
            cp.start()

    def finish(ins, outs, send, recv):
        for cp in copies(ins[0], outs[0], send, recv):
            cp.wait()

    return _Exchange([g], [jax.ShapeDtypeStruct((4, half, cols), BF16)], {}, 4, start, finish)


def _half_tiles(half, cols):
    tr = max(d for d in _divisors(half, 16) if d * cols * 4 <= 4 * 2 ** 20)
    return tr, half // tr


def _half_spec(col_sharded, tr, nt, cols, own):
    which = (lambda s, s_ref: s_ref[0]) if own else (lambda s, s_ref: (s_ref[0] + 1 + s) % 4)
    if col_sharded:
        return pl.BlockSpec((tr, cols), lambda s, t, c_ref, s_ref: (c_ref[0] * nt + t, which(s, s_ref)))
    return pl.BlockSpec((tr, cols), lambda s, t, c_ref, s_ref: ((2 * which(s, s_ref) + c_ref[0]) * nt + t, 0))


def _rs_add(g, r1, c_id, s_id, col_sharded, rows, cols, name):
    half = rows // 2
    tr, nt = _half_tiles(half, cols)

    def kern(c_ref, s_ref, g_ref, r_ref, o_ref):
        o_ref[0] = (g_ref[...].astype(F32) + r_ref[0].astype(F32)).astype(BF16)

    slab = pl.BlockSpec((1, tr, cols), lambda s, t, c_ref, s_ref: ((s_ref[0] + 1 + s) % 4, t, 0))
    return pl.pallas_call(
        kern, name=name,
        grid_spec=pltpu.PrefetchScalarGridSpec(
            num_scalar_prefetch=2, grid=(3, nt),
            in_specs=[_half_spec(col_sharded, tr, nt, cols, False), slab], out_specs=slab),
        out_shape=jax.ShapeDtypeStruct((4, half, cols), BF16),
        compiler_params=_cp(("parallel", "parallel")),
    )(c_id, s_id, g, r1)


def _scatter_exchange(p):
    _, half, cols = p.shape

    def copies(p_ref, o_ref, send, recv):
        x, y, c = _coords()
        return [pltpu.make_async_remote_copy(p_ref.at[2 * px + py], o_ref.at[j], send.at[j], recv.at[j],
                                             device_id=(px, py, c), device_id_type=MESH)
                for j, (px, py) in enumerate(_other_chips(x, y))]

    def start(ins, outs, send, recv):
        for cp in copies(ins[0], outs[0], send, recv):
            cp.start()

    def finish(ins, outs, send, recv):
        for cp in copies(ins[0], outs[0], send, recv):
            cp.wait()

    return _Exchange([p], [jax.ShapeDtypeStruct((3, half, cols), BF16)], {}, 3, start, finish)


def _rs_total(g, r1, r2, c_id, s_id, col_sharded, rows, cols, name):
    half = rows // 2
    tr, nt = _half_tiles(half, cols)

    def kern(c_ref, s_ref, g_ref, r1_ref, r2_ref, o_ref):
        tot = g_ref[...].astype(F32) + r1_ref[0].astype(F32)
        for j in range(3):
            tot = tot + r2_ref[j].astype(F32)
        o_ref[...] = tot

    return pl.pallas_call(
        kern, name=name,
        grid_spec=pltpu.PrefetchScalarGridSpec(
            num_scalar_prefetch=2, grid=(1, nt),
            in_specs=[_half_spec(col_sharded, tr, nt, cols, True),
                      pl.BlockSpec((1, tr, cols), lambda s, t, c_ref, s_ref: (s_ref[0], t, 0)),
                      pl.BlockSpec((3, tr, cols), lambda s, t, c_ref, s_ref: (0, t, 0))],
            out_specs=pl.BlockSpec((tr, cols), lambda s, t, c_ref, s_ref: (c_ref[0] * nt + t, 0))),
        out_shape=jax.ShapeDtypeStruct((rows, cols), F32),
        compiler_params=_cp(("parallel", "parallel")),
    )(c_id, s_id, g, r1, r2)


def _rs_exchange(t, name):
    rows, cols = t.shape
    half = rows // 2

    def kern(t_ref, o_ref, send, recv):
        x, y, c = _coords()
        mine = pl.ds(pl.multiple_of(c * half, 8), half)
        cp = pltpu.make_async_remote_copy(t_ref.at[mine, :], o_ref.at[mine, :], send, recv,
                                          device_id=(x, y, 1 - c), device_id_type=MESH)
        cp.start()
        cp.wait()

    return pl.pallas_call(
        kern, name=name, in_specs=[ANY], out_specs=ANY, input_output_aliases={0: 0},
        out_shape=jax.ShapeDtypeStruct((rows, cols), F32),
        scratch_shapes=[pltpu.SemaphoreType.DMA, pltpu.SemaphoreType.DMA],
    )(t)


class _Whole:
    def __init__(self, full):
        self.full, self.grads = dict(full), {}

    def weight(self, n):
        return self.full[n]

    def carry(self, host):
        return []

    def took(self, host, outs):
        pass

    def grad(self, n, g):
        self.grads[n] = g


BIG = ("w_in", "w_out", "w_up", "w_down")
COL_SHARDED = dict(w_in=True, w_out=False, w_up=True, w_down=False)
GATHER_ON = dict(prenorm1="w_in", mm_proj="w_out", sb_fwd="w_up", mm_up="w_down")
RELAY_AT = dict(prenorm1=1.0, mm_proj=0.6, sb_fwd=0.85, mm_up=0.7)
PAIR_ON = dict(mm_dn2="w_down", mm_dmix="w_up")
SCATTER_ON = dict(mm_gup=("w_down",), sb_bwd=("w_up", "w_out"), mm_dn1=("w_in",))


class _Sharded(_Whole):
    def __init__(self, shards, c_id, s_id):
        self.shards, self.c_id, self.s_id = shards, c_id, s_id
        self.full, self.grads, self.r1, self.p, self.r2 = {}, {}, {}, {}, {}
        self.placed = {n: _cast_place(shards[n], c_id, s_id, COL_SHARDED[n], "place_" + n) for n in BIG}

    def _gather(self, n, relay_at=1.0):
        return _gather_exchange(self.placed[n], *self.shards[n].shape, COL_SHARDED[n], relay_at)

    def _pair(self, n):
        return _pair_exchange(self.grads[n], COL_SHARDED[n], *self.shards[n].shape)

    def _paired(self, n, r1):
        rows, cols = self.shards[n].shape
        self.r1[n] = r1
        self.p[n] = _rs_add(self.grads[n], r1, self.c_id, self.s_id, COL_SHARDED[n], rows, cols, "rs_add_" + n)

    def carry(self, host):
        if host in GATHER_ON:
            return [self._gather(GATHER_ON[host], RELAY_AT[host])]
        if host in PAIR_ON:
            return [self._pair(PAIR_ON[host])]
        return [_scatter_exchange(self.p[n]) for n in SCATTER_ON.get(host, ())]

    def took(self, host, outs):
        if host in GATHER_ON:
            self.full[GATHER_ON[host]], = outs
        if host in PAIR_ON:
            self._paired(PAIR_ON[host], outs[0])
        for n, r2 in zip(SCATTER_ON.get(host, ()), outs):
            self.r2[n] = r2

    def grad(self, n, g):
        self.grads[n] = g
        if n not in PAIR_ON.values():
            self._paired(n, _run_exchange(self._pair(n), "rs_pair_" + n)[0])

    def reduced(self, n):
        rows, cols = self.shards[n].shape
        t = _rs_total(self.grads[n], self.r1[n], self.r2[n], self.c_id, self.s_id, COL_SHARDED[n], rows, cols,
                      "rs_total_" + n)
        return _rs_exchange(t, "rs_exchange_" + n)


def _adamw_vals(w, g, m, v):
    m = ADAM_B1 * m + (1.0 - ADAM_B1) * g
    v = ADAM_B2 * v + (1.0 - ADAM_B2) * (g * g)
    m_hat = m / (1.0 - ADAM_B1 ** ADAM_STEP)
    v_hat = v / (1.0 - ADAM_B2 ** ADAM_STEP)
    delta = -ADAM_LR * (m_hat / (jnp.sqrt(v_hat) + ADAM_EPS) + ADAM_WD * w)
    return delta, m, v


def _adamw(w, g, m, v, name):
    rows, cols = w.shape
    tr = max(d for d in _divisors(rows, 8) if d * cols * 4 <= 2 * 2 ** 20)

    def kern(w_ref, g_ref, m_ref, v_ref, d_ref, mo_ref, vo_ref, go_ref):
        g = g_ref[...]
        d_ref[...], mo_ref[...], vo_ref[...] = _adamw_vals(w_ref[...], g, m_ref[...], v_ref[...])
        go_ref[...] = g

    spec = pl.BlockSpec((tr, cols), lambda i: (i, 0))
    return pl.pallas_call(
        kern, name=name, grid=(rows // tr,), in_specs=[spec] * 4, out_specs=[spec] * 4,
        out_shape=[jax.ShapeDtypeStruct((rows, cols), F32)] * 4,
        compiler_params=_cp(("parallel",)),
    )(w, g, m, v)


def _pack(arrs):
    flat = []
    for a in arrs:
        a = a.reshape(-1)
        flat.append(jnp.pad(a, (0, (-a.shape[0]) % 1024)))
    return jnp.concatenate(flat).reshape(-1, 128)


def _unpack(slab, shapes):
    out, off = [], 0
    flat = slab.reshape(slab.shape[:-2] + (-1,))
    for shp in shapes:
        n = math.prod(shp)
        out.append(flat[..., off:off + n].reshape(slab.shape[:-2] + tuple(shp)))
        off += n + (-n) % 1024
    return out


SMALL = ("meta_tokens", "attn_pre_norm_w", "ret_gn_w", "sb_norm_w", "attn_post_norm_w", "ffn_pre_norm_w",
         "conv_w", "conv_b", "ffn_post_norm_w")
ORDER = ("meta_tokens", "attn_pre_norm_w", "w_in", "ret_gn_w", "sb_norm_w", "w_out", "attn_post_norm_w",
         "ffn_pre_norm_w", "w_up", "conv_w", "conv_b", "w_down", "ffn_post_norm_w")


def kernel(x, meta_tokens, attn_pre_norm_w, w_in, ret_gn_w, sb_norm_w, w_out, attn_post_norm_w, ffn_pre_norm_w, w_up, conv_w, conv_b, w_down, ffn_post_norm_w, loss_target, m_meta_tokens, m_attn_pre_norm_w, m_w_in, m_ret_gn_w, m_sb_norm_w, m_w_out, m_attn_post_norm_w, m_ffn_pre_norm_w, m_w_up, m_conv_w, m_conv_b, m_w_down, m_ffn_post_norm_w, v_meta_tokens, v_attn_pre_norm_w, v_w_in, v_ret_gn_w, v_sb_norm_w, v_w_out, v_attn_post_norm_w, v_ffn_pre_norm_w, v_w_up, v_conv_w, v_conv_b, v_w_down, v_ffn_post_norm_w):
    args = dict(locals())
    w = {n: args[n] for n in ORDER}
    m = {n: args["m_" + n] for n in ORDER}
    v = {n: args["v_" + n] for n in ORDER}
    xi, yi, ci = _coords()
    shard_id = 2 * xi + yi
    c_id = ci.astype(jnp.int32).reshape(1)
    s_id = shard_id.astype(jnp.int32).reshape(1)
    d = x.shape[-1]

    mshape, cshape = w["meta_tokens"].shape, w["conv_w"][0].shape
    got = _unpack(_gather4_small(_pack([w["meta_tokens"], w["conv_w"][0]])), [mshape, cshape])
    meta_full = jnp.moveaxis(got[0], 0, 1).reshape(N_META, d)
    conv_w_full = jnp.moveaxis(got[1], 0, 1).reshape(CONV_W, -1)

    net = _Sharded({n: w[n][0] for n in BIG}, c_id, s_id)
    grad_x, g_small = _local_step(
        x[0], loss_target[0], meta_full, net, conv_w_full, w["conv_b"], w["attn_pre_norm_w"], w["ret_gn_w"],
        w["sb_norm_w"], w["attn_post_norm_w"], w["ffn_pre_norm_w"], w["ffn_post_norm_w"])

    names = ("loss", "meta", "pre1", "gn", "sb", "post1", "pre2", "conv_w", "conv_b", "post2")
    tot = _unpack(_allreduce8_small(_pack([g_small[n] for n in names])), [g_small[n].shape for n in names])
    tot = dict(zip(names, tot))
    loss = tot["loss"][0, 0]
    mcols, ccols = mshape[1], cshape[1]
    grads = {
        "meta_tokens": lax.dynamic_slice_in_dim(tot["meta"], shard_id * mcols, mcols, axis=1),
        "attn_pre_norm_w": tot["pre1"], "ret_gn_w": tot["gn"], "sb_norm_w": tot["sb"],
        "attn_post_norm_w": tot["post1"], "ffn_pre_norm_w": tot["pre2"],
        "conv_w": lax.dynamic_slice_in_dim(tot["conv_w"], shard_id * ccols, ccols, axis=1)[None],
        "conv_b": tot["conv_b"], "ffn_post_norm_w": tot["post2"],
    }

    delta, new_m, new_v = {}, {}, {}
    for n in BIG:
        dl, mo, vo, g = _adamw(w[n][0], net.reduced(n), m[n][0], v[n][0], "adamw_" + n)
        delta[n], new_m[n], new_v[n], grads[n] = dl[None], mo[None], vo[None], g[None]
    shapes = [w[n].shape for n in SMALL]
    packed = [_pack([src[n] for n in SMALL]) for src in (w, grads, m, v)]
    outs = _adamw(*packed, "adamw_small")[:3]
    for dst, slab in zip((delta, new_m, new_v), outs):
        for n, a in zip(SMALL, _unpack(slab, shapes)):
            dst[n] = a

    return (loss, grad_x[None], *[grads[n] for n in ORDER], *[delta[n] for n in ORDER],
            *[new_m[n] for n in ORDER], *[new_v[n] for n in ORDER])
```

```python
import functools
import math

import jax
import jax.numpy as jnp
from jax import lax
from jax.experimental import pallas as pl
from jax.experimental.pallas import tpu as pltpu

F32 = jnp.float32
BF16 = jnp.bfloat16
MESH = pl.DeviceIdType.MESH

EPS = 1e-6
ROPE_BASE = 10000.0
N_META = 16
CHUNK = 128
N_PAD = CHUNK - N_META
RET_DK = 256
SB_DH = 128
SB_HEADS_PER_STEP = 1
CONV_W = 3

ADAM_LR = 0.001
ADAM_B1 = 0.9
ADAM_B2 = 0.999
ADAM_EPS = 1e-08
ADAM_WD = 0.01
ADAM_STEP = 10

V7X_VMEM_BYTES = 64 * 2 ** 20
VMEM_LIMIT = V7X_VMEM_BYTES - 8 * 2 ** 20
MM_BUDGET = 40 * 2 ** 20
V7X_BF16_FLOPS = 0.9e15
V7X_HBM_BPS = 2.0e12
GRID_STEP_S = 0.6e-6
ACC_BPS = 2.0e13

NN = (((1,), (0,)), ((), ()))
NT = (((1,), (1,)), ((), ()))
TN = (((0,), (0,)), ((), ()))


def _cp(sem, vmem=VMEM_LIMIT):
    return pltpu.CompilerParams(dimension_semantics=sem, vmem_limit_bytes=vmem)


def _dot(a, b, dims=NN):
    return lax.dot_general(a, b, dims, preferred_element_type=F32)


def _sigmoid(x):
    return 1.0 / (1.0 + jnp.exp(-x))


def _divisors(n, align):
    return [d for d in range(align, n + 1, align) if n % d == 0]


def _mm_tiles(mode, m, n, k, out_bytes):
    best = None
    tms = [d for d in _divisors(m, 128 if mode == "tn" else 16) if d >= 128]
    tns = [d for d in _divisors(n, 256)] or [d for d in _divisors(n, 128)]
    tks = [d for d in _divisors(k, 128) if d >= 128]
    flops = 2.0 * m * n * k
    for tm in tms:
        for tn in tns:
            if tm * tn > 2112 * 1024:
                continue
            for tk in tks:
                nk = k // tk
                foot = 4 * (tm * tk + tk * tn) + 2 * tm * tn * out_bytes
                if nk > 1:
                    foot += 4 * tm * tn
                if foot > MM_BUDGET:
                    continue
                steps = (m // tm) * (n // tn) * nk
                for swap in (False, True):
                    if nk > 1:
                        traffic = 2.0 * (n // tn) * m * k + 2.0 * (m // tm) * n * k
                    elif swap:
                        traffic = 2.0 * n * k + 2.0 * (n // tn) * m * k
                    else:
                        traffic = 2.0 * m * k + 2.0 * (m // tm) * n * k
                    traffic += out_bytes * m * n
                    t = max(flops / V7X_BF16_FLOPS, traffic / V7X_HBM_BPS) + steps * GRID_STEP_S
                    if nk > 1:
                        t += 12.0 * m * n * nk / ACC_BPS
                    if best is None or t < best[0]:
                        best = (t, tm, tn, tk, swap)
    assert best is not None, (mode, m, n, k)
    return best[1:]


class _Exchange:
    def __init__(self, operands, out_shapes, aliases, nsem, start, finish, relay=None, relay_at=1.0):
        self.operands, self.out_shapes, self.aliases, self.nsem = operands, out_shapes, aliases, nsem
        self.start, self.finish, self.relay, self.relay_at = start, finish, relay, relay_at


def _carried(exchanges, in_base, out_base):
    ins = [a for e in exchanges for a in e.operands]
    outs = [s for e in exchanges for s in e.out_shapes]
    sems = [pltpu.SemaphoreType.DMA((e.nsem,)) for e in exchanges for _ in (0, 1)]
    alias, i, o = {}, in_base, out_base
    for e in exchanges:
        alias.update({i + k: o + v for k, v in e.aliases.items()})
        i, o = i + len(e.operands), o + len(e.out_shapes)
    return ins, outs, sems, alias


def _run_carried(exchanges, phase, in_refs, out_refs, sem_refs, pick=None):
    i = o = 0
    for n, e in enumerate(exchanges):
        ni, no = len(e.operands), len(e.out_shapes)
        if getattr(e, phase) is not None and pick in (None, n):
            getattr(e, phase)(in_refs[i:i + ni], out_refs[o:o + no], sem_refs[2 * n], sem_refs[2 * n + 1])
        i, o = i + ni, o + no


def _carried_before(exchanges, step, nsteps, in_refs, out_refs, sem_refs):
    if not exchanges:
        return

    @pl.when(step == 0)
    def _():
        _run_carried(exchanges, "start", in_refs, out_refs, sem_refs)

    for n, e in enumerate(exchanges):
        if e.relay is not None:
            @pl.when(step == min(nsteps - 1, int(e.relay_at * nsteps)))
            def _(n=n):
                _run_carried(exchanges, "relay", in_refs, out_refs, sem_refs, pick=n)


def _carried_after(exchanges, step, nsteps, in_refs, out_refs, sem_refs):
    if not exchanges:
        return

    @pl.when(step == nsteps - 1)
    def _():
        _run_carried(exchanges, "finish", in_refs, out_refs, sem_refs)


def _run_exchange(e, name):
    ins, outs, sems, alias = _carried([e], 0, 0)

    def kern(*refs):
        in_refs, out_refs, sem_refs = refs[:len(ins)], refs[len(ins):len(ins) + len(outs)], refs[len(ins) + len(outs):]
        for phase in ("start", "relay", "finish"):
            _run_carried([e], phase, in_refs, out_refs, sem_refs)

    return pl.pallas_call(
        kern, name=name, in_specs=[ANY] * len(ins), out_specs=[ANY] * len(outs), out_shape=outs,
        input_output_aliases=alias, scratch_shapes=sems,
    )(*ins)


def _matmul(a, b, mode, out_dtype, name, carry=()):
    if mode == "nn":
        (m, k), (k2, n) = a.shape, b.shape
    elif mode == "nt":
        (m, k), (n, k2) = a.shape, b.shape
    else:
        (k, m), (k2, n) = a.shape, b.shape
    assert k == k2 and a.dtype == BF16 and b.dtype == BF16
    tm, tn, tk, swap = _mm_tiles(mode, m, n, k, jnp.dtype(out_dtype).itemsize)
    nk = k // tk
    dims = {"nn": NN, "nt": NT, "tn": TN}[mode]

    def ij(g0, g1):
        return (g1, g0) if swap else (g0, g1)

    if mode == "tn":
        a_spec = pl.BlockSpec((tk, tm), lambda g0, g1, kk: (kk, ij(g0, g1)[0]))
    else:
        a_spec = pl.BlockSpec((tm, tk), lambda g0, g1, kk: (ij(g0, g1)[0], kk))
    if mode == "nt":
        b_spec = pl.BlockSpec((tn, tk), lambda g0, g1, kk: (ij(g0, g1)[1], kk))
    else:
        b_spec = pl.BlockSpec((tk, tn), lambda g0, g1, kk: (kk, ij(g0, g1)[1]))
    o_spec = pl.BlockSpec((tm, tn), lambda g0, g1, kk: ij(g0, g1))

    grid = (n // tn, m // tm, nk) if swap else (m // tm, n // tn, nk)
    carry = list(carry)
    x_in, x_out, x_sems, alias = _carried(carry, 2, 1)
    acc_shapes = [pltpu.VMEM((tm, tn), F32)] if nk > 1 else []

    def kern(a_ref, b_ref, *rest):
        x_in_refs, o_ref = rest[:len(x_in)], rest[len(x_in)]
        x_out_refs = rest[len(x_in) + 1:len(x_in) + 1 + len(x_out)]
        tail = rest[len(x_in) + 1 + len(x_out):]
        acc, sem_refs = tail[:len(acc_shapes)], tail[len(acc_shapes):]
        pid = [pl.program_id(ax) for ax in range(3)]
        step, nsteps = (pid[0] * grid[1] + pid[1]) * grid[2] + pid[2], grid[0] * grid[1] * grid[2]
        _carried_before(carry, step, nsteps, x_in_refs, x_out_refs, sem_refs)
        prod = _dot(a_ref[...], b_ref[...], dims)
        if nk == 1:
            o_ref[...] = prod.astype(out_dtype)
        else:
            kk = pid[2]

            @pl.when(kk == 0)
            def _():
                acc[0][...] = prod

            @pl.when(kk > 0)
            def _():
                acc[0][...] += prod

            @pl.when(kk == nk - 1)
            def _():
                o_ref[...] = acc[0][...].astype(out_dtype)

        _carried_after(carry, step, nsteps, x_in_refs, x_out_refs, sem_refs)

    sem = ("arbitrary",) * 3 if carry else ("parallel", "parallel", "arbitrary")
    res = pl.pallas_call(
        kern, name=name, grid=grid, in_specs=[a_spec, b_spec] + [ANY] * len(x_in),
        out_specs=[o_spec] + [ANY] * len(x_out),
        out_shape=[jax.ShapeDtypeStruct((m, n), out_dtype)] + x_out,
        input_output_aliases=alias, scratch_shapes=acc_shapes + x_sems,
        compiler_params=_cp(sem),
    )(a, b, *x_in)
    return (res[0], res[1:]) if carry else res[0]


def _rms_fwd(x, w):
    r = lax.rsqrt(jnp.mean(x * x, axis=-1, keepdims=True) + EPS)
    return x * r * w


def _rms_bwd(x, w, g):
    r = lax.rsqrt(jnp.mean(x * x, axis=-1, keepdims=True) + EPS)
    gw = g * w
    dx = r * gw - x * (r * r * r * jnp.mean(gw * x, axis=-1, keepdims=True))
    return dx, g * (x * r)


def _row_spec(d):
    return pl.BlockSpec((CHUNK, d), lambda i: (i, 0))


def _vec_spec(d):
    return pl.BlockSpec((1, d), lambda i: (0, 0))


def _prenorm(x, meta, w, carry=()):
    s, d = x.shape
    tp = s + CHUNK
    carry = list(carry)
    x_in, x_out, x_sems, alias = _carried(carry, 3, 2)
    nsteps = tp // CHUNK

    def kern(x_ref, m_ref, w_ref, *rest):
        x_in_refs, (h_ref, o_ref) = rest[:len(x_in)], rest[len(x_in):len(x_in) + 2]
        x_out_refs, sem_refs = rest[len(x_in) + 2:len(x_in) + 2 + len(x_out)], rest[len(x_in) + 2 + len(x_out):]
        i = pl.program_id(0)
        _carried_before(carry, i, nsteps, x_in_refs, x_out_refs, sem_refs)

        def emit(h):
            h_ref[...] = h
            o_ref[...] = _rms_fwd(h, w_ref[...]).astype(BF16)

        @pl.when(i == 0)
        def _():
            emit(jnp.concatenate([jnp.zeros((N_PAD, d), F32), m_ref[...]], axis=0))

        @pl.when(i > 0)
        def _():
            emit(x_ref[...])

        _carried_after(carry, i, nsteps, x_in_refs, x_out_refs, sem_refs)

    res = pl.pallas_call(
        kern, name="prenorm1", grid=(nsteps,),
        in_specs=[pl.BlockSpec((CHUNK, d), lambda i: (jnp.maximum(i - 1, 0), 0)),
                  pl.BlockSpec((N_META, d), lambda i: (0, 0)), _vec_spec(d)] + [ANY] * len(x_in),
        out_specs=[_row_spec(d), _row_spec(d)] + [ANY] * len(x_out),
        out_shape=[jax.ShapeDtypeStruct((tp, d), F32), jax.ShapeDtypeStruct((tp, d), BF16)] + x_out,
        input_output_aliases=alias, scratch_shapes=x_sems,
        compiler_params=_cp(("arbitrary",) if carry else ("parallel",)),
    )(x, meta, w, *x_in)
    return res[0], res[1], res[2:]


def _mid_fwd(h0, a, w_post, w_pre):
    tp, d = h0.shape

    def kern(h_ref, a_ref, wp_ref, wq_ref, h1_ref, n2_ref):
        h1 = h_ref[...] + _rms_fwd(a_ref[...], wp_ref[...])
        h1_ref[...] = h1
        n2_ref[...] = _rms_fwd(h1, wq_ref[...]).astype(BF16)

    return pl.pallas_call(
        kern, name="mid_fwd", grid=(tp // CHUNK,),
        in_specs=[_row_spec(d), _row_spec(d), _vec_spec(d), _vec_spec(d)],
        out_specs=[_row_spec(d), _row_spec(d)],
        out_shape=[jax.ShapeDtypeStruct((tp, d), F32), jax.ShapeDtypeStruct((tp, d), BF16)],
        compiler_params=_cp(("parallel",)),
    )(h0, a, w_post, w_pre)


def _loss_bwd(h1, f, tgt, w_post):
    tp, d = h1.shape

    def kern(h_ref, f_ref, t_ref, w_ref, dy_ref, df_ref, loss_ref, dw_ref):
        i = pl.program_id(0)

        @pl.when(i == 0)
        def _():
            dy_ref[...] = jnp.zeros_like(dy_ref)
            df_ref[...] = jnp.zeros_like(df_ref)
            loss_ref[...] = jnp.zeros_like(loss_ref)
            dw_ref[...] = jnp.zeros_like(dw_ref)

        @pl.when(i > 0)
        def _():
            fv = f_ref[...]
            w = w_ref[...]
            err = h_ref[...] + _rms_fwd(fv, w) - t_ref[...]
            loss_ref[...] += 0.5 * jnp.sum(jnp.mean(err * err, axis=-1, keepdims=True))
            dy = err * (1.0 / d)
            dy_ref[...] = dy
            dfv, dwr = _rms_bwd(fv, w, dy)
            df_ref[...] = dfv.astype(BF16)
            dw_ref[...] += jnp.sum(dwr, axis=0, keepdims=True)

    return pl.pallas_call(
        kern, name="loss_bwd", grid=(tp // CHUNK,),
        in_specs=[_row_spec(d), _row_spec(d),
                  pl.BlockSpec((CHUNK, d), lambda i: (jnp.maximum(i - 1, 0), 0)), _vec_spec(d)],
        out_specs=[_row_spec(d), _row_spec(d), pl.BlockSpec((1, 128), lambda i: (0, 0)), _vec_spec(d)],
        out_shape=[jax.ShapeDtypeStruct((tp, d), F32), jax.ShapeDtypeStruct((tp, d), BF16),
                   jax.ShapeDtypeStruct((1, 128), F32), jax.ShapeDtypeStruct((1, d), F32)],
        compiler_params=_cp(("arbitrary",)),
    )(h1, f, tgt, w_post)


def _mid_bwd(dy, dn2, h1, a, w_pre, w_post):
    tp, d = h1.shape

    def kern(dy_ref, dn_ref, h_ref, a_ref, wq_ref, wp_ref, dh_ref, da_ref, dwq_ref, dwp_ref):
        @pl.when(pl.program_id(0) == 0)
        def _():
            dwq_ref[...] = jnp.zeros_like(dwq_ref)
            dwp_ref[...] = jnp.zeros_like(dwp_ref)

        dx, dwq = _rms_bwd(h_ref[...], wq_ref[...], dn_ref[...])
        dh = dy_ref[...] + dx
        dh_ref[...] = dh
        da, dwp = _rms_bwd(a_ref[...], wp_ref[...], dh)
        da_ref[...] = da.astype(BF16)
        dwq_ref[...] += jnp.sum(dwq, axis=0, keepdims=True)
        dwp_ref[...] += jnp.sum(dwp, axis=0, keepdims=True)

    return pl.pallas_call(
        kern, name="mid_bwd", grid=(tp // CHUNK,),
        in_specs=[_row_spec(d)] * 4 + [_vec_spec(d)] * 2,
        out_specs=[_row_spec(d), _row_spec(d), _vec_spec(d), _vec_spec(d)],
        out_shape=[jax.ShapeDtypeStruct((tp, d), F32), jax.ShapeDtypeStruct((tp, d), BF16),
                   jax.ShapeDtypeStruct((1, d), F32), jax.ShapeDtypeStruct((1, d), F32)],
        compiler_params=_cp(("arbitrary",)),
    )(dy, dn2, h1, a, w_pre, w_post)


def _pre_bwd(dh1, dn1, h0, w_pre):
    tp, d = h0.shape
    s = tp - CHUNK

    def kern(dh_ref, dn_ref, h_ref, w_ref, gx_ref, gm_ref, dw_ref):
        i = pl.program_id(0)
        dx, dwr = _rms_bwd(h_ref[...], w_ref[...], dn_ref[...])
        dh0 = dh_ref[...] + dx
        gx_ref[...] = dh0

        @pl.when(i == 0)
        def _():
            gm_ref[...] = dh0[N_PAD:, :]
            dw_ref[...] = jnp.zeros_like(dw_ref)

        dw_ref[...] += jnp.sum(dwr, axis=0, keepdims=True)

    return pl.pallas_call(
        kern, name="pre_bwd", grid=(tp // CHUNK,),
        in_specs=[_row_spec(d)] * 3 + [_vec_spec(d)],
        out_specs=[pl.BlockSpec((CHUNK, d), lambda i: (jnp.maximum(i - 1, 0), 0)),
                   pl.BlockSpec((N_META, d), lambda i: (0, 0)), _vec_spec(d)],
        out_shape=[jax.ShapeDtypeStruct((s, d), F32), jax.ShapeDtypeStruct((N_META, d), F32),
                   jax.ShapeDtypeStruct((1, d), F32)],
        compiler_params=_cp(("arbitrary",)),
    )(dh1, dn1, h0, w_pre)


def _ffn_cols(dff):
    return dff // 2 if dff % 256 == 0 else dff


HALO = 16


def _ffn_fwd(gu, conv_w, conv_b):
    tp, two_dff = gu.shape
    dff = two_dff // 2
    tc = _ffn_cols(dff)
    nj = dff // tc
    r8 = CHUNK // HALO

    def kern(g_ref, gp_ref, u_ref, w_ref, b_ref, o_ref):
        i = pl.program_id(1)
        prev = gp_ref[...].astype(F32) * (i > 0).astype(F32)
        ext = jnp.concatenate([prev, g_ref[...].astype(F32)], axis=0)
        w = w_ref[...]
        conv = (b_ref[...] + w[0:1] * pltpu.roll(ext, 2, 0)[HALO:] + w[1:2] * pltpu.roll(ext, 1, 0)[HALO:]
                + w[2:3] * ext[HALO:])
        o_ref[...] = (conv * _sigmoid(conv) * u_ref[...].astype(F32)).astype(BF16)

    return pl.pallas_call(
        kern, name="ffn_fwd", grid=(nj, tp // CHUNK),
        in_specs=[pl.BlockSpec((CHUNK, tc), lambda j, i: (i, j)),
                  pl.BlockSpec((HALO, tc), lambda j, i: (jnp.maximum(i * r8 - 1, 0), j)),
                  pl.BlockSpec((CHUNK, tc), lambda j, i: (i, j + nj)),
                  pl.BlockSpec((CONV_W, tc), lambda j, i: (0, j)),
                  pl.BlockSpec((1, tc), lambda j, i: (0, j))],
        out_specs=pl.BlockSpec((CHUNK, tc), lambda j, i: (i, j)),
        out_shape=jax.ShapeDtypeStruct((tp, dff), BF16),
        compiler_params=_cp(("parallel", "parallel")),
    )(gu, gu, gu, conv_w, conv_b)


def _ffn_bwd(gu, dact, conv_w, conv_b):
    tp, two_dff = gu.shape
    dff = two_dff // 2
    tc = _ffn_cols(dff)
    nj = dff // tc
    ni = tp // CHUNK
    r8 = CHUNK // HALO

    def kern(g_ref, gp_ref, gn_ref, u_ref, un_ref, d_ref, dn_ref, w_ref, b_ref, dg_ref, du_ref, st_ref):
        i = pl.program_id(1)

        @pl.when(i == 0)
        def _():
            st_ref[...] = jnp.zeros_like(st_ref)

        first = (i > 0).astype(F32)
        last = (i < ni - 1).astype(F32)
        gate = g_ref[...].astype(F32)
        d_main = d_ref[...].astype(F32)
        ext = jnp.concatenate([gp_ref[...].astype(F32) * first, gate, gn_ref[...].astype(F32)], axis=0)
        w = w_ref[...]
        r1 = pltpu.roll(ext, 1, 0)
        r2 = pltpu.roll(ext, 2, 0)
        conv = (b_ref[...] + w[0:1] * r2 + w[1:2] * r1 + w[2:3] * ext)[HALO:]
        up = jnp.concatenate([u_ref[...].astype(F32), un_ref[...].astype(F32)], axis=0)
        da = jnp.concatenate([d_main, dn_ref[...].astype(F32) * last], axis=0)
        sg = _sigmoid(conv)
        dc = da * up * (sg * (1.0 + conv * (1.0 - sg)))
        du_ref[...] = (d_main * (conv * sg)[:CHUNK]).astype(BF16)
        n = CHUNK + HALO
        dgate = w[2:3] * dc + w[1:2] * pltpu.roll(dc, n - 1, 0) + w[0:1] * pltpu.roll(dc, n - 2, 0)
        dg_ref[...] = dgate[:CHUNK].astype(BF16)
        dcm = dc[:CHUNK]
        s0 = jnp.sum(dcm * r2[HALO:HALO + CHUNK], axis=0, keepdims=True)
        s1 = jnp.sum(dcm * r1[HALO:HALO + CHUNK], axis=0, keepdims=True)
        s2 = jnp.sum(dcm * gate, axis=0, keepdims=True)
        s3 = jnp.sum(dcm, axis=0, keepdims=True)
        row = lax.broadcasted_iota(jnp.int32, (8, tc), 0)
        st_ref[...] += jnp.where(row == 0, s0, jnp.where(row == 1, s1, jnp.where(row == 2, s2,
                                 jnp.where(row == 3, s3, 0.0))))

    main = lambda off: pl.BlockSpec((CHUNK, tc), lambda j, i: (i, j + off))
    nxt = lambda off: pl.BlockSpec((HALO, tc), lambda j, i: (jnp.minimum((i + 1) * r8, ni * r8 - 1), j + off))
    return pl.pallas_call(
        kern, name="ffn_bwd", grid=(nj, ni),
        in_specs=[main(0), pl.BlockSpec((HALO, tc), lambda j, i: (jnp.maximum(i * r8 - 1, 0), j)), nxt(0),
                  main(nj), nxt(nj), main(0), nxt(0),
                  pl.BlockSpec((CONV_W, tc), lambda j, i: (0, j)), pl.BlockSpec((1, tc), lambda j, i: (0, j))],
        out_specs=[main(0), main(0), pl.BlockSpec((8, tc), lambda j, i: (0, j))],
        out_shape=[jax.ShapeDtypeStruct((tp, dff), BF16), jax.ShapeDtypeStruct((tp, dff), BF16),
                   jax.ShapeDtypeStruct((8, dff), F32)],
        compiler_params=_cp(("parallel", "arbitrary")),
    )(gu, gu, gu, gu, gu, dact, dact, conv_w, conv_b)


def _rot(x, cs, sn):
    x1, x2 = x[:, :128], x[:, 128:]
    return jnp.concatenate([x1 * cs - x2 * sn, x1 * sn + x2 * cs], axis=1)


def _rot_t(x, cs, sn):
    x1, x2 = x[:, :128], x[:, 128:]
    return jnp.concatenate([x1 * cs + x2 * sn, x2 * cs - x1 * sn], axis=1)


def _ret_tables(rh):
    lg = jnp.log(1.0 - 2.0 ** (-5.0 - jnp.arange(rh, dtype=F32)))
    idx = jnp.arange(CHUNK, dtype=F32)
    diff = idx[:, None] - idx[None, :]
    intra = jnp.where(diff[None] >= 0, jnp.exp(jnp.maximum(diff, 0.0)[None] * lg[:, None, None]), 0.0)
    qdec = jnp.exp((idx[None, :] + 1.0) * lg[:, None])[..., None]
    kdec = jnp.exp((CHUNK - 1.0 - idx[None, :]) * lg[:, None])[..., None]
    cdec = jnp.exp(CHUNK * lg)[:, None, None]
    return intra, qdec, kdec, cdec


def _ret_specs(rh, nc, rev):
    hp = max(d for d in (8, 4, 2, 1) if rh % d == 0)
    ng, wd = rh // hp, hp * RET_DK
    cc = (lambda c: nc - 1 - c) if rev else (lambda c: c)
    col = lambda sec: pl.BlockSpec((CHUNK, wd), lambda h, c: (cc(c), sec * ng + h))
    tab = [pl.BlockSpec((CHUNK, 128), lambda h, c: (cc(c), 0))] * 2
    dec = [pl.BlockSpec((hp, CHUNK, CHUNK), lambda h, c: (h, 0, 0)),
           pl.BlockSpec((hp, CHUNK, 1), lambda h, c: (h, 0, 0)),
           pl.BlockSpec((hp, CHUNK, 1), lambda h, c: (h, 0, 0)),
           pl.BlockSpec((hp, 1, 1), lambda h, c: (h, 0, 0))]
    hw = pl.BlockSpec((1, wd), lambda h, c: (0, h))
    hcol = pl.BlockSpec((CHUNK, wd), lambda h, c: (cc(c), h))
    st = pl.BlockSpec((hp, 1, RET_DK, RET_DK), lambda h, c: (h, cc(c), 0, 0))
    heads = [(hh, slice(hh * RET_DK, (hh + 1) * RET_DK)) for hh in range(hp)]
    return heads, ng, col, tab, dec, hw, hcol, st


def _ret_fwd(proj, cos, sin, tables, gnw, rh):
    tp = proj.shape[0]
    nc = tp // CHUNK
    heads, ng, col, tab, dec, hw, hcol, st = _ret_specs(rh, nc, False)

    def kern(q_ref, k_ref, v_ref, g_ref, cos_ref, sin_ref, in_ref, qd_ref, kd_ref, cd_ref, w_ref,
             out_ref, ry_ref, st_ref, state):
        @pl.when(pl.program_id(1) == 0)
        def _():
            state[...] = jnp.zeros_like(state)

        cs, sn = cos_ref[...], sin_ref[...]
        for hh, sl in heads:
            q = (_rot(q_ref[:, sl], cs, sn) * (RET_DK ** -0.5)).astype(BF16)
            kf = _rot(k_ref[:, sl], cs, sn)
            k = kf.astype(BF16)
            v = v_ref[:, sl].astype(BF16)
            s_old = state[hh]
            s_b = s_old.astype(BF16)
            st_ref[hh, 0] = s_b
            sc = _dot(q, k, NT) * in_ref[hh]
            ry = _dot(sc.astype(BF16), v) + _dot(q, s_b) * qd_ref[hh]
            state[hh] = s_old * cd_ref[hh] + _dot((kf * kd_ref[hh]).astype(BF16), v, TN)
            ry_ref[:, sl] = ry
            g = g_ref[:, sl]
            out_ref[:, sl] = (g * _sigmoid(g) * _rms_fwd(ry, w_ref[:, sl])).astype(BF16)

    return pl.pallas_call(
        kern, name="ret_fwd", grid=(ng, nc),
        in_specs=[col(0), col(1), col(2), col(3)] + tab + dec + [hw],
        out_specs=[hcol, hcol, st],
        out_shape=[jax.ShapeDtypeStruct((tp, rh * RET_DK), BF16), jax.ShapeDtypeStruct((tp, rh * RET_DK), F32),
                   jax.ShapeDtypeStruct((rh, nc, RET_DK, RET_DK), BF16)],
        scratch_shapes=[pltpu.VMEM((len(heads), RET_DK, RET_DK), F32)],
        compiler_params=_cp(("parallel", "arbitrary")),
    )(proj, proj, proj, proj, cos, sin, *tables, gnw)


def _ret_bwd(proj, dmix, ry_all, states, cos, sin, tables, gnw, rh):
    tp = proj.shape[0]
    nc = tp // CHUNK
    heads, ng, col, tab, dec, hw, hcol, st = _ret_specs(rh, nc, True)

    def kern(q_ref, k_ref, v_ref, g_ref, cos_ref, sin_ref, in_ref, qd_ref, kd_ref, cd_ref, w_ref,
             do_ref, ry_ref, st_ref, dq_ref, dk_ref, dv_ref, dg_ref, dw_ref, ds):
        @pl.when(pl.program_id(1) == 0)
        def _():
            ds[...] = jnp.zeros_like(ds)
            dw_ref[...] = jnp.zeros_like(dw_ref)

        cs, sn = cos_ref[...], sin_ref[...]
        for hh, sl in heads:
            q = (_rot(q_ref[:, sl], cs, sn) * (RET_DK ** -0.5)).astype(BF16)
            kf = _rot(k_ref[:, sl], cs, sn)
            k = kf.astype(BF16)
            v = v_ref[:, sl].astype(BF16)
            g = g_ref[:, sl]
            ry = ry_ref[:, sl]
            w = w_ref[:, sl]
            dout = do_ref[:, sl]
            sg = _sigmoid(g)
            dhn = dout * (g * sg)
            dry, dwr = _rms_bwd(ry, w, dhn)
            dg_ref[:, sl] = (dout * _rms_fwd(ry, w) * (sg * (1.0 + g * (1.0 - sg)))).astype(BF16)
            dw_ref[:, sl] += jnp.sum(dwr, axis=0, keepdims=True)

            dmat = in_ref[hh]
            qd, kd = qd_ref[hh], kd_ref[hh]
            dyb = dry.astype(BF16)
            p = (_dot(q, k, NT) * dmat).astype(BF16)
            dp = (_dot(dyb, v, NT) * dmat).astype(BF16)
            ady = (dry * qd).astype(BF16)
            ds_old = ds[hh]
            ds_b = ds_old.astype(BF16)
            dq = _dot(dp, k) + _dot(ady, st_ref[hh, 0], NT)
            dk = _dot(dp, q, TN) + _dot(v, ds_b, NT) * kd
            dv = _dot(p, dyb, TN) + _dot((kf * kd).astype(BF16), ds_b)
            ds[hh] = ds_old * cd_ref[hh] + _dot(q, ady, TN)
            dq_ref[:, sl] = _rot_t(dq * (RET_DK ** -0.5), cs, sn).astype(BF16)
            dk_ref[:, sl] = _rot_t(dk, cs, sn).astype(BF16)
            dv_ref[:, sl] = dv.astype(BF16)

    rw = rh * RET_DK
    outs = pl.pallas_call(
        kern, name="ret_bwd", grid=(ng, nc),
        in_specs=[col(0), col(1), col(2), col(3)] + tab + dec + [hw, hcol, hcol, st],
        out_specs=[hcol, hcol, hcol, hcol, hw],
        out_shape=[jax.ShapeDtypeStruct((tp, rw), BF16)] * 4 + [jax.ShapeDtypeStruct((1, rw), F32)],
        scratch_shapes=[pltpu.VMEM((len(heads), RET_DK, RET_DK), F32)],
        compiler_params=_cp(("parallel", "arbitrary")),
    )(proj, proj, proj, proj, cos, sin, *tables, gnw, dmix, ry_all, states)
    return outs


def _sb_tile(tp):
    return 3 * CHUNK if tp % (3 * CHUNK) == 0 else CHUNK


def _sb_heads(sh, col0):
    hp = SB_HEADS_PER_STEP if sh % SB_HEADS_PER_STEP == 0 and col0 % SB_HEADS_PER_STEP == 0 else 1
    heads = [(hh, slice(hh * SB_DH, (hh + 1) * SB_DH)) for hh in range(hp)]
    return heads, sh // hp, hp * SB_DH, col0 // hp


def _sb_block(q, k, qpos, kb, scale, masked):
    z = _dot(q, k, NT) * scale
    t = jnp.log(1.0 + jnp.exp(-jnp.abs(z)))
    lb = jnp.minimum(z, 0.0) - t
    lk = -jnp.maximum(z, 0.0) - t
    if not masked:
        return None, lb, lk
    kpos = kb * CHUNK + lax.broadcasted_iota(jnp.int32, qpos.shape, 1)
    mask = (kpos < qpos) & (kpos >= N_PAD)
    return mask, lb, jnp.where(mask, lk, 0.0)


def _keep(mask, x):
    return x if mask is None else jnp.where(mask, x, 0.0)


def _sb_trips(i, trip):
    lax.fori_loop(0, 1, trip(True), 0)
    lax.fori_loop(1, i, trip(False), 0)
    lax.fori_loop(jnp.maximum(i, 1), i + 1, trip(True), 0)


def _tri_sum(x, tri):
    hi = x.astype(BF16)
    lo = (x - hi.astype(F32)).astype(BF16)
    if tri.shape[0] == CHUNK:
        return _dot(hi, tri) + _dot(lo, tri)
    return _dot(jnp.concatenate([hi, lo], axis=1), tri)


def _tri(strict_upper, copies=1):
    r = lax.broadcasted_iota(jnp.int32, (copies * CHUNK, CHUNK), 0) % CHUNK
    c = lax.broadcasted_iota(jnp.int32, (copies * CHUNK, CHUNK), 1)
    return ((r > c) if strict_upper else (r < c)).astype(BF16)


def _sb_fwd(proj, sbw, sh, col0, carry=()):
    tp = proj.shape[0]
    tq = _sb_tile(tp)
    nsub, nq = tq // CHUNK, tp // tq
    assert tp // CHUNK <= 128
    scale = 1.0 / math.sqrt(SB_DH)
    heads, ng, wd, cb = _sb_heads(sh, col0)
    hq = pl.BlockSpec((tq, wd), lambda h, i: (i, h))

    carry = list(carry)
    x_in, x_out, x_sems, alias = _carried(carry, 4, 3)

    def kern(q_ref, k_ref, v_ref, w_ref, *rest):
        x_in_refs, rest = rest[:len(x_in)], rest[len(x_in):]
        out_ref, sy_ref, ao_ref = rest[:3]
        x_out_refs, (a_run, k16, v16), sem_refs = rest[3:3 + len(x_out)], rest[3 + len(x_out):6 + len(x_out)], rest[6 + len(x_out):]
        h, i = pl.program_id(0), pl.program_id(1)
        _carried_before(carry, h * nq + i, ng * nq, x_in_refs, x_out_refs, sem_refs)

        @pl.when(i == 0)
        def _():
            k16[...] = k_ref[...].astype(BF16)
            v16[...] = v_ref[...].astype(BF16)

        qs = [q_ref[:, sl].astype(BF16) for _, sl in heads]
        upper = _tri(True)
        lane = lax.broadcasted_iota(jnp.int32, (tq, CHUNK), 1)
        qpos = i * tq + lax.broadcasted_iota(jnp.int32, (tq, CHUNK), 0)
        a_run[...] = jnp.zeros_like(a_run)
        sy_ref[...] = jnp.zeros_like(sy_ref)
        ao_ref[...] = jnp.zeros_like(ao_ref)

        def trip(masked):
            def body(jj, carry):
                for hh, sl in heads:
                    a, acc, at = a_run[hh], sy_ref[:, sl], ao_ref[hh]
                    for sub in reversed(range(nsub)):
                        kb = (i - jj) * nsub + sub
                        rows = pl.ds(pl.multiple_of(kb * CHUNK, CHUNK), CHUNK)
                        mask, lb, lk = _sb_block(qs[hh], k16[rows, sl], qpos, kb, scale, masked)
                        wgt = _keep(mask, jnp.exp(lb + a + _tri_sum(lk, upper)))
                        acc = acc + _dot(wgt.astype(BF16), v16[rows, sl])
                        at = jnp.where(lane == kb, a, at)
                        a = a + jnp.sum(lk, axis=1, keepdims=True)
                    a_run[hh], sy_ref[:, sl], ao_ref[hh] = a, acc, at
                return carry
            return body

        _sb_trips(i, trip)
        for _, sl in heads:
            out_ref[:, sl] = _rms_fwd(sy_ref[:, sl], w_ref[:, sl]).astype(BF16)
        _carried_after(carry, h * nq + i, ng * nq, x_in_refs, x_out_refs, sem_refs)

    kv = lambda sec: pl.BlockSpec((tp, wd), lambda h, i: (0, cb + sec * ng + h))
    res = pl.pallas_call(
        kern, name="sb_fwd", grid=(ng, nq),
        in_specs=[pl.BlockSpec((tq, wd), lambda h, i: (i, cb + h)), kv(1), kv(2),
                  pl.BlockSpec((1, wd), lambda h, i: (0, h))] + [ANY] * len(x_in),
        out_specs=[hq, hq, pl.BlockSpec((len(heads), tq, 128), lambda h, i: (h, i, 0))] + [ANY] * len(x_out),
        out_shape=[jax.ShapeDtypeStruct((tp, sh * SB_DH), BF16), jax.ShapeDtypeStruct((tp, sh * SB_DH), F32),
                   jax.ShapeDtypeStruct((sh, tp, 128), F32)] + x_out,
        input_output_aliases=alias,
        scratch_shapes=[pltpu.VMEM((len(heads), tq, CHUNK), F32), pltpu.VMEM((tp, wd), BF16),
                        pltpu.VMEM((tp, wd), BF16)] + x_sems,
        compiler_params=_cp(("arbitrary", "arbitrary") if carry else ("parallel", "arbitrary")),
    )(proj, proj, proj, sbw, *x_in)
    return res[:3], res[3:]


def _sb_bwd(proj, dmix, sy_all, aoff, sbw, sh, col0, dcol0, carry=()):
    tp = proj.shape[0]
    tq = _sb_tile(tp)
    nsub, nq = tq // CHUNK, tp // tq
    scale = 1.0 / math.sqrt(SB_DH)
    hq = pl.BlockSpec((tq, SB_DH), lambda h, i: (i, h))
    carry = list(carry)
    x_in, x_out, x_sems, alias = _carried(carry, 7, 4)

    def kern(q_ref, k_ref, v_ref, w_ref, do_ref, sy_ref, ao_ref, *rest):
        x_in_refs, rest = rest[:len(x_in)], rest[len(x_in):]
        dq_ref, dk_ref, dv_ref, dw_ref = rest[:4]
        x_out_refs, rest = rest[4:4 + len(x_out)], rest[4 + len(x_out):]
        (dk_acc, dv_acc, dq_acc, e_run, k16, v16), sem_refs = rest[:6], rest[6:]
        h, i = pl.program_id(0), pl.program_id(1)
        _carried_before(carry, h * nq + i, sh * nq, x_in_refs, x_out_refs, sem_refs)

        @pl.when(i == 0)
        def _():
            dk_acc[...] = jnp.zeros_like(dk_acc)
            dv_acc[...] = jnp.zeros_like(dv_acc)
            dw_ref[...] = jnp.zeros_like(dw_ref)
            k16[...] = k_ref[...].astype(BF16)
            v16[...] = v_ref[...].astype(BF16)

        qf = q_ref[...]
        q = qf.astype(BF16)
        dsy, dwr = _rms_bwd(sy_ref[...], w_ref[...], do_ref[...])
        dw_ref[...] += jnp.sum(dwr, axis=0, keepdims=True)
        dsy_b = dsy.astype(BF16)
        q_t = qf.T.astype(BF16)
        dsy_t = dsy.T.astype(BF16)
        atile = ao_ref[0]
        upper = _tri(True, 2)
        lower = _tri(False, 2)
        lane = lax.broadcasted_iota(jnp.int32, (tq, CHUNK), 1)
        qpos = i * tq + lax.broadcasted_iota(jnp.int32, (tq, CHUNK), 0)

        e_run[...] = jnp.zeros_like(e_run)
        dq_acc[...] = jnp.zeros_like(dq_acc)

        def trip(masked):
            def body(jj, carry):
                e_prev, dq = e_run[...], dq_acc[...]
                for sub in range(nsub):
                    kb = jj * nsub + sub
                    rows = pl.ds(pl.multiple_of(kb * CHUNK, CHUNK), CHUNK)
                    k, v = k16[rows, :], v16[rows, :]
                    mask, lb, lk = _sb_block(q, k, qpos, kb, scale, masked)
                    a = jnp.sum(jnp.where(lane == kb, atile, 0.0), axis=1, keepdims=True)
                    wgt = _keep(mask, jnp.exp(lb + a + _tri_sum(lk, upper)))
                    e = wgt * _dot(dsy_b, v, NT)
                    dv_acc[kb] += _dot(dsy_t, wgt.astype(BF16))
                    sig = jnp.exp(lb)
                    e_all = e_prev + _tri_sum(e, lower)
                    dz = (_keep(mask, e - sig * (e + e_all)) * scale).astype(BF16)
                    dk_acc[kb] += _dot(q_t, dz)
                    dq = dq + _dot(dz, k)
                    e_prev = e_prev + jnp.sum(e, axis=1, keepdims=True)
                e_run[...], dq_acc[...] = e_prev, dq
                return carry
            return body

        _sb_trips(i, trip)
        dq_ref[...] = dq_acc[...].astype(BF16)

        @pl.when(i == nq - 1)
        def _():
            def untranspose(kb, c):
                rows = pl.ds(pl.multiple_of(kb * CHUNK, CHUNK), CHUNK)
                dk_ref[rows, :] = dk_acc[kb].T.astype(BF16)
                dv_ref[rows, :] = dv_acc[kb].T.astype(BF16)
                return c

            lax.fori_loop(0, tp // CHUNK, untranspose, 0)

        _carried_after(carry, h * nq + i, sh * nq, x_in_refs, x_out_refs, sem_refs)

    kv = lambda sec: pl.BlockSpec((tp, SB_DH), lambda h, i: (0, col0 + sec * sh + h))
    hfull = pl.BlockSpec((tp, SB_DH), lambda h, i: (0, h))
    sw = sh * SB_DH
    acc_t = pltpu.VMEM((tp // CHUNK, SB_DH, CHUNK), F32)
    res = pl.pallas_call(
        kern, name="sb_bwd", grid=(sh, nq),
        in_specs=[pl.BlockSpec((tq, SB_DH), lambda h, i: (i, col0 + h)), kv(1), kv(2),
                  pl.BlockSpec((1, SB_DH), lambda h, i: (0, h)),
                  pl.BlockSpec((tq, SB_DH), lambda h, i: (i, dcol0 + h)), hq,
                  pl.BlockSpec((1, tq, 128), lambda h, i: (h, i, 0))] + [ANY] * len(x_in),
        out_specs=[hq, hfull, hfull, pl.BlockSpec((1, SB_DH), lambda h, i: (0, h))] + [ANY] * len(x_out),
        out_shape=[jax.ShapeDtypeStruct((tp, sw), BF16)] * 3 + [jax.ShapeDtypeStruct((1, sw), F32)] + x_out,
        input_output_aliases=alias,
        scratch_shapes=[acc_t, acc_t, pltpu.VMEM((tq, SB_DH), F32), pltpu.VMEM((tq, CHUNK), F32),
                        pltpu.VMEM((tp, SB_DH), BF16), pltpu.VMEM((tp, SB_DH), BF16)] + x_sems,
        compiler_params=_cp(("arbitrary", "arbitrary") if carry else ("parallel", "arbitrary")),
    )(proj, proj, proj, sbw, dmix, sy_all, aoff, *x_in)
    return res[:4], res[4:]


def _local_step(x, tgt, meta, net, conv_w, conv_b, pre1_w, gn_w, sb_w, post1_w, pre2_w, post2_w):
    s, d = x.shape

    def mm(host, a, b, mode, dtype):
        carry = net.carry(host)
        out = _matmul(a, b, mode, dtype, host, carry)
        if carry:
            out, extra = out
            net.took(host, extra)
        return out

    tp = s + CHUNK
    rh, sh = d // 512, d // 256
    rw = rh * RET_DK
    pos = jnp.arange(tp, dtype=F32) - N_PAD
    inv = ROPE_BASE ** (-jnp.arange(128, dtype=F32) / 128)
    ang = pos[:, None] * inv[None, :]
    cos, sin = jnp.cos(ang), jnp.sin(ang)
    tables = _ret_tables(rh)
    sb_col0 = 4 * rw // SB_DH

    h0, n1, extra = _prenorm(x, meta, pre1_w, net.carry("prenorm1"))
    net.took("prenorm1", extra)
    proj = mm("mm_proj", n1, net.weight("w_in"), "nn", F32)
    ret_out, ry, states = _ret_fwd(proj, cos, sin, tables, gn_w, rh)
    (sb_out, sy, aoff), extra = _sb_fwd(proj, sb_w, sh, sb_col0, net.carry("sb_fwd"))
    net.took("sb_fwd", extra)
    mixed = jnp.concatenate([ret_out, sb_out], axis=1)
    a = mm("mm_out", mixed, net.weight("w_out"), "nn", F32)
    h1, n2 = _mid_fwd(h0, a, post1_w, pre2_w)
    gu = mm("mm_up", n2, net.weight("w_up"), "nn", BF16)
    act = _ffn_fwd(gu, conv_w, conv_b)
    f = mm("mm_down", act, net.weight("w_down"), "nn", F32)

    dy, d_f, loss, dw_post2 = _loss_bwd(h1, f, tgt, post2_w)
    d_act = mm("mm_dact", d_f, net.weight("w_down"), "nt", BF16)
    net.grad("w_down", mm("mm_gdown", act, d_f, "tn", BF16))
    d_gate, d_up, ffn_stats = _ffn_bwd(gu, d_act, conv_w, conv_b)
    d_gu = jnp.concatenate([d_gate, d_up], axis=1)
    d_n2 = mm("mm_dn2", d_gu, net.weight("w_up"), "nt", F32)
    net.grad("w_up", mm("mm_gup", n2, d_gu, "tn", BF16))
    dh1, d_a, dw_pre2, dw_post1 = _mid_bwd(dy, d_n2, h1, a, pre2_w, post1_w)
    d_mix = mm("mm_dmix", d_a, net.weight("w_out"), "nt", F32)
    net.grad("w_out", mm("mm_gout", mixed, d_a, "tn", BF16))
    d_rq, d_rk, d_rv, d_rg, dw_gn = _ret_bwd(proj, d_mix, ry, states, cos, sin, tables, gn_w, rh)
    (d_sq, d_sk, d_sv, dw_sb), extra = _sb_bwd(proj, d_mix, sy, aoff, sb_w, sh, sb_col0, rw // SB_DH,
                                               net.carry("sb_bwd"))
    net.took("sb_bwd", extra)
    d_proj = jnp.concatenate([d_rq, d_rk, d_rv, d_rg, d_sq, d_sk, d_sv], axis=1)
    net.grad("w_in", mm("mm_gin", n1, d_proj, "tn", BF16))
    d_n1 = mm("mm_dn1", d_proj, net.weight("w_in"), "nt", F32)
    grad_x, g_meta, dw_pre1 = _pre_bwd(dh1, d_n1, h0, pre1_w)

    small = dict(loss=loss, meta=g_meta, pre1=dw_pre1, gn=dw_gn, sb=dw_sb, post1=dw_post1, pre2=dw_pre2,
                 conv_w=ffn_stats[0:3], conv_b=ffn_stats[3:4], post2=dw_post2)
    return grad_x, small


def _coords():
    return lax.axis_index("x"), lax.axis_index("y"), lax.axis_index("c")


def _other_chips(x, y):
    return [(1 - x, y), (x, 1 - y), (1 - x, 1 - y)]


ANY = pl.BlockSpec(memory_space=pl.ANY)
VM = pl.BlockSpec(memory_space=pltpu.VMEM)


def _gather4_small(v):
    r = v.shape[0]

    def kern(v_ref, o_ref, send, recv):
        x, y, c = _coords()
        o_ref[2 * x + y] = v_ref[...]
        cps = [pltpu.make_async_remote_copy(v_ref, o_ref.at[2 * x + y], send.at[j], recv.at[j],
                                            device_id=(px, py, c), device_id_type=MESH)
               for j, (px, py) in enumerate(_other_chips(x, y))]
        for cp in cps:
            cp.start()
        for cp in cps:
            cp.wait()

    return pl.pallas_call(
        kern, name="gather_small", in_specs=[VM], out_specs=VM,
        out_shape=jax.ShapeDtypeStruct((4, r, 128), F32),
        scratch_shapes=[pltpu.SemaphoreType.DMA((3,)), pltpu.SemaphoreType.DMA((3,))],
    )(v)


def _allreduce8_small(v):
    r = v.shape[0]
    flips = [(fx, fy, fc) for fx in (0, 1) for fy in (0, 1) for fc in (0, 1)][1:]

    def kern(v_ref, o_ref, buf, send, recv):
        x, y, c = _coords()
        me = 4 * x + 2 * y + c
        buf[me] = v_ref[...]
        cps = [pltpu.make_async_remote_copy(v_ref, buf.at[me], send.at[j], recv.at[j],
                                            device_id=(x ^ fx, y ^ fy, c ^ fc), device_id_type=MESH)
               for j, (fx, fy, fc) in enumerate(flips)]
        for cp in cps:
            cp.start()
        for cp in cps:
            cp.wait()
        tot = buf[0]
        for j in range(1, 8):
            tot = tot + buf[j]
        o_ref[...] = tot

    return pl.pallas_call(
        kern, name="allreduce_small", in_specs=[VM], out_specs=VM,
        out_shape=jax.ShapeDtypeStruct((r, 128), F32),
        scratch_shapes=[pltpu.VMEM((8, r, 128), F32), pltpu.SemaphoreType.DMA((7,)),
                        pltpu.SemaphoreType.DMA((7,))],
    )(v)


def _piece(ref, col_sharded, rows, cols, s, hc):
    half = rows // 2
    if col_sharded:
        return ref.at[pl.ds(pl.multiple_of(hc * half, 16), half), pl.ds(pl.multiple_of(s * cols, 128), cols)]
    return ref.at[pl.ds(pl.multiple_of(s * rows + hc * half, 16), half), :]


def _cast_place(w32, c_id, s_id, col_sharded, name):
    rows, cols = w32.shape
    full_shape = (rows, 4 * cols) if col_sharded else (4 * rows, cols)
    tr = max(d for d in _divisors(rows, 16) if d * cols * 4 <= 4 * 2 ** 20)
    nt = rows // tr

    def kern(c_ref, s_ref, w_ref, o_ref):
        o_ref[...] = w_ref[...].astype(BF16)

    if col_sharded:
        o_spec = pl.BlockSpec((tr, cols), lambda t, c_ref, s_ref: (t, s_ref[0]))
    else:
        o_spec = pl.BlockSpec((tr, cols), lambda t, c_ref, s_ref: (s_ref[0] * nt + t, 0))
    return pl.pallas_call(
        kern, name=name,
        grid_spec=pltpu.PrefetchScalarGridSpec(
            num_scalar_prefetch=2, grid=(nt,),
            in_specs=[pl.BlockSpec((tr, cols), lambda t, c_ref, s_ref: (t, 0))], out_specs=o_spec),
        out_shape=jax.ShapeDtypeStruct(full_shape, BF16),
        compiler_params=_cp(("parallel",)),
    )(c_id, s_id, w32)


def _gather_exchange(full, rows, cols, col_sharded, relay_at=1.0):
    def copies(w_ref, o_ref, send, recv):
        x, y, c = _coords()
        s = 2 * x + y
        sib = (x, y, 1 - c)
        chips = _other_chips(x, y)
        pc = functools.partial(_piece, o_ref, col_sharded, rows, cols)
        mine = _piece(w_ref, col_sharded, rows, cols, s, c)
        first = [pltpu.make_async_remote_copy(mine, pc(s, c), send.at[j], recv.at[j],
                                              device_id=(px, py, c), device_id_type=MESH)
                 for j, (px, py) in enumerate(chips)]
        passed = [pltpu.make_async_remote_copy(pc(2 * px + py, c), pc(2 * px + py, c), send.at[3 + j], recv.at[3 + j],
                                               device_id=sib, device_id_type=MESH)
                  for j, (px, py) in enumerate(chips)]
        from_sib = [pltpu.make_async_remote_copy(pc(2 * px + py, 1 - c), pc(2 * px + py, 1 - c), send.at[3 + j],
                                                 recv.at[3 + j], device_id=sib, device_id_type=MESH)
                    for j, (px, py) in enumerate(chips)]
        return first, passed, from_sib

    def start(ins, outs, send, recv):
        for cp in copies(ins[0], outs[0], send, recv)[0]:
            cp.start()

    def relay(ins, outs, send, recv):
        first, passed, _ = copies(ins[0], outs[0], send, recv)
        for j in range(3):
            first[j].wait_recv()
            passed[j].start()

    def finish(ins, outs, send, recv):
        first, passed, from_sib = copies(ins[0], outs[0], send, recv)
        for cp in from_sib:
            cp.wait_recv()
        for cp in first + passed:
            cp.wait_send()

    return _Exchange([full], [jax.ShapeDtypeStruct(full.shape, BF16)], {0: 0}, 6, start, finish, relay, relay_at)


def _pair_exchange(g, col_sharded, rows, cols):
    half = rows // 2

    def copies(g_ref, o_ref, send, recv):
        x, y, c = _coords()
        return [pltpu.make_async_remote_copy(_piece(g_ref, col_sharded, rows, cols, s, 1 - c), o_ref.at[s],
                                             send.at[s], recv.at[s], device_id=(x, y, 1 - c), device_id_type=MESH)
                for s in range(4)]

    def start(ins, outs, send, recv):
        for cp in copies(ins[0], outs[0], send, recv):
            cp.start()

    def finish(ins, outs, send, recv):
        for cp in copies(ins[0], outs[0], send, recv):
            cp.wait()

    return _Exchange([g], [jax.ShapeDtypeStruct((4, half, cols), BF16)], {}, 4, start, finish)


def _half_tiles(half, cols):
    tr = max(d for d in _divisors(half, 16) if d * cols * 4 <= 4 * 2 ** 20)
    return tr, half // tr


def _half_spec(col_sharded, tr, nt, cols, own):
    which = (lambda s, s_ref: s_ref[0]) if own else (lambda s, s_ref: (s_ref[0] + 1 + s) % 4)
    if col_sharded:
        return pl.BlockSpec((tr, cols), lambda s, t, c_ref, s_ref: (c_ref[0] * nt + t, which(s, s_ref)))
    return pl.BlockSpec((tr, cols), lambda s, t, c_ref, s_ref: ((2 * which(s, s_ref) + c_ref[0]) * nt + t, 0))


def _rs_add(g, r1, c_id, s_id, col_sharded, rows, cols, name):
    half = rows // 2
    tr, nt = _half_tiles(half, cols)

    def kern(c_ref, s_ref, g_ref, r_ref, o_ref):
        o_ref[0] = (g_ref[...].astype(F32) + r_ref[0].astype(F32)).astype(BF16)

    slab = pl.BlockSpec((1, tr, cols), lambda s, t, c_ref, s_ref: ((s_ref[0] + 1 + s) % 4, t, 0))
    return pl.pallas_call(
        kern, name=name,
        grid_spec=pltpu.PrefetchScalarGridSpec(
            num_scalar_prefetch=2, grid=(3, nt),
            in_specs=[_half_spec(col_sharded, tr, nt, cols, False), slab], out_specs=slab),
        out_shape=jax.ShapeDtypeStruct((4, half, cols), BF16),
        compiler_params=_cp(("parallel", "parallel")),
    )(c_id, s_id, g, r1)


def _scatter_exchange(p):
    _, half, cols = p.shape

    def copies(p_ref, o_ref, send, recv):
        x, y, c = _coords()
        return [pltpu.make_async_remote_copy(p_ref.at[2 * px + py], o_ref.at[j], send.at[j], recv.at[j],
                                             device_id=(px, py, c), device_id_type=MESH)
                for j, (px, py) in enumerate(_other_chips(x, y))]

    def start(ins, outs, send, recv):
        for cp in copies(ins[0], outs[0], send, recv):
            cp.start()

    def finish(ins, outs, send, recv):
        for cp in copies(ins[0], outs[0], send, recv):
            cp.wait()

    return _Exchange([p], [jax.ShapeDtypeStruct((3, half, cols), BF16)], {}, 3, start, finish)


def _rs_total(g, r1, r2, c_id, s_id, col_sharded, rows, cols, name):
    half = rows // 2
    tr, nt = _half_tiles(half, cols)

    def kern(c_ref, s_ref, g_ref, r1_ref, r2_ref, o_ref):
        tot = g_ref[...].astype(F32) + r1_ref[0].astype(F32)
        for j in range(3):
            tot = tot + r2_ref[j].astype(F32)
        o_ref[...] = tot

    return pl.pallas_call(
        kern, name=name,
        grid_spec=pltpu.PrefetchScalarGridSpec(
            num_scalar_prefetch=2, grid=(1, nt),
            in_specs=[_half_spec(col_sharded, tr, nt, cols, True),
                      pl.BlockSpec((1, tr, cols), lambda s, t, c_ref, s_ref: (s_ref[0], t, 0)),
                      pl.BlockSpec((3, tr, cols), lambda s, t, c_ref, s_ref: (0, t, 0))],
            out_specs=pl.BlockSpec((tr, cols), lambda s, t, c_ref, s_ref: (c_ref[0] * nt + t, 0))),
        out_shape=jax.ShapeDtypeStruct((rows, cols), F32),
        compiler_params=_cp(("parallel", "parallel")),
    )(c_id, s_id, g, r1, r2)


def _rs_exchange(t, name):
    rows, cols = t.shape
    half = rows // 2

    def kern(t_ref, o_ref, send, recv):
        x, y, c = _coords()
        mine = pl.ds(pl.multiple_of(c * half, 8), half)
        cp = pltpu.make_async_remote_copy(t_ref.at[mine, :], o_ref.at[mine, :], send, recv,
                                          device_id=(x, y, 1 - c), device_id_type=MESH)
        cp.start()
        cp.wait()

    return pl.pallas_call(
        kern, name=name, in_specs=[ANY], out_specs=ANY, input_output_aliases={0: 0},
        out_shape=jax.ShapeDtypeStruct((rows, cols), F32),
        scratch_shapes=[pltpu.SemaphoreType.DMA, pltpu.SemaphoreType.DMA],
    )(t)


class _Whole:
    def __init__(self, full):
        self.full, self.grads = dict(full), {}

    def weight(self, n):
        return self.full[n]

    def carry(self, host):
        return []

    def took(self, host, outs):
        pass

    def grad(self, n, g):
        self.grads[n] = g


BIG = ("w_in", "w_out", "w_up", "w_down")
COL_SHARDED = dict(w_in=True, w_out=False, w_up=True, w_down=False)
GATHER_ON = dict(prenorm1="w_in", mm_proj="w_out", sb_fwd="w_up", mm_up="w_down")
RELAY_AT = dict(prenorm1=1.0, mm_proj=0.6, sb_fwd=0.85, mm_up=0.7)
PAIR_ON = dict(mm_dn2="w_down", mm_dmix="w_up")
SCATTER_ON = dict(mm_gup=("w_down",), sb_bwd=("w_up", "w_out"), mm_dn1=("w_in",))


class _Sharded(_Whole):
    def __init__(self, shards, c_id, s_id):
        self.shards, self.c_id, self.s_id = shards, c_id, s_id
        self.full, self.grads, self.r1, self.p, self.r2 = {}, {}, {}, {}, {}
        self.placed = {n: _cast_place(shards[n], c_id, s_id, COL_SHARDED[n], "place_" + n) for n in BIG}

    def _gather(self, n, relay_at=1.0):
        return _gather_exchange(self.placed[n], *self.shards[n].shape, COL_SHARDED[n], relay_at)

    def _pair(self, n):
        return _pair_exchange(self.grads[n], COL_SHARDED[n], *self.shards[n].shape)

    def _paired(self, n, r1):
        rows, cols = self.shards[n].shape
        self.r1[n] = r1
        self.p[n] = _rs_add(self.grads[n], r1, self.c_id, self.s_id, COL_SHARDED[n], rows, cols, "rs_add_" + n)

    def carry(self, host):
        if host in GATHER_ON:
            return [self._gather(GATHER_ON[host], RELAY_AT[host])]
        if host in PAIR_ON:
            return [self._pair(PAIR_ON[host])]
        return [_scatter_exchange(self.p[n]) for n in SCATTER_ON.get(host, ())]

    def took(self, host, outs):
        if host in GATHER_ON:
            self.full[GATHER_ON[host]], = outs
        if host in PAIR_ON:
            self._paired(PAIR_ON[host], outs[0])
        for n, r2 in zip(SCATTER_ON.get(host, ()), outs):
            self.r2[n] = r2

    def grad(self, n, g):
        self.grads[n] = g
        if n not in PAIR_ON.values():
            self._paired(n, _run_exchange(self._pair(n), "rs_pair_" + n)[0])

    def reduced(self, n):
        rows, cols = self.shards[n].shape
        t = _rs_total(self.grads[n], self.r1[n], self.r2[n], self.c_id, self.s_id, COL_SHARDED[n], rows, cols,
                      "rs_total_" + n)
        return _rs_exchange(t, "rs_exchange_" + n)


def _adamw_vals(w, g, m, v):
    m = ADAM_B1 * m + (1.0 - ADAM_B1) * g
    v = ADAM_B2 * v + (1.0 - ADAM_B2) * (g * g)
    m_hat = m / (1.0 - ADAM_B1 ** ADAM_STEP)
    v_hat = v / (1.0 - ADAM_B2 ** ADAM_STEP)
    delta = -ADAM_LR * (m_hat / (jnp.sqrt(v_hat) + ADAM_EPS) + ADAM_WD * w)
    return delta, m, v


def _adamw(w, g, m, v, name):
    rows, cols = w.shape
    tr = max(d for d in _divisors(rows, 8) if d * cols * 4 <= 2 * 2 ** 20)

    def kern(w_ref, g_ref, m_ref, v_ref, d_ref, mo_ref, vo_ref, go_ref):
        g = g_ref[...]
        d_ref[...], mo_ref[...], vo_ref[...] = _adamw_vals(w_ref[...], g, m_ref[...], v_ref[...])
        go_ref[...] = g

    spec = pl.BlockSpec((tr, cols), lambda i: (i, 0))
    return pl.pallas_call(
        kern, name=name, grid=(rows // tr,), in_specs=[spec] * 4, out_specs=[spec] * 4,
        out_shape=[jax.ShapeDtypeStruct((rows, cols), F32)] * 4,
        compiler_params=_cp(("parallel",)),
    )(w, g, m, v)


def _pack(arrs):
    flat = []
    for a in arrs:
        a = a.reshape(-1)
        flat.append(jnp.pad(a, (0, (-a.shape[0]) % 1024)))
    return jnp.concatenate(flat).reshape(-1, 128)


def _unpack(slab, shapes):
    out, off = [], 0
    flat = slab.reshape(slab.shape[:-2] + (-1,))
    for shp in shapes:
        n = math.prod(shp)
        out.append(flat[..., off:off + n].reshape(slab.shape[:-2] + tuple(shp)))
        off += n + (-n) % 1024
    return out


SMALL = ("meta_tokens", "attn_pre_norm_w", "ret_gn_w", "sb_norm_w", "attn_post_norm_w", "ffn_pre_norm_w",
         "conv_w", "conv_b", "ffn_post_norm_w")
ORDER = ("meta_tokens", "attn_pre_norm_w", "w_in", "ret_gn_w", "sb_norm_w", "w_out", "attn_post_norm_w",
         "ffn_pre_norm_w", "w_up", "conv_w", "conv_b", "w_down", "ffn_post_norm_w")


def kernel(x, meta_tokens, attn_pre_norm_w, w_in, ret_gn_w, sb_norm_w, w_out, attn_post_norm_w, ffn_pre_norm_w, w_up, conv_w, conv_b, w_down, ffn_post_norm_w, loss_target, m_meta_tokens, m_attn_pre_norm_w, m_w_in, m_ret_gn_w, m_sb_norm_w, m_w_out, m_attn_post_norm_w, m_ffn_pre_norm_w, m_w_up, m_conv_w, m_conv_b, m_w_down, m_ffn_post_norm_w, v_meta_tokens, v_attn_pre_norm_w, v_w_in, v_ret_gn_w, v_sb_norm_w, v_w_out, v_attn_post_norm_w, v_ffn_pre_norm_w, v_w_up, v_conv_w, v_conv_b, v_w_down, v_ffn_post_norm_w):
    args = dict(locals())
    w = {n: args[n] for n in ORDER}
    m = {n: args["m_" + n] for n in ORDER}
    v = {n: args["v_" + n] for n in ORDER}
    xi, yi, ci = _coords()
    shard_id = 2 * xi + yi
    c_id = ci.astype(jnp.int32).reshape(1)
    s_id = shard_id.astype(jnp.int32).reshape(1)
    d = x.shape[-1]

    mshape, cshape = w["meta_tokens"].shape, w["conv_w"][0].shape
    got = _unpack(_gather4_small(_pack([w["meta_tokens"], w["conv_w"][0]])), [mshape, cshape])
    meta_full = jnp.moveaxis(got[0], 0, 1).reshape(N_META, d)
    conv_w_full = jnp.moveaxis(got[1], 0, 1).reshape(CONV_W, -1)

    net = _Sharded({n: w[n][0] for n in BIG}, c_id, s_id)
    grad_x, g_small = _local_step(
        x[0], loss_target[0], meta_full, net, conv_w_full, w["conv_b"], w["attn_pre_norm_w"], w["ret_gn_w"],
        w["sb_norm_w"], w["attn_post_norm_w"], w["ffn_pre_norm_w"], w["ffn_post_norm_w"])

    names = ("loss", "meta", "pre1", "gn", "sb", "post1", "pre2", "conv_w", "conv_b", "post2")
    tot = _unpack(_allreduce8_small(_pack([g_small[n] for n in names])), [g_small[n].shape for n in names])
    tot = dict(zip(names, tot))
    loss = tot["loss"][0, 0]
    mcols, ccols = mshape[1], cshape[1]
    grads = {
        "meta_tokens": lax.dynamic_slice_in_dim(tot["meta"], shard_id * mcols, mcols, axis=1),
        "attn_pre_norm_w": tot["pre1"], "ret_gn_w": tot["gn"], "sb_norm_w": tot["sb"],
        "attn_post_norm_w": tot["post1"], "ffn_pre_norm_w": tot["pre2"],
        "conv_w": lax.dynamic_slice_in_dim(tot["conv_w"], shard_id * ccols, ccols, axis=1)[None],
        "conv_b": tot["conv_b"], "ffn_post_norm_w": tot["post2"],
    }

    delta, new_m, new_v = {}, {}, {}
    for n in BIG:
        dl, mo, vo, g = _adamw(w[n][0], net.reduced(n), m[n][0], v[n][0], "adamw_" + n)
        delta[n], new_m[n], new_v[n], grads[n] = dl[None], mo[None], vo[None], g[None]
    shapes = [w[n].shape for n in SMALL]
    packed = [_pack([src[n] for n in SMALL]) for src in (w, grads, m, v)]
    outs = _adamw(*packed, "adamw_small")[:3]
    for dst, slab in zip((delta, new_m, new_v), outs):
        for n, a in zip(SMALL, _unpack(slab, shapes)):
            dst[n] = a

    return (loss, grad_x[None], *[grads[n] for n in ORDER], *[delta[n] for n in ORDER],
            *[new_m[n] for n in ORDER], *[new_v[n] for n in ORDER])
```

```python
import functools
import math

import jax
import jax.numpy as jnp
from jax import lax
from jax.experimental import pallas as pl
from jax.experimental.pallas import tpu as pltpu

F32 = jnp.float32
BF16 = jnp.bfloat16
MESH = pl.DeviceIdType.MESH

EPS = 1e-6
ROPE_BASE = 10000.0
N_META = 16
CHUNK = 128
N_PAD = CHUNK - N_META
RET_DK = 256
SB_DH = 128
SB_HEADS_PER_STEP = 1
CONV_W = 3

ADAM_LR = 0.001
ADAM_B1 = 0.9
ADAM_B2 = 0.999
ADAM_EPS = 1e-08
ADAM_WD = 0.01
ADAM_STEP = 10

V7X_VMEM_BYTES = 64 * 2 ** 20
VMEM_LIMIT = V7X_VMEM_BYTES - 8 * 2 ** 20
MM_BUDGET = 40 * 2 ** 20
V7X_BF16_FLOPS = 0.9e15
V7X_HBM_BPS = 2.0e12
GRID_STEP_S = 0.6e-6
ACC_BPS = 2.0e13

NN = (((1,), (0,)), ((), ()))
NT = (((1,), (1,)), ((), ()))
TN = (((0,), (0,)), ((), ()))


def _cp(sem, vmem=VMEM_LIMIT):
    return pltpu.CompilerParams(dimension_semantics=sem, vmem_limit_bytes=vmem)


def _dot(a, b, dims=NN):
    return lax.dot_general(a, b, dims, preferred_element_type=F32)


def _sigmoid(x):
    return 1.0 / (1.0 + jnp.exp(-x))


def _divisors(n, align):
    return [d for d in range(align, n + 1, align) if n % d == 0]


def _mm_tiles(mode, m, n, k, out_bytes):
    best = None
    tms = [d for d in _divisors(m, 128 if mode == "tn" else 16) if d >= 128]
    tns = [d for d in _divisors(n, 256)] or [d for d in _divisors(n, 128)]
    tks = [d for d in _divisors(k, 128) if d >= 128]
    flops = 2.0 * m * n * k
    for tm in tms:
        for tn in tns:
            if tm * tn > 2112 * 1024:
                continue
            for tk in tks:
                nk = k // tk
                foot = 4 * (tm * tk + tk * tn) + 2 * tm * tn * out_bytes
                if nk > 1:
                    foot += 4 * tm * tn
                if foot > MM_BUDGET:
                    continue
                steps = (m // tm) * (n // tn) * nk
                for swap in (False, True):
                    if nk > 1:
                        traffic = 2.0 * (n // tn) * m * k + 2.0 * (m // tm) * n * k
                    elif swap:
                        traffic = 2.0 * n * k + 2.0 * (n // tn) * m * k
                    else:
                        traffic = 2.0 * m * k + 2.0 * (m // tm) * n * k
                    traffic += out_bytes * m * n
                    t = max(flops / V7X_BF16_FLOPS, traffic / V7X_HBM_BPS) + steps * GRID_STEP_S
                    if nk > 1:
                        t += 12.0 * m * n * nk / ACC_BPS
                    if best is None or t < best[0]:
                        best = (t, tm, tn, tk, swap)
    assert best is not None, (mode, m, n, k)
    return best[1:]


class _Exchange:
    def __init__(self, operands, out_shapes, aliases, nsem, start, finish, relay=None, relay_at=1.0):
        self.operands, self.out_shapes, self.aliases, self.nsem = operands, out_shapes, aliases, nsem
        self.start, self.finish, self.relay, self.relay_at = start, finish, relay, relay_at


def _carried(exchanges, in_base, out_base):
    ins = [a for e in exchanges for a in e.operands]
    outs = [s for e in exchanges for s in e.out_shapes]
    sems = [pltpu.SemaphoreType.DMA((e.nsem,)) for e in exchanges for _ in (0, 1)]
    alias, i, o = {}, in_base, out_base
    for e in exchanges:
        alias.update({i + k: o + v for k, v in e.aliases.items()})
        i, o = i + len(e.operands), o + len(e.out_shapes)
    return ins, outs, sems, alias


def _run_carried(exchanges, phase, in_refs, out_refs, sem_refs, pick=None):
    i = o = 0
    for n, e in enumerate(exchanges):
        ni, no = len(e.operands), len(e.out_shapes)
        if getattr(e, phase) is not None and pick in (None, n):
            getattr(e, phase)(in_refs[i:i + ni], out_refs[o:o + no], sem_refs[2 * n], sem_refs[2 * n + 1])
        i, o = i + ni, o + no


def _carried_before(exchanges, step, nsteps, in_refs, out_refs, sem_refs):
    if not exchanges:
        return

    @pl.when(step == 0)
    def _():
        _run_carried(exchanges, "start", in_refs, out_refs, sem_refs)

    for n, e in enumerate(exchanges):
        if e.relay is not None:
            @pl.when(step == min(nsteps - 1, int(e.relay_at * nsteps)))
            def _(n=n):
                _run_carried(exchanges, "relay", in_refs, out_refs, sem_refs, pick=n)


def _carried_after(exchanges, step, nsteps, in_refs, out_refs, sem_refs):
    if not exchanges:
        return

    @pl.when(step == nsteps - 1)
    def _():
        _run_carried(exchanges, "finish", in_refs, out_refs, sem_refs)


def _run_exchange(e, name):
    ins, outs, sems, alias = _carried([e], 0, 0)

    def kern(*refs):
        in_refs, out_refs, sem_refs = refs[:len(ins)], refs[len(ins):len(ins) + len(outs)], refs[len(ins) + len(outs):]
        for phase in ("start", "relay", "finish"):
            _run_carried([e], phase, in_refs, out_refs, sem_refs)

    return pl.pallas_call(
        kern, name=name, in_specs=[ANY] * len(ins), out_specs=[ANY] * len(outs), out_shape=outs,
        input_output_aliases=alias, scratch_shapes=sems,
    )(*ins)


def _matmul(a, b, mode, out_dtype, name, carry=()):
    if mode == "nn":
        (m, k), (k2, n) = a.shape, b.shape
    elif mode == "nt":
        (m, k), (n, k2) = a.shape, b.shape
    else:
        (k, m), (k2, n) = a.shape, b.shape
    assert k == k2 and a.dtype == BF16 and b.dtype == BF16
    tm, tn, tk, swap = _mm_tiles(mode, m, n, k, jnp.dtype(out_dtype).itemsize)
    nk = k // tk
    dims = {"nn": NN, "nt": NT, "tn": TN}[mode]

    def ij(g0, g1):
        return (g1, g0) if swap else (g0, g1)

    if mode == "tn":
        a_spec = pl.BlockSpec((tk, tm), lambda g0, g1, kk: (kk, ij(g0, g1)[0]))
    else:
        a_spec = pl.BlockSpec((tm, tk), lambda g0, g1, kk: (ij(g0, g1)[0], kk))
    if mode == "nt":
        b_spec = pl.BlockSpec((tn, tk), lambda g0, g1, kk: (ij(g0, g1)[1], kk))
    else:
        b_spec = pl.BlockSpec((tk, tn), lambda g0, g1, kk: (kk, ij(g0, g1)[1]))
    o_spec = pl.BlockSpec((tm, tn), lambda g0, g1, kk: ij(g0, g1))

    grid = (n // tn, m // tm, nk) if swap else (m // tm, n // tn, nk)
    carry = list(carry)
    x_in, x_out, x_sems, alias = _carried(carry, 2, 1)
    acc_shapes = [pltpu.VMEM((tm, tn), F32)] if nk > 1 else []

    def kern(a_ref, b_ref, *rest):
        x_in_refs, o_ref = rest[:len(x_in)], rest[len(x_in)]
        x_out_refs = rest[len(x_in) + 1:len(x_in) + 1 + len(x_out)]
        tail = rest[len(x_in) + 1 + len(x_out):]
        acc, sem_refs = tail[:len(acc_shapes)], tail[len(acc_shapes):]
        pid = [pl.program_id(ax) for ax in range(3)]
        step, nsteps = (pid[0] * grid[1] + pid[1]) * grid[2] + pid[2], grid[0] * grid[1] * grid[2]
        _carried_before(carry, step, nsteps, x_in_refs, x_out_refs, sem_refs)
        prod = _dot(a_ref[...], b_ref[...], dims)
        if nk == 1:
            o_ref[...] = prod.astype(out_dtype)
        else:
            kk = pid[2]

            @pl.when(kk == 0)
            def _():
                acc[0][...] = prod

            @pl.when(kk > 0)
            def _():
                acc[0][...] += prod

            @pl.when(kk == nk - 1)
            def _():
                o_ref[...] = acc[0][...].astype(out_dtype)

        _carried_after(carry, step, nsteps, x_in_refs, x_out_refs, sem_refs)

    sem = ("arbitrary",) * 3 if carry else ("parallel", "parallel", "arbitrary")
    res = pl.pallas_call(
        kern, name=name, grid=grid, in_specs=[a_spec, b_spec] + [ANY] * len(x_in),
        out_specs=[o_spec] + [ANY] * len(x_out),
        out_shape=[jax.ShapeDtypeStruct((m, n), out_dtype)] + x_out,
        input_output_aliases=alias, scratch_shapes=acc_shapes + x_sems,
        compiler_params=_cp(sem),
    )(a, b, *x_in)
    return (res[0], res[1:]) if carry else res[0]


def _rms_fwd(x, w):
    r = lax.rsqrt(jnp.mean(x * x, axis=-1, keepdims=True) + EPS)
    return x * r * w


def _rms_bwd(x, w, g):
    r = lax.rsqrt(jnp.mean(x * x, axis=-1, keepdims=True) + EPS)
    gw = g * w
    dx = r * gw - x * (r * r * r * jnp.mean(gw * x, axis=-1, keepdims=True))
    return dx, g * (x * r)


def _row_spec(d):
    return pl.BlockSpec((CHUNK, d), lambda i: (i, 0))


def _vec_spec(d):
    return pl.BlockSpec((1, d), lambda i: (0, 0))


def _prenorm(x, meta, w, carry=()):
    s, d = x.shape
    tp = s + CHUNK
    carry = list(carry)
    x_in, x_out, x_sems, alias = _carried(carry, 3, 2)
    nsteps = tp // CHUNK

    def kern(x_ref, m_ref, w_ref, *rest):
        x_in_refs, (h_ref, o_ref) = rest[:len(x_in)], rest[len(x_in):len(x_in) + 2]
        x_out_refs, sem_refs = rest[len(x_in) + 2:len(x_in) + 2 + len(x_out)], rest[len(x_in) + 2 + len(x_out):]
        i = pl.program_id(0)
        _carried_before(carry, i, nsteps, x_in_refs, x_out_refs, sem_refs)

        def emit(h):
            h_ref[...] = h
            o_ref[...] = _rms_fwd(h, w_ref[...]).astype(BF16)

        @pl.when(i == 0)
        def _():
            emit(jnp.concatenate([jnp.zeros((N_PAD, d), F32), m_ref[...]], axis=0))

        @pl.when(i > 0)
        def _():
            emit(x_ref[...])

        _carried_after(carry, i, nsteps, x_in_refs, x_out_refs, sem_refs)

    res = pl.pallas_call(
        kern, name="prenorm1", grid=(nsteps,),
        in_specs=[pl.BlockSpec((CHUNK, d), lambda i: (jnp.maximum(i - 1, 0), 0)),
                  pl.BlockSpec((N_META, d), lambda i: (0, 0)), _vec_spec(d)] + [ANY] * len(x_in),
        out_specs=[_row_spec(d), _row_spec(d)] + [ANY] * len(x_out),
        out_shape=[jax.ShapeDtypeStruct((tp, d), F32), jax.ShapeDtypeStruct((tp, d), BF16)] + x_out,
        input_output_aliases=alias, scratch_shapes=x_sems,
        compiler_params=_cp(("arbitrary",) if carry else ("parallel",)),
    )(x, meta, w, *x_in)
    return res[0], res[1], res[2:]


def _mid_fwd(h0, a, w_post, w_pre):
    tp, d = h0.shape

    def kern(h_ref, a_ref, wp_ref, wq_ref, h1_ref, n2_ref):
        h1 = h_ref[...] + _rms_fwd(a_ref[...], wp_ref[...])
        h1_ref[...] = h1
        n2_ref[...] = _rms_fwd(h1, wq_ref[...]).astype(BF16)

    return pl.pallas_call(
        kern, name="mid_fwd", grid=(tp // CHUNK,),
        in_specs=[_row_spec(d), _row_spec(d), _vec_spec(d), _vec_spec(d)],
        out_specs=[_row_spec(d), _row_spec(d)],
        out_shape=[jax.ShapeDtypeStruct((tp, d), F32), jax.ShapeDtypeStruct((tp, d), BF16)],
        compiler_params=_cp(("parallel",)),
    )(h0, a, w_post, w_pre)


def _loss_bwd(h1, f, tgt, w_post):
    tp, d = h1.shape

    def kern(h_ref, f_ref, t_ref, w_ref, dy_ref, df_ref, loss_ref, dw_ref):
        i = pl.program_id(0)

        @pl.when(i == 0)
        def _():
            dy_ref[...] = jnp.zeros_like(dy_ref)
            df_ref[...] = jnp.zeros_like(df_ref)
            loss_ref[...] = jnp.zeros_like(loss_ref)
            dw_ref[...] = jnp.zeros_like(dw_ref)

        @pl.when(i > 0)
        def _():
            fv = f_ref[...]
            w = w_ref[...]
            err = h_ref[...] + _rms_fwd(fv, w) - t_ref[...]
            loss_ref[...] += 0.5 * jnp.sum(jnp.mean(err * err, axis=-1, keepdims=True))
            dy = err * (1.0 / d)
            dy_ref[...] = dy
            dfv, dwr = _rms_bwd(fv, w, dy)
            df_ref[...] = dfv.astype(BF16)
            dw_ref[...] += jnp.sum(dwr, axis=0, keepdims=True)

    return pl.pallas_call(
        kern, name="loss_bwd", grid=(tp // CHUNK,),
        in_specs=[_row_spec(d), _row_spec(d),
                  pl.BlockSpec((CHUNK, d), lambda i: (jnp.maximum(i - 1, 0), 0)), _vec_spec(d)],
        out_specs=[_row_spec(d), _row_spec(d), pl.BlockSpec((1, 128), lambda i: (0, 0)), _vec_spec(d)],
        out_shape=[jax.ShapeDtypeStruct((tp, d), F32), jax.ShapeDtypeStruct((tp, d), BF16),
                   jax.ShapeDtypeStruct((1, 128), F32), jax.ShapeDtypeStruct((1, d), F32)],
        compiler_params=_cp(("arbitrary",)),
    )(h1, f, tgt, w_post)


def _mid_bwd(dy, dn2, h1, a, w_pre, w_post):
    tp, d = h1.shape

    def kern(dy_ref, dn_ref, h_ref, a_ref, wq_ref, wp_ref, dh_ref, da_ref, dwq_ref, dwp_ref):
        @pl.when(pl.program_id(0) == 0)
        def _():
            dwq_ref[...] = jnp.zeros_like(dwq_ref)
            dwp_ref[...] = jnp.zeros_like(dwp_ref)

        dx, dwq = _rms_bwd(h_ref[...], wq_ref[...], dn_ref[...])
        dh = dy_ref[...] + dx
        dh_ref[...] = dh
        da, dwp = _rms_bwd(a_ref[...], wp_ref[...], dh)
        da_ref[...] = da.astype(BF16)
        dwq_ref[...] += jnp.sum(dwq, axis=0, keepdims=True)
        dwp_ref[...] += jnp.sum(dwp, axis=0, keepdims=True)

    return pl.pallas_call(
        kern, name="mid_bwd", grid=(tp // CHUNK,),
        in_specs=[_row_spec(d)] * 4 + [_vec_spec(d)] * 2,
        out_specs=[_row_spec(d), _row_spec(d), _vec_spec(d), _vec_spec(d)],
        out_shape=[jax.ShapeDtypeStruct((tp, d), F32), jax.ShapeDtypeStruct((tp, d), BF16),
                   jax.ShapeDtypeStruct((1, d), F32), jax.ShapeDtypeStruct((1, d), F32)],
        compiler_params=_cp(("arbitrary",)),
    )(dy, dn2, h1, a, w_pre, w_post)


def _pre_bwd(dh1, dn1, h0, w_pre):
    tp, d = h0.shape
    s = tp - CHUNK

    def kern(dh_ref, dn_ref, h_ref, w_ref, gx_ref, gm_ref, dw_ref):
        i = pl.program_id(0)
        dx, dwr = _rms_bwd(h_ref[...], w_ref[...], dn_ref[...])
        dh0 = dh_ref[...] + dx
        gx_ref[...] = dh0

        @pl.when(i == 0)
        def _():
            gm_ref[...] = dh0[N_PAD:, :]
            dw_ref[...] = jnp.zeros_like(dw_ref)

        dw_ref[...] += jnp.sum(dwr, axis=0, keepdims=True)

    return pl.pallas_call(
        kern, name="pre_bwd", grid=(tp // CHUNK,),
        in_specs=[_row_spec(d)] * 3 + [_vec_spec(d)],
        out_specs=[pl.BlockSpec((CHUNK, d), lambda i: (jnp.maximum(i - 1, 0), 0)),
                   pl.BlockSpec((N_META, d), lambda i: (0, 0)), _vec_spec(d)],
        out_shape=[jax.ShapeDtypeStruct((s, d), F32), jax.ShapeDtypeStruct((N_META, d), F32),
                   jax.ShapeDtypeStruct((1, d), F32)],
        compiler_params=_cp(("arbitrary",)),
    )(dh1, dn1, h0, w_pre)


def _ffn_cols(dff):
    return dff // 2 if dff % 256 == 0 else dff


HALO = 16


def _ffn_fwd(gu, conv_w, conv_b):
    tp, two_dff = gu.shape
    dff = two_dff // 2
    tc = _ffn_cols(dff)
    nj = dff // tc
    r8 = CHUNK // HALO

    def kern(g_ref, gp_ref, u_ref, w_ref, b_ref, o_ref):
        i = pl.program_id(1)
        prev = gp_ref[...].astype(F32) * (i > 0).astype(F32)
        ext = jnp.concatenate([prev, g_ref[...].astype(F32)], axis=0)
        w = w_ref[...]
        conv = (b_ref[...] + w[0:1] * pltpu.roll(ext, 2, 0)[HALO:] + w[1:2] * pltpu.roll(ext, 1, 0)[HALO:]
                + w[2:3] * ext[HALO:])
        o_ref[...] = (conv * _sigmoid(conv) * u_ref[...].astype(F32)).astype(BF16)

    return pl.pallas_call(
        kern, name="ffn_fwd", grid=(nj, tp // CHUNK),
        in_specs=[pl.BlockSpec((CHUNK, tc), lambda j, i: (i, j)),
                  pl.BlockSpec((HALO, tc), lambda j, i: (jnp.maximum(i * r8 - 1, 0), j)),
                  pl.BlockSpec((CHUNK, tc), lambda j, i: (i, j + nj)),
                  pl.BlockSpec((CONV_W, tc), lambda j, i: (0, j)),
                  pl.BlockSpec((1, tc), lambda j, i: (0, j))],
        out_specs=pl.BlockSpec((CHUNK, tc), lambda j, i: (i, j)),
        out_shape=jax.ShapeDtypeStruct((tp, dff), BF16),
        compiler_params=_cp(("parallel", "parallel")),
    )(gu, gu, gu, conv_w, conv_b)


def _ffn_bwd(gu, dact, conv_w, conv_b):
    tp, two_dff = gu.shape
    dff = two_dff // 2
    tc = _ffn_cols(dff)
    nj = dff // tc
    ni = tp // CHUNK
    r8 = CHUNK // HALO

    def kern(g_ref, gp_ref, gn_ref, u_ref, un_ref, d_ref, dn_ref, w_ref, b_ref, dg_ref, du_ref, st_ref):
        i = pl.program_id(1)

        @pl.when(i == 0)
        def _():
            st_ref[...] = jnp.zeros_like(st_ref)

        first = (i > 0).astype(F32)
        last = (i < ni - 1).astype(F32)
        gate = g_ref[...].astype(F32)
        d_main = d_ref[...].astype(F32)
        ext = jnp.concatenate([gp_ref[...].astype(F32) * first, gate, gn_ref[...].astype(F32)], axis=0)
        w = w_ref[...]
        r1 = pltpu.roll(ext, 1, 0)
        r2 = pltpu.roll(ext, 2, 0)
        conv = (b_ref[...] + w[0:1] * r2 + w[1:2] * r1 + w[2:3] * ext)[HALO:]
        up = jnp.concatenate([u_ref[...].astype(F32), un_ref[...].astype(F32)], axis=0)
        da = jnp.concatenate([d_main, dn_ref[...].astype(F32) * last], axis=0)
        sg = _sigmoid(conv)
        dc = da * up * (sg * (1.0 + conv * (1.0 - sg)))
        du_ref[...] = (d_main * (conv * sg)[:CHUNK]).astype(BF16)
        n = CHUNK + HALO
        dgate = w[2:3] * dc + w[1:2] * pltpu.roll(dc, n - 1, 0) + w[0:1] * pltpu.roll(dc, n - 2, 0)
        dg_ref[...] = dgate[:CHUNK].astype(BF16)
        dcm = dc[:CHUNK]
        s0 = jnp.sum(dcm * r2[HALO:HALO + CHUNK], axis=0, keepdims=True)
        s1 = jnp.sum(dcm * r1[HALO:HALO + CHUNK], axis=0, keepdims=True)
        s2 = jnp.sum(dcm * gate, axis=0, keepdims=True)
        s3 = jnp.sum(dcm, axis=0, keepdims=True)
        row = lax.broadcasted_iota(jnp.int32, (8, tc), 0)
        st_ref[...] += jnp.where(row == 0, s0, jnp.where(row == 1, s1, jnp.where(row == 2, s2,
                                 jnp.where(row == 3, s3, 0.0))))

    main = lambda off: pl.BlockSpec((CHUNK, tc), lambda j, i: (i, j + off))
    nxt = lambda off: pl.BlockSpec((HALO, tc), lambda j, i: (jnp.minimum((i + 1) * r8, ni * r8 - 1), j + off))
    return pl.pallas_call(
        kern, name="ffn_bwd", grid=(nj, ni),
        in_specs=[main(0), pl.BlockSpec((HALO, tc), lambda j, i: (jnp.maximum(i * r8 - 1, 0), j)), nxt(0),
                  main(nj), nxt(nj), main(0), nxt(0),
                  pl.BlockSpec((CONV_W, tc), lambda j, i: (0, j)), pl.BlockSpec((1, tc), lambda j, i: (0, j))],
        out_specs=[main(0), main(0), pl.BlockSpec((8, tc), lambda j, i: (0, j))],
        out_shape=[jax.ShapeDtypeStruct((tp, dff), BF16), jax.ShapeDtypeStruct((tp, dff), BF16),
                   jax.ShapeDtypeStruct((8, dff), F32)],
        compiler_params=_cp(("parallel", "arbitrary")),
    )(gu, gu, gu, gu, gu, dact, dact, conv_w, conv_b)


def _rot(x, cs, sn):
    x1, x2 = x[:, :128], x[:, 128:]
    return jnp.concatenate([x1 * cs - x2 * sn, x1 * sn + x2 * cs], axis=1)


def _rot_t(x, cs, sn):
    x1, x2 = x[:, :128], x[:, 128:]
    return jnp.concatenate([x1 * cs + x2 * sn, x2 * cs - x1 * sn], axis=1)


def _ret_tables(rh):
    lg = jnp.log(1.0 - 2.0 ** (-5.0 - jnp.arange(rh, dtype=F32)))
    idx = jnp.arange(CHUNK, dtype=F32)
    diff = idx[:, None] - idx[None, :]
    intra = jnp.where(diff[None] >= 0, jnp.exp(jnp.maximum(diff, 0.0)[None] * lg[:, None, None]), 0.0)
    qdec = jnp.exp((idx[None, :] + 1.0) * lg[:, None])[..., None]
    kdec = jnp.exp((CHUNK - 1.0 - idx[None, :]) * lg[:, None])[..., None]
    cdec = jnp.exp(CHUNK * lg)[:, None, None]
    return intra, qdec, kdec, cdec


def _ret_specs(rh, nc, rev):
    hp = max(d for d in (8, 4, 2, 1) if rh % d == 0)
    ng, wd = rh // hp, hp * RET_DK
    cc = (lambda c: nc - 1 - c) if rev else (lambda c: c)
    col = lambda sec: pl.BlockSpec((CHUNK, wd), lambda h, c: (cc(c), sec * ng + h))
    tab = [pl.BlockSpec((CHUNK, 128), lambda h, c: (cc(c), 0))] * 2
    dec = [pl.BlockSpec((hp, CHUNK, CHUNK), lambda h, c: (h, 0, 0)),
           pl.BlockSpec((hp, CHUNK, 1), lambda h, c: (h, 0, 0)),
           pl.BlockSpec((hp, CHUNK, 1), lambda h, c: (h, 0, 0)),
           pl.BlockSpec((hp, 1, 1), lambda h, c: (h, 0, 0))]
    hw = pl.BlockSpec((1, wd), lambda h, c: (0, h))
    hcol = pl.BlockSpec((CHUNK, wd), lambda h, c: (cc(c), h))
    st = pl.BlockSpec((hp, 1, RET_DK, RET_DK), lambda h, c: (h, cc(c), 0, 0))
    heads = [(hh, slice(hh * RET_DK, (hh + 1) * RET_DK)) for hh in range(hp)]
    return heads, ng, col, tab, dec, hw, hcol, st


def _ret_fwd(proj, cos, sin, tables, gnw, rh):
    tp = proj.shape[0]
    nc = tp // CHUNK
    heads, ng, col, tab, dec, hw, hcol, st = _ret_specs(rh, nc, False)

    def kern(q_ref, k_ref, v_ref, g_ref, cos_ref, sin_ref, in_ref, qd_ref, kd_ref, cd_ref, w_ref,
             out_ref, ry_ref, st_ref, state):
        @pl.when(pl.program_id(1) == 0)
        def _():
            state[...] = jnp.zeros_like(state)

        cs, sn = cos_ref[...], sin_ref[...]
        for hh, sl in heads:
            q = (_rot(q_ref[:, sl], cs, sn) * (RET_DK ** -0.5)).astype(BF16)
            kf = _rot(k_ref[:, sl], cs, sn)
            k = kf.astype(BF16)
            v = v_ref[:, sl].astype(BF16)
            s_old = state[hh]
            s_b = s_old.astype(BF16)
            st_ref[hh, 0] = s_b
            sc = _dot(q, k, NT) * in_ref[hh]
            ry = _dot(sc.astype(BF16), v) + _dot(q, s_b) * qd_ref[hh]
            state[hh] = s_old * cd_ref[hh] + _dot((kf * kd_ref[hh]).astype(BF16), v, TN)
            ry_ref[:, sl] = ry
            g = g_ref[:, sl]
            out_ref[:, sl] = (g * _sigmoid(g) * _rms_fwd(ry, w_ref[:, sl])).astype(BF16)

    return pl.pallas_call(
        kern, name="ret_fwd", grid=(ng, nc),
        in_specs=[col(0), col(1), col(2), col(3)] + tab + dec + [hw],
        out_specs=[hcol, hcol, st],
        out_shape=[jax.ShapeDtypeStruct((tp, rh * RET_DK), BF16), jax.ShapeDtypeStruct((tp, rh * RET_DK), F32),
                   jax.ShapeDtypeStruct((rh, nc, RET_DK, RET_DK), BF16)],
        scratch_shapes=[pltpu.VMEM((len(heads), RET_DK, RET_DK), F32)],
        compiler_params=_cp(("parallel", "arbitrary")),
    )(proj, proj, proj, proj, cos, sin, *tables, gnw)


def _ret_bwd(proj, dmix, ry_all, states, cos, sin, tables, gnw, rh):
    tp = proj.shape[0]
    nc = tp // CHUNK
    heads, ng, col, tab, dec, hw, hcol, st = _ret_specs(rh, nc, True)

    def kern(q_ref, k_ref, v_ref, g_ref, cos_ref, sin_ref, in_ref, qd_ref, kd_ref, cd_ref, w_ref,
             do_ref, ry_ref, st_ref, dq_ref, dk_ref, dv_ref, dg_ref, dw_ref, ds):
        @pl.when(pl.program_id(1) == 0)
        def _():
            ds[...] = jnp.zeros_like(ds)
            dw_ref[...] = jnp.zeros_like(dw_ref)

        cs, sn = cos_ref[...], sin_ref[...]
        for hh, sl in heads:
            q = (_rot(q_ref[:, sl], cs, sn) * (RET_DK ** -0.5)).astype(BF16)
            kf = _rot(k_ref[:, sl], cs, sn)
            k = kf.astype(BF16)
            v = v_ref[:, sl].astype(BF16)
            g = g_ref[:, sl]
            ry = ry_ref[:, sl]
            w = w_ref[:, sl]
            dout = do_ref[:, sl]
            sg = _sigmoid(g)
            dhn = dout * (g * sg)
            dry, dwr = _rms_bwd(ry, w, dhn)
            dg_ref[:, sl] = (dout * _rms_fwd(ry, w) * (sg * (1.0 + g * (1.0 - sg)))).astype(BF16)
            dw_ref[:, sl] += jnp.sum(dwr, axis=0, keepdims=True)

            dmat = in_ref[hh]
            qd, kd = qd_ref[hh], kd_ref[hh]
            dyb = dry.astype(BF16)
            p = (_dot(q, k, NT) * dmat).astype(BF16)
            dp = (_dot(dyb, v, NT) * dmat).astype(BF16)
            ady = (dry * qd).astype(BF16)
            ds_old = ds[hh]
            ds_b = ds_old.astype(BF16)
            dq = _dot(dp, k) + _dot(ady, st_ref[hh, 0], NT)
            dk = _dot(dp, q, TN) + _dot(v, ds_b, NT) * kd
            dv = _dot(p, dyb, TN) + _dot((kf * kd).astype(BF16), ds_b)
            ds[hh] = ds_old * cd_ref[hh] + _dot(q, ady, TN)
            dq_ref[:, sl] = _rot_t(dq * (RET_DK ** -0.5), cs, sn).astype(BF16)
            dk_ref[:, sl] = _rot_t(dk, cs, sn).astype(BF16)
            dv_ref[:, sl] = dv.astype(BF16)

    rw = rh * RET_DK
    outs = pl.pallas_call(
        kern, name="ret_bwd", grid=(ng, nc),
        in_specs=[col(0), col(1), col(2), col(3)] + tab + dec + [hw, hcol, hcol, st],
        out_specs=[hcol, hcol, hcol, hcol, hw],
        out_shape=[jax.ShapeDtypeStruct((tp, rw), BF16)] * 4 + [jax.ShapeDtypeStruct((1, rw), F32)],
        scratch_shapes=[pltpu.VMEM((len(heads), RET_DK, RET_DK), F32)],
        compiler_params=_cp(("parallel", "arbitrary")),
    )(proj, proj, proj, proj, cos, sin, *tables, gnw, dmix, ry_all, states)
    return outs


def _sb_tile(tp):
    return 3 * CHUNK if tp % (3 * CHUNK) == 0 else CHUNK


def _sb_heads(sh, col0):
    hp = SB_HEADS_PER_STEP if sh % SB_HEADS_PER_STEP == 0 and col0 % SB_HEADS_PER_STEP == 0 else 1
    heads = [(hh, slice(hh * SB_DH, (hh + 1) * SB_DH)) for hh in range(hp)]
    return heads, sh // hp, hp * SB_DH, col0 // hp


def _sb_block(q, k, qpos, kb, scale, masked):
    z = _dot(q, k, NT) * scale
    t = jnp.log(1.0 + jnp.exp(-jnp.abs(z)))
    lb = jnp.minimum(z, 0.0) - t
    lk = -jnp.maximum(z, 0.0) - t
    if not masked:
        return None, lb, lk
    kpos = kb * CHUNK + lax.broadcasted_iota(jnp.int32, qpos.shape, 1)
    mask = (kpos < qpos) & (kpos >= N_PAD)
    return mask, lb, jnp.where(mask, lk, 0.0)


def _keep(mask, x):
    return x if mask is None else jnp.where(mask, x, 0.0)


def _sb_trips(i, trip):
    lax.fori_loop(0, 1, trip(True), 0)
    lax.fori_loop(1, i, trip(False), 0)
    lax.fori_loop(jnp.maximum(i, 1), i + 1, trip(True), 0)


def _tri_sum(x, tri):
    hi = x.astype(BF16)
    lo = (x - hi.astype(F32)).astype(BF16)
    if tri.shape[0] == CHUNK:
        return _dot(hi, tri) + _dot(lo, tri)
    return _dot(jnp.concatenate([hi, lo], axis=1), tri)


def _tri(strict_upper, copies=1):
    r = lax.broadcasted_iota(jnp.int32, (copies * CHUNK, CHUNK), 0) % CHUNK
    c = lax.broadcasted_iota(jnp.int32, (copies * CHUNK, CHUNK), 1)
    return ((r > c) if strict_upper else (r < c)).astype(BF16)


def _sb_fwd(proj, sbw, sh, col0, carry=()):
    tp = proj.shape[0]
    tq = _sb_tile(tp)
    nsub, nq = tq // CHUNK, tp // tq
    assert tp // CHUNK <= 128
    scale = 1.0 / math.sqrt(SB_DH)
    heads, ng, wd, cb = _sb_heads(sh, col0)
    hq = pl.BlockSpec((tq, wd), lambda h, i: (i, h))

    carry = list(carry)
    x_in, x_out, x_sems, alias = _carried(carry, 4, 3)

    def kern(q_ref, k_ref, v_ref, w_ref, *rest):
        x_in_refs, rest = rest[:len(x_in)], rest[len(x_in):]
        out_ref, sy_ref, ao_ref = rest[:3]
        x_out_refs, (a_run, k16, v16), sem_refs = rest[3:3 + len(x_out)], rest[3 + len(x_out):6 + len(x_out)], rest[6 + len(x_out):]
        h, i = pl.program_id(0), pl.program_id(1)
        _carried_before(carry, h * nq + i, ng * nq, x_in_refs, x_out_refs, sem_refs)

        @pl.when(i == 0)
        def _():
            k16[...] = k_ref[...].astype(BF16)
            v16[...] = v_ref[...].astype(BF16)

        qs = [q_ref[:, sl].astype(BF16) for _, sl in heads]
        upper = _tri(True)
        lane = lax.broadcasted_iota(jnp.int32, (tq, CHUNK), 1)
        qpos = i * tq + lax.broadcasted_iota(jnp.int32, (tq, CHUNK), 0)
        a_run[...] = jnp.zeros_like(a_run)
        sy_ref[...] = jnp.zeros_like(sy_ref)
        ao_ref[...] = jnp.zeros_like(ao_ref)

        def trip(masked):
            def body(jj, carry):
                for hh, sl in heads:
                    a, acc, at = a_run[hh], sy_ref[:, sl], ao_ref[hh]
                    for sub in reversed(range(nsub)):
                        kb = (i - jj) * nsub + sub
                        rows = pl.ds(pl.multiple_of(kb * CHUNK, CHUNK), CHUNK)
                        mask, lb, lk = _sb_block(qs[hh], k16[rows, sl], qpos, kb, scale, masked)
                        wgt = _keep(mask, jnp.exp(lb + a + _tri_sum(lk, upper)))
                        acc = acc + _dot(wgt.astype(BF16), v16[rows, sl])
                        at = jnp.where(lane == kb, a, at)
                        a = a + jnp.sum(lk, axis=1, keepdims=True)
                    a_run[hh], sy_ref[:, sl], ao_ref[hh] = a, acc, at
                return carry
            return body

        _sb_trips(i, trip)
        for _, sl in heads:
            out_ref[:, sl] = _rms_fwd(sy_ref[:, sl], w_ref[:, sl]).astype(BF16)
        _carried_after(carry, h * nq + i, ng * nq, x_in_refs, x_out_refs, sem_refs)

    kv = lambda sec: pl.BlockSpec((tp, wd), lambda h, i: (0, cb + sec * ng + h))
    res = pl.pallas_call(
        kern, name="sb_fwd", grid=(ng, nq),
        in_specs=[pl.BlockSpec((tq, wd), lambda h, i: (i, cb + h)), kv(1), kv(2),
                  pl.BlockSpec((1, wd), lambda h, i: (0, h))] + [ANY] * len(x_in),
        out_specs=[hq, hq, pl.BlockSpec((len(heads), tq, 128), lambda h, i: (h, i, 0))] + [ANY] * len(x_out),
        out_shape=[jax.ShapeDtypeStruct((tp, sh * SB_DH), BF16), jax.ShapeDtypeStruct((tp, sh * SB_DH), F32),
                   jax.ShapeDtypeStruct((sh, tp, 128), F32)] + x_out,
        input_output_aliases=alias,
        scratch_shapes=[pltpu.VMEM((len(heads), tq, CHUNK), F32), pltpu.VMEM((tp, wd), BF16),
                        pltpu.VMEM((tp, wd), BF16)] + x_sems,
        compiler_params=_cp(("arbitrary", "arbitrary") if carry else ("parallel", "arbitrary")),
    )(proj, proj, proj, sbw, *x_in)
    return res[:3], res[3:]


def _sb_bwd(proj, dmix, sy_all, aoff, sbw, sh, col0, dcol0, carry=()):
    tp = proj.shape[0]
    tq = _sb_tile(tp)
    nsub, nq = tq // CHUNK, tp // tq
    scale = 1.0 / math.sqrt(SB_DH)
    hq = pl.BlockSpec((tq, SB_DH), lambda h, i: (i, h))
    carry = list(carry)
    x_in, x_out, x_sems, alias = _carried(carry, 7, 4)

    def kern(q_ref, k_ref, v_ref, w_ref, do_ref, sy_ref, ao_ref, *rest):
        x_in_refs, rest = rest[:len(x_in)], rest[len(x_in):]
        dq_ref, dk_ref, dv_ref, dw_ref = rest[:4]
        x_out_refs, rest = rest[4:4 + len(x_out)], rest[4 + len(x_out):]
        (dk_acc, dv_acc, dq_acc, e_run, k16, v16), sem_refs = rest[:6], rest[6:]
        h, i = pl.program_id(0), pl.program_id(1)
        _carried_before(carry, h * nq + i, sh * nq, x_in_refs, x_out_refs, sem_refs)

        @pl.when(i == 0)
        def _():
            dk_acc[...] = jnp.zeros_like(dk_acc)
            dv_acc[...] = jnp.zeros_like(dv_acc)
            dw_ref[...] = jnp.zeros_like(dw_ref)
            k16[...] = k_ref[...].astype(BF16)
            v16[...] = v_ref[...].astype(BF16)

        qf = q_ref[...]
        q = qf.astype(BF16)
        dsy, dwr = _rms_bwd(sy_ref[...], w_ref[...], do_ref[...])
        dw_ref[...] += jnp.sum(dwr, axis=0, keepdims=True)
        dsy_b = dsy.astype(BF16)
        q_t = qf.T.astype(BF16)
        dsy_t = dsy.T.astype(BF16)
        atile = ao_ref[0]
        upper = _tri(True, 2)
        lower = _tri(False, 2)
        lane = lax.broadcasted_iota(jnp.int32, (tq, CHUNK), 1)
        qpos = i * tq + lax.broadcasted_iota(jnp.int32, (tq, CHUNK), 0)

        e_run[...] = jnp.zeros_like(e_run)
        dq_acc[...] = jnp.zeros_like(dq_acc)

        def trip(masked):
            def body(jj, carry):
                e_prev, dq = e_run[...], dq_acc[...]
                for sub in range(nsub):
                    kb = jj * nsub + sub
                    rows = pl.ds(pl.multiple_of(kb * CHUNK, CHUNK), CHUNK)
                    k, v = k16[rows, :], v16[rows, :]
                    mask, lb, lk = _sb_block(q, k, qpos, kb, scale, masked)
                    a = jnp.sum(jnp.where(lane == kb, atile, 0.0), axis=1, keepdims=True)
                    wgt = _keep(mask, jnp.exp(lb + a + _tri_sum(lk, upper)))
                    e = wgt * _dot(dsy_b, v, NT)
                    dv_acc[kb] += _dot(dsy_t, wgt.astype(BF16))
                    sig = jnp.exp(lb)
                    e_all = e_prev + _tri_sum(e, lower)
                    dz = (_keep(mask, e - sig * (e + e_all)) * scale).astype(BF16)
                    dk_acc[kb] += _dot(q_t, dz)
                    dq = dq + _dot(dz, k)
                    e_prev = e_prev + jnp.sum(e, axis=1, keepdims=True)
                e_run[...], dq_acc[...] = e_prev, dq
                return carry
            return body

        _sb_trips(i, trip)
        dq_ref[...] = dq_acc[...].astype(BF16)

        @pl.when(i == nq - 1)
        def _():
            def untranspose(kb, c):
                rows = pl.ds(pl.multiple_of(kb * CHUNK, CHUNK), CHUNK)
                dk_ref[rows, :] = dk_acc[kb].T.astype(BF16)
                dv_ref[rows, :] = dv_acc[kb].T.astype(BF16)
                return c

            lax.fori_loop(0, tp // CHUNK, untranspose, 0)

        _carried_after(carry, h * nq + i, sh * nq, x_in_refs, x_out_refs, sem_refs)

    kv = lambda sec: pl.BlockSpec((tp, SB_DH), lambda h, i: (0, col0 + sec * sh + h))
    hfull = pl.BlockSpec((tp, SB_DH), lambda h, i: (0, h))
    sw = sh * SB_DH
    acc_t = pltpu.VMEM((tp // CHUNK, SB_DH, CHUNK), F32)
    res = pl.pallas_call(
        kern, name="sb_bwd", grid=(sh, nq),
        in_specs=[pl.BlockSpec((tq, SB_DH), lambda h, i: (i, col0 + h)), kv(1), kv(2),
                  pl.BlockSpec((1, SB_DH), lambda h, i: (0, h)),
                  pl.BlockSpec((tq, SB_DH), lambda h, i: (i, dcol0 + h)), hq,
                  pl.BlockSpec((1, tq, 128), lambda h, i: (h, i, 0))] + [ANY] * len(x_in),
        out_specs=[hq, hfull, hfull, pl.BlockSpec((1, SB_DH), lambda h, i: (0, h))] + [ANY] * len(x_out),
        out_shape=[jax.ShapeDtypeStruct((tp, sw), BF16)] * 3 + [jax.ShapeDtypeStruct((1, sw), F32)] + x_out,
        input_output_aliases=alias,
        scratch_shapes=[acc_t, acc_t, pltpu.VMEM((tq, SB_DH), F32), pltpu.VMEM((tq, CHUNK), F32),
                        pltpu.VMEM((tp, SB_DH), BF16), pltpu.VMEM((tp, SB_DH), BF16)] + x_sems,
        compiler_params=_cp(("arbitrary", "arbitrary") if carry else ("parallel", "arbitrary")),
    )(proj, proj, proj, sbw, dmix, sy_all, aoff, *x_in)
    return res[:4], res[4:]


def _local_step(x, tgt, meta, net, conv_w, conv_b, pre1_w, gn_w, sb_w, post1_w, pre2_w, post2_w):
    s, d = x.shape

    def mm(host, a, b, mode, dtype):
        carry = net.carry(host)
        out = _matmul(a, b, mode, dtype, host, carry)
        if carry:
            out, extra = out
            net.took(host, extra)
        return out

    tp = s + CHUNK
    rh, sh = d // 512, d // 256
    rw = rh * RET_DK
    pos = jnp.arange(tp, dtype=F32) - N_PAD
    inv = ROPE_BASE ** (-jnp.arange(128, dtype=F32) / 128)
    ang = pos[:, None] * inv[None, :]
    cos, sin = jnp.cos(ang), jnp.sin(ang)
    tables = _ret_tables(rh)
    sb_col0 = 4 * rw // SB_DH

    h0, n1, extra = _prenorm(x, meta, pre1_w, net.carry("prenorm1"))
    net.took("prenorm1", extra)
    proj = mm("mm_proj", n1, net.weight("w_in"), "nn", F32)
    ret_out, ry, states = _ret_fwd(proj, cos, sin, tables, gn_w, rh)
    (sb_out, sy, aoff), extra = _sb_fwd(proj, sb_w, sh, sb_col0, net.carry("sb_fwd"))
    net.took("sb_fwd", extra)
    mixed = jnp.concatenate([ret_out, sb_out], axis=1)
    a = mm("mm_out", mixed, net.weight("w_out"), "nn", F32)
    h1, n2 = _mid_fwd(h0, a, post1_w, pre2_w)
    gu = mm("mm_up", n2, net.weight("w_up"), "nn", BF16)
    act = _ffn_fwd(gu, conv_w, conv_b)
    f = mm("mm_down", act, net.weight("w_down"), "nn", F32)

    dy, d_f, loss, dw_post2 = _loss_bwd(h1, f, tgt, post2_w)
    d_act = mm("mm_dact", d_f, net.weight("w_down"), "nt", BF16)
    net.grad("w_down", mm("mm_gdown", act, d_f, "tn", BF16))
    d_gate, d_up, ffn_stats = _ffn_bwd(gu, d_act, conv_w, conv_b)
    d_gu = jnp.concatenate([d_gate, d_up], axis=1)
    d_n2 = mm("mm_dn2", d_gu, net.weight("w_up"), "nt", F32)
    net.grad("w_up", mm("mm_gup", n2, d_gu, "tn", BF16))
    dh1, d_a, dw_pre2, dw_post1 = _mid_bwd(dy, d_n2, h1, a, pre2_w, post1_w)
    d_mix = mm("mm_dmix", d_a, net.weight("w_out"), "nt", F32)
    net.grad("w_out", mm("mm_gout", mixed, d_a, "tn", BF16))
    d_rq, d_rk, d_rv, d_rg, dw_gn = _ret_bwd(proj, d_mix, ry, states, cos, sin, tables, gn_w, rh)
    (d_sq, d_sk, d_sv, dw_sb), extra = _sb_bwd(proj, d_mix, sy, aoff, sb_w, sh, sb_col0, rw // SB_DH,
                                               net.carry("sb_bwd"))
    net.took("sb_bwd", extra)
    d_proj = jnp.concatenate([d_rq, d_rk, d_rv, d_rg, d_sq, d_sk, d_sv], axis=1)
    net.grad("w_in", mm("mm_gin", n1, d_proj, "tn", BF16))
    d_n1 = mm("mm_dn1", d_proj, net.weight("w_in"), "nt", F32)
    grad_x, g_meta, dw_pre1 = _pre_bwd(dh1, d_n1, h0, pre1_w)

    small = dict(loss=loss, meta=g_meta, pre1=dw_pre1, gn=dw_gn, sb=dw_sb, post1=dw_post1, pre2=dw_pre2,
                 conv_w=ffn_stats[0:3], conv_b=ffn_stats[3:4], post2=dw_post2)
    return grad_x, small


def _coords():
    return lax.axis_index("x"), lax.axis_index("y"), lax.axis_index("c")


def _other_chips(x, y):
    return [(1 - x, y), (x, 1 - y), (1 - x, 1 - y)]


ANY = pl.BlockSpec(memory_space=pl.ANY)
VM = pl.BlockSpec(memory_space=pltpu.VMEM)


def _gather4_small(v):
    r = v.shape[0]

    def kern(v_ref, o_ref, send, recv):
        x, y, c = _coords()
        o_ref[2 * x + y] = v_ref[...]
        cps = [pltpu.make_async_remote_copy(v_ref, o_ref.at[2 * x + y], send.at[j], recv.at[j],
                                            device_id=(px, py, c), device_id_type=MESH)
               for j, (px, py) in enumerate(_other_chips(x, y))]
        for cp in cps:
            cp.start()
        for cp in cps:
            cp.wait()

    return pl.pallas_call(
        kern, name="gather_small", in_specs=[VM], out_specs=VM,
        out_shape=jax.ShapeDtypeStruct((4, r, 128), F32),
        scratch_shapes=[pltpu.SemaphoreType.DMA((3,)), pltpu.SemaphoreType.DMA((3,))],
    )(v)


def _allreduce8_small(v):
    r = v.shape[0]
    flips = [(fx, fy, fc) for fx in (0, 1) for fy in (0, 1) for fc in (0, 1)][1:]

    def kern(v_ref, o_ref, buf, send, recv):
        x, y, c = _coords()
        me = 4 * x + 2 * y + c
        buf[me] = v_ref[...]
        cps = [pltpu.make_async_remote_copy(v_ref, buf.at[me], send.at[j], recv.at[j],
                                            device_id=(x ^ fx, y ^ fy, c ^ fc), device_id_type=MESH)
               for j, (fx, fy, fc) in enumerate(flips)]
        for cp in cps:
            cp.start()
        for cp in cps:
            cp.wait()
        tot = buf[0]
        for j in range(1, 8):
            tot = tot + buf[j]
        o_ref[...] = tot

    return pl.pallas_call(
        kern, name="allreduce_small", in_specs=[VM], out_specs=VM,
        out_shape=jax.ShapeDtypeStruct((r, 128), F32),
        scratch_shapes=[pltpu.VMEM((8, r, 128), F32), pltpu.SemaphoreType.DMA((7,)),
                        pltpu.SemaphoreType.DMA((7,))],
    )(v)


def _piece(ref, col_sharded, rows, cols, s, hc):
    half = rows // 2
    if col_sharded:
        return ref.at[pl.ds(pl.multiple_of(hc * half, 16), half), pl.ds(pl.multiple_of(s * cols, 128), cols)]
    return ref.at[pl.ds(pl.multiple_of(s * rows + hc * half, 16), half), :]


def _cast_place(w32, c_id, s_id, col_sharded, name):
    rows, cols = w32.shape
    full_shape = (rows, 4 * cols) if col_sharded else (4 * rows, cols)
    tr = max(d for d in _divisors(rows, 16) if d * cols * 4 <= 4 * 2 ** 20)
    nt = rows // tr

    def kern(c_ref, s_ref, w_ref, o_ref):
        o_ref[...] = w_ref[...].astype(BF16)

    if col_sharded:
        o_spec = pl.BlockSpec((tr, cols), lambda t, c_ref, s_ref: (t, s_ref[0]))
    else:
        o_spec = pl.BlockSpec((tr, cols), lambda t, c_ref, s_ref: (s_ref[0] * nt + t, 0))
    return pl.pallas_call(
        kern, name=name,
        grid_spec=pltpu.PrefetchScalarGridSpec(
            num_scalar_prefetch=2, grid=(nt,),
            in_specs=[pl.BlockSpec((tr, cols), lambda t, c_ref, s_ref: (t, 0))], out_specs=o_spec),
        out_shape=jax.ShapeDtypeStruct(full_shape, BF16),
        compiler_params=_cp(("parallel",)),
    )(c_id, s_id, w32)


def _gather_exchange(full, rows, cols, col_sharded, relay_at=1.0):
    def copies(w_ref, o_ref, send, recv):
        x, y, c = _coords()
        s = 2 * x + y
        sib = (x, y, 1 - c)
        chips = _other_chips(x, y)
        pc = functools.partial(_piece, o_ref, col_sharded, rows, cols)
        mine = _piece(w_ref, col_sharded, rows, cols, s, c)
        first = [pltpu.make_async_remote_copy(mine, pc(s, c), send.at[j], recv.at[j],
                                              device_id=(px, py, c), device_id_type=MESH)
                 for j, (px, py) in enumerate(chips)]
        passed = [pltpu.make_async_remote_copy(pc(2 * px + py, c), pc(2 * px + py, c), send.at[3 + j], recv.at[3 + j],
                                               device_id=sib, device_id_type=MESH)
                  for j, (px, py) in enumerate(chips)]
        from_sib = [pltpu.make_async_remote_copy(pc(2 * px + py, 1 - c), pc(2 * px + py, 1 - c), send.at[3 + j],
                                                 recv.at[3 + j], device_id=sib, device_id_type=MESH)
                    for j, (px, py) in enumerate(chips)]
        return first, passed, from_sib

    def start(ins, outs, send, recv):
        for cp in copies(ins[0], outs[0], send, recv)[0]:
            cp.start()

    def relay(ins, outs, send, recv):
        first, passed, _ = copies(ins[0], outs[0], send, recv)
        for j in range(3):
            first[j].wait_recv()
            passed[j].start()

    def finish(ins, outs, send, recv):
        first, passed, from_sib = copies(ins[0], outs[0], send, recv)
        for cp in from_sib:
            cp.wait_recv()
        for cp in first + passed:
            cp.wait_send()

    return _Exchange([full], [jax.ShapeDtypeStruct(full.shape, BF16)], {0: 0}, 6, start, finish, relay, relay_at)


def _pair_exchange(g, col_sharded, rows, cols):
    half = rows // 2

    def copies(g_ref, o_ref, send, recv):
        x, y, c = _coords()
        return [pltpu.make_async_remote_copy(_piece(g_ref, col_sharded, rows, cols, s, 1 - c), o_ref.at[s],
                                             send.at[s], recv.at[s], device_id=(x, y, 1 - c), device_id_type=MESH)
                for s in range(4)]

    def start(ins, outs, send, recv):
        for cp in copies(ins[0], outs[0], send, recv):
            cp.start()

    def finish(ins, outs, send, recv):
        for cp in copies(ins[0], outs[0], send, recv):
            cp.wait()

    return _Exchange([g], [jax.ShapeDtypeStruct((4, half, cols), BF16)], {}, 4, start, finish)


def _half_tiles(half, cols):
    tr = max(d for d in _divisors(half, 16) if d * cols * 4 <= 4 * 2 ** 20)
    return tr, half // tr


def _half_spec(col_sharded, tr, nt, cols, own):
    which = (lambda s, s_ref: s_ref[0]) if own else (lambda s, s_ref: (s_ref[0] + 1 + s) % 4)
    if col_sharded:
        return pl.BlockSpec((tr, cols), lambda s, t, c_ref, s_ref: (c_ref[0] * nt + t, which(s, s_ref)))
    return pl.BlockSpec((tr, cols), lambda s, t, c_ref, s_ref: ((2 * which(s, s_ref) + c_ref[0]) * nt + t, 0))


def _rs_add(g, r1, c_id, s_id, col_sharded, rows, cols, name):
    half = rows // 2
    tr, nt = _half_tiles(half, cols)

    def kern(c_ref, s_ref, g_ref, r_ref, o_ref):
        o_ref[0] = (g_ref[...].astype(F32) + r_ref[0].astype(F32)).astype(BF16)

    slab = pl.BlockSpec((1, tr, cols), lambda s, t, c_ref, s_ref: ((s_ref[0] + 1 + s) % 4, t, 0))
    return pl.pallas_call(
        kern, name=name,
        grid_spec=pltpu.PrefetchScalarGridSpec(
            num_scalar_prefetch=2, grid=(3, nt),
            in_specs=[_half_spec(col_sharded, tr, nt, cols, False), slab], out_specs=slab),
        out_shape=jax.ShapeDtypeStruct((4, half, cols), BF16),
        compiler_params=_cp(("parallel", "parallel")),
    )(c_id, s_id, g, r1)


def _scatter_exchange(p):
    _, half, cols = p.shape

    def copies(p_ref, o_ref, send, recv):
        x, y, c = _coords()
        return [pltpu.make_async_remote_copy(p_ref.at[2 * px + py], o_ref.at[j], send.at[j], recv.at[j],
                                             device_id=(px, py, c), device_id_type=MESH)
                for j, (px, py) in enumerate(_other_chips(x, y))]

    def start(ins, outs, send, recv):
        for cp in copies(ins[0], outs[0], send, recv):
            cp.start()

    def finish(ins, outs, send, recv):
        for cp in copies(ins[0], outs[0], send, recv):
            cp.wait()

    return _Exchange([p], [jax.ShapeDtypeStruct((3, half, cols), BF16)], {}, 3, start, finish)


def _rs_total(g, r1, r2, c_id, s_id, col_sharded, rows, cols, name):
    half = rows // 2
    tr, nt = _half_tiles(half, cols)

    def kern(c_ref, s_ref, g_ref, r1_ref, r2_ref, o_ref):
        tot = g_ref[...].astype(F32) + r1_ref[0].astype(F32)
        for j in range(3):
            tot = tot + r2_ref[j].astype(F32)
        o_ref[...] = tot

    return pl.pallas_call(
        kern, name=name,
        grid_spec=pltpu.PrefetchScalarGridSpec(
            num_scalar_prefetch=2, grid=(1, nt),
            in_specs=[_half_spec(col_sharded, tr, nt, cols, True),
                      pl.BlockSpec((1, tr, cols), lambda s, t, c_ref, s_ref: (s_ref[0], t, 0)),
                      pl.BlockSpec((3, tr, cols), lambda s, t, c_ref, s_ref: (0, t, 0))],
            out_specs=pl.BlockSpec((tr, cols), lambda s, t, c_ref, s_ref: (c_ref[0] * nt + t, 0))),
        out_shape=jax.ShapeDtypeStruct((rows, cols), F32),
        compiler_params=_cp(("parallel", "parallel")),
    )(c_id, s_id, g, r1, r2)


def _rs_exchange(ts, name):
    nw = len(ts)

    def kern(*refs):
        t_refs, o_refs, (send, recv) = refs[:nw], refs[nw:2 * nw], refs[2 * nw:]
        x, y, c = _coords()
        cps = []
        for k, (t_ref, o_ref) in enumerate(zip(t_refs, o_refs)):
            half = t_ref.shape[0] // 2
            mine = pl.ds(pl.multiple_of(c * half, 8), half)
            cps.append(pltpu.make_async_remote_copy(t_ref.at[mine, :], o_ref.at[mine, :], send.at[k], recv.at[k],
                                                    device_id=(x, y, 1 - c), device_id_type=MESH))
        for cp in cps:
            cp.start()
        for cp in cps:
            cp.wait()

    return pl.pallas_call(
        kern, name=name, in_specs=[ANY] * nw, out_specs=[ANY] * nw, input_output_aliases={k: k for k in range(nw)},
        out_shape=[jax.ShapeDtypeStruct(t.shape, F32) for t in ts],
        scratch_shapes=[pltpu.SemaphoreType.DMA((nw,)), pltpu.SemaphoreType.DMA((nw,))],
    )(*ts)


class _Whole:
    def __init__(self, full):
        self.full, self.grads = dict(full), {}

    def weight(self, n):
        return self.full[n]

    def carry(self, host):
        return []

    def took(self, host, outs):
        pass

    def grad(self, n, g):
        self.grads[n] = g


BIG = ("w_in", "w_out", "w_up", "w_down")
COL_SHARDED = dict(w_in=True, w_out=False, w_up=True, w_down=False)
GATHER_ON = dict(prenorm1="w_in", mm_proj="w_out", sb_fwd="w_up", mm_up="w_down")
RELAY_AT = dict(prenorm1=1.0, mm_proj=0.6, sb_fwd=0.85, mm_up=0.7)
PAIR_ON = dict(mm_dn2="w_down", mm_dmix="w_up")
SCATTER_ON = dict(mm_gup=("w_down",), sb_bwd=("w_up", "w_out"), mm_dn1=("w_in",))


class _Sharded(_Whole):
    def __init__(self, shards, c_id, s_id):
        self.shards, self.c_id, self.s_id = shards, c_id, s_id
        self.full, self.grads, self.r1, self.p, self.r2 = {}, {}, {}, {}, {}
        self.placed = {n: _cast_place(shards[n], c_id, s_id, COL_SHARDED[n], "place_" + n) for n in BIG}

    def _gather(self, n, relay_at=1.0):
        return _gather_exchange(self.placed[n], *self.shards[n].shape, COL_SHARDED[n], relay_at)

    def _pair(self, n):
        return _pair_exchange(self.grads[n], COL_SHARDED[n], *self.shards[n].shape)

    def _paired(self, n, r1):
        rows, cols = self.shards[n].shape
        self.r1[n] = r1
        self.p[n] = _rs_add(self.grads[n], r1, self.c_id, self.s_id, COL_SHARDED[n], rows, cols, "rs_add_" + n)

    def carry(self, host):
        if host in GATHER_ON:
            return [self._gather(GATHER_ON[host], RELAY_AT[host])]
        if host in PAIR_ON:
            return [self._pair(PAIR_ON[host])]
        return [_scatter_exchange(self.p[n]) for n in SCATTER_ON.get(host, ())]

    def took(self, host, outs):
        if host in GATHER_ON:
            self.full[GATHER_ON[host]], = outs
        if host in PAIR_ON:
            self._paired(PAIR_ON[host], outs[0])
        for n, r2 in zip(SCATTER_ON.get(host, ()), outs):
            self.r2[n] = r2

    def grad(self, n, g):
        self.grads[n] = g
        if n not in PAIR_ON.values():
            self._paired(n, _run_exchange(self._pair(n), "rs_pair_" + n)[0])

    def reduced(self):
        ts = [_rs_total(self.grads[n], self.r1[n], self.r2[n], self.c_id, self.s_id, COL_SHARDED[n],
                        *self.shards[n].shape, "rs_total_" + n) for n in BIG]
        return dict(zip(BIG, _rs_exchange(ts, "rs_exchange")))


def _adamw_vals(w, g, m, v):
    m = ADAM_B1 * m + (1.0 - ADAM_B1) * g
    v = ADAM_B2 * v + (1.0 - ADAM_B2) * (g * g)
    m_hat = m / (1.0 - ADAM_B1 ** ADAM_STEP)
    v_hat = v / (1.0 - ADAM_B2 ** ADAM_STEP)
    delta = -ADAM_LR * (m_hat / (jnp.sqrt(v_hat) + ADAM_EPS) + ADAM_WD * w)
    return delta, m, v


def _adamw(w, g, m, v, name):
    rows, cols = w.shape
    tr = max(d for d in _divisors(rows, 8) if d * cols * 4 <= 2 * 2 ** 20)

    def kern(w_ref, g_ref, m_ref, v_ref, d_ref, mo_ref, vo_ref, go_ref):
        g = g_ref[...]
        d_ref[...], mo_ref[...], vo_ref[...] = _adamw_vals(w_ref[...], g, m_ref[...], v_ref[...])
        go_ref[...] = g

    spec = pl.BlockSpec((tr, cols), lambda i: (i, 0))
    return pl.pallas_call(
        kern, name=name, grid=(rows // tr,), in_specs=[spec] * 4, out_specs=[spec] * 4,
        out_shape=[jax.ShapeDtypeStruct((rows, cols), F32)] * 4,
        compiler_params=_cp(("parallel",)),
    )(w, g, m, v)


def _pack(arrs):
    flat = []
    for a in arrs:
        a = a.reshape(-1)
        flat.append(jnp.pad(a, (0, (-a.shape[0]) % 1024)))
    return jnp.concatenate(flat).reshape(-1, 128)


def _unpack(slab, shapes):
    out, off = [], 0
    flat = slab.reshape(slab.shape[:-2] + (-1,))
    for shp in shapes:
        n = math.prod(shp)
        out.append(flat[..., off:off + n].reshape(slab.shape[:-2] + tuple(shp)))
        off += n + (-n) % 1024
    return out


SMALL = ("meta_tokens", "attn_pre_norm_w", "ret_gn_w", "sb_norm_w", "attn_post_norm_w", "ffn_pre_norm_w",
         "conv_w", "conv_b", "ffn_post_norm_w")
ORDER = ("meta_tokens", "attn_pre_norm_w", "w_in", "ret_gn_w", "sb_norm_w", "w_out", "attn_post_norm_w",
         "ffn_pre_norm_w", "w_up", "conv_w", "conv_b", "w_down", "ffn_post_norm_w")


def kernel(x, meta_tokens, attn_pre_norm_w, w_in, ret_gn_w, sb_norm_w, w_out, attn_post_norm_w, ffn_pre_norm_w, w_up, conv_w, conv_b, w_down, ffn_post_norm_w, loss_target, m_meta_tokens, m_attn_pre_norm_w, m_w_in, m_ret_gn_w, m_sb_norm_w, m_w_out, m_attn_post_norm_w, m_ffn_pre_norm_w, m_w_up, m_conv_w, m_conv_b, m_w_down, m_ffn_post_norm_w, v_meta_tokens, v_attn_pre_norm_w, v_w_in, v_ret_gn_w, v_sb_norm_w, v_w_out, v_attn_post_norm_w, v_ffn_pre_norm_w, v_w_up, v_conv_w, v_conv_b, v_w_down, v_ffn_post_norm_w):
    args = dict(locals())
    w = {n: args[n] for n in ORDER}
    m = {n: args["m_" + n] for n in ORDER}
    v = {n: args["v_" + n] for n in ORDER}
    xi, yi, ci = _coords()
    shard_id = 2 * xi + yi
    c_id = ci.astype(jnp.int32).reshape(1)
    s_id = shard_id.astype(jnp.int32).reshape(1)
    d = x.shape[-1]

    mshape, cshape = w["meta_tokens"].shape, w["conv_w"][0].shape
    got = _unpack(_gather4_small(_pack([w["meta_tokens"], w["conv_w"][0]])), [mshape, cshape])
    meta_full = jnp.moveaxis(got[0], 0, 1).reshape(N_META, d)
    conv_w_full = jnp.moveaxis(got[1], 0, 1).reshape(CONV_W, -1)

    net = _Sharded({n: w[n][0] for n in BIG}, c_id, s_id)
    grad_x, g_small = _local_step(
        x[0], loss_target[0], meta_full, net, conv_w_full, w["conv_b"], w["attn_pre_norm_w"], w["ret_gn_w"],
        w["sb_norm_w"], w["attn_post_norm_w"], w["ffn_pre_norm_w"], w["ffn_post_norm_w"])

    names = ("loss", "meta", "pre1", "gn", "sb", "post1", "pre2", "conv_w", "conv_b", "post2")
    tot = _unpack(_allreduce8_small(_pack([g_small[n] for n in names])), [g_small[n].shape for n in names])
    tot = dict(zip(names, tot))
    loss = tot["loss"][0, 0]
    mcols, ccols = mshape[1], cshape[1]
    grads = {
        "meta_tokens": lax.dynamic_slice_in_dim(tot["meta"], shard_id * mcols, mcols, axis=1),
        "attn_pre_norm_w": tot["pre1"], "ret_gn_w": tot["gn"], "sb_norm_w": tot["sb"],
        "attn_post_norm_w": tot["post1"], "ffn_pre_norm_w": tot["pre2"],
        "conv_w": lax.dynamic_slice_in_dim(tot["conv_w"], shard_id * ccols, ccols, axis=1)[None],
        "conv_b": tot["conv_b"], "ffn_post_norm_w": tot["post2"],
    }

    delta, new_m, new_v = {}, {}, {}
    reduced = net.reduced()
    for n in BIG:
        dl, mo, vo, g = _adamw(w[n][0], reduced[n], m[n][0], v[n][0], "adamw_" + n)
        delta[n], new_m[n], new_v[n], grads[n] = dl[None], mo[None], vo[None], g[None]
    shapes = [w[n].shape for n in SMALL]
    packed = [_pack([src[n] for n in SMALL]) for src in (w, grads, m, v)]
    outs = _adamw(*packed, "adamw_small")[:3]
    for dst, slab in zip((delta, new_m, new_v), outs):
        for n, a in zip(SMALL, _unpack(slab, shapes)):
            dst[n] = a

    return (loss, grad_x[None], *[grads[n] for n in ORDER], *[delta[n] for n in ORDER],
            *[new_m[n] for n in ORDER], *[new_v[n] for n in ORDER])
```
